```python
import jax, jax.numpy as jnp
from jax import lax
import numpy as np

D_MODEL = 1024
BATCH = 8
SEQ = 8192
DEPTH = 4

CHUNK = 64
GLA_HEADS = 4
GLA_DK = D_MODEL // 2
GLA_DV = D_MODEL
GLA_HEAD_DK = GLA_DK // GLA_HEADS
GLA_HEAD_DV = GLA_DV // GLA_HEADS
GATE_RANK = 16
GATE_TAU = 16.0
POOL_WIDTH = D_MODEL
POOL_WINDOWS = (2, 4, 8, 16)
POOL_GROUPS = len(POOL_WINDOWS)
POOL_GROUP_DIM = POOL_WIDTH // POOL_GROUPS
N_BRANCHES = 2
IN_SPLITS = (GLA_DK, GLA_DK, GLA_DV, GLA_DV, GATE_RANK, POOL_WIDTH, POOL_WIDTH, N_BRANCHES * D_MODEL)
IN_COLS = sum(IN_SPLITS)
IN_OFFSETS = tuple(int(v) for v in np.cumsum(IN_SPLITS)[:-1])
DEEPNORM_ALPHA = (2.0 * DEPTH) ** 0.25
DEEPNORM_BETA = (8.0 * DEPTH) ** -0.25
EPS = 1e-5

kernel_name = "hybrid_gla_pool_deepnorm_encoder"


def _layernorm(x, g, b):
    x32 = x.astype(jnp.float32)
    mu = jnp.mean(x32, axis=-1, keepdims=True)
    var = jnp.mean(jnp.square(x32 - mu), axis=-1, keepdims=True)
    return ((x32 - mu) * lax.rsqrt(var + EPS) * g + b).astype(x.dtype)


def _gla_chunked(q, k, v, log_alpha):
    B, S, H, dk = q.shape
    dv = v.shape[-1]
    nc = S // CHUNK

    def to_chunks(a):
        return a.reshape(B, nc, CHUNK, H, a.shape[-1]).transpose(1, 0, 3, 2, 4)

    qc, kc, vc, lac = to_chunks(q), to_chunks(k), to_chunks(v), to_chunks(log_alpha)

    def step(state, inp):
        qb, kb, vb, lab = inp
        G = jnp.cumsum(lab, axis=2)
        decay = jnp.exp(-jnp.abs(G[:, :, :, None, :] - G[:, :, None, :, :]))
        scores = jnp.einsum('bhtd,bhsd,bhtsd->bhts', qb, kb, decay)
        o_intra = jnp.einsum('bhts,bhsv->bhtv', scores, vb)
        o_inter = jnp.einsum('bhtd,bhdv->bhtv', qb * jnp.exp(G), state)
        G_last = G[:, :, -1:, :]
        new_state = jnp.exp(G_last)[:, :, 0, :, None] * state + jnp.einsum(
            'bhsd,bhsv->bhdv', kb * jnp.exp(G_last - G), vb)
        return new_state, o_intra + o_inter

    state0 = jnp.zeros((B, H, dk, dv), jnp.float32)
    _, o = lax.scan(step, state0, (qc, kc, vc, lac))
    return o.transpose(1, 0, 3, 2, 4).reshape(B, S, H, dv)


def _multiscale_pool(u, w_grp, scale):
    B, S, _ = u.shape
    ug = u.astype(jnp.float32).reshape(B, S, POOL_GROUPS, POOL_GROUP_DIM)
    csum = jnp.cumsum(ug, axis=1)
    pos = jnp.arange(1, S + 1)
    means = []
    for g, w in enumerate(POOL_WINDOWS):
        c = csum[:, :, g]
        shifted = jnp.pad(c, ((0, 0), (w, 0), (0, 0)))[:, :S]
        cnt = jnp.minimum(pos, w).astype(jnp.float32)
        means.append((c - shifted) / cnt[None, :, None])
    pooled = jnp.stack(means, axis=2) - ug
    mixed = jnp.einsum('bsgi,gio->bsgo', pooled, w_grp)
    return mixed.reshape(B, S, POOL_WIDTH) * scale


def _layer(x, w_in, w_alpha_up, b_alpha, gla_norm_g, w_pool_grp, pool_scale,
           b_merge, w_proj_a, w_proj_b, w_out, ln_g, ln_b):
    B, S, _ = x.shape
    h = x @ w_in
    q, k, v, gate_a, alpha_low, pool_in, gate_b, merge_logits = jnp.split(h, IN_OFFSETS, axis=-1)

    log_alpha = jax.nn.log_sigmoid((alpha_low @ w_alpha_up + b_alpha).astype(jnp.float32)) / GATE_TAU
    qh = q.reshape(B, S, GLA_HEADS, GLA_HEAD_DK) * (GLA_HEAD_DK ** -0.5)
    kh = k.reshape(B, S, GLA_HEADS, GLA_HEAD_DK)
    vh = v.reshape(B, S, GLA_HEADS, GLA_HEAD_DV)
    lah = log_alpha.reshape(B, S, GLA_HEADS, GLA_HEAD_DK)
    o = _gla_chunked(qh, kh, vh, lah)
    o = o * lax.rsqrt(jnp.mean(jnp.square(o), axis=-1, keepdims=True) + EPS) * gla_norm_g
    y_a = o.reshape(B, S, GLA_DV).astype(x.dtype) * jax.nn.silu(gate_a)

    y_b = _multiscale_pool(pool_in, w_pool_grp, pool_scale).astype(x.dtype) * jax.nn.silu(gate_b)

    gates = jax.nn.sigmoid(merge_logits + b_merge)
    g_a, g_b = jnp.split(gates, N_BRANCHES, axis=-1)
    merged = g_a * (y_a @ w_proj_a) + g_b * (y_b @ w_proj_b)
    y = merged @ w_out

    return _layernorm(DEEPNORM_ALPHA * x + y, ln_g, ln_b)


def _fwd_setup_inputs(seed: int = 0) -> dict:
    key = jax.random.key(seed)
    ks = jax.random.split(key, 14)
    f32 = jnp.float32
    nrm = lambda k, shape, s: jax.random.normal(k, shape, f32) * s
    return {
        "x": jax.random.normal(ks[0], (BATCH, SEQ, D_MODEL), f32),
        "w_in": nrm(ks[1], (DEPTH, D_MODEL, IN_COLS), D_MODEL ** -0.5),
        "w_alpha_up": nrm(ks[2], (DEPTH, GATE_RANK, GLA_DK), GATE_RANK ** -0.5),
        "b_alpha": nrm(ks[3], (DEPTH, GLA_DK), 0.01),
        "gla_norm_g": 1.0 + nrm(ks[4], (DEPTH, GLA_HEADS, GLA_HEAD_DV), 0.02),
        "w_pool_grp": nrm(ks[5], (DEPTH, POOL_GROUPS, POOL_GROUP_DIM, POOL_GROUP_DIM), POOL_GROUP_DIM ** -0.5),
        "pool_scale": 1.0 + nrm(ks[6], (DEPTH, POOL_WIDTH), 0.02),
        "b_merge": nrm(ks[7], (DEPTH, N_BRANCHES * D_MODEL), 0.01),
        "w_proj_a": nrm(ks[8], (DEPTH, GLA_DV, D_MODEL), DEEPNORM_BETA * GLA_DV ** -0.5),
        "w_proj_b": nrm(ks[9], (DEPTH, POOL_WIDTH, D_MODEL), DEEPNORM_BETA * POOL_WIDTH ** -0.5),
        "w_out": nrm(ks[10], (DEPTH, D_MODEL, D_MODEL), DEEPNORM_BETA * D_MODEL ** -0.5),
        "ln_g": 1.0 + nrm(ks[11], (DEPTH, D_MODEL), 0.02),
        "ln_b": nrm(ks[12], (DEPTH, D_MODEL), 0.01),
    }


def _fwd_reference(x, w_in, w_alpha_up, b_alpha, gla_norm_g, w_pool_grp, pool_scale,
              b_merge, w_proj_a, w_proj_b, w_out, ln_g, ln_b):
    for l in range(DEPTH):
        x = _layer(x, w_in[l], w_alpha_up[l], b_alpha[l], gla_norm_g[l], w_pool_grp[l],
                   pool_scale[l], b_merge[l], w_proj_a[l], w_proj_b[l], w_out[l],
                   ln_g[l], ln_b[l])
    return x


import jax as _jax
import jax.numpy as _jnp

TWIN_FORMAT = 'train_step'
FWD_PARAMS = ['x', 'w_in', 'w_alpha_up', 'b_alpha', 'gla_norm_g', 'w_pool_grp', 'pool_scale', 'b_merge', 'w_proj_a', 'w_proj_b', 'w_out', 'ln_g', 'ln_b']
TWIN_WEIGHTS = ['w_in', 'w_alpha_up', 'b_alpha', 'gla_norm_g', 'w_pool_grp', 'pool_scale', 'b_merge', 'w_proj_a', 'w_proj_b', 'w_out', 'ln_g', 'ln_b']
TWIN_DIFF_INPUT = 'x'
TWIN_INPUTS = ['x', 'w_in', 'w_alpha_up', 'b_alpha', 'gla_norm_g', 'w_pool_grp', 'pool_scale', 'b_merge', 'w_proj_a', 'w_proj_b', 'w_out', 'ln_g', 'ln_b', 'loss_target', 'm_w_in', 'm_w_alpha_up', 'm_b_alpha', 'm_gla_norm_g', 'm_w_pool_grp', 'm_pool_scale', 'm_b_merge', 'm_w_proj_a', 'm_w_proj_b', 'm_w_out', 'm_ln_g', 'm_ln_b', 'v_w_in', 'v_w_alpha_up', 'v_b_alpha', 'v_gla_norm_g', 'v_w_pool_grp', 'v_pool_scale', 'v_b_merge', 'v_w_proj_a', 'v_w_proj_b', 'v_w_out', 'v_ln_g', 'v_ln_b']
TWIN_OUTPUTS = ['loss', 'grad_x', 'grad_w_in', 'grad_w_alpha_up', 'grad_b_alpha', 'grad_gla_norm_g', 'grad_w_pool_grp', 'grad_pool_scale', 'grad_b_merge', 'grad_w_proj_a', 'grad_w_proj_b', 'grad_w_out', 'grad_ln_g', 'grad_ln_b', 'delta_w_in', 'delta_w_alpha_up', 'delta_b_alpha', 'delta_gla_norm_g', 'delta_w_pool_grp', 'delta_pool_scale', 'delta_b_merge', 'delta_w_proj_a', 'delta_w_proj_b', 'delta_w_out', 'delta_ln_g', 'delta_ln_b', 'new_m_w_in', 'new_m_w_alpha_up', 'new_m_b_alpha', 'new_m_gla_norm_g', 'new_m_w_pool_grp', 'new_m_pool_scale', 'new_m_b_merge', 'new_m_w_proj_a', 'new_m_w_proj_b', 'new_m_w_out', 'new_m_ln_g', 'new_m_ln_b', 'new_v_w_in', 'new_v_w_alpha_up', 'new_v_b_alpha', 'new_v_gla_norm_g', 'new_v_w_pool_grp', 'new_v_pool_scale', 'new_v_b_merge', 'new_v_w_proj_a', 'new_v_w_proj_b', 'new_v_w_out', 'new_v_ln_g', 'new_v_ln_b']
TWIN_LEAF_KINDS = {'loss': 'loss', 'grad_x': 'grad_x', 'grad_w_in': 'grad_w', 'grad_w_alpha_up': 'grad_w', 'grad_b_alpha': 'grad_w', 'grad_gla_norm_g': 'grad_w', 'grad_w_pool_grp': 'grad_w', 'grad_pool_scale': 'grad_w', 'grad_b_merge': 'grad_w', 'grad_w_proj_a': 'grad_w', 'grad_w_proj_b': 'grad_w', 'grad_w_out': 'grad_w', 'grad_ln_g': 'grad_w', 'grad_ln_b': 'grad_w', 'delta_w_in': 'delta_w', 'delta_w_alpha_up': 'delta_w', 'delta_b_alpha': 'delta_w', 'delta_gla_norm_g': 'delta_w', 'delta_w_pool_grp': 'delta_w', 'delta_pool_scale': 'delta_w', 'delta_b_merge': 'delta_w', 'delta_w_proj_a': 'delta_w', 'delta_w_proj_b': 'delta_w', 'delta_w_out': 'delta_w', 'delta_ln_g': 'delta_w', 'delta_ln_b': 'delta_w', 'new_m_w_in': 'new_m', 'new_m_w_alpha_up': 'new_m', 'new_m_b_alpha': 'new_m', 'new_m_gla_norm_g': 'new_m', 'new_m_w_pool_grp': 'new_m', 'new_m_pool_scale': 'new_m', 'new_m_b_merge': 'new_m', 'new_m_w_proj_a': 'new_m', 'new_m_w_proj_b': 'new_m', 'new_m_w_out': 'new_m', 'new_m_ln_g': 'new_m', 'new_m_ln_b': 'new_m', 'new_v_w_in': 'new_v', 'new_v_w_alpha_up': 'new_v', 'new_v_b_alpha': 'new_v', 'new_v_gla_norm_g': 'new_v', 'new_v_w_pool_grp': 'new_v', 'new_v_pool_scale': 'new_v', 'new_v_b_merge': 'new_v', 'new_v_w_proj_a': 'new_v', 'new_v_w_proj_b': 'new_v', 'new_v_w_out': 'new_v', 'new_v_ln_g': 'new_v', 'new_v_ln_b': 'new_v'}


def _forward(args):
    return _fwd_reference(*[args[k] for k in FWD_PARAMS])


def _output_shape():
    def fwd():
        inp = _fwd_setup_inputs(0)
        return _fwd_reference(*[inp[k] for k in FWD_PARAMS])
    out = _jax.eval_shape(fwd)
    return out.shape, out.dtype

N_MICROBATCH = 1
ADAM_LR = 0.001
ADAM_B1 = 0.9
ADAM_B2 = 0.999
ADAM_EPS = 1e-08
ADAM_WD = 0.01
ADAM_STEP = 10
PER_EXAMPLE_BATCH_AXIS = {'x': 0, 'loss_target': 0}
SHARED_INPUTS = []
_WEIGHT_DTYPES = {'w_in': _jnp.float32, 'w_alpha_up': _jnp.float32, 'b_alpha': _jnp.float32, 'gla_norm_g': _jnp.float32, 'w_pool_grp': _jnp.float32, 'pool_scale': _jnp.float32, 'b_merge': _jnp.float32, 'w_proj_a': _jnp.float32, 'w_proj_b': _jnp.float32, 'w_out': _jnp.float32, 'ln_g': _jnp.float32, 'ln_b': _jnp.float32}
MOMENT_SCALE = {'w_in': 7.879771e-03, 'w_alpha_up': 1.301997e-03, 'b_alpha': 5.218966e-03, 'gla_norm_g': 8.615046e-03, 'w_pool_grp': 7.452860e-03, 'pool_scale': 7.619822e-03, 'b_merge': 3.109153e-03, 'w_proj_a': 2.027097e-02, 'w_proj_b': 1.777049e-02, 'w_out': 2.687935e-02, 'ln_g': 3.211658e+01, 'ln_b': 6.377270e-01}


def _to_microbatches(a, axis):
    t = _jnp.moveaxis(a, axis, 0)
    t = t.reshape((N_MICROBATCH, t.shape[0] // N_MICROBATCH) + t.shape[1:])
    return _jnp.moveaxis(t, 1, axis + 1)


def setup_inputs(seed: int = 0) -> dict:
    inp = _fwd_setup_inputs(seed)
    key = _jax.random.fold_in(_jax.random.key(seed), 7919)
    shape, _ = _output_shape()
    out = dict(inp)
    out["loss_target"] = _jax.random.normal(_jax.random.fold_in(key, 0), shape, _jnp.float32)
    for i, name in enumerate(TWIN_WEIGHTS):
        w = inp[name].astype(_jnp.float32)
        if MOMENT_SCALE is None:
            s = _jnp.sqrt(_jnp.mean(_jnp.square(w)) + 1e-30)
        else:
            s = MOMENT_SCALE[name]
        km, kv = _jax.random.split(_jax.random.fold_in(key, i + 1))
        out[name] = w
        out["m_" + name] = s * _jax.random.normal(km, w.shape, _jnp.float32)
        out["v_" + name] = (s * s) * _jax.random.uniform(kv, w.shape, _jnp.float32, 0.5, 1.5)
    if N_MICROBATCH > 1:
        for name, axis in PER_EXAMPLE_BATCH_AXIS.items():
            out[name] = _to_microbatches(out[name], axis)
    return {'x': out['x'], 'w_in': out['w_in'], 'w_alpha_up': out['w_alpha_up'], 'b_alpha': out['b_alpha'], 'gla_norm_g': out['gla_norm_g'], 'w_pool_grp': out['w_pool_grp'], 'pool_scale': out['pool_scale'], 'b_merge': out['b_merge'], 'w_proj_a': out['w_proj_a'], 'w_proj_b': out['w_proj_b'], 'w_out': out['w_out'], 'ln_g': out['ln_g'], 'ln_b': out['ln_b'], 'loss_target': out['loss_target'], 'm_w_in': out['m_w_in'], 'm_w_alpha_up': out['m_w_alpha_up'], 'm_b_alpha': out['m_b_alpha'], 'm_gla_norm_g': out['m_gla_norm_g'], 'm_w_pool_grp': out['m_w_pool_grp'], 'm_pool_scale': out['m_pool_scale'], 'm_b_merge': out['m_b_merge'], 'm_w_proj_a': out['m_w_proj_a'], 'm_w_proj_b': out['m_w_proj_b'], 'm_w_out': out['m_w_out'], 'm_ln_g': out['m_ln_g'], 'm_ln_b': out['m_ln_b'], 'v_w_in': out['v_w_in'], 'v_w_alpha_up': out['v_w_alpha_up'], 'v_b_alpha': out['v_b_alpha'], 'v_gla_norm_g': out['v_gla_norm_g'], 'v_w_pool_grp': out['v_w_pool_grp'], 'v_pool_scale': out['v_pool_scale'], 'v_b_merge': out['v_b_merge'], 'v_w_proj_a': out['v_w_proj_a'], 'v_w_proj_b': out['v_w_proj_b'], 'v_w_out': out['v_w_out'], 'v_ln_g': out['v_ln_g'], 'v_ln_b': out['v_ln_b']}


def _loss(weights, diff, rest, loss_target):
    with _jax.named_scope("forward"):
        args = {**rest, TWIN_DIFF_INPUT: diff, **{k: w.astype(_WEIGHT_DTYPES[k]) for k, w in weights.items()}}
        y = _forward(args)
    with _jax.named_scope("loss_head"):
        err = _jnp.square(y.astype(_jnp.float32) - loss_target)
        return 0.5 * _jnp.sum(_jnp.mean(err, axis=-1)) if err.ndim else 0.5 * err


def _adamw(w, g, m, v):
    m = ADAM_B1 * m + (1.0 - ADAM_B1) * g
    v = ADAM_B2 * v + (1.0 - ADAM_B2) * _jnp.square(g)
    m_hat = m / (1.0 - ADAM_B1 ** ADAM_STEP)
    v_hat = v / (1.0 - ADAM_B2 ** ADAM_STEP)
    delta = -ADAM_LR * (m_hat / (_jnp.sqrt(v_hat) + ADAM_EPS) + ADAM_WD * w)
    return delta, m, v


def reference(x, w_in, w_alpha_up, b_alpha, gla_norm_g, w_pool_grp, pool_scale, b_merge, w_proj_a, w_proj_b, w_out, ln_g, ln_b, loss_target, m_w_in, m_w_alpha_up, m_b_alpha, m_gla_norm_g, m_w_pool_grp, m_pool_scale, m_b_merge, m_w_proj_a, m_w_proj_b, m_w_out, m_ln_g, m_ln_b, v_w_in, v_w_alpha_up, v_b_alpha, v_gla_norm_g, v_w_pool_grp, v_pool_scale, v_b_merge, v_w_proj_a, v_w_proj_b, v_w_out, v_ln_g, v_ln_b):
    given = dict(x=x, w_in=w_in, w_alpha_up=w_alpha_up, b_alpha=b_alpha, gla_norm_g=gla_norm_g, w_pool_grp=w_pool_grp, pool_scale=pool_scale, b_merge=b_merge, w_proj_a=w_proj_a, w_proj_b=w_proj_b, w_out=w_out, ln_g=ln_g, ln_b=ln_b, loss_target=loss_target, m_w_in=m_w_in, m_w_alpha_up=m_w_alpha_up, m_b_alpha=m_b_alpha, m_gla_norm_g=m_gla_norm_g, m_w_pool_grp=m_w_pool_grp, m_pool_scale=m_pool_scale, m_b_merge=m_b_merge, m_w_proj_a=m_w_proj_a, m_w_proj_b=m_w_proj_b, m_w_out=m_w_out, m_ln_g=m_ln_g, m_ln_b=m_ln_b, v_w_in=v_w_in, v_w_alpha_up=v_w_alpha_up, v_b_alpha=v_b_alpha, v_gla_norm_g=v_gla_norm_g, v_w_pool_grp=v_w_pool_grp, v_pool_scale=v_pool_scale, v_b_merge=v_b_merge, v_w_proj_a=v_w_proj_a, v_w_proj_b=v_w_proj_b, v_w_out=v_w_out, v_ln_g=v_ln_g, v_ln_b=v_ln_b)
    weights = {n: given[n] for n in TWIN_WEIGHTS}
    shared = {n: given[n] for n in SHARED_INPUTS}
    per_example = {n: given[n] for n in ['x']}
    grad_fn = _jax.value_and_grad(_loss, argnums=(0, 1))

    def one_microbatch(ex, loss_target):
        ex = dict(ex)
        diff = ex.pop(TWIN_DIFF_INPUT)
        return grad_fn(weights, diff, {**shared, **ex}, loss_target)

    if N_MICROBATCH == 1:
        loss, (grad_w, grad_x) = one_microbatch(per_example, given["loss_target"])
    else:
        def body(carry, xs):
            loss_sum, grad_sum = carry
            l_k, (gw_k, gx_k) = one_microbatch(xs[0], xs[1])
            with _jax.named_scope("update"):
                return (loss_sum + l_k, _jax.tree.map(_jnp.add, grad_sum, gw_k)), gx_k

        init = (_jnp.zeros((), _jnp.float32), _jax.tree.map(_jnp.zeros_like, weights))
        (loss, grad_w), grad_x = _jax.lax.scan(body, init, (per_example, given["loss_target"]))
    with _jax.named_scope("update"):
        delta_w, new_m, new_v = {}, {}, {}
        for n in TWIN_WEIGHTS:
            delta_w[n], new_m[n], new_v[n] = _adamw(weights[n], grad_w[n], given["m_" + n], given["v_" + n])
    return (loss, grad_x, *[grad_w[n] for n in TWIN_WEIGHTS], *[delta_w[n] for n in TWIN_WEIGHTS],
            *[new_m[n] for n in TWIN_WEIGHTS], *[new_v[n] for n in TWIN_WEIGHTS])
```

```python
import functools

import jax
import jax.numpy as jnp
from jax import lax
from jax.experimental import pallas as pl
from jax.experimental.pallas import tpu as pltpu

F32 = jnp.float32
MXU = jnp.bfloat16

N_DEV = 8
DEPTH = 4
D = 1024
HEADS = 4
DK = D // 2
HDK = DK // HEADS
HDV = D // HEADS
RANK = 16
CHUNK = 64
GATE_TAU = 16.0
POOL_WINDOWS = (2, 4, 8, 16)
PG = D // len(POOL_WINDOWS)
HALO = 16
IN_COLS = 7184
ALPHA = (2.0 * DEPTH) ** 0.25
EPS = 1e-5
Q_SCALE = HDK ** -0.5

ADAM_LR, ADAM_B1, ADAM_B2, ADAM_EPS, ADAM_WD, ADAM_STEP = 0.001, 0.9, 0.999, 1e-08, 0.01, 10

H_MAIN = 7168
HP = H_MAIN + 128
AL_ORIG = 3072

VMEM_BIG = 56 * 1024 * 1024
VMEM_MID = 40 * 1024 * 1024

NN = ((1,), (0,))
NT = ((1,), (1,))
TN = ((0,), (0,))


def _dot(a, b, dims):
    return lax.dot_general(a.astype(MXU), b.astype(MXU), (dims, ((), ())), preferred_element_type=F32)


def _params(sem, vmem):
    return pltpu.CompilerParams(dimension_semantics=sem, vmem_limit_bytes=vmem)


def _sigmoid(x):
    return 1.0 / (1.0 + jnp.exp(-x))


def _log_sigmoid(z):
    return jnp.minimum(z, 0.0) - jnp.log(1.0 + jnp.exp(-jnp.abs(z)))


def _exchange(src, *, gather, name):
    blk = src.shape if gather else src.shape[1:]

    def body(src_ref, out_ref, send_sems, recv_sems, local_sem):
        x, y, c = lax.axis_index("x"), lax.axis_index("y"), lax.axis_index("c")
        me = 4 * x + 2 * y + c
        mine = pltpu.make_async_copy(src_ref if gather else src_ref.at[me], out_ref.at[me], local_sem)
        mine.start()
        copies = []
        for k in range(1, N_DEV):
            px = 1 - x if k & 4 else x
            py = 1 - y if k & 2 else y
            pc = 1 - c if k & 1 else c
            peer = 4 * px + 2 * py + pc
            copies.append(pltpu.make_async_remote_copy(
                src_ref=src_ref if gather else src_ref.at[peer],
                dst_ref=out_ref.at[me],
                send_sem=send_sems.at[k - 1],
                recv_sem=recv_sems.at[k - 1],
                device_id=(px, py, pc),
                device_id_type=pl.DeviceIdType.MESH,
            ))
        for cp in copies:
            cp.start()
        for cp in copies:
            cp.wait()
        mine.wait()

    hbm = pl.BlockSpec(memory_space=pltpu.HBM)
    return pl.pallas_call(
        body,
        name=name,
        out_shape=jax.ShapeDtypeStruct((N_DEV,) + tuple(blk), src.dtype),
        in_specs=[hbm],
        out_specs=hbm,
        scratch_shapes=[
            pltpu.SemaphoreType.DMA((N_DEV - 1,)),
            pltpu.SemaphoreType.DMA((N_DEV - 1,)),
            pltpu.SemaphoreType.DMA,
        ],
    )(src)


def _mm_nn(a, b, *, tm, tn, name):
    m, k = a.shape
    n = b.shape[1]

    def body(a_ref, b_ref, o_ref):
        o_ref[...] = _dot(a_ref[...], b_ref[...], NN)

    return pl.pallas_call(
        body, name=name, grid=(n // tn, m // tm),
        in_specs=[pl.BlockSpec((tm, k), lambda j, i: (i, 0)), pl.BlockSpec((k, tn), lambda j, i: (0, j))],
        out_specs=pl.BlockSpec((tm, tn), lambda j, i: (i, j)),
        out_shape=jax.ShapeDtypeStruct((m, n), F32),
        compiler_params=_params(("parallel", "parallel"), VMEM_BIG),
    )(a, b)


def _mm_nt(dc, w, add, add_scale, *, tm, tn, name):
    m, n = dc.shape
    k = w.shape[0]
    nj = n // tn

    def body(*refs):
        if add is None:
            dc_ref, w_ref, o_ref = refs
        else:
            dc_ref, w_ref, add_ref, o_ref = refs
        j = pl.program_id(1)

        @pl.when(j == 0)
        def _():
            if add is None:
                o_ref[...] = jnp.zeros_like(o_ref)
            else:
                o_ref[...] = add_scale * add_ref[...]

        o_ref[...] += _dot(dc_ref[...], w_ref[...], NT)

    in_specs = [pl.BlockSpec((tm, tn), lambda i, j: (i, j)), pl.BlockSpec((k, tn), lambda i, j: (0, j))]
    args = [dc, w]
    if add is not None:
        in_specs.append(pl.BlockSpec((tm, k), lambda i, j: (i, 0)))
        args.append(add)
    return pl.pallas_call(
        body, name=name, grid=(m // tm, nj),
        in_specs=in_specs,
        out_specs=pl.BlockSpec((tm, k), lambda i, j: (i, 0)),
        out_shape=jax.ShapeDtypeStruct((m, k), F32),
        compiler_params=_params(("parallel", "arbitrary"), VMEM_BIG),
    )(*args)


def _mm_tn(a, dc, *, tm, tn, name):
    m, k = a.shape
    n = dc.shape[1]

    def body(a_ref, dc_ref, o_ref):
        @pl.when(pl.program_id(1) == 0)
        def _():
            o_ref[...] = jnp.zeros_like(o_ref)

        o_ref[...] += _dot(a_ref[...], dc_ref[...], TN)

    return pl.pallas_call(
        body, name=name, grid=(n // tn, m // tm),
        in_specs=[pl.BlockSpec((tm, k), lambda j, i: (i, 0)), pl.BlockSpec((tm, tn), lambda j, i: (i, j))],
        out_specs=pl.BlockSpec((k, tn), lambda j, i: (0, j)),
        out_shape=jax.ShapeDtypeStruct((k, n), F32),
        compiler_params=_params(("parallel", "arbitrary"), VMEM_BIG),
    )(a, dc)


def _seg_cumsum(v, rowmod):
    sh = 1
    while sh < CHUNK:
        v = v + jnp.where(rowmod >= sh, pltpu.roll(v, sh, 0), 0.0)
        sh *= 2
    return v


def _seg_rcumsum(v, rowmod):
    t = v.shape[0]
    sh = 1
    while sh < CHUNK:
        v = v + jnp.where(rowmod < CHUNK - sh, pltpu.roll(v, t - sh, 0), 0.0)
        sh *= 2
    return v


def _gla_decay(alpha_ref, wup_ref, b_ref, g_scr):
    z = _dot(alpha_ref[...], wup_ref[...], NN) + b_ref[...]
    rowmod = lax.broadcasted_iota(jnp.int32, z.shape, 0) % CHUNK
    g_scr[...] = _seg_cumsum(_log_sigmoid(z) * (1.0 / GATE_TAU), rowmod)
    return z, rowmod


def _chunk_terms(q_ref, k_ref, g_scr, c):
    r0 = c * CHUNK
    g = g_scr[r0:r0 + CHUNK, :]
    g_first = g_scr[r0:r0 + 1, :]
    g_last = g_scr[r0 + CHUNK - 1:r0 + CHUNK, :]
    ref = 0.5 * (g_first + g_last)
    ep = jnp.exp(g - ref)
    em = jnp.exp(ref - g)
    qs = q_ref[r0:r0 + CHUNK, :] * Q_SCALE
    k = k_ref[r0:r0 + CHUNK, :]
    return dict(
        ep=ep, em=em, a=jnp.exp(g), dl=jnp.exp(g_last - g), egl=jnp.exp(g_last),
        qe1=qs * ep, ke1=k * em, qe2=qs * em, ke2=k * ep, qa=qs * jnp.exp(g), kd=k * jnp.exp(g_last - g),
    )


def _scores(t, lower):
    a1 = _dot(t["qe1"], t["ke1"], NT)
    a2 = _dot(t["qe2"], t["ke2"], NT)
    return jnp.where(lower, a1, a2)


def _h_specs(tt, rev_tiles=None):
    def row(i):
        return i if rev_tiles is None else rev_tiles - 1 - i
    return [
        pl.BlockSpec((tt, HDK), lambda h, i: (row(i), h)),
        pl.BlockSpec((tt, HDK), lambda h, i: (row(i), DK // HDK + h)),
        pl.BlockSpec((tt, HDV), lambda h, i: (row(i), (2 * DK) // HDV + h)),
        pl.BlockSpec((tt, HDV), lambda h, i: (row(i), (2 * DK + D) // HDV + h)),
        pl.BlockSpec((tt, 128), lambda h, i: (row(i), H_MAIN // 128)),
    ]


def _gla_fwd(hh, wup, b_alpha, gnorm, *, tt):
    s = hh.shape[0]
    nt = s // tt
    nct = tt // CHUNK

    def body(q_ref, k_ref, v_ref, ga_ref, al_ref, wup_ref, b_ref, gn_ref, o_ref, ya_ref, st_ref, state, g_scr):
        @pl.when(pl.program_id(1) == 0)
        def _():
            state[...] = jnp.zeros_like(state)

        _gla_decay(al_ref, wup_ref, b_ref, g_scr)
        lower = lax.broadcasted_iota(jnp.int32, (CHUNK, CHUNK), 0) >= lax.broadcasted_iota(jnp.int32, (CHUNK, CHUNK), 1)
        for c in range(nct):
            r0 = c * CHUNK
            t = _chunk_terms(q_ref, k_ref, g_scr, c)
            v = v_ref[r0:r0 + CHUNK, :]
            st = state[...]
            st_ref[0, c] = st
            o_ref[r0:r0 + CHUNK, :] = _dot(_scores(t, lower), v, NN) + _dot(t["qa"], st, NT)
            state[...] = st * t["egl"] + _dot(v, t["kd"], TN)
        o = o_ref[...]
        ohat = o * lax.rsqrt(jnp.mean(o * o, axis=-1, keepdims=True) + EPS)
        ga = ga_ref[...]
        ya_ref[...] = ohat * gn_ref[...] * (ga * _sigmoid(ga))

    return pl.pallas_call(
        body, name="gla_fwd", grid=(HEADS, nt),
        in_specs=_h_specs(tt) + [
            pl.BlockSpec((128, HDK), lambda h, i: (0, h)),
            pl.BlockSpec((1, HDK), lambda h, i: (0, h)),
            pl.BlockSpec((1, HDV), lambda h, i: (0, h)),
        ],
        out_specs=[
            pl.BlockSpec((tt, HDV), lambda h, i: (i, h)),
            pl.BlockSpec((tt, HDV), lambda h, i: (i, h)),
            pl.BlockSpec((1, nct, HDV, HDK), lambda h, i: (h, i, 0, 0)),
        ],
        out_shape=[
            jax.ShapeDtypeStruct((s, D), F32),
            jax.ShapeDtypeStruct((s, D), F32),
            jax.ShapeDtypeStruct((HEADS, s // CHUNK, HDV, HDK), F32),
        ],
        scratch_shapes=[pltpu.VMEM((HDV, HDK), F32), pltpu.VMEM((tt, HDK), F32)],
        compiler_params=_params(("arbitrary", "arbitrary"), VMEM_MID),
    )(hh, hh, hh, hh, hh, wup, b_alpha, gnorm)


def _gla_bwd(hh, wup, b_alpha, gnorm, o, states, dya, *, tt):
    s = hh.shape[0]
    nt = s // tt
    nct = tt // CHUNK

    def body(q_ref, k_ref, v_ref, ga_ref, al_ref, wup_ref, b_ref, gn_ref, o_ref, st_ref, dya_ref,
             dq_ref, dk_ref, dv_ref, dga_ref, dz_ref, dgn_ref, db_ref, dstate, g_scr, dg_scr):
        @pl.when(pl.program_id(1) == 0)
        def _():
            dstate[...] = jnp.zeros_like(dstate)
            dgn_ref[...] = jnp.zeros_like(dgn_ref)
            db_ref[...] = jnp.zeros_like(db_ref)

        z, rowmod = _gla_decay(al_ref, wup_ref, b_ref, g_scr)

        o_t = o_ref[...]
        rstd = lax.rsqrt(jnp.mean(o_t * o_t, axis=-1, keepdims=True) + EPS)
        ohat = o_t * rstd
        ga = ga_ref[...]
        sg = _sigmoid(ga)
        dya_t = dya_ref[...]
        gn = gn_ref[...]
        dga_ref[...] = dya_t * ohat * gn * (sg * (1.0 + ga * (1.0 - sg)))
        don = dya_t * (ga * sg)
        dgn_ref[0] += jnp.sum(don * ohat, axis=0, keepdims=True)
        dohat = don * gn
        dga_scr_do = rstd * (dohat - ohat * jnp.mean(dohat * ohat, axis=-1, keepdims=True))
        dv_ref[...] = dga_scr_do

        row = lax.broadcasted_iota(jnp.int32, (CHUNK, CHUNK), 0)
        col = lax.broadcasted_iota(jnp.int32, (CHUNK, CHUNK), 1)
        lower = row >= col
        last_row = lax.broadcasted_iota(jnp.int32, (CHUNK, HDK), 0) == CHUNK - 1
        for c in range(nct - 1, -1, -1):
            r0 = c * CHUNK
            t = _chunk_terms(q_ref, k_ref, g_scr, c)
            v = v_ref[r0:r0 + CHUNK, :]
            do = dv_ref[r0:r0 + CHUNK, :]
            st = st_ref[0, c]
            dst = dstate[...]
            a = _scores(t, lower)
            da = _dot(do, v, NT)
            da1 = jnp.where(lower, da, 0.0)
            da2 = jnp.where(lower, 0.0, da)
            dqe1 = _dot(da1, t["ke1"], NN)
            dke1 = _dot(da1, t["qe1"], TN)
            dqe2 = _dot(da2, t["ke2"], NN)
            dke2 = _dot(da2, t["qe2"], TN)
            dqa = _dot(do, st, NN)
            dkd = _dot(v, dst, NN)
            dv_ref[r0:r0 + CHUNK, :] = _dot(a, do, TN) + _dot(t["kd"], dst, NT)
            dq_ref[r0:r0 + CHUNK, :] = (dqe1 * t["ep"] + dqe2 * t["em"] + dqa * t["a"]) * Q_SCALE
            dk_ref[r0:r0 + CHUNK, :] = dke1 * t["em"] + dke2 * t["ep"] + dkd * t["dl"]
            dkd_kd = dkd * t["kd"]
            dgl = jnp.sum(dkd_kd, axis=0, keepdims=True) + t["egl"] * jnp.sum(dst * st, axis=0, keepdims=True)
            dg = dqe1 * t["qe1"] - dke1 * t["ke1"] - dqe2 * t["qe2"] + dke2 * t["ke2"] + dqa * t["qa"] - dkd_kd
            dg_scr[r0:r0 + CHUNK, :] = dg + jnp.where(last_row, dgl, 0.0)
            dstate[...] = dst * t["egl"] + _dot(do, t["qa"], TN)

        dla = _seg_rcumsum(dg_scr[...], rowmod)
        dz = dla * _sigmoid(-z) * (1.0 / GATE_TAU)
        dz_ref[...] = dz
        db_ref[0] += jnp.sum(dz, axis=0, keepdims=True)

    rev = lambda h, i: nt - 1 - i
    return pl.pallas_call(
        body, name="gla_bwd", grid=(HEADS, nt),
        in_specs=_h_specs(tt, rev_tiles=nt) + [
            pl.BlockSpec((128, HDK), lambda h, i: (0, h)),
            pl.BlockSpec((1, HDK), lambda h, i: (0, h)),
            pl.BlockSpec((1, HDV), lambda h, i: (0, h)),
            pl.BlockSpec((tt, HDV), lambda h, i: (rev(h, i), h)),
            pl.BlockSpec((1, nct, HDV, HDK), lambda h, i: (h, rev(h, i), 0, 0)),
            pl.BlockSpec((tt, HDV), lambda h, i: (rev(h, i), h)),
        ],
        out_specs=[
            pl.BlockSpec((tt, HDK), lambda h, i: (rev(h, i), h)),
            pl.BlockSpec((tt, HDK), lambda h, i: (rev(h, i), h)),
            pl.BlockSpec((tt, HDV), lambda h, i: (rev(h, i), h)),
            pl.BlockSpec((tt, HDV), lambda h, i: (rev(h, i), h)),
            pl.BlockSpec((tt, HDK), lambda h, i: (rev(h, i), h)),
            pl.BlockSpec((1, 1, HDV), lambda h, i: (h, 0, 0)),
            pl.BlockSpec((1, 1, HDK), lambda h, i: (h, 0, 0)),
        ],
        out_shape=[
            jax.ShapeDtypeStruct((s, DK), F32),
            jax.ShapeDtypeStruct((s, DK), F32),
            jax.ShapeDtypeStruct((s, D), F32),
            jax.ShapeDtypeStruct((s, D), F32),
            jax.ShapeDtypeStruct((s, DK), F32),
            jax.ShapeDtypeStruct((HEADS, 1, HDV), F32),
            jax.ShapeDtypeStruct((HEADS, 1, HDK), F32),
        ],
        scratch_shapes=[pltpu.VMEM((HDV, HDK), F32), pltpu.VMEM((tt, HDK), F32), pltpu.VMEM((tt, HDK), F32)],
        compiler_params=_params(("arbitrary", "arbitrary"), VMEM_MID),
    )(hh, hh, hh, hh, hh, wup, b_alpha, gnorm, o, states, dya)


def _window_count(tile, tt, w):
    pos = tile * tt + lax.broadcasted_iota(jnp.int32, (tt, PG), 0) + 1
    return jnp.minimum(pos, w).astype(F32)


def _pool_fwd(hh, wpool, scale, *, tt):
    s = hh.shape[0]
    nt = s // tt

    def body(u_ref, gb_ref, w_ref, sc_ref, pooled_ref, mixed_ref, yb_ref, halo):
        i = pl.program_id(0)

        @pl.when(i == 0)
        def _():
            halo[...] = jnp.zeros_like(halo)

        for g, w in enumerate(POOL_WINDOWS):
            cols = slice(g * PG, (g + 1) * PG)
            u = u_ref[:, cols]
            run = jnp.concatenate([halo[:, cols], u], axis=0)
            sh = 1
            while sh < w:
                run = run + pltpu.roll(run, sh, 0)
                sh *= 2
            pooled = run[HALO:, :] / _window_count(i, tt, w) - u
            pooled_ref[:, cols] = pooled
            mixed = _dot(pooled, w_ref[g], NN)
            mixed_ref[:, cols] = mixed
            gb = gb_ref[:, cols]
            yb_ref[:, cols] = mixed * sc_ref[:, cols] * (gb * _sigmoid(gb))
        halo[...] = u_ref[tt - HALO:tt, :]

    return pl.pallas_call(
        body, name="pool_fwd", grid=(nt,),
        in_specs=[
            pl.BlockSpec((tt, D), lambda i: (i, (2 * DK + 2 * D) // D)),
            pl.BlockSpec((tt, D), lambda i: (i, (2 * DK + 3 * D) // D)),
            pl.BlockSpec((len(POOL_WINDOWS), PG, PG), lambda i: (0, 0, 0)),
            pl.BlockSpec((1, D), lambda i: (0, 0)),
        ],
        out_specs=[pl.BlockSpec((tt, D), lambda i: (i, 0))] * 3,
        out_shape=[jax.ShapeDtypeStruct((s, D), F32)] * 3,
        scratch_shapes=[pltpu.VMEM((HALO, D), F32)],
        compiler_params=_params(("arbitrary",), VMEM_MID),
    )(hh, hh, wpool, scale)


def _pool_bwd(hh, wpool, scale, pooled, mixed, dyb, *, tt):
    s = hh.shape[0]
    nt = s // tt

    def body(gb_ref, w_ref, sc_ref, pooled_ref, mixed_ref, dyb_ref, du_ref, dgb_ref, dw_ref, dsc_ref, halo):
        i = pl.program_id(0)
        tile = nt - 1 - i

        @pl.when(i == 0)
        def _():
            halo[...] = jnp.zeros_like(halo)
            dw_ref[...] = jnp.zeros_like(dw_ref)
            dsc_ref[...] = jnp.zeros_like(dsc_ref)

        for g, w in enumerate(POOL_WINDOWS):
            cols = slice(g * PG, (g + 1) * PG)
            gb = gb_ref[:, cols]
            sg = _sigmoid(gb)
            mixed = mixed_ref[:, cols]
            sc = sc_ref[:, cols]
            dyb = dyb_ref[:, cols]
            dgb_ref[:, cols] = dyb * mixed * sc * (sg * (1.0 + gb * (1.0 - sg)))
            dms = dyb * (gb * sg)
            dsc_ref[:, cols] += jnp.sum(dms * mixed, axis=0, keepdims=True)
            dmixed = dms * sc
            dpooled = _dot(dmixed, w_ref[g], NT)
            dw_ref[g] += _dot(pooled_ref[:, cols], dmixed, TN)
            e = dpooled / _window_count(tile, tt, w)
            run = jnp.concatenate([e, halo[:, cols]], axis=0)
            sh = 1
            while sh < w:
                run = run + pltpu.roll(run, tt + HALO - sh, 0)
                sh *= 2
            du_ref[:, cols] = run[:tt, :] - dpooled
            halo[:, cols] = e[:HALO, :]

    rev = lambda i: nt - 1 - i
    return pl.pallas_call(
        body, name="pool_bwd", grid=(nt,),
        in_specs=[
            pl.BlockSpec((tt, D), lambda i: (rev(i), (2 * DK + 3 * D) // D)),
            pl.BlockSpec((len(POOL_WINDOWS), PG, PG), lambda i: (0, 0, 0)),
            pl.BlockSpec((1, D), lambda i: (0, 0)),
            pl.BlockSpec((tt, D), lambda i: (rev(i), 0)),
            pl.BlockSpec((tt, D), lambda i: (rev(i), 0)),
            pl.BlockSpec((tt, D), lambda i: (rev(i), 0)),
        ],
        out_specs=[
            pl.BlockSpec((tt, D), lambda i: (rev(i), 0)),
            pl.BlockSpec((tt, D), lambda i: (rev(i), 0)),
            pl.BlockSpec((len(POOL_WINDOWS), PG, PG), lambda i: (0, 0, 0)),
            pl.BlockSpec((1, D), lambda i: (0, 0)),
        ],
        out_shape=[
            jax.ShapeDtypeStruct((s, D), F32),
            jax.ShapeDtypeStruct((s, D), F32),
            jax.ShapeDtypeStruct((len(POOL_WINDOWS), PG, PG), F32),
            jax.ShapeDtypeStruct((1, D), F32),
        ],
        scratch_shapes=[pltpu.VMEM((HALO, D), F32)],
        compiler_params=_params(("arbitrary",), VMEM_MID),
    )(hh, wpool, scale, pooled, mixed, dyb)


ML1 = (2 * DK + 4 * D) // D
ML2 = ML1 + 1


def _merge_fwd(hh, x, ya, yb, wpa, wpb, wout, b_merge, ln_g, ln_b, *, tt):
    s = x.shape[0]

    def body(ml1_ref, ml2_ref, x_ref, ya_ref, yb_ref, wpa_ref, wpb_ref, wout_ref, bm_ref, g_ref, b_ref,
             pa_ref, pb_ref, r_ref, xn_ref):
        pa = _dot(ya_ref[...], wpa_ref[...], NN)
        pb = _dot(yb_ref[...], wpb_ref[...], NN)
        pa_ref[...] = pa
        pb_ref[...] = pb
        merged = _sigmoid(ml1_ref[...] + bm_ref[:, :D]) * pa + _sigmoid(ml2_ref[...] + bm_ref[:, D:]) * pb
        r = ALPHA * x_ref[...] + _dot(merged, wout_ref[...], NN)
        r_ref[...] = r
        mu = jnp.mean(r, axis=-1, keepdims=True)
        xc = r - mu
        var = jnp.mean(xc * xc, axis=-1, keepdims=True)
        xn_ref[...] = xc * lax.rsqrt(var + EPS) * g_ref[...] + b_ref[...]

    tile = pl.BlockSpec((tt, D), lambda i: (i, 0))
    full = pl.BlockSpec((D, D), lambda i: (0, 0))
    vec = pl.BlockSpec((1, D), lambda i: (0, 0))
    return pl.pallas_call(
        body, name="merge_fwd", grid=(s // tt,),
        in_specs=[
            pl.BlockSpec((tt, D), lambda i: (i, ML1)), pl.BlockSpec((tt, D), lambda i: (i, ML2)),
            tile, tile, tile, full, full, full, pl.BlockSpec((1, 2 * D), lambda i: (0, 0)), vec, vec,
        ],
        out_specs=[tile] * 4,
        out_shape=[jax.ShapeDtypeStruct((s, D), F32)] * 4,
        compiler_params=_params(("parallel",), VMEM_BIG),
    )(hh, hh, x, ya, yb, wpa, wpb, wout, b_merge, ln_g, ln_b)


def _merge_bwd(hh, r, pa, pb, dout, wout, b_merge, ln_g, *, tt):
    s = r.shape[0]

    def body(ml1_ref, ml2_ref, r_ref, pa_ref, pb_ref, do_ref, wout_ref, bm_ref, g_ref,
             dr_ref, dml_ref, dpa_ref, dpb_ref, dwout_ref, dg_ref, db_ref, dbm_ref):
        @pl.when(pl.program_id(0) == 0)
        def _():
            dwout_ref[...] = jnp.zeros_like(dwout_ref)
            dg_ref[...] = jnp.zeros_like(dg_ref)
            db_ref[...] = jnp.zeros_like(db_ref)
            dbm_ref[...] = jnp.zeros_like(dbm_ref)

        rr = r_ref[...]
        mu = jnp.mean(rr, axis=-1, keepdims=True)
        xc = rr - mu
        rstd = lax.rsqrt(jnp.mean(xc * xc, axis=-1, keepdims=True) + EPS)
        xhat = xc * rstd
        do = do_ref[...]
        dg_ref[...] += jnp.sum(do * xhat, axis=0, keepdims=True)
        db_ref[...] += jnp.sum(do, axis=0, keepdims=True)
        dxh = do * g_ref[...]
        dr = rstd * (dxh - jnp.mean(dxh, axis=-1, keepdims=True) - xhat * jnp.mean(dxh * xhat, axis=-1, keepdims=True))
        dr_ref[...] = dr
        g_a = _sigmoid(ml1_ref[...] + bm_ref[:, :D])
        g_b = _sigmoid(ml2_ref[...] + bm_ref[:, D:])
        pa = pa_ref[...]
        pb = pb_ref[...]
        dwout_ref[...] += _dot(g_a * pa + g_b * pb, dr, TN)
        dm = _dot(dr, wout_ref[...], NT)
        dpa_ref[...] = dm * g_a
        dpb_ref[...] = dm * g_b
        dml_a = dm * pa * g_a * (1.0 - g_a)
        dml_b = dm * pb * g_b * (1.0 - g_b)
        dml_ref[:, :D] = dml_a
        dml_ref[:, D:] = dml_b
        dbm_ref[:, :D] += jnp.sum(dml_a, axis=0, keepdims=True)
        dbm_ref[:, D:] += jnp.sum(dml_b, axis=0, keepdims=True)

    tile = pl.BlockSpec((tt, D), lambda i: (i, 0))
    full = pl.BlockSpec((D, D), lambda i: (0, 0))
    vec = pl.BlockSpec((1, D), lambda i: (0, 0))
    vec2 = pl.BlockSpec((1, 2 * D), lambda i: (0, 0))
    return pl.pallas_call(
        body, name="merge_bwd", grid=(s // tt,),
        in_specs=[
            pl.BlockSpec((tt, D), lambda i: (i, ML1)), pl.BlockSpec((tt, D), lambda i: (i, ML2)),
            tile, tile, tile, tile, full, vec2, vec,
        ],
        out_specs=[tile, pl.BlockSpec((tt, 2 * D), lambda i: (i, 0)), tile, tile, full, vec, vec, vec2],
        out_shape=[
            jax.ShapeDtypeStruct((s, D), F32), jax.ShapeDtypeStruct((s, 2 * D), F32),
            jax.ShapeDtypeStruct((s, D), F32), jax.ShapeDtypeStruct((s, D), F32),
            jax.ShapeDtypeStruct((D, D), F32), jax.ShapeDtypeStruct((1, D), F32),
            jax.ShapeDtypeStruct((1, D), F32), jax.ShapeDtypeStruct((1, 2 * D), F32),
        ],
        compiler_params=_params(("arbitrary",), VMEM_BIG),
    )(hh, hh, r, pa, pb, dout, wout, b_merge, ln_g)


def _proj_bwd(y, dp, w, *, tt, name):
    s = y.shape[0]

    def body(y_ref, dp_ref, w_ref, dy_ref, dw_ref):
        @pl.when(pl.program_id(0) == 0)
        def _():
            dw_ref[...] = jnp.zeros_like(dw_ref)

        dp = dp_ref[...]
        dy_ref[...] = _dot(dp, w_ref[...], NT)
        dw_ref[...] += _dot(y_ref[...], dp, TN)

    tile = pl.BlockSpec((tt, D), lambda i: (i, 0))
    full = pl.BlockSpec((D, D), lambda i: (0, 0))
    return pl.pallas_call(
        body, name=name, grid=(s // tt,),
        in_specs=[tile, tile, full],
        out_specs=[tile, full],
        out_shape=[jax.ShapeDtypeStruct((s, D), F32), jax.ShapeDtypeStruct((D, D), F32)],
        compiler_params=_params(("arbitrary",), VMEM_MID),
    )(y, dp, w)


def _loss_head(y, target, *, tt):
    s = y.shape[0]

    def body(y_ref, t_ref, loss_ref, dy_ref):
        @pl.when(pl.program_id(0) == 0)
        def _():
            loss_ref[...] = jnp.zeros_like(loss_ref)

        err = y_ref[...] - t_ref[...]
        dy_ref[...] = err * (1.0 / D)
        per_tok = jnp.mean(err * err, axis=-1, keepdims=True)
        loss_ref[...] += 0.5 * jnp.sum(per_tok, axis=0, keepdims=True)

    tile = pl.BlockSpec((tt, D), lambda i: (i, 0))
    return pl.pallas_call(
        body, name="loss_head", grid=(s // tt,),
        in_specs=[tile, tile],
        out_specs=[pl.BlockSpec((1, 1), lambda i: (0, 0)), tile],
        out_shape=[jax.ShapeDtypeStruct((1, 1), F32), jax.ShapeDtypeStruct((s, D), F32)],
        compiler_params=_params(("arbitrary",), VMEM_MID),
    )(y, target)


def _adamw(parts, w, m, v, *, tr, name):
    rows, cols = w.shape

    def body(p_ref, w_ref, m_ref, v_ref, g_ref, d_ref, nm_ref, nv_ref):
        g = p_ref[0]
        for q in range(1, N_DEV):
            g = g + p_ref[q]
        g_ref[...] = g
        nm = ADAM_B1 * m_ref[...] + (1.0 - ADAM_B1) * g
        nv = ADAM_B2 * v_ref[...] + (1.0 - ADAM_B2) * (g * g)
        nm_ref[...] = nm
        nv_ref[...] = nv
        m_hat = nm / (1.0 - ADAM_B1 ** ADAM_STEP)
        v_hat = nv / (1.0 - ADAM_B2 ** ADAM_STEP)
        d_ref[...] = -ADAM_LR * (m_hat / (jnp.sqrt(v_hat) + ADAM_EPS) + ADAM_WD * w_ref[...])

    tile = pl.BlockSpec((tr, cols), lambda i: (i, 0))
    return pl.pallas_call(
        body, name=name, grid=(rows // tr,),
        in_specs=[pl.BlockSpec((N_DEV, tr, cols), lambda i: (0, i, 0)), tile, tile, tile],
        out_specs=[tile] * 4,
        out_shape=[jax.ShapeDtypeStruct((rows, cols), F32)] * 4,
        compiler_params=_params(("parallel",), VMEM_MID),
    )(parts, w, m, v)


def _from_devices(g, axis):
    nd = g.ndim - 1
    perm = list(range(1, axis + 1)) + [0] + list(range(axis + 1, nd + 1))
    t = jnp.transpose(g, perm)
    shape = list(g.shape[1:])
    shape[axis] *= N_DEV
    return t.reshape(shape)


def _to_devices(a, axis):
    shape = list(a.shape)
    blk = shape[axis] // N_DEV
    t = a.reshape(shape[:axis] + [N_DEV, blk] + shape[axis + 1:])
    perm = [axis] + list(range(0, axis)) + list(range(axis + 1, t.ndim))
    return jnp.transpose(t, perm)


def kernel(x, w_in, w_alpha_up, b_alpha, gla_norm_g, w_pool_grp, pool_scale, b_merge, w_proj_a, w_proj_b, w_out, ln_g, ln_b, loss_target, m_w_in, m_w_alpha_up, m_b_alpha, m_gla_norm_g, m_w_pool_grp, m_pool_scale, m_b_merge, m_w_proj_a, m_w_proj_b, m_w_out, m_ln_g, m_ln_b, v_w_in, v_w_alpha_up, v_b_alpha, v_gla_norm_g, v_w_pool_grp, v_pool_scale, v_b_merge, v_w_proj_a, v_w_proj_b, v_w_out, v_ln_g, v_ln_b):
    s = x.shape[1]
    tt = min(256, s)
    tm = min(512, s)
    tn = HP // 3
    xs = x[0]

    w_in_all = _from_devices(_exchange(w_in.astype(MXU), gather=True, name="gather_w_in"), 2)
    proj_all = _from_devices(_exchange(jnp.stack([w_proj_a, w_proj_b, w_out]).astype(MXU), gather=True,
                                       name="gather_w_proj"), 2)
    pool_all = _from_devices(_exchange(w_pool_grp.astype(MXU), gather=True, name="gather_w_pool"), 2)
    up_all = _from_devices(_exchange(w_alpha_up.astype(MXU), gather=True, name="gather_w_up"), 2)
    gn_all = _from_devices(_exchange(gla_norm_g, gather=True, name="gather_gnorm"), 2)

    wh = jnp.concatenate([w_in_all[:, :, :AL_ORIG], w_in_all[:, :, AL_ORIG + RANK:], w_in_all[:, :, AL_ORIG:AL_ORIG + RANK],
                          jnp.zeros((DEPTH, D, HP - IN_COLS), MXU)], axis=2)
    wup = jnp.pad(up_all, ((0, 0), (0, 128 - RANK), (0, 0)))
    gn = gn_all.reshape(DEPTH, 1, D)

    saved = []
    cur = xs
    for l in range(DEPTH):
        hh = _mm_nn(cur, wh[l], tm=tm, tn=tn, name="in_proj")
        o, ya, states = _gla_fwd(hh, wup[l], b_alpha[l:l + 1], gn[l], tt=tt)
        pooled, mixed, yb = _pool_fwd(hh, pool_all[l], pool_scale[l:l + 1], tt=tt)
        pa, pb, r, nxt = _merge_fwd(hh, cur, ya, yb, proj_all[0, l], proj_all[1, l], proj_all[2, l],
                                    b_merge[l:l + 1], ln_g[l:l + 1], ln_b[l:l + 1], tt=tt)
        saved.append(dict(x=cur, hh=hh, o=o, ya=ya, states=states, pooled=pooled, mixed=mixed, yb=yb, pa=pa, pb=pb, r=r))
        cur = nxt

    loss_part, dcur = _loss_head(cur, loss_target[0], tt=tt)
    loss = lax.psum(loss_part[0, 0], ("x", "y", "c"))

    grads = {k: [None] * DEPTH for k in ("w_in", "w_up", "b_alpha", "gnorm", "w_pool", "pool_scale", "b_merge",
                                          "w_pa", "w_pb", "w_out", "ln_g", "ln_b")}
    for l in range(DEPTH - 1, -1, -1):
        sv = saved[l]
        hh = sv["hh"]
        dr, dml, dpa, dpb, dw_out, dln_g, dln_b, db_merge = _merge_bwd(
            hh, sv["r"], sv["pa"], sv["pb"], dcur, proj_all[2, l], b_merge[l:l + 1], ln_g[l:l + 1], tt=tt)
        dya, dw_pa = _proj_bwd(sv["ya"], dpa, proj_all[0, l], tt=tt, name="proj_a_bwd")
        dyb, dw_pb = _proj_bwd(sv["yb"], dpb, proj_all[1, l], tt=tt, name="proj_b_bwd")
        dpool_in, dgb, dw_pool, dscale = _pool_bwd(hh, pool_all[l], pool_scale[l:l + 1], sv["pooled"], sv["mixed"], dyb, tt=tt)
        dq, dk, dv, dga, dz, dgn, db_al = _gla_bwd(hh, wup[l], b_alpha[l:l + 1], gn[l], sv["o"], sv["states"], dya, tt=tt)
        dal = _mm_nt(dz, wup[l], None, 1.0, tm=tm, tn=DK, name="alpha_bwd")
        dw_up = _mm_tn(hh[:, H_MAIN:], dz, tm=tm, tn=DK, name="w_up_grad")
        dh = jnp.concatenate([dq, dk, dv, dga, dpool_in, dgb, dml, dal], axis=1)
        dw_h = _mm_tn(sv["x"], dh, tm=tm, tn=tn, name="w_in_grad")
        dcur = _mm_nt(dh, wh[l], dr, ALPHA, tm=tm, tn=tn, name="in_proj_bwd")

        grads["w_in"][l] = jnp.concatenate([dw_h[:, :AL_ORIG], dw_h[:, H_MAIN:H_MAIN + RANK], dw_h[:, AL_ORIG:H_MAIN]], axis=1)
        grads["w_up"][l] = dw_up[:RANK]
        grads["b_alpha"][l] = db_al.reshape(DK)
        grads["gnorm"][l] = dgn.reshape(HEADS, HDV)
        grads["w_pool"][l] = dw_pool
        grads["pool_scale"][l] = dscale[0]
        grads["b_merge"][l] = db_merge[0]
        grads["w_pa"][l], grads["w_pb"][l], grads["w_out"][l] = dw_pa, dw_pb, dw_out
        grads["ln_g"][l], grads["ln_b"][l] = dln_g[0], dln_b[0]
    grad_x = dcur[None]
    gs = {k: jnp.stack(v) for k, v in grads.items()}

    def update(parts, w, m, v, tr, name):
        shape = w.shape
        cols = shape[-1]
        flat = lambda a: a.reshape(-1, cols)
        outs = _adamw(parts.reshape(N_DEV, -1, cols), flat(w), flat(m), flat(v), tr=tr, name=name)
        return [o_.reshape(shape) for o_ in outs]

    res = {}
    res["w_in"] = update(_exchange(_to_devices(gs["w_in"], 2), gather=False, name="scatter_w_in"),
                         w_in, m_w_in, v_w_in, 128, "adamw_w_in")
    proj_parts = _exchange(_to_devices(jnp.stack([gs["w_pa"], gs["w_pb"], gs["w_out"]]), 2), gather=False, name="scatter_w_proj")
    for j, (nm, w, m, v) in enumerate((("w_proj_a", w_proj_a, m_w_proj_a, v_w_proj_a), ("w_proj_b", w_proj_b, m_w_proj_b, v_w_proj_b),
                                       ("w_out", w_out, m_w_out, v_w_out))):
        res[nm] = update(proj_parts[:, j], w, m, v, 128, "adamw_" + nm)
    res["w_pool_grp"] = update(_exchange(_to_devices(gs["w_pool"], 2), gather=False, name="scatter_w_pool"),
                               w_pool_grp, m_w_pool_grp, v_w_pool_grp, 128, "adamw_w_pool")
    res["w_alpha_up"] = update(_exchange(_to_devices(gs["w_up"], 2), gather=False, name="scatter_w_up"),
                               w_alpha_up, m_w_alpha_up, v_w_alpha_up, DEPTH * RANK, "adamw_w_up")
    res["gla_norm_g"] = update(_exchange(_to_devices(gs["gnorm"], 2), gather=False, name="scatter_gnorm"),
                               gla_norm_g, m_gla_norm_g, v_gla_norm_g, DEPTH * HEADS, "adamw_gnorm")

    rep = (("b_alpha", b_alpha, m_b_alpha, v_b_alpha), ("pool_scale", pool_scale, m_pool_scale, v_pool_scale),
           ("b_merge", b_merge, m_b_merge, v_b_merge), ("ln_g", ln_g, m_ln_g, v_ln_g), ("ln_b", ln_b, m_ln_b, v_ln_b))
    cat = lambda arrs: jnp.concatenate(arrs, axis=1)
    rep_parts = _exchange(cat([gs[nm] for nm, _, _, _ in rep]), gather=True, name="gather_small_grads")
    rep_out = _adamw(rep_parts, cat([w for _, w, _, _ in rep]), cat([m for _, _, m, _ in rep]), cat([v for _, _, _, v in rep]),
                     tr=DEPTH, name="adamw_small")
    off = 0
    for nm, w, _, _ in rep:
        n = w.shape[1]
        res[nm] = [o_[:, off:off + n] for o_ in rep_out]
        off += n

    order = ("w_in", "w_alpha_up", "b_alpha", "gla_norm_g", "w_pool_grp", "pool_scale", "b_merge", "w_proj_a", "w_proj_b",
             "w_out", "ln_g", "ln_b")
    return (loss, grad_x, *[res[n][0] for n in order], *[res[n][1] for n in order],
            *[res[n][2] for n in order], *[res[n][3] for n in order])
```

```python
import jax
import jax.numpy as jnp
from jax import lax
from jax.experimental import pallas as pl
from jax.experimental.pallas import tpu as pltpu

F32 = jnp.float32
MXU = jnp.bfloat16
WIRE = jnp.bfloat16

N_DEV = 8
DEPTH = 4
D = 1024
HEADS = 4
DK = D // 2
HDK = DK // HEADS
HDV = D // HEADS
RANK = 16
CHUNK = 64
GATE_TAU = 16.0
POOL_WINDOWS = (2, 4, 8, 16)
PG = D // len(POOL_WINDOWS)
HALO = 16
IN_COLS = 7184
ALPHA = (2.0 * DEPTH) ** 0.25
EPS = 1e-5
Q_SCALE = HDK ** -0.5

ADAM_LR, ADAM_B1, ADAM_B2, ADAM_EPS, ADAM_WD, ADAM_STEP = 0.001, 0.9, 0.999, 1e-08, 0.01, 10

PI0, GB0, ML0, AL0, AL_W = 0, D, 2 * D, 4 * D, 512
HD0 = AL0 + AL_W
HEAD_W = 2 * HDK + 2 * HDV
HP = HD0 + HEADS * HEAD_W
O_Q, O_K, O_V, O_GA, O_AL, O_PI, O_GB, O_ML = 0, DK, 2 * DK, 2 * DK + D, 2 * DK + 2 * D, 2 * DK + 2 * D + RANK, \
    2 * DK + 3 * D + RANK, 2 * DK + 4 * D + RANK

VMEM_BIG = 56 * 1024 * 1024
VMEM_MID = 40 * 1024 * 1024

NN = ((1,), (0,))
NT = ((1,), (1,))
TN = ((0,), (0,))

HBM = pl.BlockSpec(memory_space=pltpu.HBM)
ANY = pl.BlockSpec(memory_space=pl.ANY)


def _dot(a, b, dims):
    return lax.dot_general(a.astype(MXU), b.astype(MXU), (dims, ((), ())), preferred_element_type=F32)


def _params(sem, vmem):
    return pltpu.CompilerParams(dimension_semantics=sem, vmem_limit_bytes=vmem)


def _sigmoid(x):
    return 1.0 / (1.0 + jnp.exp(-x))


def _log_sigmoid(z):
    return jnp.minimum(z, 0.0) - jnp.log(1.0 + jnp.exp(-jnp.abs(z)))


class _Exchange:
    def __init__(self, items):
        self.items = [(s, bool(g)) for s, g in items]
        self.n = len(self.items)
        self.srcs = [s for s, _ in self.items]
        self.in_specs = [HBM] * self.n
        self.out_specs = [HBM] * self.n
        self.out_shape = [jax.ShapeDtypeStruct((N_DEV,) + tuple(s.shape if g else s.shape[1:]), s.dtype) for s, g in self.items]
        self.scratch = [pltpu.SemaphoreType.DMA((self.n * (N_DEV - 1),)), pltpu.SemaphoreType.DMA((self.n * (N_DEV - 1),)),
                        pltpu.SemaphoreType.DMA((self.n,))]

    def copies(self, src_refs, out_refs, send_sems, recv_sems, local_sems):
        x, y, c = lax.axis_index("x"), lax.axis_index("y"), lax.axis_index("c")
        me = 4 * x + 2 * y + c
        copies = []
        for t, (_, gather) in enumerate(self.items):
            src_ref, out_ref = src_refs[t], out_refs[t]
            copies.append(pltpu.make_async_copy(src_ref if gather else src_ref.at[me], out_ref.at[me], local_sems.at[t]))
            for k in range(1, N_DEV):
                px = 1 - x if k & 4 else x
                py = 1 - y if k & 2 else y
                pc = 1 - c if k & 1 else c
                peer = 4 * px + 2 * py + pc
                sem = t * (N_DEV - 1) + k - 1
                copies.append(pltpu.make_async_remote_copy(
                    src_ref=src_ref if gather else src_ref.at[peer],
                    dst_ref=out_ref.at[me],
                    send_sem=send_sems.at[sem],
                    recv_sem=recv_sems.at[sem],
                    device_id=(px, py, pc),
                    device_id_type=pl.DeviceIdType.MESH,
                ))
        return copies


def _grid_ends(grid):
    first = last = None
    for a, n in enumerate(grid):
        f = pl.program_id(a) == 0
        e = pl.program_id(a) == n - 1
        first = f if first is None else first & f
        last = e if last is None else last & e
    return first, last


def _call(body, *, name, grid, in_specs, out_specs, out_shape, args, scratch=(), sem=None, vmem=VMEM_MID, ride=None, aliases=None):
    n_in, n_out, n_scr = len(in_specs), len(out_specs), len(scratch)
    sem = sem or ("arbitrary",) * len(grid)
    if ride is None:
        return pl.pallas_call(body, name=name, grid=grid, in_specs=in_specs, out_specs=out_specs, out_shape=out_shape,
                              scratch_shapes=list(scratch), compiler_params=_params(sem, vmem),
                              input_output_aliases=aliases or {})(*args)
    r = ride.n

    def riding(*refs):
        ins, rsrc = refs[:n_in], refs[n_in:n_in + r]
        outs, rout = refs[n_in + r:n_in + r + n_out], refs[n_in + r + n_out:n_in + 2 * r + n_out]
        scr = refs[n_in + 2 * r + n_out:n_in + 2 * r + n_out + n_scr]
        send_sems, recv_sems, local_sems = refs[n_in + 2 * r + n_out + n_scr:]
        first, last = _grid_ends(grid)
        copies = ride.copies(rsrc, rout, send_sems, recv_sems, local_sems)

        @pl.when(first)
        def _():
            for cp in copies:
                cp.start()

        body(*ins, *outs, *scr)

        @pl.when(last)
        def _():
            for cp in copies:
                cp.wait()

    return pl.pallas_call(riding, name=name, grid=grid, in_specs=list(in_specs) + ride.in_specs,
                          out_specs=list(out_specs) + ride.out_specs, out_shape=list(out_shape) + ride.out_shape,
                          scratch_shapes=list(scratch) + ride.scratch,
                          compiler_params=_params(("arbitrary",) * len(grid), vmem),
                          input_output_aliases=aliases or {})(*args, *ride.srcs)


def _exchange(items, *, name):
    ex = _Exchange(items)

    def body(*refs):
        copies = ex.copies(refs[:ex.n], refs[ex.n:2 * ex.n], *refs[2 * ex.n:])
        for cp in copies:
            cp.start()
        for cp in copies:
            cp.wait()

    return pl.pallas_call(body, name=name, in_specs=ex.in_specs, out_specs=ex.out_specs, out_shape=ex.out_shape,
                          scratch_shapes=ex.scratch)(*ex.srcs)


def _mm_nn(a, b, *, tm, tn, name, ride=None):
    m, k = a.shape
    n = b.shape[1]

    def body(a_ref, b_ref, o_ref):
        o_ref[...] = _dot(a_ref[...], b_ref[...], NN)

    return _call(
        body, name=name, grid=(n // tn, m // tm),
        in_specs=[pl.BlockSpec((tm, k), lambda j, i: (i, 0)), pl.BlockSpec((k, tn), lambda j, i: (0, j))],
        out_specs=[pl.BlockSpec((tm, tn), lambda j, i: (i, j))],
        out_shape=[jax.ShapeDtypeStruct((m, n), F32)],
        args=(a, b), sem=("parallel", "parallel"), vmem=VMEM_BIG, ride=ride)


def _mm_nt(dc, w, *, tm, tn, name, add=None, add_scale=1.0, into=None, ride=None):
    m, n = dc.shape
    k = w.shape[0]

    def body(*refs):
        dc_ref, w_ref = refs[0], refs[1]
        add_ref = refs[2] if add is not None else None
        o_ref = refs[-1]

        @pl.when(pl.program_id(1) == 0)
        def _():
            if add is None:
                o_ref[...] = jnp.zeros_like(o_ref)
            else:
                o_ref[...] = add_scale * add_ref[...]

        o_ref[...] += _dot(dc_ref[...], w_ref[...], NT)

    in_specs = [pl.BlockSpec((tm, tn), lambda i, j: (i, j)), pl.BlockSpec((k, tn), lambda i, j: (0, j))]
    args = [dc, w]
    if add is not None:
        in_specs.append(pl.BlockSpec((tm, k), lambda i, j: (i, 0)))
        args.append(add)
    aliases = None
    col = 0
    out_shape = jax.ShapeDtypeStruct((m, k), F32)
    if into is not None:
        arr, col = into
        in_specs.append(ANY)
        args.append(arr)
        aliases = {len(args) - 1: 0}
        out_shape = jax.ShapeDtypeStruct(arr.shape, arr.dtype)
    return _call(
        body, name=name, grid=(m // tm, n // tn), in_specs=in_specs,
        out_specs=[pl.BlockSpec((tm, k), lambda i, j: (i, col))], out_shape=[out_shape],
        args=args, sem=("parallel", "arbitrary"), vmem=VMEM_BIG, ride=ride, aliases=aliases)


def _mm_tn(a, dc, *, tm, tn, name, a_block=None):
    m = a.shape[0]
    k, a_col = (a.shape[1], 0) if a_block is None else a_block
    n = dc.shape[1]

    def body(a_ref, dc_ref, o_ref):
        @pl.when(pl.program_id(1) == 0)
        def _():
            o_ref[...] = jnp.zeros_like(o_ref)

        o_ref[...] += _dot(a_ref[...], dc_ref[...], TN)

    return _call(
        body, name=name, grid=(n // tn, m // tm),
        in_specs=[pl.BlockSpec((tm, k), lambda j, i: (i, a_col)), pl.BlockSpec((tm, tn), lambda j, i: (i, j))],
        out_specs=[pl.BlockSpec((k, tn), lambda j, i: (0, j))],
        out_shape=[jax.ShapeDtypeStruct((k, n), F32)],
        args=(a, dc), sem=("parallel", "arbitrary"), vmem=VMEM_BIG)[0]


QC, KC, VC, GC = slice(0, HDK), slice(HDK, 2 * HDK), slice(2 * HDK, 2 * HDK + HDV), slice(2 * HDK + HDV, HEAD_W)


def _seg_cumsum(v, rowmod):
    sh = 1
    while sh < CHUNK:
        v = v + jnp.where(rowmod >= sh, pltpu.roll(v, sh, 0), 0.0)
        sh *= 2
    return v


def _seg_rcumsum(v, rowmod):
    t = v.shape[0]
    sh = 1
    while sh < CHUNK:
        v = v + jnp.where(rowmod < CHUNK - sh, pltpu.roll(v, t - sh, 0), 0.0)
        sh *= 2
    return v


def _gla_decay(alpha_ref, wup_ref, b_ref, g_scr):
    z = _dot(alpha_ref[...], wup_ref[...], NN) + b_ref[...]
    rowmod = lax.broadcasted_iota(jnp.int32, z.shape, 0) % CHUNK
    g_scr[...] = _seg_cumsum(_log_sigmoid(z) * (1.0 / GATE_TAU), rowmod)
    return z, rowmod


def _chunk_terms(hd_ref, g_scr, c):
    r0 = c * CHUNK
    g = g_scr[r0:r0 + CHUNK, :]
    g_first = g_scr[r0:r0 + 1, :]
    g_last = g_scr[r0 + CHUNK - 1:r0 + CHUNK, :]
    ref = 0.5 * (g_first + g_last)
    ep = jnp.exp(g - ref)
    em = jnp.exp(ref - g)
    a = jnp.exp(g)
    dl = jnp.exp(g_last - g)
    qs = hd_ref[r0:r0 + CHUNK, QC] * Q_SCALE
    k = hd_ref[r0:r0 + CHUNK, KC]
    return dict(ep=ep, em=em, a=a, dl=dl, egl=jnp.exp(g_last),
                qe1=qs * ep, ke1=k * em, qe2=qs * em, ke2=k * ep, qa=qs * a, kd=k * dl)


def _scores(t, lower):
    return jnp.where(lower, _dot(t["qe1"], t["ke1"], NT), _dot(t["qe2"], t["ke2"], NT))


def _gla_specs(tt, row):
    return [
        pl.BlockSpec((tt, HEAD_W), lambda h, i: (row(i), HD0 // HEAD_W + h)),
        pl.BlockSpec((tt, 128), lambda h, i: (row(i), AL0 // 128)),
        pl.BlockSpec((128, HDK), lambda h, i: (0, h)),
        pl.BlockSpec((1, HDK), lambda h, i: (0, h)),
        pl.BlockSpec((1, HDV), lambda h, i: (0, h)),
    ]


def _gla_fwd(hh, wup, b_alpha, gnorm, *, tt, ride=None):
    s = hh.shape[0]
    nt = s // tt
    nct = tt // CHUNK

    def body(hd_ref, al_ref, wup_ref, b_ref, gn_ref, o_ref, ya_ref, st_ref, state, g_scr):
        @pl.when(pl.program_id(1) == 0)
        def _():
            state[...] = jnp.zeros_like(state)

        _gla_decay(al_ref, wup_ref, b_ref, g_scr)
        lower = lax.broadcasted_iota(jnp.int32, (CHUNK, CHUNK), 0) >= lax.broadcasted_iota(jnp.int32, (CHUNK, CHUNK), 1)
        for c in range(nct):
            r0 = c * CHUNK
            t = _chunk_terms(hd_ref, g_scr, c)
            v = hd_ref[r0:r0 + CHUNK, VC]
            st = state[...]
            st_ref[0, c] = st
            o_ref[r0:r0 + CHUNK, :] = _dot(_scores(t, lower), v, NN) + _dot(t["qa"], st, NT)
            state[...] = st * t["egl"] + _dot(v, t["kd"], TN)
        o = o_ref[...]
        ohat = o * lax.rsqrt(jnp.mean(o * o, axis=-1, keepdims=True) + EPS)
        ga = hd_ref[:, GC]
        ya_ref[...] = ohat * gn_ref[...] * (ga * _sigmoid(ga))

    return _call(
        body, name="gla_fwd", grid=(HEADS, nt),
        in_specs=_gla_specs(tt, lambda i: i),
        out_specs=[
            pl.BlockSpec((tt, HDV), lambda h, i: (i, h)),
            pl.BlockSpec((tt, HDV), lambda h, i: (i, h)),
            pl.BlockSpec((1, nct, HDV, HDK), lambda h, i: (h, i, 0, 0)),
        ],
        out_shape=[
            jax.ShapeDtypeStruct((s, D), F32),
            jax.ShapeDtypeStruct((s, D), F32),
            jax.ShapeDtypeStruct((HEADS, s // CHUNK, HDV, HDK), F32),
        ],
        scratch=[pltpu.VMEM((HDV, HDK), F32), pltpu.VMEM((tt, HDK), F32)],
        args=(hh, hh, wup, b_alpha, gnorm), ride=ride)


def _gla_bwd(hh, wup, b_alpha, gnorm, o, states, dya, dh, *, tt, ride=None):
    s = hh.shape[0]
    nt = s // tt
    nct = tt // CHUNK

    def body(hd_ref, al_ref, wup_ref, b_ref, gn_ref, o_ref, st_ref, dya_ref, _dh_in,
             dh_ref, dz_ref, dgn_ref, db_ref, dstate, g_scr, dg_scr):
        @pl.when(pl.program_id(1) == 0)
        def _():
            dstate[...] = jnp.zeros_like(dstate)
            dgn_ref[...] = jnp.zeros_like(dgn_ref)
            db_ref[...] = jnp.zeros_like(db_ref)

        z, rowmod = _gla_decay(al_ref, wup_ref, b_ref, g_scr)

        o_t = o_ref[...]
        rstd = lax.rsqrt(jnp.mean(o_t * o_t, axis=-1, keepdims=True) + EPS)
        ohat = o_t * rstd
        ga = hd_ref[:, GC]
        sg = _sigmoid(ga)
        dya_t = dya_ref[...]
        gn = gn_ref[...]
        dh_ref[:, GC] = dya_t * ohat * gn * (sg * (1.0 + ga * (1.0 - sg)))
        don = dya_t * (ga * sg)
        dgn_ref[0] += jnp.sum(don * ohat, axis=0, keepdims=True)
        dohat = don * gn
        dh_ref[:, VC] = rstd * (dohat - ohat * jnp.mean(dohat * ohat, axis=-1, keepdims=True))

        lower = lax.broadcasted_iota(jnp.int32, (CHUNK, CHUNK), 0) >= lax.broadcasted_iota(jnp.int32, (CHUNK, CHUNK), 1)
        last_row = lax.broadcasted_iota(jnp.int32, (CHUNK, HDK), 0) == CHUNK - 1
        for c in range(nct - 1, -1, -1):
            r0 = c * CHUNK
            t = _chunk_terms(hd_ref, g_scr, c)
            v = hd_ref[r0:r0 + CHUNK, VC]
            do = dh_ref[r0:r0 + CHUNK, VC]
            st = st_ref[0, c]
            dst = dstate[...]
            a = _scores(t, lower)
            da = _dot(do, v, NT)
            da1 = jnp.where(lower, da, 0.0)
            da2 = jnp.where(lower, 0.0, da)
            dqe1 = _dot(da1, t["ke1"], NN)
            dke1 = _dot(da1, t["qe1"], TN)
            dqe2 = _dot(da2, t["ke2"], NN)
            dke2 = _dot(da2, t["qe2"], TN)
            dqa = _dot(do, st, NN)
            dkd = _dot(v, dst, NN)
            dh_ref[r0:r0 + CHUNK, VC] = _dot(a, do, TN) + _dot(t["kd"], dst, NT)
            dh_ref[r0:r0 + CHUNK, QC] = (dqe1 * t["ep"] + dqe2 * t["em"] + dqa * t["a"]) * Q_SCALE
            dh_ref[r0:r0 + CHUNK, KC] = dke1 * t["em"] + dke2 * t["ep"] + dkd * t["dl"]
            dkd_kd = dkd * t["kd"]
            dgl = jnp.sum(dkd_kd, axis=0, keepdims=True) + t["egl"] * jnp.sum(dst * st, axis=0, keepdims=True)
            dg = dqe1 * t["qe1"] - dke1 * t["ke1"] - dqe2 * t["qe2"] + dke2 * t["ke2"] + dqa * t["qa"] - dkd_kd
            dg_scr[r0:r0 + CHUNK, :] = dg + jnp.where(last_row, dgl, 0.0)
            dstate[...] = dst * t["egl"] + _dot(do, t["qa"], TN)

        dz = _seg_rcumsum(dg_scr[...], rowmod) * _sigmoid(-z) * (1.0 / GATE_TAU)
        dz_ref[...] = dz
        db_ref[0] += jnp.sum(dz, axis=0, keepdims=True)

    rev = lambda i: nt - 1 - i
    in_specs = _gla_specs(tt, rev) + [
        pl.BlockSpec((tt, HDV), lambda h, i: (rev(i), h)),
        pl.BlockSpec((1, nct, HDV, HDK), lambda h, i: (h, rev(i), 0, 0)),
        pl.BlockSpec((tt, HDV), lambda h, i: (rev(i), h)),
        ANY,
    ]
    return _call(
        body, name="gla_bwd", grid=(HEADS, nt), in_specs=in_specs,
        out_specs=[
            pl.BlockSpec((tt, HEAD_W), lambda h, i: (rev(i), HD0 // HEAD_W + h)),
            pl.BlockSpec((tt, HDK), lambda h, i: (rev(i), h)),
            pl.BlockSpec((1, 1, HDV), lambda h, i: (h, 0, 0)),
            pl.BlockSpec((1, 1, HDK), lambda h, i: (h, 0, 0)),
        ],
        out_shape=[
            jax.ShapeDtypeStruct(dh.shape, F32),
            jax.ShapeDtypeStruct((s, DK), F32),
            jax.ShapeDtypeStruct((HEADS, 1, HDV), F32),
            jax.ShapeDtypeStruct((HEADS, 1, HDK), F32),
        ],
        scratch=[pltpu.VMEM((HDV, HDK), F32), pltpu.VMEM((tt, HDK), F32), pltpu.VMEM((tt, HDK), F32)],
        args=(hh, hh, wup, b_alpha, gnorm, o, states, dya, dh), ride=ride, aliases={8: 0})


def _window_count(tile, tt, w):
    pos = tile * tt + lax.broadcasted_iota(jnp.int32, (tt, PG), 0) + 1
    return jnp.minimum(pos, w).astype(F32)


def _pool_fwd(hh, wpool, scale, *, tt):
    s = hh.shape[0]
    nt = s // tt

    def body(ug_ref, w_ref, sc_ref, pooled_ref, mixed_ref, yb_ref, halo):
        i = pl.program_id(0)

        @pl.when(i == 0)
        def _():
            halo[...] = jnp.zeros_like(halo)

        for g, w in enumerate(POOL_WINDOWS):
            cols = slice(g * PG, (g + 1) * PG)
            u = ug_ref[:, cols]
            run = jnp.concatenate([halo[:, cols], u], axis=0)
            sh = 1
            while sh < w:
                run = run + pltpu.roll(run, sh, 0)
                sh *= 2
            pooled = run[HALO:, :] / _window_count(i, tt, w) - u
            pooled_ref[:, cols] = pooled
            mixed = _dot(pooled, w_ref[g], NN)
            mixed_ref[:, cols] = mixed
            gb = ug_ref[:, slice(D + g * PG, D + (g + 1) * PG)]
            yb_ref[:, cols] = mixed * sc_ref[:, cols] * (gb * _sigmoid(gb))
        halo[...] = ug_ref[tt - HALO:tt, :D]

    tile = pl.BlockSpec((tt, D), lambda i: (i, 0))
    return _call(
        body, name="pool_fwd", grid=(nt,),
        in_specs=[
            pl.BlockSpec((tt, 2 * D), lambda i: (i, PI0 // (2 * D))),
            pl.BlockSpec((len(POOL_WINDOWS), PG, PG), lambda i: (0, 0, 0)),
            pl.BlockSpec((1, D), lambda i: (0, 0)),
        ],
        out_specs=[tile] * 3, out_shape=[jax.ShapeDtypeStruct((s, D), F32)] * 3,
        scratch=[pltpu.VMEM((HALO, D), F32)], args=(hh, wpool, scale))


def _pool_bwd(hh, wpool, scale, pooled, mixed, dyb, dh, *, tt):
    s = hh.shape[0]
    nt = s // tt

    def body(ug_ref, w_ref, sc_ref, pooled_ref, mixed_ref, dyb_ref, _dh_in, dh_ref, dw_ref, dsc_ref, halo):
        i = pl.program_id(0)
        tile = nt - 1 - i

        @pl.when(i == 0)
        def _():
            halo[...] = jnp.zeros_like(halo)
            dw_ref[...] = jnp.zeros_like(dw_ref)
            dsc_ref[...] = jnp.zeros_like(dsc_ref)

        for g, w in enumerate(POOL_WINDOWS):
            cols = slice(g * PG, (g + 1) * PG)
            gcols = slice(D + g * PG, D + (g + 1) * PG)
            gb = ug_ref[:, gcols]
            sg = _sigmoid(gb)
            mixed = mixed_ref[:, cols]
            sc = sc_ref[:, cols]
            dyb = dyb_ref[:, cols]
            dh_ref[:, gcols] = dyb * mixed * sc * (sg * (1.0 + gb * (1.0 - sg)))
            dms = dyb * (gb * sg)
            dsc_ref[:, cols] += jnp.sum(dms * mixed, axis=0, keepdims=True)
            dmixed = dms * sc
            dpooled = _dot(dmixed, w_ref[g], NT)
            dw_ref[g] += _dot(pooled_ref[:, cols], dmixed, TN)
            e = dpooled / _window_count(tile, tt, w)
            run = jnp.concatenate([e, halo[:, cols]], axis=0)
            sh = 1
            while sh < w:
                run = run + pltpu.roll(run, tt + HALO - sh, 0)
                sh *= 2
            dh_ref[:, cols] = run[:tt, :] - dpooled
            halo[:, cols] = e[:HALO, :]

    rev = lambda i: nt - 1 - i
    tile = pl.BlockSpec((tt, D), lambda i: (rev(i), 0))
    wspec = pl.BlockSpec((len(POOL_WINDOWS), PG, PG), lambda i: (0, 0, 0))
    vec = pl.BlockSpec((1, D), lambda i: (0, 0))
    return _call(
        body, name="pool_bwd", grid=(nt,),
        in_specs=[pl.BlockSpec((tt, 2 * D), lambda i: (rev(i), PI0 // (2 * D))), wspec, vec, tile, tile, tile, ANY],
        out_specs=[pl.BlockSpec((tt, 2 * D), lambda i: (rev(i), PI0 // (2 * D))), wspec, vec],
        out_shape=[jax.ShapeDtypeStruct(dh.shape, F32), jax.ShapeDtypeStruct((len(POOL_WINDOWS), PG, PG), F32),
                   jax.ShapeDtypeStruct((1, D), F32)],
        scratch=[pltpu.VMEM((HALO, D), F32)], args=(hh, wpool, scale, pooled, mixed, dyb, dh), aliases={6: 0})


def _merge_fwd(hh, x, ya, yb, wpa, wpb, wout, b_merge, ln_g, ln_b, *, tt):
    s = x.shape[0]

    def body(ml_ref, x_ref, ya_ref, yb_ref, wpa_ref, wpb_ref, wout_ref, bm_ref, g_ref, b_ref,
             pa_ref, pb_ref, r_ref, xn_ref):
        pa = _dot(ya_ref[...], wpa_ref[...], NN)
        pb = _dot(yb_ref[...], wpb_ref[...], NN)
        pa_ref[...] = pa
        pb_ref[...] = pb
        merged = _sigmoid(ml_ref[:, :D] + bm_ref[:, :D]) * pa + _sigmoid(ml_ref[:, D:] + bm_ref[:, D:]) * pb
        r = ALPHA * x_ref[...] + _dot(merged, wout_ref[...], NN)
        r_ref[...] = r
        mu = jnp.mean(r, axis=-1, keepdims=True)
        xc = r - mu
        var = jnp.mean(xc * xc, axis=-1, keepdims=True)
        xn_ref[...] = xc * lax.rsqrt(var + EPS) * g_ref[...] + b_ref[...]

    tile = pl.BlockSpec((tt, D), lambda i: (i, 0))
    full = pl.BlockSpec((D, D), lambda i: (0, 0))
    vec = pl.BlockSpec((1, D), lambda i: (0, 0))
    return _call(
        body, name="merge_fwd", grid=(s // tt,),
        in_specs=[pl.BlockSpec((tt, 2 * D), lambda i: (i, ML0 // (2 * D))), tile, tile, tile, full, full, full,
                  pl.BlockSpec((1, 2 * D), lambda i: (0, 0)), vec, vec],
        out_specs=[tile] * 4, out_shape=[jax.ShapeDtypeStruct((s, D), F32)] * 4,
        args=(hh, x, ya, yb, wpa, wpb, wout, b_merge, ln_g, ln_b), sem=("parallel",), vmem=VMEM_BIG)


def _merge_bwd(hh, r, pa, pb, dout, wout, b_merge, ln_g, *, tt):
    s = r.shape[0]

    def body(ml_ref, r_ref, pa_ref, pb_ref, do_ref, wout_ref, bm_ref, g_ref,
             dh_ref, dr_ref, dpa_ref, dpb_ref, dwout_ref, dg_ref, db_ref, dbm_ref):
        @pl.when(pl.program_id(0) == 0)
        def _():
            dwout_ref[...] = jnp.zeros_like(dwout_ref)
            dg_ref[...] = jnp.zeros_like(dg_ref)
            db_ref[...] = jnp.zeros_like(db_ref)
            dbm_ref[...] = jnp.zeros_like(dbm_ref)

        rr = r_ref[...]
        mu = jnp.mean(rr, axis=-1, keepdims=True)
        xc = rr - mu
        rstd = lax.rsqrt(jnp.mean(xc * xc, axis=-1, keepdims=True) + EPS)
        xhat = xc * rstd
        do = do_ref[...]
        dg_ref[...] += jnp.sum(do * xhat, axis=0, keepdims=True)
        db_ref[...] += jnp.sum(do, axis=0, keepdims=True)
        dxh = do * g_ref[...]
        dr = rstd * (dxh - jnp.mean(dxh, axis=-1, keepdims=True) - xhat * jnp.mean(dxh * xhat, axis=-1, keepdims=True))
        dr_ref[...] = dr
        g_a = _sigmoid(ml_ref[:, :D] + bm_ref[:, :D])
        g_b = _sigmoid(ml_ref[:, D:] + bm_ref[:, D:])
        pa = pa_ref[...]
        pb = pb_ref[...]
        dwout_ref[...] += _dot(g_a * pa + g_b * pb, dr, TN)
        dm = _dot(dr, wout_ref[...], NT)
        dpa_ref[...] = dm * g_a
        dpb_ref[...] = dm * g_b
        dml_a = dm * pa * g_a * (1.0 - g_a)
        dml_b = dm * pb * g_b * (1.0 - g_b)
        dh_ref[:, :D] = dml_a
        dh_ref[:, D:] = dml_b
        dbm_ref[:, :D] += jnp.sum(dml_a, axis=0, keepdims=True)
        dbm_ref[:, D:] += jnp.sum(dml_b, axis=0, keepdims=True)

    tile = pl.BlockSpec((tt, D), lambda i: (i, 0))
    full = pl.BlockSpec((D, D), lambda i: (0, 0))
    vec = pl.BlockSpec((1, D), lambda i: (0, 0))
    vec2 = pl.BlockSpec((1, 2 * D), lambda i: (0, 0))
    mlb = pl.BlockSpec((tt, 2 * D), lambda i: (i, ML0 // (2 * D)))
    return _call(
        body, name="merge_bwd", grid=(s // tt,),
        in_specs=[mlb, tile, tile, tile, tile, full, vec2, vec],
        out_specs=[mlb, tile, tile, tile, full, vec, vec, vec2],
        out_shape=[
            jax.ShapeDtypeStruct((s, HP), F32), jax.ShapeDtypeStruct((s, D), F32),
            jax.ShapeDtypeStruct((s, D), F32), jax.ShapeDtypeStruct((s, D), F32),
            jax.ShapeDtypeStruct((D, D), F32), jax.ShapeDtypeStruct((1, D), F32),
            jax.ShapeDtypeStruct((1, D), F32), jax.ShapeDtypeStruct((1, 2 * D), F32),
        ],
        args=(hh, r, pa, pb, dout, wout, b_merge, ln_g), vmem=VMEM_BIG)


def _proj_bwd(y, dp, w, *, tt, name):
    s = y.shape[0]

    def body(y_ref, dp_ref, w_ref, dy_ref, dw_ref):
        @pl.when(pl.program_id(0) == 0)
        def _():
            dw_ref[...] = jnp.zeros_like(dw_ref)

        dp = dp_ref[...]
        dy_ref[...] = _dot(dp, w_ref[...], NT)
        dw_ref[...] += _dot(y_ref[...], dp, TN)

    tile = pl.BlockSpec((tt, D), lambda i: (i, 0))
    full = pl.BlockSpec((D, D), lambda i: (0, 0))
    return _call(
        body, name=name, grid=(s // tt,), in_specs=[tile, tile, full], out_specs=[tile, full],
        out_shape=[jax.ShapeDtypeStruct((s, D), F32), jax.ShapeDtypeStruct((D, D), F32)], args=(y, dp, w))


def _loss_head(y, target, *, tt):
    s = y.shape[0]

    def body(y_ref, t_ref, loss_ref, dy_ref):
        @pl.when(pl.program_id(0) == 0)
        def _():
            loss_ref[...] = jnp.zeros_like(loss_ref)

        err = y_ref[...] - t_ref[...]
        dy_ref[...] = err * (1.0 / D)
        per_tok = jnp.mean(err * err, axis=-1, keepdims=True)
        loss_ref[...] += 0.5 * jnp.sum(per_tok, axis=0, keepdims=True)

    tile = pl.BlockSpec((tt, D), lambda i: (i, 0))
    return _call(
        body, name="loss_head", grid=(s // tt,), in_specs=[tile, tile],
        out_specs=[pl.BlockSpec((1, 1), lambda i: (0, 0)), tile],
        out_shape=[jax.ShapeDtypeStruct((1, 1), F32), jax.ShapeDtypeStruct((s, D), F32)], args=(y, target))


def _adamw(parts, w, m, v, *, tr, name):
    nl, rows, cols = w.shape

    def body(p_ref, w_ref, m_ref, v_ref, g_ref, d_ref, nm_ref, nv_ref):
        g = p_ref[0, 0].astype(F32)
        for q in range(1, N_DEV):
            g = g + p_ref[0, q].astype(F32)
        g_ref[0] = g
        nm = ADAM_B1 * m_ref[0] + (1.0 - ADAM_B1) * g
        nv = ADAM_B2 * v_ref[0] + (1.0 - ADAM_B2) * (g * g)
        nm_ref[0] = nm
        nv_ref[0] = nv
        m_hat = nm / (1.0 - ADAM_B1 ** ADAM_STEP)
        v_hat = nv / (1.0 - ADAM_B2 ** ADAM_STEP)
        d_ref[0] = -ADAM_LR * (m_hat / (jnp.sqrt(v_hat) + ADAM_EPS) + ADAM_WD * w_ref[0])

    tile = pl.BlockSpec((1, tr, cols), lambda l, i: (l, i, 0))
    return _call(
        body, name=name, grid=(nl, rows // tr),
        in_specs=[pl.BlockSpec((1, N_DEV, tr, cols), lambda l, i: (l, 0, i, 0)), tile, tile, tile],
        out_specs=[tile] * 4, out_shape=[jax.ShapeDtypeStruct((nl, rows, cols), F32)] * 4,
        args=(parts, w, m, v), sem=("parallel", "parallel"))


def _from_devices(g, axis):
    nd = g.ndim - 1
    perm = list(range(1, axis + 1)) + [0] + list(range(axis + 1, nd + 1))
    shape = list(g.shape[1:])
    shape[axis] *= N_DEV
    return jnp.transpose(g, perm).reshape(shape)


def _to_devices(a, axis):
    shape = list(a.shape)
    t = a.reshape(shape[:axis] + [N_DEV, shape[axis] // N_DEV] + shape[axis + 1:])
    return jnp.transpose(t, [axis] + list(range(0, axis)) + list(range(axis + 1, t.ndim)))


def _h_weight(w_in_parts):
    w = _from_devices(w_in_parts, 1)
    heads = [w[:, sl] for h in range(HEADS) for sl in (
        slice(O_Q + h * HDK, O_Q + (h + 1) * HDK), slice(O_K + h * HDK, O_K + (h + 1) * HDK),
        slice(O_V + h * HDV, O_V + (h + 1) * HDV), slice(O_GA + h * HDV, O_GA + (h + 1) * HDV))]
    return jnp.concatenate([w[:, O_PI:O_GB], w[:, O_GB:O_ML], w[:, O_ML:], w[:, O_AL:O_PI],
                            jnp.zeros((D, AL_W - RANK), w.dtype)] + heads, axis=1)


def _w_in_grad_parts(dw_h):
    hd = lambda h, sl: dw_h[:, HD0 + h * HEAD_W:HD0 + (h + 1) * HEAD_W][:, sl]
    g = jnp.concatenate([hd(h, QC) for h in range(HEADS)] + [hd(h, KC) for h in range(HEADS)]
                        + [hd(h, VC) for h in range(HEADS)] + [hd(h, GC) for h in range(HEADS)]
                        + [dw_h[:, AL0:AL0 + RANK], dw_h[:, PI0:GB0], dw_h[:, GB0:ML0], dw_h[:, ML0:AL0]], axis=1)
    return _to_devices(g, 1).astype(WIRE)


def kernel(x, w_in, w_alpha_up, b_alpha, gla_norm_g, w_pool_grp, pool_scale, b_merge, w_proj_a, w_proj_b, w_out, ln_g, ln_b, loss_target, m_w_in, m_w_alpha_up, m_b_alpha, m_gla_norm_g, m_w_pool_grp, m_pool_scale, m_b_merge, m_w_proj_a, m_w_proj_b, m_w_out, m_ln_g, m_ln_b, v_w_in, v_w_alpha_up, v_b_alpha, v_gla_norm_g, v_w_pool_grp, v_pool_scale, v_b_merge, v_w_proj_a, v_w_proj_b, v_w_out, v_ln_g, v_ln_b):
    s = x.shape[1]
    tt = min(256, s)
    tm = min(512, s)
    tn = HP // 3
    xs = x.reshape(s, D)

    w_in_s = w_in.astype(WIRE)
    proj_s = jnp.stack([w_proj_a, w_proj_b, w_out], axis=1).astype(WIRE)
    pool_s = w_pool_grp.astype(WIRE)

    g_in, g_proj, g_pool, g_up, g_gn = _exchange(
        [(w_in_s[0], True), (proj_s[0], True), (pool_s[0], True), (w_alpha_up.astype(WIRE), True), (gla_norm_g, True)],
        name="gather_first")
    wup = jnp.pad(_from_devices(g_up, 2), ((0, 0), (0, AL_W - RANK), (0, 0)))
    gn = _from_devices(g_gn, 2).reshape(DEPTH, 1, D)

    saved, wh_all, proj_all, pool_all = [], [], [], []
    cur = xs
    for l in range(DEPTH):
        wh = _h_weight(g_in)
        proj = _from_devices(g_proj, 1)
        pool = _from_devices(g_pool, 1)
        wh_all.append(wh), proj_all.append(proj), pool_all.append(pool)
        nxt_l = l + 1 < DEPTH
        res = _mm_nn(cur, wh, tm=tm, tn=tn, name="in_proj", ride=_Exchange([(w_in_s[l + 1], True)]) if nxt_l else None)
        hh = res[0]
        if nxt_l:
            g_in = res[1]
        res = _gla_fwd(hh, wup[l], b_alpha[l:l + 1], gn[l], tt=tt,
                       ride=_Exchange([(proj_s[l + 1], True), (pool_s[l + 1], True)]) if nxt_l else None)
        o, ya, states = res[:3]
        if nxt_l:
            g_proj, g_pool = res[3:]
        pooled, mixed, yb = _pool_fwd(hh, pool, pool_scale[l:l + 1], tt=tt)
        pa, pb, r, nxt = _merge_fwd(hh, cur, ya, yb, proj[0], proj[1], proj[2],
                                    b_merge[l:l + 1], ln_g[l:l + 1], ln_b[l:l + 1], tt=tt)
        saved.append(dict(x=cur, hh=hh, o=o, ya=ya, states=states, pooled=pooled, mixed=mixed, yb=yb, pa=pa, pb=pb, r=r))
        cur = nxt

    loss_part, dcur = _loss_head(cur, loss_target.reshape(s, D), tt=tt)
    loss = lax.psum(loss_part[0, 0], ("x", "y", "c"))

    small = {k: [None] * DEPTH for k in ("w_up", "b_alpha", "gnorm", "pool_scale", "b_merge", "ln_g", "ln_b")}
    parts = {k: [None] * DEPTH for k in ("w_in", "proj", "pool")}
    for l in range(DEPTH - 1, -1, -1):
        sv = saved[l]
        hh = sv["hh"]
        dh, dr, dpa, dpb, dw_out, dln_g, dln_b, db_merge = _merge_bwd(
            hh, sv["r"], sv["pa"], sv["pb"], dcur, proj_all[l][2], b_merge[l:l + 1], ln_g[l:l + 1], tt=tt)
        dya, dw_pa = _proj_bwd(sv["ya"], dpa, proj_all[l][0], tt=tt, name="proj_a_bwd")
        dyb, dw_pb = _proj_bwd(sv["yb"], dpb, proj_all[l][1], tt=tt, name="proj_b_bwd")
        dh, dw_pool, dscale = _pool_bwd(hh, pool_all[l], pool_scale[l:l + 1], sv["pooled"], sv["mixed"], dyb, dh, tt=tt)
        ride = _Exchange([(_to_devices(jnp.stack([dw_pa, dw_pb, dw_out]), 1).astype(WIRE), False),
                          (_to_devices(dw_pool, 1).astype(WIRE), False)])
        dh, dz, dgn, db_al, parts["proj"][l], parts["pool"][l] = _gla_bwd(
            hh, wup[l], b_alpha[l:l + 1], gn[l], sv["o"], sv["states"], dya, dh, tt=tt, ride=ride)
        dh = _mm_nt(dz, wup[l], tm=tm, tn=DK, name="alpha_bwd", into=(dh, AL0 // AL_W))[0]
        dw_up = _mm_tn(hh, dz, tm=tm, tn=DK, name="w_up_grad", a_block=(AL_W, AL0 // AL_W))
        dw_h = _mm_tn(sv["x"], dh, tm=tm, tn=tn, name="w_in_grad")
        dcur, parts["w_in"][l] = _mm_nt(dh, wh_all[l], tm=tm, tn=tn, name="in_proj_bwd", add=dr, add_scale=ALPHA,
                                        ride=_Exchange([(_w_in_grad_parts(dw_h), False)]))

        small["w_up"][l] = dw_up[:RANK]
        small["b_alpha"][l] = db_al.reshape(DK)
        small["gnorm"][l] = dgn.reshape(HEADS, HDV)
        small["pool_scale"][l] = dscale[0]
        small["b_merge"][l] = db_merge[0]
        small["ln_g"][l], small["ln_b"][l] = dln_g[0], dln_b[0]
    grad_x = dcur[None]
    sm = {k: jnp.stack(v) for k, v in small.items()}

    rep = (("b_alpha", b_alpha, m_b_alpha, v_b_alpha), ("pool_scale", pool_scale, m_pool_scale, v_pool_scale),
           ("b_merge", b_merge, m_b_merge, v_b_merge), ("ln_g", ln_g, m_ln_g, v_ln_g), ("ln_b", ln_b, m_ln_b, v_ln_b))
    cat = lambda arrs: jnp.concatenate(arrs, axis=1)
    p_up, p_gn, p_rep = _exchange([(_to_devices(sm["w_up"], 2), False), (_to_devices(sm["gnorm"], 2), False),
                                   (cat([sm[nm] for nm, _, _, _ in rep]), True)], name="exchange_small_grads")

    def update(p, w, m, v, tr, name, layered=True):
        shape = w.shape
        nl = shape[0] if layered else 1
        cols = shape[-1]
        flat = lambda a: a.reshape(nl, -1, cols)
        outs = _adamw(p.reshape(nl, N_DEV, -1, cols), flat(w), flat(m), flat(v), tr=tr, name=name)
        return [o_.reshape(shape) for o_ in outs]

    res = {}
    res["w_in"] = update(jnp.stack(parts["w_in"]), w_in, m_w_in, v_w_in, 128, "adamw_w_in")
    proj_p = jnp.stack(parts["proj"])
    for j, (nm, w, m, v) in enumerate((("w_proj_a", w_proj_a, m_w_proj_a, v_w_proj_a), ("w_proj_b", w_proj_b, m_w_proj_b, v_w_proj_b),
                                       ("w_out", w_out, m_w_out, v_w_out))):
        res[nm] = update(proj_p[:, :, j], w, m, v, D // N_DEV, "adamw_" + nm)
    res["w_pool_grp"] = update(jnp.stack(parts["pool"]), w_pool_grp, m_w_pool_grp, v_w_pool_grp, 128, "adamw_w_pool")
    res["w_alpha_up"] = update(p_up, w_alpha_up, m_w_alpha_up, v_w_alpha_up, DEPTH * RANK, "adamw_w_up", layered=False)
    res["gla_norm_g"] = update(p_gn, gla_norm_g, m_gla_norm_g, v_gla_norm_g, DEPTH * HEADS, "adamw_gnorm", layered=False)
    rep_out = update(p_rep, cat([w for _, w, _, _ in rep]), cat([m for _, _, m, _ in rep]), cat([v for _, _, _, v in rep]),
                     DEPTH, "adamw_small", layered=False)
    off = 0
    for nm, w, _, _ in rep:
        n = w.shape[1]
        res[nm] = [o_[:, off:off + n] for o_ in rep_out]
        off += n

    order = ("w_in", "w_alpha_up", "b_alpha", "gla_norm_g", "w_pool_grp", "pool_scale", "b_merge", "w_proj_a", "w_proj_b",
             "w_out", "ln_g", "ln_b")
    return (loss, grad_x, *[res[n][0] for n in order], *[res[n][1] for n in order],
            *[res[n][2] for n in order], *[res[n][3] for n in order])
```

```python
import jax
import jax.numpy as jnp
from jax import lax
from jax.experimental import pallas as pl
from jax.experimental.pallas import tpu as pltpu

F32 = jnp.float32
MXU = jnp.bfloat16
WIRE = jnp.bfloat16

N_DEV = 8
DEPTH = 4
D = 1024
HEADS = 4
DK = D // 2
HDK = DK // HEADS
HDV = D // HEADS
RANK = 16
CHUNK = 64
GATE_TAU = 16.0
POOL_WINDOWS = (2, 4, 8, 16)
PG = D // len(POOL_WINDOWS)
HALO = 16
IN_COLS = 7184
SHARD = IN_COLS // N_DEV
ALPHA = (2.0 * DEPTH) ** 0.25
EPS = 1e-5
Q_SCALE = HDK ** -0.5

ADAM_LR, ADAM_B1, ADAM_B2, ADAM_EPS, ADAM_WD, ADAM_STEP = 0.001, 0.9, 0.999, 1e-08, 0.01, 10

PI0, GB0, ML0, AL0, AL_W = 0, D, 2 * D, 4 * D, 512
HD0 = AL0 + AL_W
HEAD_W = 2 * HDK + 2 * HDV
HP = HD0 + HEADS * HEAD_W
HPB = 2
O_Q, O_K, O_V, O_GA, O_AL, O_PI, O_GB, O_ML = 0, DK, 2 * DK, 2 * DK + D, 2 * DK + 2 * D, 2 * DK + 2 * D + RANK, \
    2 * DK + 3 * D + RANK, 2 * DK + 4 * D + RANK

VMEM_BIG = 56 * 1024 * 1024
VMEM_MID = 40 * 1024 * 1024

NN = ((1,), (0,))
NT = ((1,), (1,))
TN = ((0,), (0,))

HBM = pl.BlockSpec(memory_space=pltpu.HBM)
ANY = pl.BlockSpec(memory_space=pl.ANY)


def _dot(a, b, dims):
    return lax.dot_general(a.astype(MXU), b.astype(MXU), (dims, ((), ())), preferred_element_type=F32)


def _params(sem, vmem):
    return pltpu.CompilerParams(dimension_semantics=sem, vmem_limit_bytes=vmem)


def _sigmoid(x):
    return 1.0 / (1.0 + jnp.exp(-x))


def _log_sigmoid(z):
    return jnp.minimum(z, 0.0) - jnp.log(1.0 + jnp.exp(-jnp.abs(z)))


class _Exchange:
    def __init__(self, items):
        self.items = [(s, bool(g)) for s, g in items]
        self.n = len(self.items)
        self.srcs = [s for s, _ in self.items]
        self.in_specs = [HBM] * self.n
        self.out_specs = [HBM] * self.n
        self.out_shape = [jax.ShapeDtypeStruct((N_DEV,) + tuple(s.shape if g else s.shape[1:]), s.dtype) for s, g in self.items]
        self.scratch = [pltpu.SemaphoreType.DMA((self.n * (N_DEV - 1),)), pltpu.SemaphoreType.DMA((self.n * (N_DEV - 1),)),
                        pltpu.SemaphoreType.DMA((self.n,))]

    def copies(self, src_refs, out_refs, send_sems, recv_sems, local_sems):
        x, y, c = lax.axis_index("x"), lax.axis_index("y"), lax.axis_index("c")
        me = 4 * x + 2 * y + c
        copies = []
        for t, (_, gather) in enumerate(self.items):
            src_ref, out_ref = src_refs[t], out_refs[t]
            copies.append(pltpu.make_async_copy(src_ref if gather else src_ref.at[me], out_ref.at[me], local_sems.at[t]))
            for k in range(1, N_DEV):
                px = 1 - x if k & 4 else x
                py = 1 - y if k & 2 else y
                pc = 1 - c if k & 1 else c
                peer = 4 * px + 2 * py + pc
                sem = t * (N_DEV - 1) + k - 1
                copies.append(pltpu.make_async_remote_copy(
                    src_ref=src_ref if gather else src_ref.at[peer],
                    dst_ref=out_ref.at[me],
                    send_sem=send_sems.at[sem],
                    recv_sem=recv_sems.at[sem],
                    device_id=(px, py, pc),
                    device_id_type=pl.DeviceIdType.MESH,
                ))
        return copies


def _grid_ends(grid):
    first = last = None
    for a, n in enumerate(grid):
        f = pl.program_id(a) == 0
        e = pl.program_id(a) == n - 1
        first = f if first is None else first & f
        last = e if last is None else last & e
    return first, last


def _call(body, *, name, grid, in_specs, out_specs, out_shape, args, scratch=(), sem=None, vmem=VMEM_MID, ride=None, aliases=None):
    n_in, n_out, n_scr = len(in_specs), len(out_specs), len(scratch)
    sem = sem or ("arbitrary",) * len(grid)
    if ride is None:
        return pl.pallas_call(body, name=name, grid=grid, in_specs=in_specs, out_specs=out_specs, out_shape=out_shape,
                              scratch_shapes=list(scratch), compiler_params=_params(sem, vmem),
                              input_output_aliases=aliases or {})(*args)
    r = ride.n

    def riding(*refs):
        ins, rsrc = refs[:n_in], refs[n_in:n_in + r]
        outs, rout = refs[n_in + r:n_in + r + n_out], refs[n_in + r + n_out:n_in + 2 * r + n_out]
        scr = refs[n_in + 2 * r + n_out:n_in + 2 * r + n_out + n_scr]
        send_sems, recv_sems, local_sems = refs[n_in + 2 * r + n_out + n_scr:]
        first, last = _grid_ends(grid)
        copies = ride.copies(rsrc, rout, send_sems, recv_sems, local_sems)

        @pl.when(first)
        def _():
            for cp in copies:
                cp.start()

        body(*ins, *outs, *scr)

        @pl.when(last)
        def _():
            for cp in copies:
                cp.wait()

    return pl.pallas_call(riding, name=name, grid=grid, in_specs=list(in_specs) + ride.in_specs,
                          out_specs=list(out_specs) + ride.out_specs, out_shape=list(out_shape) + ride.out_shape,
                          scratch_shapes=list(scratch) + ride.scratch,
                          compiler_params=_params(("arbitrary",) * len(grid), vmem),
                          input_output_aliases=aliases or {})(*args, *ride.srcs)


def _exchange(items, *, name):
    ex = _Exchange(items)

    def body(*refs):
        copies = ex.copies(refs[:ex.n], refs[ex.n:2 * ex.n], *refs[2 * ex.n:])
        for cp in copies:
            cp.start()
        for cp in copies:
            cp.wait()

    return pl.pallas_call(body, name=name, in_specs=ex.in_specs, out_specs=ex.out_specs, out_shape=ex.out_shape,
                          scratch_shapes=ex.scratch)(*ex.srcs)


def _in_proj(xb, wt, *, tm, tn, ride=None):
    m, k = xb.shape
    n = wt.shape[0]

    def body(x_ref, w_ref, o_ref):
        o_ref[...] = _dot(x_ref[...], w_ref[...], NT)

    return _call(
        body, name="in_proj", grid=(n // tn, m // tm),
        in_specs=[pl.BlockSpec((tm, k), lambda j, i: (i, 0)), pl.BlockSpec((tn, k), lambda j, i: (j, 0))],
        out_specs=[pl.BlockSpec((tm, tn), lambda j, i: (i, j))],
        out_shape=[jax.ShapeDtypeStruct((m, n), F32)],
        args=(xb, wt), sem=("parallel", "parallel"), vmem=VMEM_BIG, ride=ride)


def _in_proj_bwd(dh, wt, dr, *, tm, ride=None):
    m, n = dh.shape
    k = wt.shape[1]

    def body(dh_ref, w_ref, dr_ref, o_ref):
        o_ref[...] = ALPHA * dr_ref[...] + _dot(dh_ref[...], w_ref[...], NN)

    return _call(
        body, name="in_proj_bwd", grid=(m // tm,),
        in_specs=[pl.BlockSpec((tm, n), lambda i: (i, 0)), pl.BlockSpec((n, k), lambda i: (0, 0)),
                  pl.BlockSpec((tm, k), lambda i: (i, 0))],
        out_specs=[pl.BlockSpec((tm, k), lambda i: (i, 0))],
        out_shape=[jax.ShapeDtypeStruct((m, k), F32)],
        args=(dh, wt, dr), sem=("parallel",), vmem=VMEM_BIG, ride=ride)


def _mm_nt_into(dc, w, arr, col, *, tm, name):
    m, n = dc.shape
    k = w.shape[0]

    def body(dc_ref, w_ref, _arr_in, o_ref):
        o_ref[...] = _dot(dc_ref[...], w_ref[...], NT).astype(o_ref.dtype)

    return _call(
        body, name=name, grid=(m // tm,),
        in_specs=[pl.BlockSpec((tm, n), lambda i: (i, 0)), pl.BlockSpec((k, n), lambda i: (0, 0)), ANY],
        out_specs=[pl.BlockSpec((tm, k), lambda i: (i, col))],
        out_shape=[jax.ShapeDtypeStruct(arr.shape, arr.dtype)],
        args=(dc, w, arr), sem=("parallel",), aliases={2: 0})[0]


def _mm_tn(a, dc, *, tm, tk, name, a_block=None, dc_block=None):
    m = a.shape[0]
    k, a_col = (a.shape[1], None) if a_block is None else a_block
    n, dc_col = (dc.shape[1], 0) if dc_block is None else dc_block

    def body(a_ref, dc_ref, o_ref):
        @pl.when(pl.program_id(1) == 0)
        def _():
            o_ref[...] = jnp.zeros_like(o_ref)

        o_ref[...] += _dot(a_ref[...], dc_ref[...], TN)

    a_map = (lambda j, i: (i, j)) if a_col is None else (lambda j, i: (i, a_col))
    return _call(
        body, name=name, grid=(k // tk, m // tm),
        in_specs=[pl.BlockSpec((tm, tk), a_map), pl.BlockSpec((tm, n), lambda j, i: (i, dc_col))],
        out_specs=[pl.BlockSpec((tk, n), lambda j, i: (j, 0))],
        out_shape=[jax.ShapeDtypeStruct((k, n), F32)],
        args=(a, dc), sem=("parallel", "arbitrary"), vmem=VMEM_BIG)[0]


def _head_cols(p):
    b = p * HEAD_W
    return (slice(b, b + HDK), slice(b + HDK, b + 2 * HDK), slice(b + 2 * HDK, b + 2 * HDK + HDV),
            slice(b + 2 * HDK + HDV, b + HEAD_W))


def _seg_cumsum(v, rowmod):
    sh = 1
    while sh < CHUNK:
        v = v + jnp.where(rowmod >= sh, pltpu.roll(v, sh, 0), 0.0)
        sh *= 2
    return v


def _seg_rcumsum(v, rowmod):
    t = v.shape[0]
    sh = 1
    while sh < CHUNK:
        v = v + jnp.where(rowmod < CHUNK - sh, pltpu.roll(v, t - sh, 0), 0.0)
        sh *= 2
    return v


def _gla_decay(alpha_ref, wup_ref, b_ref, g_scr):
    z = _dot(alpha_ref[...], wup_ref[...], NN) + b_ref[...]
    rowmod = lax.broadcasted_iota(jnp.int32, z.shape, 0) % CHUNK
    g_scr[...] = _seg_cumsum(_log_sigmoid(z) * (1.0 / GATE_TAU), rowmod)
    return z, rowmod


def _chunk_terms(hd_ref, g_scr, c, p):
    r0 = c * CHUNK
    gc = slice(p * HDK, (p + 1) * HDK)
    qc, kc, _, _ = _head_cols(p)
    g = g_scr[r0:r0 + CHUNK, gc]
    g_first = g_scr[r0:r0 + 1, gc]
    g_last = g_scr[r0 + CHUNK - 1:r0 + CHUNK, gc]
    ref = 0.5 * (g_first + g_last)
    ep = jnp.exp(g - ref)
    em = jnp.exp(ref - g)
    a = jnp.exp(g)
    dl = jnp.exp(g_last - g)
    qs = hd_ref[r0:r0 + CHUNK, qc] * Q_SCALE
    k = hd_ref[r0:r0 + CHUNK, kc]
    return dict(ep=ep, em=em, a=a, dl=dl, egl=jnp.exp(g_last),
                qe1=qs * ep, ke1=k * em, qe2=qs * em, ke2=k * ep, qa=qs * a, kd=k * dl)


def _scores(t, lower):
    return jnp.where(lower, _dot(t["qe1"], t["ke1"], NT), _dot(t["qe2"], t["ke2"], NT))


def _gla_specs(tt, row):
    return [
        pl.BlockSpec((tt, HPB * HEAD_W), lambda h, i: (row(i), HD0 // (HPB * HEAD_W) + h)),
        pl.BlockSpec((tt, 128), lambda h, i: (row(i), AL0 // 128)),
        pl.BlockSpec((128, HPB * HDK), lambda h, i: (0, h)),
        pl.BlockSpec((1, HPB * HDK), lambda h, i: (0, h)),
        pl.BlockSpec((1, HPB * HDV), lambda h, i: (0, h)),
    ]


def _gla_fwd(hh, wup, b_alpha, gnorm, *, tt, ride=None):
    s = hh.shape[0]
    nt = s // tt
    nct = tt // CHUNK

    def body(hd_ref, al_ref, wup_ref, b_ref, gn_ref, o_ref, ya_ref, st_ref, state, g_scr):
        @pl.when(pl.program_id(1) == 0)
        def _():
            state[...] = jnp.zeros_like(state)

        _gla_decay(al_ref, wup_ref, b_ref, g_scr)
        lower = lax.broadcasted_iota(jnp.int32, (CHUNK, CHUNK), 0) >= lax.broadcasted_iota(jnp.int32, (CHUNK, CHUNK), 1)
        for c in range(nct):
            r0 = c * CHUNK
            for p in range(HPB):
                _, _, vc, _ = _head_cols(p)
                t = _chunk_terms(hd_ref, g_scr, c, p)
                v = hd_ref[r0:r0 + CHUNK, vc]
                st = state[p]
                st_ref[p, c] = st
                o_ref[r0:r0 + CHUNK, p * HDV:(p + 1) * HDV] = _dot(_scores(t, lower), v, NN) + _dot(t["qa"], st, NT)
                state[p] = st * t["egl"] + _dot(v, t["kd"], TN)
        for p in range(HPB):
            oc = slice(p * HDV, (p + 1) * HDV)
            o = o_ref[:, oc]
            ohat = o * lax.rsqrt(jnp.mean(o * o, axis=-1, keepdims=True) + EPS)
            ga = hd_ref[:, _head_cols(p)[3]]
            ya_ref[:, oc] = (ohat * gn_ref[:, oc] * (ga * _sigmoid(ga))).astype(ya_ref.dtype)

    return _call(
        body, name="gla_fwd", grid=(HEADS // HPB, nt),
        in_specs=_gla_specs(tt, lambda i: i),
        out_specs=[
            pl.BlockSpec((tt, HPB * HDV), lambda h, i: (i, h)),
            pl.BlockSpec((tt, HPB * HDV), lambda h, i: (i, h)),
            pl.BlockSpec((HPB, nct, HDV, HDK), lambda h, i: (h, i, 0, 0)),
        ],
        out_shape=[
            jax.ShapeDtypeStruct((s, D), F32),
            jax.ShapeDtypeStruct((s, D), MXU),
            jax.ShapeDtypeStruct((HEADS, s // CHUNK, HDV, HDK), F32),
        ],
        scratch=[pltpu.VMEM((HPB, HDV, HDK), F32), pltpu.VMEM((tt, HPB * HDK), F32)],
        args=(hh, hh, wup, b_alpha, gnorm), ride=ride)


def _gla_bwd(hh, wup, b_alpha, gnorm, o, states, dya, dh, *, tt, ride=None):
    s = hh.shape[0]
    nt = s // tt
    nct = tt // CHUNK

    def body(hd_ref, al_ref, wup_ref, b_ref, gn_ref, o_ref, st_ref, dya_ref, _dh_in,
             dh_ref, dz_ref, dgn_ref, db_ref, dstate, g_scr, dg_scr, do_scr):
        @pl.when(pl.program_id(1) == 0)
        def _():
            dstate[...] = jnp.zeros_like(dstate)
            dgn_ref[...] = jnp.zeros_like(dgn_ref)
            db_ref[...] = jnp.zeros_like(db_ref)

        z, rowmod = _gla_decay(al_ref, wup_ref, b_ref, g_scr)

        for p in range(HPB):
            oc = slice(p * HDV, (p + 1) * HDV)
            gac = _head_cols(p)[3]
            o_t = o_ref[:, oc]
            rstd = lax.rsqrt(jnp.mean(o_t * o_t, axis=-1, keepdims=True) + EPS)
            ohat = o_t * rstd
            ga = hd_ref[:, gac]
            sg = _sigmoid(ga)
            dya_t = dya_ref[:, oc]
            gn = gn_ref[:, oc]
            dh_ref[:, gac] = (dya_t * ohat * gn * (sg * (1.0 + ga * (1.0 - sg)))).astype(dh_ref.dtype)
            don = dya_t * (ga * sg)
            dgn_ref[p] += jnp.sum(don * ohat, axis=0, keepdims=True)
            dohat = don * gn
            do_scr[:, oc] = rstd * (dohat - ohat * jnp.mean(dohat * ohat, axis=-1, keepdims=True))

        lower = lax.broadcasted_iota(jnp.int32, (CHUNK, CHUNK), 0) >= lax.broadcasted_iota(jnp.int32, (CHUNK, CHUNK), 1)
        last_row = lax.broadcasted_iota(jnp.int32, (CHUNK, HDK), 0) == CHUNK - 1
        for c in range(nct - 1, -1, -1):
            r0 = c * CHUNK
            rows = slice(r0, r0 + CHUNK)
            for p in range(HPB):
                qc, kc, vc, _ = _head_cols(p)
                t = _chunk_terms(hd_ref, g_scr, c, p)
                v = hd_ref[rows, vc]
                do = do_scr[rows, p * HDV:(p + 1) * HDV]
                st = st_ref[p, c]
                dst = dstate[p]
                a = _scores(t, lower)
                da = _dot(do, v, NT)
                da1 = jnp.where(lower, da, 0.0)
                da2 = jnp.where(lower, 0.0, da)
                dqe1 = _dot(da1, t["ke1"], NN)
                dke1 = _dot(da1, t["qe1"], TN)
                dqe2 = _dot(da2, t["ke2"], NN)
                dke2 = _dot(da2, t["qe2"], TN)
                dqa = _dot(do, st, NN)
                dkd = _dot(v, dst, NN)
                dh_ref[rows, vc] = (_dot(a, do, TN) + _dot(t["kd"], dst, NT)).astype(dh_ref.dtype)
                dh_ref[rows, qc] = ((dqe1 * t["ep"] + dqe2 * t["em"] + dqa * t["a"]) * Q_SCALE).astype(dh_ref.dtype)
                dh_ref[rows, kc] = (dke1 * t["em"] + dke2 * t["ep"] + dkd * t["dl"]).astype(dh_ref.dtype)
                dkd_kd = dkd * t["kd"]
                dgl = jnp.sum(dkd_kd, axis=0, keepdims=True) + t["egl"] * jnp.sum(dst * st, axis=0, keepdims=True)
                dg = dqe1 * t["qe1"] - dke1 * t["ke1"] - dqe2 * t["qe2"] + dke2 * t["ke2"] + dqa * t["qa"] - dkd_kd
                dg_scr[rows, p * HDK:(p + 1) * HDK] = dg + jnp.where(last_row, dgl, 0.0)
                dstate[p] = dst * t["egl"] + _dot(do, t["qa"], TN)

        dz = _seg_rcumsum(dg_scr[...], rowmod) * _sigmoid(-z) * (1.0 / GATE_TAU)
        dz_ref[...] = dz.astype(dz_ref.dtype)
        for p in range(HPB):
            db_ref[p] += jnp.sum(dz[:, p * HDK:(p + 1) * HDK], axis=0, keepdims=True)

    rev = lambda i: nt - 1 - i
    in_specs = _gla_specs(tt, rev) + [
        pl.BlockSpec((tt, HPB * HDV), lambda h, i: (rev(i), h)),
        pl.BlockSpec((HPB, nct, HDV, HDK), lambda h, i: (h, rev(i), 0, 0)),
        pl.BlockSpec((tt, HPB * HDV), lambda h, i: (rev(i), h)),
        ANY,
    ]
    return _call(
        body, name="gla_bwd", grid=(HEADS // HPB, nt), in_specs=in_specs,
        out_specs=[
            pl.BlockSpec((tt, HPB * HEAD_W), lambda h, i: (rev(i), HD0 // (HPB * HEAD_W) + h)),
            pl.BlockSpec((tt, HPB * HDK), lambda h, i: (rev(i), h)),
            pl.BlockSpec((HPB, 1, HDV), lambda h, i: (h, 0, 0)),
            pl.BlockSpec((HPB, 1, HDK), lambda h, i: (h, 0, 0)),
        ],
        out_shape=[
            jax.ShapeDtypeStruct(dh.shape, dh.dtype),
            jax.ShapeDtypeStruct((s, DK), MXU),
            jax.ShapeDtypeStruct((HEADS, 1, HDV), F32),
            jax.ShapeDtypeStruct((HEADS, 1, HDK), F32),
        ],
        scratch=[pltpu.VMEM((HPB, HDV, HDK), F32), pltpu.VMEM((tt, HPB * HDK), F32), pltpu.VMEM((tt, HPB * HDK), F32),
                 pltpu.VMEM((tt, HPB * HDV), F32)],
        args=(hh, hh, wup, b_alpha, gnorm, o, states, dya, dh), ride=ride, aliases={8: 0})


def _window_count(tile, tt, w):
    pos = tile * tt + lax.broadcasted_iota(jnp.int32, (tt, PG), 0) + 1
    return jnp.minimum(pos, w).astype(F32)


def _pool_fwd(hh, wpool, scale, *, tt):
    s = hh.shape[0]
    nt = s // tt

    def body(ug_ref, w_ref, sc_ref, pooled_ref, mixed_ref, yb_ref, halo):
        i = pl.program_id(0)

        @pl.when(i == 0)
        def _():
            halo[...] = jnp.zeros_like(halo)

        for g, w in enumerate(POOL_WINDOWS):
            cols = slice(g * PG, (g + 1) * PG)
            u = ug_ref[:, cols]
            run = jnp.concatenate([halo[:, cols], u], axis=0)
            sh = 1
            while sh < w:
                run = run + pltpu.roll(run, sh, 0)
                sh *= 2
            pooled = run[HALO:, :] / _window_count(i, tt, w) - u
            pooled_ref[:, cols] = pooled.astype(pooled_ref.dtype)
            mixed = _dot(pooled, w_ref[g], NN)
            mixed_ref[:, cols] = mixed
            gb = ug_ref[:, slice(D + g * PG, D + (g + 1) * PG)]
            yb_ref[:, cols] = (mixed * sc_ref[:, cols] * (gb * _sigmoid(gb))).astype(yb_ref.dtype)
        halo[...] = ug_ref[tt - HALO:tt, :D]

    tile = pl.BlockSpec((tt, D), lambda i: (i, 0))
    return _call(
        body, name="pool_fwd", grid=(nt,),
        in_specs=[
            pl.BlockSpec((tt, 2 * D), lambda i: (i, PI0 // (2 * D))),
            pl.BlockSpec((len(POOL_WINDOWS), PG, PG), lambda i: (0, 0, 0)),
            pl.BlockSpec((1, D), lambda i: (0, 0)),
        ],
        out_specs=[tile] * 3,
        out_shape=[jax.ShapeDtypeStruct((s, D), MXU), jax.ShapeDtypeStruct((s, D), F32), jax.ShapeDtypeStruct((s, D), MXU)],
        scratch=[pltpu.VMEM((HALO, D), F32)], args=(hh, wpool, scale))


def _pool_bwd(hh, wpool, scale, pooled, mixed, dyb, dh, *, tt):
    s = hh.shape[0]
    nt = s // tt

    def body(ug_ref, w_ref, sc_ref, pooled_ref, mixed_ref, dyb_ref, _dh_in, dh_ref, dw_ref, dsc_ref, halo):
        i = pl.program_id(0)
        tile = nt - 1 - i

        @pl.when(i == 0)
        def _():
            halo[...] = jnp.zeros_like(halo)
            dw_ref[...] = jnp.zeros_like(dw_ref)
            dsc_ref[...] = jnp.zeros_like(dsc_ref)

        for g, w in enumerate(POOL_WINDOWS):
            cols = slice(g * PG, (g + 1) * PG)
            gcols = slice(D + g * PG, D + (g + 1) * PG)
            gb = ug_ref[:, gcols]
            sg = _sigmoid(gb)
            mixed = mixed_ref[:, cols]
            sc = sc_ref[:, cols]
            dyb = dyb_ref[:, cols]
            dh_ref[:, gcols] = (dyb * mixed * sc * (sg * (1.0 + gb * (1.0 - sg)))).astype(dh_ref.dtype)
            dms = dyb * (gb * sg)
            dsc_ref[:, cols] += jnp.sum(dms * mixed, axis=0, keepdims=True)
            dmixed = dms * sc
            dpooled = _dot(dmixed, w_ref[g], NT)
            dw_ref[g] += _dot(pooled_ref[:, cols], dmixed, TN)
            e = dpooled / _window_count(tile, tt, w)
            run = jnp.concatenate([e, halo[:, cols]], axis=0)
            sh = 1
            while sh < w:
                run = run + pltpu.roll(run, tt + HALO - sh, 0)
                sh *= 2
            dh_ref[:, cols] = (run[:tt, :] - dpooled).astype(dh_ref.dtype)
            halo[:, cols] = e[:HALO, :]

    rev = lambda i: nt - 1 - i
    tile = pl.BlockSpec((tt, D), lambda i: (rev(i), 0))
    wspec = pl.BlockSpec((len(POOL_WINDOWS), PG, PG), lambda i: (0, 0, 0))
    vec = pl.BlockSpec((1, D), lambda i: (0, 0))
    return _call(
        body, name="pool_bwd", grid=(nt,),
        in_specs=[pl.BlockSpec((tt, 2 * D), lambda i: (rev(i), PI0 // (2 * D))), wspec, vec, tile, tile, tile, ANY],
        out_specs=[pl.BlockSpec((tt, 2 * D), lambda i: (rev(i), PI0 // (2 * D))), wspec, vec],
        out_shape=[jax.ShapeDtypeStruct(dh.shape, dh.dtype), jax.ShapeDtypeStruct((len(POOL_WINDOWS), PG, PG), F32),
                   jax.ShapeDtypeStruct((1, D), F32)],
        scratch=[pltpu.VMEM((HALO, D), F32)], args=(hh, wpool, scale, pooled, mixed, dyb, dh), aliases={6: 0})


def _merge_fwd(hh, x, ya, yb, wpa, wpb, wout, b_merge, ln_g, ln_b, *, tt):
    s = x.shape[0]

    def body(ml_ref, x_ref, ya_ref, yb_ref, wpa_ref, wpb_ref, wout_ref, bm_ref, g_ref, b_ref,
             pa_ref, pb_ref, r_ref, xn_ref, xnb_ref):
        pa = _dot(ya_ref[...], wpa_ref[...], NN)
        pb = _dot(yb_ref[...], wpb_ref[...], NN)
        pa_ref[...] = pa
        pb_ref[...] = pb
        merged = _sigmoid(ml_ref[:, :D] + bm_ref[:, :D]) * pa + _sigmoid(ml_ref[:, D:] + bm_ref[:, D:]) * pb
        r = ALPHA * x_ref[...] + _dot(merged, wout_ref[...], NN)
        r_ref[...] = r
        mu = jnp.mean(r, axis=-1, keepdims=True)
        xc = r - mu
        var = jnp.mean(xc * xc, axis=-1, keepdims=True)
        xn = xc * lax.rsqrt(var + EPS) * g_ref[...] + b_ref[...]
        xn_ref[...] = xn
        xnb_ref[...] = xn.astype(xnb_ref.dtype)

    tile = pl.BlockSpec((tt, D), lambda i: (i, 0))
    full = pl.BlockSpec((D, D), lambda i: (0, 0))
    vec = pl.BlockSpec((1, D), lambda i: (0, 0))
    return _call(
        body, name="merge_fwd", grid=(s // tt,),
        in_specs=[pl.BlockSpec((tt, 2 * D), lambda i: (i, ML0 // (2 * D))), tile, tile, tile, full, full, full,
                  pl.BlockSpec((1, 2 * D), lambda i: (0, 0)), vec, vec],
        out_specs=[tile] * 5, out_shape=[jax.ShapeDtypeStruct((s, D), F32)] * 4 + [jax.ShapeDtypeStruct((s, D), MXU)],
        args=(hh, x, ya, yb, wpa, wpb, wout, b_merge, ln_g, ln_b), sem=("parallel",), vmem=VMEM_BIG)


def _merge_bwd(hh, r, pa, pb, dout, wout, b_merge, ln_g, *, tt):
    s = r.shape[0]

    def body(ml_ref, r_ref, pa_ref, pb_ref, do_ref, wout_ref, bm_ref, g_ref,
             dh_ref, dr_ref, dpa_ref, dpb_ref, dwout_ref, dg_ref, db_ref, dbm_ref):
        @pl.when(pl.program_id(0) == 0)
        def _():
            dwout_ref[...] = jnp.zeros_like(dwout_ref)
            dg_ref[...] = jnp.zeros_like(dg_ref)
            db_ref[...] = jnp.zeros_like(db_ref)
            dbm_ref[...] = jnp.zeros_like(dbm_ref)

        rr = r_ref[...]
        mu = jnp.mean(rr, axis=-1, keepdims=True)
        xc = rr - mu
        rstd = lax.rsqrt(jnp.mean(xc * xc, axis=-1, keepdims=True) + EPS)
        xhat = xc * rstd
        do = do_ref[...]
        dg_ref[...] += jnp.sum(do * xhat, axis=0, keepdims=True)
        db_ref[...] += jnp.sum(do, axis=0, keepdims=True)
        dxh = do * g_ref[...]
        dr = rstd * (dxh - jnp.mean(dxh, axis=-1, keepdims=True) - xhat * jnp.mean(dxh * xhat, axis=-1, keepdims=True))
        dr_ref[...] = dr
        g_a = _sigmoid(ml_ref[:, :D] + bm_ref[:, :D])
        g_b = _sigmoid(ml_ref[:, D:] + bm_ref[:, D:])
        pa = pa_ref[...]
        pb = pb_ref[...]
        dwout_ref[...] += _dot(g_a * pa + g_b * pb, dr, TN)
        dm = _dot(dr, wout_ref[...], NT)
        dpa_ref[...] = (dm * g_a).astype(dpa_ref.dtype)
        dpb_ref[...] = (dm * g_b).astype(dpb_ref.dtype)
        dml_a = dm * pa * g_a * (1.0 - g_a)
        dml_b = dm * pb * g_b * (1.0 - g_b)
        dh_ref[:, :D] = dml_a.astype(dh_ref.dtype)
        dh_ref[:, D:] = dml_b.astype(dh_ref.dtype)
        dbm_ref[:, :D] += jnp.sum(dml_a, axis=0, keepdims=True)
        dbm_ref[:, D:] += jnp.sum(dml_b, axis=0, keepdims=True)

    tile = pl.BlockSpec((tt, D), lambda i: (i, 0))
    full = pl.BlockSpec((D, D), lambda i: (0, 0))
    vec = pl.BlockSpec((1, D), lambda i: (0, 0))
    vec2 = pl.BlockSpec((1, 2 * D), lambda i: (0, 0))
    mlb = pl.BlockSpec((tt, 2 * D), lambda i: (i, ML0 // (2 * D)))
    return _call(
        body, name="merge_bwd", grid=(s // tt,),
        in_specs=[mlb, tile, tile, tile, tile, full, vec2, vec],
        out_specs=[mlb, tile, tile, tile, full, vec, vec, vec2],
        out_shape=[
            jax.ShapeDtypeStruct((s, HP), MXU), jax.ShapeDtypeStruct((s, D), F32),
            jax.ShapeDtypeStruct((s, D), MXU), jax.ShapeDtypeStruct((s, D), MXU),
            jax.ShapeDtypeStruct((D, D), F32), jax.ShapeDtypeStruct((1, D), F32),
            jax.ShapeDtypeStruct((1, D), F32), jax.ShapeDtypeStruct((1, 2 * D), F32),
        ],
        args=(hh, r, pa, pb, dout, wout, b_merge, ln_g), vmem=VMEM_BIG)


def _proj_bwd(y, dp, w, *, tt, name):
    s = y.shape[0]

    def body(y_ref, dp_ref, w_ref, dy_ref, dw_ref):
        @pl.when(pl.program_id(0) == 0)
        def _():
            dw_ref[...] = jnp.zeros_like(dw_ref)

        dp = dp_ref[...]
        dy_ref[...] = _dot(dp, w_ref[...], NT)
        dw_ref[...] += _dot(y_ref[...], dp, TN)

    tile = pl.BlockSpec((tt, D), lambda i: (i, 0))
    full = pl.BlockSpec((D, D), lambda i: (0, 0))
    return _call(
        body, name=name, grid=(s // tt,), in_specs=[tile, tile, full], out_specs=[tile, full],
        out_shape=[jax.ShapeDtypeStruct((s, D), F32), jax.ShapeDtypeStruct((D, D), F32)], args=(y, dp, w))


def _loss_head(y, target, *, tt):
    s = y.shape[0]

    def body(y_ref, t_ref, loss_ref, dy_ref):
        @pl.when(pl.program_id(0) == 0)
        def _():
            loss_ref[...] = jnp.zeros_like(loss_ref)

        err = y_ref[...] - t_ref[...]
        dy_ref[...] = err * (1.0 / D)
        per_tok = jnp.mean(err * err, axis=-1, keepdims=True)
        loss_ref[...] += 0.5 * jnp.sum(per_tok, axis=0, keepdims=True)

    tile = pl.BlockSpec((tt, D), lambda i: (i, 0))
    return _call(
        body, name="loss_head", grid=(s // tt,), in_specs=[tile, tile],
        out_specs=[pl.BlockSpec((1, 1), lambda i: (0, 0)), tile],
        out_shape=[jax.ShapeDtypeStruct((1, 1), F32), jax.ShapeDtypeStruct((s, D), F32)], args=(y, target))


def _adamw(parts, w, m, v, *, tr, tc, name):
    nl, rows, cols = w.shape

    def body(p_ref, w_ref, m_ref, v_ref, g_ref, d_ref, nm_ref, nv_ref):
        g = p_ref[0, 0].astype(F32)
        for q in range(1, N_DEV):
            g = g + p_ref[0, q].astype(F32)
        g_ref[0] = g
        nm = ADAM_B1 * m_ref[0] + (1.0 - ADAM_B1) * g
        nv = ADAM_B2 * v_ref[0] + (1.0 - ADAM_B2) * (g * g)
        nm_ref[0] = nm
        nv_ref[0] = nv
        m_hat = nm / (1.0 - ADAM_B1 ** ADAM_STEP)
        v_hat = nv / (1.0 - ADAM_B2 ** ADAM_STEP)
        d_ref[0] = -ADAM_LR * (m_hat / (jnp.sqrt(v_hat) + ADAM_EPS) + ADAM_WD * w_ref[0])

    tile = pl.BlockSpec((1, tr, tc), lambda l, i, j: (l, i, j))
    return _call(
        body, name=name, grid=(nl, rows // tr, cols // tc),
        in_specs=[pl.BlockSpec((1, N_DEV, tr, tc), lambda l, i, j: (l, 0, i, j)), tile, tile, tile],
        out_specs=[tile] * 4, out_shape=[jax.ShapeDtypeStruct((nl, rows, cols), F32)] * 4,
        args=(parts, w, m, v), sem=("parallel", "parallel", "parallel"))


def _from_devices(g, axis):
    nd = g.ndim - 1
    perm = list(range(1, axis + 1)) + [0] + list(range(axis + 1, nd + 1))
    shape = list(g.shape[1:])
    shape[axis] *= N_DEV
    return jnp.transpose(g, perm).reshape(shape)


def _to_devices(a, axis):
    shape = list(a.shape)
    t = a.reshape(shape[:axis] + [N_DEV, shape[axis] // N_DEV] + shape[axis + 1:])
    return jnp.transpose(t, [axis] + list(range(0, axis)) + list(range(axis + 1, t.ndim)))


def _h_weight_t(parts):
    w = parts.reshape(IN_COLS, D)
    heads = [w[sl] for h in range(HEADS) for sl in (
        slice(O_Q + h * HDK, O_Q + (h + 1) * HDK), slice(O_K + h * HDK, O_K + (h + 1) * HDK),
        slice(O_V + h * HDV, O_V + (h + 1) * HDV), slice(O_GA + h * HDV, O_GA + (h + 1) * HDV))]
    return jnp.concatenate([w[O_PI:O_GB], w[O_GB:O_ML], w[O_ML:], w[O_AL:O_PI], jnp.zeros((AL_W - RANK, D), w.dtype)] + heads,
                           axis=0)


def _w_in_grad_parts_t(dwt):
    hd = lambda h, a, b: dwt[HD0 + h * HEAD_W + a:HD0 + h * HEAD_W + b]
    g = jnp.concatenate([hd(h, 0, HDK) for h in range(HEADS)] + [hd(h, HDK, 2 * HDK) for h in range(HEADS)]
                        + [hd(h, 2 * HDK, 2 * HDK + HDV) for h in range(HEADS)] + [hd(h, 2 * HDK + HDV, HEAD_W) for h in range(HEADS)]
                        + [dwt[AL0:AL0 + RANK], dwt[PI0:GB0], dwt[GB0:ML0], dwt[ML0:AL0]], axis=0)
    return g.astype(WIRE).reshape(N_DEV, SHARD, D)


def kernel(x, w_in, w_alpha_up, b_alpha, gla_norm_g, w_pool_grp, pool_scale, b_merge, w_proj_a, w_proj_b, w_out, ln_g, ln_b, loss_target, m_w_in, m_w_alpha_up, m_b_alpha, m_gla_norm_g, m_w_pool_grp, m_pool_scale, m_b_merge, m_w_proj_a, m_w_proj_b, m_w_out, m_ln_g, m_ln_b, v_w_in, v_w_alpha_up, v_b_alpha, v_gla_norm_g, v_w_pool_grp, v_pool_scale, v_b_merge, v_w_proj_a, v_w_proj_b, v_w_out, v_ln_g, v_ln_b):
    s = x.shape[1]
    tt = min(256, s)
    tm = min(512, s)
    tn = HP // 3
    xs = x.reshape(s, D)

    tr3 = lambda a: jnp.transpose(a, (0, 2, 1))
    w_in_s = tr3(w_in).astype(WIRE)
    proj_s = jnp.stack([w_proj_a, w_proj_b, w_out], axis=1).astype(WIRE)
    pool_s = w_pool_grp.astype(WIRE)

    g_in, g_proj, g_pool, g_up, g_gn = _exchange(
        [(w_in_s[0], True), (proj_s[0], True), (pool_s[0], True), (w_alpha_up.astype(WIRE), True), (gla_norm_g, True)],
        name="gather_first")
    wup = jnp.pad(_from_devices(g_up, 2), ((0, 0), (0, AL_W - RANK), (0, 0)))
    gn = _from_devices(g_gn, 2).reshape(DEPTH, 1, D)

    saved, wt_all, proj_all, pool_all = [], [], [], []
    cur, cur_b = xs, xs.astype(MXU)
    for l in range(DEPTH):
        wt = _h_weight_t(g_in)
        proj = _from_devices(g_proj, 1)
        pool = _from_devices(g_pool, 1)
        wt_all.append(wt), proj_all.append(proj), pool_all.append(pool)
        nxt_l = l + 1 < DEPTH
        res = _in_proj(cur_b, wt, tm=tm, tn=tn, ride=_Exchange([(w_in_s[l + 1], True)]) if nxt_l else None)
        hh = res[0]
        if nxt_l:
            g_in = res[1]
        res = _gla_fwd(hh, wup[l], b_alpha[l:l + 1], gn[l], tt=tt,
                       ride=_Exchange([(proj_s[l + 1], True), (pool_s[l + 1], True)]) if nxt_l else None)
        o, ya, states = res[:3]
        if nxt_l:
            g_proj, g_pool = res[3:]
        pooled, mixed, yb = _pool_fwd(hh, pool, pool_scale[l:l + 1], tt=tt)
        pa, pb, r, nxt, nxt_b = _merge_fwd(hh, cur, ya, yb, proj[0], proj[1], proj[2],
                                           b_merge[l:l + 1], ln_g[l:l + 1], ln_b[l:l + 1], tt=tt)
        saved.append(dict(xb=cur_b, hh=hh, o=o, ya=ya, states=states, pooled=pooled, mixed=mixed, yb=yb, pa=pa, pb=pb, r=r))
        cur, cur_b = nxt, nxt_b

    loss_part, dcur = _loss_head(cur, loss_target.reshape(s, D), tt=tt)
    loss = lax.psum(loss_part[0, 0], ("x", "y", "c"))

    small = {k: [None] * DEPTH for k in ("w_up", "b_alpha", "gnorm", "pool_scale", "b_merge", "ln_g", "ln_b")}
    parts = {k: [None] * DEPTH for k in ("w_in", "proj", "pool")}
    for l in range(DEPTH - 1, -1, -1):
        sv = saved[l]
        hh = sv["hh"]
        dh, dr, dpa, dpb, dw_out, dln_g, dln_b, db_merge = _merge_bwd(
            hh, sv["r"], sv["pa"], sv["pb"], dcur, proj_all[l][2], b_merge[l:l + 1], ln_g[l:l + 1], tt=tt)
        dya, dw_pa = _proj_bwd(sv["ya"], dpa, proj_all[l][0], tt=tt, name="proj_a_bwd")
        dyb, dw_pb = _proj_bwd(sv["yb"], dpb, proj_all[l][1], tt=tt, name="proj_b_bwd")
        dh, dw_pool, dscale = _pool_bwd(hh, pool_all[l], pool_scale[l:l + 1], sv["pooled"], sv["mixed"], dyb, dh, tt=tt)
        ride = _Exchange([(_to_devices(jnp.stack([dw_pa, dw_pb, dw_out]), 1).astype(WIRE), False),
                          (_to_devices(dw_pool, 1).astype(WIRE), False)])
        dh, dz, dgn, db_al, parts["proj"][l], parts["pool"][l] = _gla_bwd(
            hh, wup[l], b_alpha[l:l + 1], gn[l], sv["o"], sv["states"], dya, dh, tt=tt, ride=ride)
        dh = _mm_nt_into(dz, wup[l], dh, AL0 // AL_W, tm=tm, name="alpha_bwd")
        dw_up = _mm_tn(hh, dz, tm=tm, tk=AL_W, name="w_up_grad", a_block=(AL_W, AL0 // AL_W))
        dwt = _mm_tn(dh, sv["xb"], tm=tm, tk=tn, name="w_in_grad")
        dcur, parts["w_in"][l] = _in_proj_bwd(dh, wt_all[l], dr, tm=tt, ride=_Exchange([(_w_in_grad_parts_t(dwt), False)]))

        small["w_up"][l] = dw_up[:RANK]
        small["b_alpha"][l] = db_al.reshape(DK)
        small["gnorm"][l] = dgn.reshape(HEADS, HDV)
        small["pool_scale"][l] = dscale[0]
        small["b_merge"][l] = db_merge[0]
        small["ln_g"][l], small["ln_b"][l] = dln_g[0], dln_b[0]
    grad_x = dcur[None]
    sm = {k: jnp.stack(v) for k, v in small.items()}

    rep = (("b_alpha", b_alpha, m_b_alpha, v_b_alpha), ("pool_scale", pool_scale, m_pool_scale, v_pool_scale),
           ("b_merge", b_merge, m_b_merge, v_b_merge), ("ln_g", ln_g, m_ln_g, v_ln_g), ("ln_b", ln_b, m_ln_b, v_ln_b))
    cat = lambda arrs: jnp.concatenate(arrs, axis=1)
    p_up, p_gn, p_rep = _exchange([(_to_devices(sm["w_up"], 2), False), (_to_devices(sm["gnorm"], 2), False),
                                   (cat([sm[nm] for nm, _, _, _ in rep]), True)], name="exchange_small_grads")

    def update(p, w, m, v, tr, name, layered=True, tc=None):
        shape = w.shape
        nl = shape[0] if layered else 1
        cols = shape[-1]
        flat = lambda a: a.reshape(nl, -1, cols)
        outs = _adamw(p.reshape(nl, N_DEV, -1, cols), flat(w), flat(m), flat(v), tr=tr, tc=tc or cols, name=name)
        return [o_.reshape(shape) for o_ in outs]

    res = {}
    res["w_in"] = [tr3(o_) for o_ in update(jnp.stack(parts["w_in"]), tr3(w_in), tr3(m_w_in), tr3(v_w_in), SHARD, "adamw_w_in", tc=256)]
    proj_p = jnp.stack(parts["proj"])
    for j, (nm, w, m, v) in enumerate((("w_proj_a", w_proj_a, m_w_proj_a, v_w_proj_a), ("w_proj_b", w_proj_b, m_w_proj_b, v_w_proj_b),
                                       ("w_out", w_out, m_w_out, v_w_out))):
        res[nm] = update(proj_p[:, :, j], w, m, v, D // N_DEV, "adamw_" + nm)
    res["w_pool_grp"] = update(jnp.stack(parts["pool"]), w_pool_grp, m_w_pool_grp, v_w_pool_grp, 128, "adamw_w_pool")
    res["w_alpha_up"] = update(p_up, w_alpha_up, m_w_alpha_up, v_w_alpha_up, DEPTH * RANK, "adamw_w_up", layered=False)
    res["gla_norm_g"] = update(p_gn, gla_norm_g, m_gla_norm_g, v_gla_norm_g, DEPTH * HEADS, "adamw_gnorm", layered=False)
    rep_out = update(p_rep, cat([w for _, w, _, _ in rep]), cat([m for _, _, m, _ in rep]), cat([v for _, _, _, v in rep]),
                     DEPTH, "adamw_small", layered=False)
    off = 0
    for nm, w, _, _ in rep:
        n = w.shape[1]
        res[nm] = [o_[:, off:off + n] for o_ in rep_out]
        off += n

    order = ("w_in", "w_alpha_up", "b_alpha", "gla_norm_g", "w_pool_grp", "pool_scale", "b_merge", "w_proj_a", "w_proj_b",
             "w_out", "ln_g", "ln_b")
    return (loss, grad_x, *[res[n][0] for n in order], *[res[n][1] for n in order],
            *[res[n][2] for n in order], *[res[n][3] for n in order])
```

```python
import jax
import jax.numpy as jnp
from jax import lax
from jax.experimental import pallas as pl
from jax.experimental.pallas import tpu as pltpu

F32 = jnp.float32
MXU = jnp.bfloat16
WIRE = jnp.bfloat16

N_DEV = 8
DEPTH = 4
D = 1024
HEADS = 4
DK = D // 2
HDK = DK // HEADS
HDV = D // HEADS
RANK = 16
CHUNK = 64
GATE_TAU = 16.0
POOL_WINDOWS = (2, 4, 8, 16)
PG = D // len(POOL_WINDOWS)
HALO = 16
IN_COLS = 7184
SHARD = IN_COLS // N_DEV
ALPHA = (2.0 * DEPTH) ** 0.25
EPS = 1e-5
Q_SCALE = HDK ** -0.5

ADAM_LR, ADAM_B1, ADAM_B2, ADAM_EPS, ADAM_WD, ADAM_STEP = 0.001, 0.9, 0.999, 1e-08, 0.01, 10

PI0, GB0, ML0, AL0, AL_W = 0, D, 2 * D, 4 * D, 512
HD0 = AL0 + AL_W
HEAD_W = 2 * HDK + 2 * HDV
HP = HD0 + HEADS * HEAD_W
HPB = 2
O_Q, O_K, O_V, O_GA, O_AL, O_PI, O_GB, O_ML = 0, DK, 2 * DK, 2 * DK + D, 2 * DK + 2 * D, 2 * DK + 2 * D + RANK, \
    2 * DK + 3 * D + RANK, 2 * DK + 4 * D + RANK

VMEM_BIG = 56 * 1024 * 1024
VMEM_MID = 40 * 1024 * 1024

NN = ((1,), (0,))
NT = ((1,), (1,))
TN = ((0,), (0,))

HBM = pl.BlockSpec(memory_space=pltpu.HBM)
ANY = pl.BlockSpec(memory_space=pl.ANY)


def _dot(a, b, dims):
    return lax.dot_general(a.astype(MXU), b.astype(MXU), (dims, ((), ())), preferred_element_type=F32)


def _params(sem, vmem):
    return pltpu.CompilerParams(dimension_semantics=sem, vmem_limit_bytes=vmem)


def _sigmoid(x):
    return 1.0 / (1.0 + jnp.exp(-x))


def _log_sigmoid(z):
    return jnp.minimum(z, 0.0) - jnp.log(1.0 + jnp.exp(-jnp.abs(z)))


class _Exchange:
    def __init__(self, items):
        self.items = [(s, bool(g)) for s, g in items]
        self.n = len(self.items)
        self.srcs = [s for s, _ in self.items]
        self.in_specs = [HBM] * self.n
        self.out_specs = [HBM] * self.n
        self.out_shape = [jax.ShapeDtypeStruct((N_DEV,) + tuple(s.shape if g else s.shape[1:]), s.dtype) for s, g in self.items]
        self.scratch = [pltpu.SemaphoreType.DMA((self.n * (N_DEV - 1),)), pltpu.SemaphoreType.DMA((self.n * (N_DEV - 1),)),
                        pltpu.SemaphoreType.DMA((self.n,))]

    def copies(self, src_refs, out_refs, send_sems, recv_sems, local_sems):
        x, y, c = lax.axis_index("x"), lax.axis_index("y"), lax.axis_index("c")
        me = 4 * x + 2 * y + c
        copies = []
        for t, (_, gather) in enumerate(self.items):
            src_ref, out_ref = src_refs[t], out_refs[t]
            copies.append(pltpu.make_async_copy(src_ref if gather else src_ref.at[me], out_ref.at[me], local_sems.at[t]))
            for k in range(1, N_DEV):
                px = 1 - x if k & 4 else x
                py = 1 - y if k & 2 else y
                pc = 1 - c if k & 1 else c
                peer = 4 * px + 2 * py + pc
                sem = t * (N_DEV - 1) + k - 1
                copies.append(pltpu.make_async_remote_copy(
                    src_ref=src_ref if gather else src_ref.at[peer],
                    dst_ref=out_ref.at[me],
                    send_sem=send_sems.at[sem],
                    recv_sem=recv_sems.at[sem],
                    device_id=(px, py, pc),
                    device_id_type=pl.DeviceIdType.MESH,
                ))
        return copies


def _grid_ends(grid):
    first = last = None
    for a, n in enumerate(grid):
        f = pl.program_id(a) == 0
        e = pl.program_id(a) == n - 1
        first = f if first is None else first & f
        last = e if last is None else last & e
    return first, last


def _call(body, *, name, grid, in_specs, out_specs, out_shape, args, scratch=(), sem=None, vmem=VMEM_MID, ride=None, aliases=None):
    n_in, n_out, n_scr = len(in_specs), len(out_specs), len(scratch)
    sem = sem or ("arbitrary",) * len(grid)
    if ride is None:
        return pl.pallas_call(body, name=name, grid=grid, in_specs=in_specs, out_specs=out_specs, out_shape=out_shape,
                              scratch_shapes=list(scratch), compiler_params=_params(sem, vmem),
                              input_output_aliases=aliases or {})(*args)
    r = ride.n

    def riding(*refs):
        ins, rsrc = refs[:n_in], refs[n_in:n_in + r]
        outs, rout = refs[n_in + r:n_in + r + n_out], refs[n_in + r + n_out:n_in + 2 * r + n_out]
        scr = refs[n_in + 2 * r + n_out:n_in + 2 * r + n_out + n_scr]
        send_sems, recv_sems, local_sems = refs[n_in + 2 * r + n_out + n_scr:]
        first, last = _grid_ends(grid)
        copies = ride.copies(rsrc, rout, send_sems, recv_sems, local_sems)

        @pl.when(first)
        def _():
            for cp in copies:
                cp.start()

        body(*ins, *outs, *scr)

        @pl.when(last)
        def _():
            for cp in copies:
                cp.wait()

    return pl.pallas_call(riding, name=name, grid=grid, in_specs=list(in_specs) + ride.in_specs,
                          out_specs=list(out_specs) + ride.out_specs, out_shape=list(out_shape) + ride.out_shape,
                          scratch_shapes=list(scratch) + ride.scratch,
                          compiler_params=_params(("arbitrary",) * len(grid), vmem),
                          input_output_aliases=aliases or {})(*args, *ride.srcs)


def _exchange(items, *, name):
    ex = _Exchange(items)

    def body(*refs):
        copies = ex.copies(refs[:ex.n], refs[ex.n:2 * ex.n], *refs[2 * ex.n:])
        for cp in copies:
            cp.start()
        for cp in copies:
            cp.wait()

    return pl.pallas_call(body, name=name, in_specs=ex.in_specs, out_specs=ex.out_specs, out_shape=ex.out_shape,
                          scratch_shapes=ex.scratch)(*ex.srcs)


def _gather_first(big, smalls, *, name):
    ex = _Exchange([(a, True) for a in smalls])

    def body(*refs):
        big_ref, small_src = refs[0], refs[1:1 + ex.n]
        out_ref, small_out = refs[1 + ex.n], refs[2 + ex.n:2 + 2 * ex.n]
        send_sems, recv_sems, local_sem = refs[2 + 2 * ex.n:5 + 2 * ex.n]
        x, y, c = lax.axis_index("x"), lax.axis_index("y"), lax.axis_index("c")
        me, sibling = (x, y, c), (x, y, 1 - c)
        chips = [(1 - x, y), (x, 1 - y), (1 - x, 1 - y)]

        def slot(px, py, pc):
            return out_ref.at[4 * px + 2 * py + pc]

        def copy(k, block, to, src=None):
            return pltpu.make_async_remote_copy(
                src_ref=slot(*block) if src is None else src, dst_ref=slot(*block),
                send_sem=send_sems.at[k], recv_sem=recv_sems.at[k], device_id=to, device_id_type=pl.DeviceIdType.MESH)

        small = ex.copies(small_src, small_out, *refs[5 + 2 * ex.n:])
        mine = pltpu.make_async_copy(big_ref, slot(*me), local_sem)
        mine.start()
        first = [copy(0, me, sibling, src=big_ref)] + [copy(1 + j, me, (*chip, c), src=big_ref) for j, chip in enumerate(chips)]
        for cp in first + small:
            cp.start()
        passed = [copy(4 + j, (*chip, c), sibling) for j, chip in enumerate(chips)]
        for j, chip in enumerate(chips):
            copy(1 + j, (*chip, c), me).wait_recv()
            passed[j].start()
        copy(0, sibling, me).wait_recv()
        for j, chip in enumerate(chips):
            copy(4 + j, (*chip, 1 - c), me).wait_recv()
        for cp in first + passed:
            cp.wait_send()
        mine.wait()
        for cp in small:
            cp.wait()

    return pl.pallas_call(
        body, name=name, in_specs=[HBM] + ex.in_specs, out_specs=[HBM] + ex.out_specs,
        out_shape=[jax.ShapeDtypeStruct((N_DEV,) + tuple(big.shape), big.dtype)] + ex.out_shape,
        scratch_shapes=[pltpu.SemaphoreType.DMA((N_DEV - 1,)), pltpu.SemaphoreType.DMA((N_DEV - 1,)), pltpu.SemaphoreType.DMA]
        + ex.scratch)(big, *ex.srcs)


def _move_rows(src, segs, out_rows, *, name, zero=None):
    cols = src.shape[1]

    def body(src_ref, out_ref, sems, *zbuf):
        copies = [pltpu.make_async_copy(src_ref.at[pl.ds(s0, n)], out_ref.at[pl.ds(d0, n)], sems.at[i])
                  for i, (s0, d0, n) in enumerate(segs)]
        if zero is not None:
            zbuf[0][...] = jnp.zeros_like(zbuf[0])
            copies.append(pltpu.make_async_copy(zbuf[0], out_ref.at[pl.ds(zero[0], zero[1])], sems.at[len(segs)]))
        for cp in copies:
            cp.start()
        for cp in copies:
            cp.wait()

    return pl.pallas_call(
        body, name=name, in_specs=[HBM], out_specs=HBM, out_shape=jax.ShapeDtypeStruct((out_rows, cols), src.dtype),
        scratch_shapes=[pltpu.SemaphoreType.DMA((len(segs) + 1,))]
        + ([pltpu.VMEM((zero[1], cols), src.dtype)] if zero is not None else []))(src)


def _in_proj(xb, wt, *, tm, tn, ride=None):
    m, k = xb.shape
    n = wt.shape[0]

    def body(x_ref, w_ref, o_ref):
        o_ref[...] = _dot(x_ref[...], w_ref[...], NT)

    return _call(
        body, name="in_proj", grid=(n // tn, m // tm),
        in_specs=[pl.BlockSpec((tm, k), lambda j, i: (i, 0)), pl.BlockSpec((tn, k), lambda j, i: (j, 0))],
        out_specs=[pl.BlockSpec((tm, tn), lambda j, i: (i, j))],
        out_shape=[jax.ShapeDtypeStruct((m, n), F32)],
        args=(xb, wt), sem=("parallel", "parallel"), vmem=VMEM_BIG, ride=ride)


def _in_proj_bwd(dh, wt, dr, *, tm, ride=None):
    m, n = dh.shape
    k = wt.shape[1]

    def body(dh_ref, w_ref, dr_ref, o_ref):
        o_ref[...] = ALPHA * dr_ref[...] + _dot(dh_ref[...], w_ref[...], NN)

    return _call(
        body, name="in_proj_bwd", grid=(m // tm,),
        in_specs=[pl.BlockSpec((tm, n), lambda i: (i, 0)), pl.BlockSpec((n, k), lambda i: (0, 0)),
                  pl.BlockSpec((tm, k), lambda i: (i, 0))],
        out_specs=[pl.BlockSpec((tm, k), lambda i: (i, 0))],
        out_shape=[jax.ShapeDtypeStruct((m, k), F32)],
        args=(dh, wt, dr), sem=("parallel",), vmem=VMEM_BIG, ride=ride)


def _mm_nt_into(dc, w, arr, col, *, tm, name):
    m, n = dc.shape
    k = w.shape[0]

    def body(dc_ref, w_ref, _arr_in, o_ref):
        o_ref[...] = _dot(dc_ref[...], w_ref[...], NT).astype(o_ref.dtype)

    return _call(
        body, name=name, grid=(m // tm,),
        in_specs=[pl.BlockSpec((tm, n), lambda i: (i, 0)), pl.BlockSpec((k, n), lambda i: (0, 0)), ANY],
        out_specs=[pl.BlockSpec((tm, k), lambda i: (i, col))],
        out_shape=[jax.ShapeDtypeStruct(arr.shape, arr.dtype)],
        args=(dc, w, arr), sem=("parallel",), aliases={2: 0})[0]


def _mm_tn(a, dc, *, tm, tk, name, a_block=None, out_dtype=F32):
    m = a.shape[0]
    k, a_col = (a.shape[1], None) if a_block is None else a_block
    n = dc.shape[1]
    ni = m // tm

    def body(a_ref, dc_ref, o_ref, acc):
        i = pl.program_id(1)

        @pl.when(i == 0)
        def _():
            acc[...] = jnp.zeros_like(acc)

        acc[...] += _dot(a_ref[...], dc_ref[...], TN)

        @pl.when(i == ni - 1)
        def _():
            o_ref[...] = acc[...].astype(o_ref.dtype)

    a_map = (lambda j, i: (i, j)) if a_col is None else (lambda j, i: (i, a_col))
    return _call(
        body, name=name, grid=(k // tk, ni),
        in_specs=[pl.BlockSpec((tm, tk), a_map), pl.BlockSpec((tm, n), lambda j, i: (i, 0))],
        out_specs=[pl.BlockSpec((tk, n), lambda j, i: (j, 0))],
        out_shape=[jax.ShapeDtypeStruct((k, n), out_dtype)],
        scratch=[pltpu.VMEM((tk, n), F32)],
        args=(a, dc), sem=("parallel", "arbitrary"), vmem=VMEM_BIG)[0]


def _head_cols(p):
    b = p * HEAD_W
    return (slice(b, b + HDK), slice(b + HDK, b + 2 * HDK), slice(b + 2 * HDK, b + 2 * HDK + HDV),
            slice(b + 2 * HDK + HDV, b + HEAD_W))


def _seg_cumsum(v, rowmod):
    sh = 1
    while sh < CHUNK:
        v = v + jnp.where(rowmod >= sh, pltpu.roll(v, sh, 0), 0.0)
        sh *= 2
    return v


def _seg_rcumsum(v, rowmod):
    t = v.shape[0]
    sh = 1
    while sh < CHUNK:
        v = v + jnp.where(rowmod < CHUNK - sh, pltpu.roll(v, t - sh, 0), 0.0)
        sh *= 2
    return v


def _gla_decay(alpha_ref, wup_ref, b_ref, g_scr):
    z = _dot(alpha_ref[...], wup_ref[...], NN) + b_ref[...]
    rowmod = lax.broadcasted_iota(jnp.int32, z.shape, 0) % CHUNK
    g_scr[...] = _seg_cumsum(_log_sigmoid(z) * (1.0 / GATE_TAU), rowmod)
    return z, rowmod


def _chunk_terms(hd_ref, g_scr, c, p):
    r0 = c * CHUNK
    gc = slice(p * HDK, (p + 1) * HDK)
    qc, kc, _, _ = _head_cols(p)
    g = g_scr[r0:r0 + CHUNK, gc]
    g_first = g_scr[r0:r0 + 1, gc]
    g_last = g_scr[r0 + CHUNK - 1:r0 + CHUNK, gc]
    ref = 0.5 * (g_first + g_last)
    ep = jnp.exp(g - ref)
    em = jnp.exp(ref - g)
    a = jnp.exp(g)
    dl = jnp.exp(g_last - g)
    qs = hd_ref[r0:r0 + CHUNK, qc] * Q_SCALE
    k = hd_ref[r0:r0 + CHUNK, kc]
    return dict(ep=ep, em=em, a=a, dl=dl, egl=jnp.exp(g_last),
                qe1=qs * ep, ke1=k * em, qe2=qs * em, ke2=k * ep, qa=qs * a, kd=k * dl)


def _scores(t, lower):
    return jnp.where(lower, _dot(t["qe1"], t["ke1"], NT), _dot(t["qe2"], t["ke2"], NT))


def _gla_specs(tt, row):
    return [
        pl.BlockSpec((tt, HPB * HEAD_W), lambda h, i: (row(i), HD0 // (HPB * HEAD_W) + h)),
        pl.BlockSpec((tt, 128), lambda h, i: (row(i), AL0 // 128)),
        pl.BlockSpec((128, HPB * HDK), lambda h, i: (0, h)),
        pl.BlockSpec((1, HPB * HDK), lambda h, i: (0, h)),
        pl.BlockSpec((1, HPB * HDV), lambda h, i: (0, h)),
    ]


def _gla_fwd(hh, wup, b_alpha, gnorm, *, tt, ride=None):
    s = hh.shape[0]
    nt = s // tt
    nct = tt // CHUNK

    def body(hd_ref, al_ref, wup_ref, b_ref, gn_ref, o_ref, ya_ref, st_ref, state, g_scr):
        @pl.when(pl.program_id(1) == 0)
        def _():
            state[...] = jnp.zeros_like(state)

        _gla_decay(al_ref, wup_ref, b_ref, g_scr)
        lower = lax.broadcasted_iota(jnp.int32, (CHUNK, CHUNK), 0) >= lax.broadcasted_iota(jnp.int32, (CHUNK, CHUNK), 1)
        for c in range(nct):
            r0 = c * CHUNK
            for p in range(HPB):
                _, _, vc, _ = _head_cols(p)
                t = _chunk_terms(hd_ref, g_scr, c, p)
                v = hd_ref[r0:r0 + CHUNK, vc]
                st = state[p]
                st_ref[p, c] = st
                o_ref[r0:r0 + CHUNK, p * HDV:(p + 1) * HDV] = _dot(_scores(t, lower), v, NN) + _dot(t["qa"], st, NT)
                state[p] = st * t["egl"] + _dot(v, t["kd"], TN)
        for p in range(HPB):
            oc = slice(p * HDV, (p + 1) * HDV)
            o = o_ref[:, oc]
            ohat = o * lax.rsqrt(jnp.mean(o * o, axis=-1, keepdims=True) + EPS)
            ga = hd_ref[:, _head_cols(p)[3]]
            ya_ref[:, oc] = (ohat * gn_ref[:, oc] * (ga * _sigmoid(ga))).astype(ya_ref.dtype)

    return _call(
        body, name="gla_fwd", grid=(HEADS // HPB, nt),
        in_specs=_gla_specs(tt, lambda i: i),
        out_specs=[
            pl.BlockSpec((tt, HPB * HDV), lambda h, i: (i, h)),
            pl.BlockSpec((tt, HPB * HDV), lambda h, i: (i, h)),
            pl.BlockSpec((HPB, nct, HDV, HDK), lambda h, i: (h, i, 0, 0)),
        ],
        out_shape=[
            jax.ShapeDtypeStruct((s, D), F32),
            jax.ShapeDtypeStruct((s, D), MXU),
            jax.ShapeDtypeStruct((HEADS, s // CHUNK, HDV, HDK), F32),
        ],
        scratch=[pltpu.VMEM((HPB, HDV, HDK), F32), pltpu.VMEM((tt, HPB * HDK), F32)],
        args=(hh, hh, wup, b_alpha, gnorm), ride=ride)


def _gla_bwd(hh, wup, b_alpha, gnorm, o, states, dya, dh, *, tt, ride=None):
    s = hh.shape[0]
    nt = s // tt
    nct = tt // CHUNK

    def body(hd_ref, al_ref, wup_ref, b_ref, gn_ref, o_ref, st_ref, dya_ref, _dh_in,
             dh_ref, dz_ref, dgn_ref, db_ref, dstate, g_scr, dg_scr, do_scr):
        @pl.when(pl.program_id(1) == 0)
        def _():
            dstate[...] = jnp.zeros_like(dstate)
            dgn_ref[...] = jnp.zeros_like(dgn_ref)
            db_ref[...] = jnp.zeros_like(db_ref)

        z, rowmod = _gla_decay(al_ref, wup_ref, b_ref, g_scr)

        for p in range(HPB):
            oc = slice(p * HDV, (p + 1) * HDV)
            gac = _head_cols(p)[3]
            o_t = o_ref[:, oc]
            rstd = lax.rsqrt(jnp.mean(o_t * o_t, axis=-1, keepdims=True) + EPS)
            ohat = o_t * rstd
            ga = hd_ref[:, gac]
            sg = _sigmoid(ga)
            dya_t = dya_ref[:, oc]
            gn = gn_ref[:, oc]
            dh_ref[:, gac] = (dya_t * ohat * gn * (sg * (1.0 + ga * (1.0 - sg)))).astype(dh_ref.dtype)
            don = dya_t * (ga * sg)
            dgn_ref[p] += jnp.sum(don * ohat, axis=0, keepdims=True)
            dohat = don * gn
            do_scr[:, oc] = rstd * (dohat - ohat * jnp.mean(dohat * ohat, axis=-1, keepdims=True))

        lower = lax.broadcasted_iota(jnp.int32, (CHUNK, CHUNK), 0) >= lax.broadcasted_iota(jnp.int32, (CHUNK, CHUNK), 1)
        last_row = lax.broadcasted_iota(jnp.int32, (CHUNK, HDK), 0) == CHUNK - 1
        for c in range(nct - 1, -1, -1):
            r0 = c * CHUNK
            rows = slice(r0, r0 + CHUNK)
            for p in range(HPB):
                qc, kc, vc, _ = _head_cols(p)
                t = _chunk_terms(hd_ref, g_scr, c, p)
                v = hd_ref[rows, vc]
                do = do_scr[rows, p * HDV:(p + 1) * HDV]
                st = st_ref[p, c]
                dst = dstate[p]
                a = _scores(t, lower)
                da = _dot(do, v, NT)
                da1 = jnp.where(lower, da, 0.0)
                da2 = jnp.where(lower, 0.0, da)
                dqe1 = _dot(da1, t["ke1"], NN)
                dke1 = _dot(da1, t["qe1"], TN)
                dqe2 = _dot(da2, t["ke2"], NN)
                dke2 = _dot(da2, t["qe2"], TN)
                dqa = _dot(do, st, NN)
                dkd = _dot(v, dst, NN)
                dh_ref[rows, vc] = (_dot(a, do, TN) + _dot(t["kd"], dst, NT)).astype(dh_ref.dtype)
                dh_ref[rows, qc] = ((dqe1 * t["ep"] + dqe2 * t["em"] + dqa * t["a"]) * Q_SCALE).astype(dh_ref.dtype)
                dh_ref[rows, kc] = (dke1 * t["em"] + dke2 * t["ep"] + dkd * t["dl"]).astype(dh_ref.dtype)
                dkd_kd = dkd * t["kd"]
                dgl = jnp.sum(dkd_kd, axis=0, keepdims=True) + t["egl"] * jnp.sum(dst * st, axis=0, keepdims=True)
                dg = dqe1 * t["qe1"] - dke1 * t["ke1"] - dqe2 * t["qe2"] + dke2 * t["ke2"] + dqa * t["qa"] - dkd_kd
                dg_scr[rows, p * HDK:(p + 1) * HDK] = dg + jnp.where(last_row, dgl, 0.0)
                dstate[p] = dst * t["egl"] + _dot(do, t["qa"], TN)

        dz = _seg_rcumsum(dg_scr[...], rowmod) * _sigmoid(-z) * (1.0 / GATE_TAU)
        dz_ref[...] = dz.astype(dz_ref.dtype)
        for p in range(HPB):
            db_ref[p] += jnp.sum(dz[:, p * HDK:(p + 1) * HDK], axis=0, keepdims=True)

    rev = lambda i: nt - 1 - i
    in_specs = _gla_specs(tt, rev) + [
        pl.BlockSpec((tt, HPB * HDV), lambda h, i: (rev(i), h)),
        pl.BlockSpec((HPB, nct, HDV, HDK), lambda h, i: (h, rev(i), 0, 0)),
        pl.BlockSpec((tt, HPB * HDV), lambda h, i: (rev(i), h)),
        ANY,
    ]
    return _call(
        body, name="gla_bwd", grid=(HEADS // HPB, nt), in_specs=in_specs,
        out_specs=[
            pl.BlockSpec((tt, HPB * HEAD_W), lambda h, i: (rev(i), HD0 // (HPB * HEAD_W) + h)),
            pl.BlockSpec((tt, HPB * HDK), lambda h, i: (rev(i), h)),
            pl.BlockSpec((HPB, 1, HDV), lambda h, i: (h, 0, 0)),
            pl.BlockSpec((HPB, 1, HDK), lambda h, i: (h, 0, 0)),
        ],
        out_shape=[
            jax.ShapeDtypeStruct(dh.shape, dh.dtype),
            jax.ShapeDtypeStruct((s, DK), MXU),
            jax.ShapeDtypeStruct((HEADS, 1, HDV), F32),
            jax.ShapeDtypeStruct((HEADS, 1, HDK), F32),
        ],
        scratch=[pltpu.VMEM((HPB, HDV, HDK), F32), pltpu.VMEM((tt, HPB * HDK), F32), pltpu.VMEM((tt, HPB * HDK), F32),
                 pltpu.VMEM((tt, HPB * HDV), F32)],
        args=(hh, hh, wup, b_alpha, gnorm, o, states, dya, dh), ride=ride, aliases={8: 0})


def _window_count(tile, tt, w):
    pos = tile * tt + lax.broadcasted_iota(jnp.int32, (tt, PG), 0) + 1
    return jnp.minimum(pos, w).astype(F32)


def _pool_fwd(hh, wpool, scale, *, tt):
    s = hh.shape[0]
    nt = s // tt

    def body(ug_ref, w_ref, sc_ref, pooled_ref, yb_ref, halo):
        i = pl.program_id(0)

        @pl.when(i == 0)
        def _():
            halo[...] = jnp.zeros_like(halo)

        for g, w in enumerate(POOL_WINDOWS):
            cols = slice(g * PG, (g + 1) * PG)
            u = ug_ref[:, cols]
            run = jnp.concatenate([halo[:, cols], u], axis=0)
            sh = 1
            while sh < w:
                run = run + pltpu.roll(run, sh, 0)
                sh *= 2
            pooled = run[HALO:, :] / _window_count(i, tt, w) - u
            pooled_ref[:, cols] = pooled.astype(pooled_ref.dtype)
            mixed = _dot(pooled, w_ref[g], NN)
            gb = ug_ref[:, slice(D + g * PG, D + (g + 1) * PG)]
            yb_ref[:, cols] = (mixed * sc_ref[:, cols] * (gb * _sigmoid(gb))).astype(yb_ref.dtype)
        halo[...] = ug_ref[tt - HALO:tt, :D]

    tile = pl.BlockSpec((tt, D), lambda i: (i, 0))
    return _call(
        body, name="pool_fwd", grid=(nt,),
        in_specs=[
            pl.BlockSpec((tt, 2 * D), lambda i: (i, PI0 // (2 * D))),
            pl.BlockSpec((len(POOL_WINDOWS), PG, PG), lambda i: (0, 0, 0)),
            pl.BlockSpec((1, D), lambda i: (0, 0)),
        ],
        out_specs=[tile] * 2,
        out_shape=[jax.ShapeDtypeStruct((s, D), MXU), jax.ShapeDtypeStruct((s, D), MXU)],
        scratch=[pltpu.VMEM((HALO, D), F32)], args=(hh, wpool, scale))


def _pool_bwd(hh, wpool, scale, pooled, dyb, dh, *, tt):
    s = hh.shape[0]
    nt = s // tt

    def body(gb_ref, w_ref, sc_ref, pooled_ref, dyb_ref, _dh_in, dh_ref, dw_ref, dsc_ref, halo):
        i = pl.program_id(0)
        tile = nt - 1 - i

        @pl.when(i == 0)
        def _():
            halo[...] = jnp.zeros_like(halo)
            dw_ref[...] = jnp.zeros_like(dw_ref)
            dsc_ref[...] = jnp.zeros_like(dsc_ref)

        for g, w in enumerate(POOL_WINDOWS):
            cols = slice(g * PG, (g + 1) * PG)
            gcols = slice(D + g * PG, D + (g + 1) * PG)
            gb = gb_ref[:, cols]
            sg = _sigmoid(gb)
            pooled = pooled_ref[:, cols]
            mixed = _dot(pooled, w_ref[g], NN)
            sc = sc_ref[:, cols]
            dyb = dyb_ref[:, cols]
            dh_ref[:, gcols] = (dyb * mixed * sc * (sg * (1.0 + gb * (1.0 - sg)))).astype(dh_ref.dtype)
            dms = dyb * (gb * sg)
            dsc_ref[:, cols] += jnp.sum(dms * mixed, axis=0, keepdims=True)
            dmixed = dms * sc
            dpooled = _dot(dmixed, w_ref[g], NT)
            dw_ref[g] += _dot(pooled, dmixed, TN)
            e = dpooled / _window_count(tile, tt, w)
            run = jnp.concatenate([e, halo[:, cols]], axis=0)
            sh = 1
            while sh < w:
                run = run + pltpu.roll(run, tt + HALO - sh, 0)
                sh *= 2
            dh_ref[:, cols] = (run[:tt, :] - dpooled).astype(dh_ref.dtype)
            halo[:, cols] = e[:HALO, :]

    rev = lambda i: nt - 1 - i
    tile = pl.BlockSpec((tt, D), lambda i: (rev(i), 0))
    wspec = pl.BlockSpec((len(POOL_WINDOWS), PG, PG), lambda i: (0, 0, 0))
    vec = pl.BlockSpec((1, D), lambda i: (0, 0))
    return _call(
        body, name="pool_bwd", grid=(nt,),
        in_specs=[pl.BlockSpec((tt, D), lambda i: (rev(i), GB0 // D)), wspec, vec, tile, tile, ANY],
        out_specs=[pl.BlockSpec((tt, 2 * D), lambda i: (rev(i), PI0 // (2 * D))), wspec, vec],
        out_shape=[jax.ShapeDtypeStruct(dh.shape, dh.dtype), jax.ShapeDtypeStruct((len(POOL_WINDOWS), PG, PG), F32),
                   jax.ShapeDtypeStruct((1, D), F32)],
        scratch=[pltpu.VMEM((HALO, D), F32)], args=(hh, wpool, scale, pooled, dyb, dh), aliases={5: 0})


def _merge_fwd(hh, x, ya, yb, wpa, wpb, wout, b_merge, ln_g, ln_b, *, tt):
    s = x.shape[0]

    def body(ml_ref, x_ref, ya_ref, yb_ref, wpa_ref, wpb_ref, wout_ref, bm_ref, g_ref, b_ref, r_ref, xn_ref, xnb_ref):
        pa = _dot(ya_ref[...], wpa_ref[...], NN)
        pb = _dot(yb_ref[...], wpb_ref[...], NN)
        merged = _sigmoid(ml_ref[:, :D] + bm_ref[:, :D]) * pa + _sigmoid(ml_ref[:, D:] + bm_ref[:, D:]) * pb
        r = ALPHA * x_ref[...] + _dot(merged, wout_ref[...], NN)
        r_ref[...] = r
        mu = jnp.mean(r, axis=-1, keepdims=True)
        xc = r - mu
        var = jnp.mean(xc * xc, axis=-1, keepdims=True)
        xn = xc * lax.rsqrt(var + EPS) * g_ref[...] + b_ref[...]
        xn_ref[...] = xn
        xnb_ref[...] = xn.astype(xnb_ref.dtype)

    tile = pl.BlockSpec((tt, D), lambda i: (i, 0))
    full = pl.BlockSpec((D, D), lambda i: (0, 0))
    vec = pl.BlockSpec((1, D), lambda i: (0, 0))
    return _call(
        body, name="merge_fwd", grid=(s // tt,),
        in_specs=[pl.BlockSpec((tt, 2 * D), lambda i: (i, ML0 // (2 * D))), tile, tile, tile, full, full, full,
                  pl.BlockSpec((1, 2 * D), lambda i: (0, 0)), vec, vec],
        out_specs=[tile] * 3, out_shape=[jax.ShapeDtypeStruct((s, D), F32)] * 2 + [jax.ShapeDtypeStruct((s, D), MXU)],
        args=(hh, x, ya, yb, wpa, wpb, wout, b_merge, ln_g, ln_b), sem=("parallel",), vmem=VMEM_BIG)


def _merge_bwd(hh, r, ya, yb, dout, wpa, wpb, wout, b_merge, ln_g, *, tt):
    s = r.shape[0]

    def body(ml_ref, r_ref, ya_ref, yb_ref, do_ref, wpa_ref, wpb_ref, wout_ref, bm_ref, g_ref,
             dh_ref, dr_ref, dpa_ref, dpb_ref, dwout_ref, dg_ref, db_ref, dbm_ref):
        @pl.when(pl.program_id(0) == 0)
        def _():
            dwout_ref[...] = jnp.zeros_like(dwout_ref)
            dg_ref[...] = jnp.zeros_like(dg_ref)
            db_ref[...] = jnp.zeros_like(db_ref)
            dbm_ref[...] = jnp.zeros_like(dbm_ref)

        rr = r_ref[...]
        mu = jnp.mean(rr, axis=-1, keepdims=True)
        xc = rr - mu
        rstd = lax.rsqrt(jnp.mean(xc * xc, axis=-1, keepdims=True) + EPS)
        xhat = xc * rstd
        do = do_ref[...]
        dg_ref[...] += jnp.sum(do * xhat, axis=0, keepdims=True)
        db_ref[...] += jnp.sum(do, axis=0, keepdims=True)
        dxh = do * g_ref[...]
        dr = rstd * (dxh - jnp.mean(dxh, axis=-1, keepdims=True) - xhat * jnp.mean(dxh * xhat, axis=-1, keepdims=True))
        dr_ref[...] = dr
        g_a = _sigmoid(ml_ref[:, :D] + bm_ref[:, :D])
        g_b = _sigmoid(ml_ref[:, D:] + bm_ref[:, D:])
        pa = _dot(ya_ref[...], wpa_ref[...], NN)
        pb = _dot(yb_ref[...], wpb_ref[...], NN)
        dwout_ref[...] += _dot(g_a * pa + g_b * pb, dr, TN)
        dm = _dot(dr, wout_ref[...], NT)
        dpa_ref[...] = (dm * g_a).astype(dpa_ref.dtype)
        dpb_ref[...] = (dm * g_b).astype(dpb_ref.dtype)
        dml_a = dm * pa * g_a * (1.0 - g_a)
        dml_b = dm * pb * g_b * (1.0 - g_b)
        dh_ref[:, :D] = dml_a.astype(dh_ref.dtype)
        dh_ref[:, D:] = dml_b.astype(dh_ref.dtype)
        dbm_ref[:, :D] += jnp.sum(dml_a, axis=0, keepdims=True)
        dbm_ref[:, D:] += jnp.sum(dml_b, axis=0, keepdims=True)

    tile = pl.BlockSpec((tt, D), lambda i: (i, 0))
    full = pl.BlockSpec((D, D), lambda i: (0, 0))
    vec = pl.BlockSpec((1, D), lambda i: (0, 0))
    vec2 = pl.BlockSpec((1, 2 * D), lambda i: (0, 0))
    mlb = pl.BlockSpec((tt, 2 * D), lambda i: (i, ML0 // (2 * D)))
    return _call(
        body, name="merge_bwd", grid=(s // tt,),
        in_specs=[mlb, tile, tile, tile, tile, full, full, full, vec2, vec],
        out_specs=[mlb, tile, tile, tile, full, vec, vec, vec2],
        out_shape=[
            jax.ShapeDtypeStruct((s, HP), MXU), jax.ShapeDtypeStruct((s, D), F32),
            jax.ShapeDtypeStruct((s, D), MXU), jax.ShapeDtypeStruct((s, D), MXU),
            jax.ShapeDtypeStruct((D, D), F32), jax.ShapeDtypeStruct((1, D), F32),
            jax.ShapeDtypeStruct((1, D), F32), jax.ShapeDtypeStruct((1, 2 * D), F32),
        ],
        args=(hh, r, ya, yb, dout, wpa, wpb, wout, b_merge, ln_g), vmem=VMEM_BIG)


def _proj_bwd(y, dp, w, *, tt, name):
    s = y.shape[0]

    def body(y_ref, dp_ref, w_ref, dy_ref, dw_ref):
        @pl.when(pl.program_id(0) == 0)
        def _():
            dw_ref[...] = jnp.zeros_like(dw_ref)

        dp = dp_ref[...]
        dy_ref[...] = _dot(dp, w_ref[...], NT)
        dw_ref[...] += _dot(y_ref[...], dp, TN)

    tile = pl.BlockSpec((tt, D), lambda i: (i, 0))
    full = pl.BlockSpec((D, D), lambda i: (0, 0))
    return _call(
        body, name=name, grid=(s // tt,), in_specs=[tile, tile, full], out_specs=[tile, full],
        out_shape=[jax.ShapeDtypeStruct((s, D), F32), jax.ShapeDtypeStruct((D, D), F32)], args=(y, dp, w))


def _loss_head(y, target, *, tt):
    s = y.shape[0]

    def body(y_ref, t_ref, loss_ref, dy_ref):
        @pl.when(pl.program_id(0) == 0)
        def _():
            loss_ref[...] = jnp.zeros_like(loss_ref)

        err = y_ref[...] - t_ref[...]
        dy_ref[...] = err * (1.0 / D)
        per_tok = jnp.mean(err * err, axis=-1, keepdims=True)
        loss_ref[...] += 0.5 * jnp.sum(per_tok, axis=0, keepdims=True)

    tile = pl.BlockSpec((tt, D), lambda i: (i, 0))
    return _call(
        body, name="loss_head", grid=(s // tt,), in_specs=[tile, tile],
        out_specs=[pl.BlockSpec((1, 1), lambda i: (0, 0)), tile],
        out_shape=[jax.ShapeDtypeStruct((1, 1), F32), jax.ShapeDtypeStruct((s, D), F32)], args=(y, target))


def _adamw(parts, w, m, v, *, tr, tc, name):
    nl, rows, cols = w.shape

    def body(p_ref, w_ref, m_ref, v_ref, g_ref, d_ref, nm_ref, nv_ref):
        g = p_ref[0, 0].astype(F32)
        for q in range(1, N_DEV):
            g = g + p_ref[0, q].astype(F32)
        g_ref[0] = g
        nm = ADAM_B1 * m_ref[0] + (1.0 - ADAM_B1) * g
        nv = ADAM_B2 * v_ref[0] + (1.0 - ADAM_B2) * (g * g)
        nm_ref[0] = nm
        nv_ref[0] = nv
        m_hat = nm / (1.0 - ADAM_B1 ** ADAM_STEP)
        v_hat = nv / (1.0 - ADAM_B2 ** ADAM_STEP)
        d_ref[0] = -ADAM_LR * (m_hat / (jnp.sqrt(v_hat) + ADAM_EPS) + ADAM_WD * w_ref[0])

    tile = pl.BlockSpec((1, tr, tc), lambda l, i, j: (l, i, j))
    return _call(
        body, name=name, grid=(nl, rows // tr, cols // tc),
        in_specs=[pl.BlockSpec((1, N_DEV, tr, tc), lambda l, i, j: (l, 0, i, j)), tile, tile, tile],
        out_specs=[tile] * 4, out_shape=[jax.ShapeDtypeStruct((nl, rows, cols), F32)] * 4,
        args=(parts, w, m, v), sem=("parallel", "parallel", "parallel"))


def _from_devices(g, axis):
    nd = g.ndim - 1
    perm = list(range(1, axis + 1)) + [0] + list(range(axis + 1, nd + 1))
    shape = list(g.shape[1:])
    shape[axis] *= N_DEV
    return jnp.transpose(g, perm).reshape(shape)


def _to_devices(a, axis):
    shape = list(a.shape)
    t = a.reshape(shape[:axis] + [N_DEV, shape[axis] // N_DEV] + shape[axis + 1:])
    return jnp.transpose(t, [axis] + list(range(0, axis)) + list(range(axis + 1, t.ndim)))


def _h_row_segments():
    segs = [(O_PI, PI0, IN_COLS - O_PI), (O_AL, AL0, RANK)]
    for h in range(HEADS):
        base = HD0 + h * HEAD_W
        segs += [(O_Q + h * HDK, base, HDK), (O_K + h * HDK, base + HDK, HDK),
                 (O_V + h * HDV, base + 2 * HDK, HDV), (O_GA + h * HDV, base + 2 * HDK + HDV, HDV)]
    return segs


def _h_weight_t(parts):
    return _move_rows(parts.reshape(IN_COLS, D), _h_row_segments(), HP, name="w_in_rows", zero=(AL0 + RANK, AL_W - RANK))


def _w_in_grad_parts_t(dwt):
    g = _move_rows(dwt, [(d0, s0, n) for s0, d0, n in _h_row_segments()], IN_COLS, name="w_in_grad_rows")
    return g.reshape(N_DEV, SHARD, D)


def kernel(x, w_in, w_alpha_up, b_alpha, gla_norm_g, w_pool_grp, pool_scale, b_merge, w_proj_a, w_proj_b, w_out, ln_g, ln_b, loss_target, m_w_in, m_w_alpha_up, m_b_alpha, m_gla_norm_g, m_w_pool_grp, m_pool_scale, m_b_merge, m_w_proj_a, m_w_proj_b, m_w_out, m_ln_g, m_ln_b, v_w_in, v_w_alpha_up, v_b_alpha, v_gla_norm_g, v_w_pool_grp, v_pool_scale, v_b_merge, v_w_proj_a, v_w_proj_b, v_w_out, v_ln_g, v_ln_b):
    s = x.shape[1]
    tt = min(256, s)
    tm = min(512, s)
    tn = HP // 3
    xs = x.reshape(s, D)

    tr3 = lambda a: jnp.transpose(a, (0, 2, 1))
    w_in_s = tr3(w_in).astype(WIRE)
    proj_s = jnp.stack([w_proj_a, w_proj_b, w_out], axis=1).astype(WIRE)
    pool_s = w_pool_grp.astype(WIRE)

    g_in, g_up, g_gn = _gather_first(w_in_s[0], [w_alpha_up.astype(WIRE), gla_norm_g], name="gather_first")
    wup = jnp.pad(_from_devices(g_up, 2), ((0, 0), (0, AL_W - RANK), (0, 0)))
    gn = _from_devices(g_gn, 2).reshape(DEPTH, 1, D)

    saved, wt_all, proj_all, pool_all = [], [], [], []
    cur, cur_b = xs, xs.astype(MXU)
    g_proj = g_pool = None
    for l in range(DEPTH):
        wt = _h_weight_t(g_in)
        nxt_l = l + 1 < DEPTH
        res = _in_proj(cur_b, wt, tm=tm, tn=tn, ride=_Exchange([(w_in_s[l + 1], True)]) if nxt_l else None)
        hh = res[0]
        if nxt_l:
            g_in = res[1]
        layers = ([0] if l == 0 else []) + ([l + 1] if nxt_l else [])
        res = _gla_fwd(hh, wup[l], b_alpha[l:l + 1], gn[l], tt=tt,
                       ride=_Exchange([(a[j], True) for j in layers for a in (proj_s, pool_s)]) if layers else None)
        o, ya, states = res[:3]
        got = {j: res[3 + 2 * t:5 + 2 * t] for t, j in enumerate(layers)}
        if l == 0:
            g_proj, g_pool = got[0]
        proj = _from_devices(g_proj, 1)
        pool = _from_devices(g_pool, 1)
        if nxt_l:
            g_proj, g_pool = got[l + 1]
        wt_all.append(wt), proj_all.append(proj), pool_all.append(pool)
        pooled, yb = _pool_fwd(hh, pool, pool_scale[l:l + 1], tt=tt)
        r, nxt, nxt_b = _merge_fwd(hh, cur, ya, yb, proj[0], proj[1], proj[2],
                                   b_merge[l:l + 1], ln_g[l:l + 1], ln_b[l:l + 1], tt=tt)
        saved.append(dict(xb=cur_b, hh=hh, o=o, ya=ya, states=states, pooled=pooled, yb=yb, r=r))
        cur, cur_b = nxt, nxt_b

    loss_part, dcur = _loss_head(cur, loss_target.reshape(s, D), tt=tt)
    loss = lax.psum(loss_part[0, 0], ("x", "y", "c"))

    small = {k: [None] * DEPTH for k in ("w_up", "b_alpha", "gnorm", "pool_scale", "b_merge", "ln_g", "ln_b")}
    parts = {k: [None] * DEPTH for k in ("w_in", "proj", "pool")}
    for l in range(DEPTH - 1, -1, -1):
        sv = saved[l]
        hh = sv["hh"]
        dh, dr, dpa, dpb, dw_out, dln_g, dln_b, db_merge = _merge_bwd(
            hh, sv["r"], sv["ya"], sv["yb"], dcur, proj_all[l][0], proj_all[l][1], proj_all[l][2], b_merge[l:l + 1], ln_g[l:l + 1], tt=tt)
        dya, dw_pa = _proj_bwd(sv["ya"], dpa, proj_all[l][0], tt=tt, name="proj_a_bwd")
        dyb, dw_pb = _proj_bwd(sv["yb"], dpb, proj_all[l][1], tt=tt, name="proj_b_bwd")
        dh, dw_pool, dscale = _pool_bwd(hh, pool_all[l], pool_scale[l:l + 1], sv["pooled"], dyb, dh, tt=tt)
        ride = _Exchange([(_to_devices(jnp.stack([dw_pa, dw_pb, dw_out]), 1).astype(WIRE), False),
                          (_to_devices(dw_pool, 1).astype(WIRE), False)])
        dh, dz, dgn, db_al, parts["proj"][l], parts["pool"][l] = _gla_bwd(
            hh, wup[l], b_alpha[l:l + 1], gn[l], sv["o"], sv["states"], dya, dh, tt=tt, ride=ride)
        dh = _mm_nt_into(dz, wup[l], dh, AL0 // AL_W, tm=tm, name="alpha_bwd")
        dw_up = _mm_tn(hh, dz, tm=tm, tk=AL_W, name="w_up_grad", a_block=(AL_W, AL0 // AL_W))
        dwt = _mm_tn(dh, sv["xb"], tm=tm, tk=tn, name="w_in_grad", out_dtype=WIRE)
        dcur, parts["w_in"][l] = _in_proj_bwd(dh, wt_all[l], dr, tm=tt, ride=_Exchange([(_w_in_grad_parts_t(dwt), False)]))

        small["w_up"][l] = dw_up[:RANK]
        small["b_alpha"][l] = db_al.reshape(DK)
        small["gnorm"][l] = dgn.reshape(HEADS, HDV)
        small["pool_scale"][l] = dscale[0]
        small["b_merge"][l] = db_merge[0]
        small["ln_g"][l], small["ln_b"][l] = dln_g[0], dln_b[0]
    grad_x = dcur[None]
    sm = {k: jnp.stack(v) for k, v in small.items()}

    rep = (("b_alpha", b_alpha, m_b_alpha, v_b_alpha), ("pool_scale", pool_scale, m_pool_scale, v_pool_scale),
           ("b_merge", b_merge, m_b_merge, v_b_merge), ("ln_g", ln_g, m_ln_g, v_ln_g), ("ln_b", ln_b, m_ln_b, v_ln_b))
    cat = lambda arrs: jnp.concatenate(arrs, axis=1)
    p_up, p_gn, p_rep = _exchange([(_to_devices(sm["w_up"], 2), False), (_to_devices(sm["gnorm"], 2), False),
                                   (cat([sm[nm] for nm, _, _, _ in rep]), True)], name="exchange_small_grads")

    def update(p, w, m, v, tr, name, layered=True, tc=None):
        shape = w.shape
        nl = shape[0] if layered else 1
        cols = shape[-1]
        flat = lambda a: a.reshape(nl, -1, cols)
        outs = _adamw(p.reshape(nl, N_DEV, -1, cols), flat(w), flat(m), flat(v), tr=tr, tc=tc or cols, name=name)
        return [o_.reshape(shape) for o_ in outs]

    res = {}
    res["w_in"] = [tr3(o_) for o_ in update(jnp.stack(parts["w_in"]), tr3(w_in), tr3(m_w_in), tr3(v_w_in), SHARD, "adamw_w_in", tc=256)]
    proj_p = jnp.stack(parts["proj"])
    for j, (nm, w, m, v) in enumerate((("w_proj_a", w_proj_a, m_w_proj_a, v_w_proj_a), ("w_proj_b", w_proj_b, m_w_proj_b, v_w_proj_b),
                                       ("w_out", w_out, m_w_out, v_w_out))):
        res[nm] = update(proj_p[:, :, j], w, m, v, D // N_DEV, "adamw_" + nm)
    res["w_pool_grp"] = update(jnp.stack(parts["pool"]), w_pool_grp, m_w_pool_grp, v_w_pool_grp, 128, "adamw_w_pool")
    res["w_alpha_up"] = update(p_up, w_alpha_up, m_w_alpha_up, v_w_alpha_up, DEPTH * RANK, "adamw_w_up", layered=False)
    res["gla_norm_g"] = update(p_gn, gla_norm_g, m_gla_norm_g, v_gla_norm_g, DEPTH * HEADS, "adamw_gnorm", layered=False)
    rep_out = update(p_rep, cat([w for _, w, _, _ in rep]), cat([m for _, _, m, _ in rep]), cat([v for _, _, _, v in rep]),
                     DEPTH, "adamw_small", layered=False)
    off = 0
    for nm, w, _, _ in rep:
        n = w.shape[1]
        res[nm] = [o_[:, off:off + n] for o_ in rep_out]
        off += n

    order = ("w_in", "w_alpha_up", "b_alpha", "gla_norm_g", "w_pool_grp", "pool_scale", "b_merge", "w_proj_a", "w_proj_b",
             "w_out", "ln_g", "ln_b")
    return (loss, grad_x, *[res[n][0] for n in order], *[res[n][1] for n in order],
            *[res[n][2] for n in order], *[res[n][3] for n in order])
```

```python
import jax
import jax.numpy as jnp
from jax import lax
from jax.experimental import pallas as pl
from jax.experimental.pallas import tpu as pltpu

F32 = jnp.float32
MXU = jnp.bfloat16
WIRE = jnp.bfloat16

N_DEV = 8
DEPTH = 4
D = 1024
HEADS = 4
DK = D // 2
HDK = DK // HEADS
HDV = D // HEADS
RANK = 16
CHUNK = 64
GATE_TAU = 16.0
POOL_WINDOWS = (2, 4, 8, 16)
PG = D // len(POOL_WINDOWS)
HALO = 16
IN_COLS = 7184
SHARD = IN_COLS // N_DEV
ALPHA = (2.0 * DEPTH) ** 0.25
EPS = 1e-5
Q_SCALE = HDK ** -0.5

ADAM_LR, ADAM_B1, ADAM_B2, ADAM_EPS, ADAM_WD, ADAM_STEP = 0.001, 0.9, 0.999, 1e-08, 0.01, 10

PI0, GB0, ML0, AL0, AL_W = 0, D, 2 * D, 4 * D, 512
HD0 = AL0 + AL_W
HEAD_W = 2 * HDK + 2 * HDV
HP = HD0 + HEADS * HEAD_W
HPB = 2
O_Q, O_K, O_V, O_GA, O_AL, O_PI, O_GB, O_ML = 0, DK, 2 * DK, 2 * DK + D, 2 * DK + 2 * D, 2 * DK + 2 * D + RANK, \
    2 * DK + 3 * D + RANK, 2 * DK + 4 * D + RANK

VMEM_BIG = 56 * 1024 * 1024
VMEM_MID = 40 * 1024 * 1024

NN = ((1,), (0,))
NT = ((1,), (1,))
TN = ((0,), (0,))

HBM = pl.BlockSpec(memory_space=pltpu.HBM)
ANY = pl.BlockSpec(memory_space=pl.ANY)


def _dot(a, b, dims):
    return lax.dot_general(a.astype(MXU), b.astype(MXU), (dims, ((), ())), preferred_element_type=F32)


def _params(sem, vmem):
    return pltpu.CompilerParams(dimension_semantics=sem, vmem_limit_bytes=vmem)


def _sigmoid(x):
    return 1.0 / (1.0 + jnp.exp(-x))


def _log_sigmoid(z):
    return jnp.minimum(z, 0.0) - jnp.log(1.0 + jnp.exp(-jnp.abs(z)))


class _Exchange:
    def __init__(self, items):
        self.items = [(s, bool(g)) for s, g in items]
        self.n = len(self.items)
        self.srcs = [s for s, _ in self.items]
        self.in_specs = [HBM] * self.n
        self.out_specs = [HBM] * self.n
        self.out_shape = [jax.ShapeDtypeStruct((N_DEV,) + tuple(s.shape if g else s.shape[1:]), s.dtype) for s, g in self.items]
        self.scratch = [pltpu.SemaphoreType.DMA((self.n * (N_DEV - 1),)), pltpu.SemaphoreType.DMA((self.n * (N_DEV - 1),)),
                        pltpu.SemaphoreType.DMA((self.n,))]

    def copies(self, src_refs, out_refs, send_sems, recv_sems, local_sems):
        x, y, c = lax.axis_index("x"), lax.axis_index("y"), lax.axis_index("c")
        me = 4 * x + 2 * y + c
        copies = []
        for t, (_, gather) in enumerate(self.items):
            src_ref, out_ref = src_refs[t], out_refs[t]
            copies.append(pltpu.make_async_copy(src_ref if gather else src_ref.at[me], out_ref.at[me], local_sems.at[t]))
            for k in range(1, N_DEV):
                px = 1 - x if k & 4 else x
                py = 1 - y if k & 2 else y
                pc = 1 - c if k & 1 else c
                peer = 4 * px + 2 * py + pc
                sem = t * (N_DEV - 1) + k - 1
                copies.append(pltpu.make_async_remote_copy(
                    src_ref=src_ref if gather else src_ref.at[peer],
                    dst_ref=out_ref.at[me],
                    send_sem=send_sems.at[sem],
                    recv_sem=recv_sems.at[sem],
                    device_id=(px, py, pc),
                    device_id_type=pl.DeviceIdType.MESH,
                ))
        return copies


def _grid_ends(grid):
    first = last = None
    for a, n in enumerate(grid):
        f = pl.program_id(a) == 0
        e = pl.program_id(a) == n - 1
        first = f if first is None else first & f
        last = e if last is None else last & e
    return first, last


def _call(body, *, name, grid, in_specs, out_specs, out_shape, args, scratch=(), sem=None, vmem=VMEM_MID, ride=None, aliases=None):
    n_in, n_out, n_scr = len(in_specs), len(out_specs), len(scratch)
    sem = sem or ("arbitrary",) * len(grid)
    if ride is None:
        return pl.pallas_call(body, name=name, grid=grid, in_specs=in_specs, out_specs=out_specs, out_shape=out_shape,
                              scratch_shapes=list(scratch), compiler_params=_params(sem, vmem),
                              input_output_aliases=aliases or {})(*args)
    r = ride.n

    def riding(*refs):
        ins, rsrc = refs[:n_in], refs[n_in:n_in + r]
        outs, rout = refs[n_in + r:n_in + r + n_out], refs[n_in + r + n_out:n_in + 2 * r + n_out]
        scr = refs[n_in + 2 * r + n_out:n_in + 2 * r + n_out + n_scr]
        send_sems, recv_sems, local_sems = refs[n_in + 2 * r + n_out + n_scr:]
        first, last = _grid_ends(grid)
        copies = ride.copies(rsrc, rout, send_sems, recv_sems, local_sems)

        @pl.when(first)
        def _():
            for cp in copies:
                cp.start()

        body(*ins, *outs, *scr)

        @pl.when(last)
        def _():
            for cp in copies:
                cp.wait()

    return pl.pallas_call(riding, name=name, grid=grid, in_specs=list(in_specs) + ride.in_specs,
                          out_specs=list(out_specs) + ride.out_specs, out_shape=list(out_shape) + ride.out_shape,
                          scratch_shapes=list(scratch) + ride.scratch,
                          compiler_params=_params(("arbitrary",) * len(grid), vmem),
                          input_output_aliases=aliases or {})(*args, *ride.srcs)


def _exchange(items, *, name):
    ex = _Exchange(items)

    def body(*refs):
        copies = ex.copies(refs[:ex.n], refs[ex.n:2 * ex.n], *refs[2 * ex.n:])
        for cp in copies:
            cp.start()
        for cp in copies:
            cp.wait()

    return pl.pallas_call(body, name=name, in_specs=ex.in_specs, out_specs=ex.out_specs, out_shape=ex.out_shape,
                          scratch_shapes=ex.scratch)(*ex.srcs)


def _gather_first(big, smalls, *, name):
    ex = _Exchange([(a, True) for a in smalls])

    def body(*refs):
        big_ref, small_src = refs[0], refs[1:1 + ex.n]
        out_ref, small_out = refs[1 + ex.n], refs[2 + ex.n:2 + 2 * ex.n]
        send_sems, recv_sems, local_sem = refs[2 + 2 * ex.n:5 + 2 * ex.n]
        x, y, c = lax.axis_index("x"), lax.axis_index("y"), lax.axis_index("c")
        me, sibling = (x, y, c), (x, y, 1 - c)
        chips = [(1 - x, y), (x, 1 - y), (1 - x, 1 - y)]

        def slot(px, py, pc):
            return out_ref.at[4 * px + 2 * py + pc]

        def copy(k, block, to, src=None):
            return pltpu.make_async_remote_copy(
                src_ref=slot(*block) if src is None else src, dst_ref=slot(*block),
                send_sem=send_sems.at[k], recv_sem=recv_sems.at[k], device_id=to, device_id_type=pl.DeviceIdType.MESH)

        small = ex.copies(small_src, small_out, *refs[5 + 2 * ex.n:])
        mine = pltpu.make_async_copy(big_ref, slot(*me), local_sem)
        mine.start()
        first = [copy(0, me, sibling, src=big_ref)] + [copy(1 + j, me, (*chip, c), src=big_ref) for j, chip in enumerate(chips)]
        for cp in first + small:
            cp.start()
        passed = [copy(4 + j, (*chip, c), sibling) for j, chip in enumerate(chips)]
        for j, chip in enumerate(chips):
            copy(1 + j, (*chip, c), me).wait_recv()
            passed[j].start()
        copy(0, sibling, me).wait_recv()
        for j, chip in enumerate(chips):
            copy(4 + j, (*chip, 1 - c), me).wait_recv()
        for cp in first + passed:
            cp.wait_send()
        mine.wait()
        for cp in small:
            cp.wait()

    return pl.pallas_call(
        body, name=name, in_specs=[HBM] + ex.in_specs, out_specs=[HBM] + ex.out_specs,
        out_shape=[jax.ShapeDtypeStruct((N_DEV,) + tuple(big.shape), big.dtype)] + ex.out_shape,
        scratch_shapes=[pltpu.SemaphoreType.DMA((N_DEV - 1,)), pltpu.SemaphoreType.DMA((N_DEV - 1,)), pltpu.SemaphoreType.DMA]
        + ex.scratch)(big, *ex.srcs)


def _move_rows(src, segs, out_rows, *, name, zero=None):
    cols = src.shape[1]
    step = 256

    def body(src_ref, out_ref):
        for s0, d0, n in segs:
            for r in range(0, n, step):
                m = min(step, n - r)
                out_ref[d0 + r:d0 + r + m, :] = src_ref[s0 + r:s0 + r + m, :]
        if zero is not None:
            out_ref[zero[0]:zero[0] + zero[1], :] = jnp.zeros((zero[1], cols), src.dtype)

    vmem = pl.BlockSpec(memory_space=pltpu.VMEM)
    return pl.pallas_call(
        body, name=name, in_specs=[vmem], out_specs=vmem, out_shape=jax.ShapeDtypeStruct((out_rows, cols), src.dtype),
        compiler_params=pltpu.CompilerParams(vmem_limit_bytes=VMEM_BIG))(src)


def _in_proj(xb, wt, *, tm, tn, ride=None):
    m, k = xb.shape
    n = wt.shape[0]

    def body(x_ref, w_ref, o_ref):
        o_ref[...] = _dot(x_ref[...], w_ref[...], NT)

    return _call(
        body, name="in_proj", grid=(n // tn, m // tm),
        in_specs=[pl.BlockSpec((tm, k), lambda j, i: (i, 0)), pl.BlockSpec((tn, k), lambda j, i: (j, 0))],
        out_specs=[pl.BlockSpec((tm, tn), lambda j, i: (i, j))],
        out_shape=[jax.ShapeDtypeStruct((m, n), F32)],
        args=(xb, wt), sem=("parallel", "parallel"), vmem=VMEM_BIG, ride=ride)


def _in_proj_bwd(dh, wt, dr, *, tm, ride=None):
    m, n = dh.shape
    k = wt.shape[1]

    def body(dh_ref, w_ref, dr_ref, o_ref):
        o_ref[...] = ALPHA * dr_ref[...] + _dot(dh_ref[...], w_ref[...], NN)

    return _call(
        body, name="in_proj_bwd", grid=(m // tm,),
        in_specs=[pl.BlockSpec((tm, n), lambda i: (i, 0)), pl.BlockSpec((n, k), lambda i: (0, 0)),
                  pl.BlockSpec((tm, k), lambda i: (i, 0))],
        out_specs=[pl.BlockSpec((tm, k), lambda i: (i, 0))],
        out_shape=[jax.ShapeDtypeStruct((m, k), F32)],
        args=(dh, wt, dr), sem=("parallel",), vmem=VMEM_BIG, ride=ride)


def _mm_nt_into(dc, w, arr, col, *, tm, name):
    m, n = dc.shape
    k = w.shape[0]

    def body(dc_ref, w_ref, _arr_in, o_ref):
        o_ref[...] = _dot(dc_ref[...], w_ref[...], NT).astype(o_ref.dtype)

    return _call(
        body, name=name, grid=(m // tm,),
        in_specs=[pl.BlockSpec((tm, n), lambda i: (i, 0)), pl.BlockSpec((k, n), lambda i: (0, 0)), ANY],
        out_specs=[pl.BlockSpec((tm, k), lambda i: (i, col))],
        out_shape=[jax.ShapeDtypeStruct(arr.shape, arr.dtype)],
        args=(dc, w, arr), sem=("parallel",), aliases={2: 0})[0]


def _mm_tn(a, dc, *, tm, tk, name, a_block=None, out_dtype=F32):
    m = a.shape[0]
    k, a_col = (a.shape[1], None) if a_block is None else a_block
    n = dc.shape[1]
    ni = m // tm

    def body(a_ref, dc_ref, o_ref, acc):
        i = pl.program_id(1)

        @pl.when(i == 0)
        def _():
            acc[...] = jnp.zeros_like(acc)

        acc[...] += _dot(a_ref[...], dc_ref[...], TN)

        @pl.when(i == ni - 1)
        def _():
            o_ref[...] = acc[...].astype(o_ref.dtype)

    a_map = (lambda j, i: (i, j)) if a_col is None else (lambda j, i: (i, a_col))
    return _call(
        body, name=name, grid=(k // tk, ni),
        in_specs=[pl.BlockSpec((tm, tk), a_map), pl.BlockSpec((tm, n), lambda j, i: (i, 0))],
        out_specs=[pl.BlockSpec((tk, n), lambda j, i: (j, 0))],
        out_shape=[jax.ShapeDtypeStruct((k, n), out_dtype)],
        scratch=[pltpu.VMEM((tk, n), F32)],
        args=(a, dc), sem=("parallel", "arbitrary"), vmem=VMEM_BIG)[0]


def _head_cols(p):
    b = p * HEAD_W
    return (slice(b, b + HDK), slice(b + HDK, b + 2 * HDK), slice(b + 2 * HDK, b + 2 * HDK + HDV),
            slice(b + 2 * HDK + HDV, b + HEAD_W))


def _seg_cumsum(v, rowmod):
    sh = 1
    while sh < CHUNK:
        v = v + jnp.where(rowmod >= sh, pltpu.roll(v, sh, 0), 0.0)
        sh *= 2
    return v


def _seg_rcumsum(v, rowmod):
    t = v.shape[0]
    sh = 1
    while sh < CHUNK:
        v = v + jnp.where(rowmod < CHUNK - sh, pltpu.roll(v, t - sh, 0), 0.0)
        sh *= 2
    return v


def _gla_decay(alpha_ref, wup_ref, b_ref, g_scr):
    z = _dot(alpha_ref[...], wup_ref[...], NN) + b_ref[...]
    rowmod = lax.broadcasted_iota(jnp.int32, z.shape, 0) % CHUNK
    g_scr[...] = _seg_cumsum(_log_sigmoid(z) * (1.0 / GATE_TAU), rowmod)
    return z, rowmod


def _chunk_terms(hd_ref, g_scr, c, p):
    r0 = c * CHUNK
    gc = slice(p * HDK, (p + 1) * HDK)
    qc, kc, _, _ = _head_cols(p)
    g = g_scr[r0:r0 + CHUNK, gc]
    g_first = g_scr[r0:r0 + 1, gc]
    g_last = g_scr[r0 + CHUNK - 1:r0 + CHUNK, gc]
    ref = 0.5 * (g_first + g_last)
    ep = jnp.exp(g - ref)
    em = jnp.exp(ref - g)
    a = jnp.exp(g)
    dl = jnp.exp(g_last - g)
    qs = hd_ref[r0:r0 + CHUNK, qc] * Q_SCALE
    k = hd_ref[r0:r0 + CHUNK, kc]
    op = lambda v: v.astype(MXU)
    return dict(ep=ep, em=em, a=a, dl=dl, egl=jnp.exp(g_last), qs=qs, k=k,
                qe1=op(qs * ep), ke1=op(k * em), qe2=op(qs * em), ke2=op(k * ep), qa=op(qs * a), kd=op(k * dl))


def _scores(t, lower):
    return jnp.where(lower, _dot(t["qe1"], t["ke1"], NT), _dot(t["qe2"], t["ke2"], NT))


def _gla_specs(tt, row):
    return [
        pl.BlockSpec((tt, HPB * HEAD_W), lambda h, i: (row(i), HD0 // (HPB * HEAD_W) + h)),
        pl.BlockSpec((tt, 128), lambda h, i: (row(i), AL0 // 128)),
        pl.BlockSpec((128, HPB * HDK), lambda h, i: (0, h)),
        pl.BlockSpec((1, HPB * HDK), lambda h, i: (0, h)),
        pl.BlockSpec((1, HPB * HDV), lambda h, i: (0, h)),
    ]


def _gla_fwd(hh, wup, b_alpha, gnorm, *, tt, ride=None):
    s = hh.shape[0]
    nt = s // tt
    nct = tt // CHUNK

    def body(hd_ref, al_ref, wup_ref, b_ref, gn_ref, o_ref, ya_ref, st_ref, state, g_scr):
        @pl.when(pl.program_id(1) == 0)
        def _():
            state[...] = jnp.zeros_like(state)

        _gla_decay(al_ref, wup_ref, b_ref, g_scr)
        lower = lax.broadcasted_iota(jnp.int32, (CHUNK, CHUNK), 0) >= lax.broadcasted_iota(jnp.int32, (CHUNK, CHUNK), 1)
        for c in range(nct):
            r0 = c * CHUNK
            for p in range(HPB):
                _, _, vc, _ = _head_cols(p)
                t = _chunk_terms(hd_ref, g_scr, c, p)
                v = hd_ref[r0:r0 + CHUNK, vc]
                st = state[p]
                st_ref[p, c] = st
                o_ref[r0:r0 + CHUNK, p * HDV:(p + 1) * HDV] = _dot(_scores(t, lower), v, NN) + _dot(t["qa"], st, NT)
                state[p] = st * t["egl"] + _dot(v, t["kd"], TN)
        for p in range(HPB):
            oc = slice(p * HDV, (p + 1) * HDV)
            o = o_ref[:, oc]
            ohat = o * lax.rsqrt(jnp.mean(o * o, axis=-1, keepdims=True) + EPS)
            ga = hd_ref[:, _head_cols(p)[3]]
            ya_ref[:, oc] = (ohat * gn_ref[:, oc] * (ga * _sigmoid(ga))).astype(ya_ref.dtype)

    return _call(
        body, name="gla_fwd", grid=(HEADS // HPB, nt),
        in_specs=_gla_specs(tt, lambda i: i),
        out_specs=[
            pl.BlockSpec((tt, HPB * HDV), lambda h, i: (i, h)),
            pl.BlockSpec((tt, HPB * HDV), lambda h, i: (i, h)),
            pl.BlockSpec((HPB, nct, HDV, HDK), lambda h, i: (h, i, 0, 0)),
        ],
        out_shape=[
            jax.ShapeDtypeStruct((s, D), F32),
            jax.ShapeDtypeStruct((s, D), MXU),
            jax.ShapeDtypeStruct((HEADS, s // CHUNK, HDV, HDK), F32),
        ],
        scratch=[pltpu.VMEM((HPB, HDV, HDK), F32), pltpu.VMEM((tt, HPB * HDK), F32)],
        args=(hh, hh, wup, b_alpha, gnorm), ride=ride)


def _gla_bwd(hh, wup, b_alpha, gnorm, o, states, dya, dh, *, tt, ride=None):
    s = hh.shape[0]
    nt = s // tt
    nct = tt // CHUNK

    def body(hd_ref, al_ref, wup_ref, b_ref, gn_ref, o_ref, st_ref, dya_ref, _dh_in,
             dh_ref, dz_ref, dgn_ref, db_ref, dstate, g_scr, dg_scr, do_scr):
        @pl.when(pl.program_id(1) == 0)
        def _():
            dstate[...] = jnp.zeros_like(dstate)
            dgn_ref[...] = jnp.zeros_like(dgn_ref)
            db_ref[...] = jnp.zeros_like(db_ref)

        z, rowmod = _gla_decay(al_ref, wup_ref, b_ref, g_scr)

        for p in range(HPB):
            oc = slice(p * HDV, (p + 1) * HDV)
            gac = _head_cols(p)[3]
            o_t = o_ref[:, oc]
            rstd = lax.rsqrt(jnp.mean(o_t * o_t, axis=-1, keepdims=True) + EPS)
            ohat = o_t * rstd
            ga = hd_ref[:, gac]
            sg = _sigmoid(ga)
            dya_t = dya_ref[:, oc]
            gn = gn_ref[:, oc]
            dh_ref[:, gac] = (dya_t * ohat * gn * (sg * (1.0 + ga * (1.0 - sg)))).astype(dh_ref.dtype)
            don = dya_t * (ga * sg)
            dgn_ref[p] += jnp.sum(don * ohat, axis=0, keepdims=True)
            dohat = don * gn
            do_scr[:, oc] = rstd * (dohat - ohat * jnp.mean(dohat * ohat, axis=-1, keepdims=True))

        lower = lax.broadcasted_iota(jnp.int32, (CHUNK, CHUNK), 0) >= lax.broadcasted_iota(jnp.int32, (CHUNK, CHUNK), 1)
        last_row = lax.broadcasted_iota(jnp.int32, (CHUNK, HDK), 0) == CHUNK - 1
        for c in range(nct - 1, -1, -1):
            r0 = c * CHUNK
            rows = slice(r0, r0 + CHUNK)
            for p in range(HPB):
                qc, kc, vc, _ = _head_cols(p)
                t = _chunk_terms(hd_ref, g_scr, c, p)
                v = hd_ref[rows, vc]
                do = do_scr[rows, p * HDV:(p + 1) * HDV]
                st = st_ref[p, c]
                dst = dstate[p]
                a = _scores(t, lower)
                da = _dot(do, v, NT)
                da1 = jnp.where(lower, da, 0.0)
                da2 = jnp.where(lower, 0.0, da)
                dqe1 = _dot(da1, t["ke1"], NN)
                dke1 = _dot(da1, t["qe1"], TN)
                dqe2 = _dot(da2, t["ke2"], NN)
                dke2 = _dot(da2, t["qe2"], TN)
                dqa = _dot(do, st, NN)
                dkd = _dot(v, dst, NN)
                dh_ref[rows, vc] = (_dot(a, do, TN) + _dot(t["kd"], dst, NT)).astype(dh_ref.dtype)
                p1, p2, p3 = dqe1 * t["ep"], dqe2 * t["em"], dqa * t["a"]
                r1, r2, r3 = dke1 * t["em"], dke2 * t["ep"], dkd * t["dl"]
                dh_ref[rows, qc] = ((p1 + p2 + p3) * Q_SCALE).astype(dh_ref.dtype)
                dh_ref[rows, kc] = (r1 + r2 + r3).astype(dh_ref.dtype)
                r3k = r3 * t["k"]
                dgl = jnp.sum(r3k, axis=0, keepdims=True) + t["egl"] * jnp.sum(dst * st, axis=0, keepdims=True)
                dg = t["qs"] * (p1 - p2 + p3) + t["k"] * (r2 - r1) - r3k
                dg_scr[rows, p * HDK:(p + 1) * HDK] = dg + jnp.where(last_row, dgl, 0.0)
                dstate[p] = dst * t["egl"] + _dot(do, t["qa"], TN)

        dz = _seg_rcumsum(dg_scr[...], rowmod) * _sigmoid(-z) * (1.0 / GATE_TAU)
        dz_ref[...] = dz.astype(dz_ref.dtype)
        for p in range(HPB):
            db_ref[p] += jnp.sum(dz[:, p * HDK:(p + 1) * HDK], axis=0, keepdims=True)

    rev = lambda i: nt - 1 - i
    in_specs = _gla_specs(tt, rev) + [
        pl.BlockSpec((tt, HPB * HDV), lambda h, i: (rev(i), h)),
        pl.BlockSpec((HPB, nct, HDV, HDK), lambda h, i: (h, rev(i), 0, 0)),
        pl.BlockSpec((tt, HPB * HDV), lambda h, i: (rev(i), h)),
        ANY,
    ]
    return _call(
        body, name="gla_bwd", grid=(HEADS // HPB, nt), in_specs=in_specs,
        out_specs=[
            pl.BlockSpec((tt, HPB * HEAD_W), lambda h, i: (rev(i), HD0 // (HPB * HEAD_W) + h)),
            pl.BlockSpec((tt, HPB * HDK), lambda h, i: (rev(i), h)),
            pl.BlockSpec((HPB, 1, HDV), lambda h, i: (h, 0, 0)),
            pl.BlockSpec((HPB, 1, HDK), lambda h, i: (h, 0, 0)),
        ],
        out_shape=[
            jax.ShapeDtypeStruct(dh.shape, dh.dtype),
            jax.ShapeDtypeStruct((s, DK), MXU),
            jax.ShapeDtypeStruct((HEADS, 1, HDV), F32),
            jax.ShapeDtypeStruct((HEADS, 1, HDK), F32),
        ],
        scratch=[pltpu.VMEM((HPB, HDV, HDK), F32), pltpu.VMEM((tt, HPB * HDK), F32), pltpu.VMEM((tt, HPB * HDK), F32),
                 pltpu.VMEM((tt, HPB * HDV), F32)],
        args=(hh, hh, wup, b_alpha, gnorm, o, states, dya, dh), ride=ride, aliases={8: 0})


def _window_count(tile, tt, w):
    pos = tile * tt + lax.broadcasted_iota(jnp.int32, (tt, PG), 0) + 1
    return jnp.minimum(pos, w).astype(F32)


def _pool_fwd(hh, wpool, scale, *, tt):
    s = hh.shape[0]
    nt = s // tt

    def body(ug_ref, w_ref, sc_ref, pooled_ref, yb_ref, halo):
        i = pl.program_id(0)

        @pl.when(i == 0)
        def _():
            halo[...] = jnp.zeros_like(halo)

        for g, w in enumerate(POOL_WINDOWS):
            cols = slice(g * PG, (g + 1) * PG)
            u = ug_ref[:, cols]
            run = jnp.concatenate([halo[:, cols], u], axis=0)
            sh = 1
            while sh < w:
                run = run + pltpu.roll(run, sh, 0)
                sh *= 2
            pooled = run[HALO:, :] / _window_count(i, tt, w) - u
            pooled_ref[:, cols] = pooled.astype(pooled_ref.dtype)
            mixed = _dot(pooled, w_ref[g], NN)
            gb = ug_ref[:, slice(D + g * PG, D + (g + 1) * PG)]
            yb_ref[:, cols] = (mixed * sc_ref[:, cols] * (gb * _sigmoid(gb))).astype(yb_ref.dtype)
        halo[...] = ug_ref[tt - HALO:tt, :D]

    tile = pl.BlockSpec((tt, D), lambda i: (i, 0))
    return _call(
        body, name="pool_fwd", grid=(nt,),
        in_specs=[
            pl.BlockSpec((tt, 2 * D), lambda i: (i, PI0 // (2 * D))),
            pl.BlockSpec((len(POOL_WINDOWS), PG, PG), lambda i: (0, 0, 0)),
            pl.BlockSpec((1, D), lambda i: (0, 0)),
        ],
        out_specs=[tile] * 2,
        out_shape=[jax.ShapeDtypeStruct((s, D), MXU), jax.ShapeDtypeStruct((s, D), MXU)],
        scratch=[pltpu.VMEM((HALO, D), F32)], args=(hh, wpool, scale))


def _pool_bwd(hh, wpool, scale, pooled, dyb, dh, *, tt):
    s = hh.shape[0]
    nt = s // tt

    def body(gb_ref, w_ref, sc_ref, pooled_ref, dyb_ref, _dh_in, dh_ref, dw_ref, dsc_ref, halo):
        i = pl.program_id(0)
        tile = nt - 1 - i

        @pl.when(i == 0)
        def _():
            halo[...] = jnp.zeros_like(halo)
            dw_ref[...] = jnp.zeros_like(dw_ref)
            dsc_ref[...] = jnp.zeros_like(dsc_ref)

        for g, w in enumerate(POOL_WINDOWS):
            cols = slice(g * PG, (g + 1) * PG)
            gcols = slice(D + g * PG, D + (g + 1) * PG)
            gb = gb_ref[:, cols]
            sg = _sigmoid(gb)
            pooled = pooled_ref[:, cols]
            mixed = _dot(pooled, w_ref[g], NN)
            sc = sc_ref[:, cols]
            dyb = dyb_ref[:, cols]
            dh_ref[:, gcols] = (dyb * mixed * sc * (sg * (1.0 + gb * (1.0 - sg)))).astype(dh_ref.dtype)
            dms = dyb * (gb * sg)
            dsc_ref[:, cols] += jnp.sum(dms * mixed, axis=0, keepdims=True)
            dmixed = dms * sc
            dpooled = _dot(dmixed, w_ref[g], NT)
            dw_ref[g] += _dot(pooled, dmixed, TN)
            e = dpooled / _window_count(tile, tt, w)
            run = jnp.concatenate([e, halo[:, cols]], axis=0)
            sh = 1
            while sh < w:
                run = run + pltpu.roll(run, tt + HALO - sh, 0)
                sh *= 2
            dh_ref[:, cols] = (run[:tt, :] - dpooled).astype(dh_ref.dtype)
            halo[:, cols] = e[:HALO, :]

    rev = lambda i: nt - 1 - i
    tile = pl.BlockSpec((tt, D), lambda i: (rev(i), 0))
    wspec = pl.BlockSpec((len(POOL_WINDOWS), PG, PG), lambda i: (0, 0, 0))
    vec = pl.BlockSpec((1, D), lambda i: (0, 0))
    return _call(
        body, name="pool_bwd", grid=(nt,),
        in_specs=[pl.BlockSpec((tt, D), lambda i: (rev(i), GB0 // D)), wspec, vec, tile, tile, ANY],
        out_specs=[pl.BlockSpec((tt, 2 * D), lambda i: (rev(i), PI0 // (2 * D))), wspec, vec],
        out_shape=[jax.ShapeDtypeStruct(dh.shape, dh.dtype), jax.ShapeDtypeStruct((len(POOL_WINDOWS), PG, PG), F32),
                   jax.ShapeDtypeStruct((1, D), F32)],
        scratch=[pltpu.VMEM((HALO, D), F32)], args=(hh, wpool, scale, pooled, dyb, dh), aliases={5: 0})


def _merge_fwd(hh, x, ya, yb, wpa, wpb, wout, b_merge, ln_g, ln_b, *, tt):
    s = x.shape[0]

    def body(ml_ref, x_ref, ya_ref, yb_ref, wpa_ref, wpb_ref, wout_ref, bm_ref, g_ref, b_ref, r_ref, xn_ref, xnb_ref):
        pa = _dot(ya_ref[...], wpa_ref[...], NN)
        pb = _dot(yb_ref[...], wpb_ref[...], NN)
        merged = _sigmoid(ml_ref[:, :D] + bm_ref[:, :D]) * pa + _sigmoid(ml_ref[:, D:] + bm_ref[:, D:]) * pb
        r = ALPHA * x_ref[...] + _dot(merged, wout_ref[...], NN)
        r_ref[...] = r
        mu = jnp.mean(r, axis=-1, keepdims=True)
        xc = r - mu
        var = jnp.mean(xc * xc, axis=-1, keepdims=True)
        xn = xc * lax.rsqrt(var + EPS) * g_ref[...] + b_ref[...]
        xn_ref[...] = xn
        xnb_ref[...] = xn.astype(xnb_ref.dtype)

    tile = pl.BlockSpec((tt, D), lambda i: (i, 0))
    full = pl.BlockSpec((D, D), lambda i: (0, 0))
    vec = pl.BlockSpec((1, D), lambda i: (0, 0))
    return _call(
        body, name="merge_fwd", grid=(s // tt,),
        in_specs=[pl.BlockSpec((tt, 2 * D), lambda i: (i, ML0 // (2 * D))), tile, tile, tile, full, full, full,
                  pl.BlockSpec((1, 2 * D), lambda i: (0, 0)), vec, vec],
        out_specs=[tile] * 3, out_shape=[jax.ShapeDtypeStruct((s, D), F32)] * 2 + [jax.ShapeDtypeStruct((s, D), MXU)],
        args=(hh, x, ya, yb, wpa, wpb, wout, b_merge, ln_g, ln_b), sem=("parallel",), vmem=VMEM_BIG)


def _merge_bwd(hh, r, ya, yb, dout, wpa, wpb, wout, b_merge, ln_g, *, tt):
    s = r.shape[0]

    def body(ml_ref, r_ref, ya_ref, yb_ref, do_ref, wpa_ref, wpb_ref, wout_ref, bm_ref, g_ref,
             dh_ref, dr_ref, dpa_ref, dpb_ref, dwout_ref, dg_ref, db_ref, dbm_ref):
        @pl.when(pl.program_id(0) == 0)
        def _():
            dwout_ref[...] = jnp.zeros_like(dwout_ref)
            dg_ref[...] = jnp.zeros_like(dg_ref)
            db_ref[...] = jnp.zeros_like(db_ref)
            dbm_ref[...] = jnp.zeros_like(dbm_ref)

        rr = r_ref[...]
        mu = jnp.mean(rr, axis=-1, keepdims=True)
        xc = rr - mu
        rstd = lax.rsqrt(jnp.mean(xc * xc, axis=-1, keepdims=True) + EPS)
        xhat = xc * rstd
        do = do_ref[...]
        dg_ref[...] += jnp.sum(do * xhat, axis=0, keepdims=True)
        db_ref[...] += jnp.sum(do, axis=0, keepdims=True)
        dxh = do * g_ref[...]
        dr = rstd * (dxh - jnp.mean(dxh, axis=-1, keepdims=True) - xhat * jnp.mean(dxh * xhat, axis=-1, keepdims=True))
        dr_ref[...] = dr
        g_a = _sigmoid(ml_ref[:, :D] + bm_ref[:, :D])
        g_b = _sigmoid(ml_ref[:, D:] + bm_ref[:, D:])
        pa = _dot(ya_ref[...], wpa_ref[...], NN)
        pb = _dot(yb_ref[...], wpb_ref[...], NN)
        dwout_ref[...] += _dot(g_a * pa + g_b * pb, dr, TN)
        dm = _dot(dr, wout_ref[...], NT)
        dpa_ref[...] = (dm * g_a).astype(dpa_ref.dtype)
        dpb_ref[...] = (dm * g_b).astype(dpb_ref.dtype)
        dml_a = dm * pa * g_a * (1.0 - g_a)
        dml_b = dm * pb * g_b * (1.0 - g_b)
        dh_ref[:, :D] = dml_a.astype(dh_ref.dtype)
        dh_ref[:, D:] = dml_b.astype(dh_ref.dtype)
        dbm_ref[:, :D] += jnp.sum(dml_a, axis=0, keepdims=True)
        dbm_ref[:, D:] += jnp.sum(dml_b, axis=0, keepdims=True)

    tile = pl.BlockSpec((tt, D), lambda i: (i, 0))
    full = pl.BlockSpec((D, D), lambda i: (0, 0))
    vec = pl.BlockSpec((1, D), lambda i: (0, 0))
    vec2 = pl.BlockSpec((1, 2 * D), lambda i: (0, 0))
    mlb = pl.BlockSpec((tt, 2 * D), lambda i: (i, ML0 // (2 * D)))
    return _call(
        body, name="merge_bwd", grid=(s // tt,),
        in_specs=[mlb, tile, tile, tile, tile, full, full, full, vec2, vec],
        out_specs=[mlb, tile, tile, tile, full, vec, vec, vec2],
        out_shape=[
            jax.ShapeDtypeStruct((s, HP), MXU), jax.ShapeDtypeStruct((s, D), F32),
            jax.ShapeDtypeStruct((s, D), MXU), jax.ShapeDtypeStruct((s, D), MXU),
            jax.ShapeDtypeStruct((D, D), F32), jax.ShapeDtypeStruct((1, D), F32),
            jax.ShapeDtypeStruct((1, D), F32), jax.ShapeDtypeStruct((1, 2 * D), F32),
        ],
        args=(hh, r, ya, yb, dout, wpa, wpb, wout, b_merge, ln_g), vmem=VMEM_BIG)


def _proj_bwd(y, dp, w, *, tt, name):
    s = y.shape[0]

    def body(y_ref, dp_ref, w_ref, dy_ref, dw_ref):
        @pl.when(pl.program_id(0) == 0)
        def _():
            dw_ref[...] = jnp.zeros_like(dw_ref)

        dp = dp_ref[...]
        dy_ref[...] = _dot(dp, w_ref[...], NT)
        dw_ref[...] += _dot(y_ref[...], dp, TN)

    tile = pl.BlockSpec((tt, D), lambda i: (i, 0))
    full = pl.BlockSpec((D, D), lambda i: (0, 0))
    return _call(
        body, name=name, grid=(s // tt,), in_specs=[tile, tile, full], out_specs=[tile, full],
        out_shape=[jax.ShapeDtypeStruct((s, D), F32), jax.ShapeDtypeStruct((D, D), F32)], args=(y, dp, w))


def _loss_head(y, target, *, tt):
    s = y.shape[0]

    def body(y_ref, t_ref, loss_ref, dy_ref):
        @pl.when(pl.program_id(0) == 0)
        def _():
            loss_ref[...] = jnp.zeros_like(loss_ref)

        err = y_ref[...] - t_ref[...]
        dy_ref[...] = err * (1.0 / D)
        per_tok = jnp.mean(err * err, axis=-1, keepdims=True)
        loss_ref[...] += 0.5 * jnp.sum(per_tok, axis=0, keepdims=True)

    tile = pl.BlockSpec((tt, D), lambda i: (i, 0))
    return _call(
        body, name="loss_head", grid=(s // tt,), in_specs=[tile, tile],
        out_specs=[pl.BlockSpec((1, 1), lambda i: (0, 0)), tile],
        out_shape=[jax.ShapeDtypeStruct((1, 1), F32), jax.ShapeDtypeStruct((s, D), F32)], args=(y, target))


def _adamw(parts, w, m, v, *, tr, tc, name):
    nl, rows, cols = w.shape

    def body(p_ref, w_ref, m_ref, v_ref, g_ref, d_ref, nm_ref, nv_ref):
        g = p_ref[0, 0].astype(F32)
        for q in range(1, N_DEV):
            g = g + p_ref[0, q].astype(F32)
        g_ref[0] = g
        nm = ADAM_B1 * m_ref[0] + (1.0 - ADAM_B1) * g
        nv = ADAM_B2 * v_ref[0] + (1.0 - ADAM_B2) * (g * g)
        nm_ref[0] = nm
        nv_ref[0] = nv
        m_hat = nm / (1.0 - ADAM_B1 ** ADAM_STEP)
        v_hat = nv / (1.0 - ADAM_B2 ** ADAM_STEP)
        d_ref[0] = -ADAM_LR * (m_hat / (jnp.sqrt(v_hat) + ADAM_EPS) + ADAM_WD * w_ref[0])

    tile = pl.BlockSpec((1, tr, tc), lambda l, i, j: (l, i, j))
    return _call(
        body, name=name, grid=(nl, rows // tr, cols // tc),
        in_specs=[pl.BlockSpec((1, N_DEV, tr, tc), lambda l, i, j: (l, 0, i, j)), tile, tile, tile],
        out_specs=[tile] * 4, out_shape=[jax.ShapeDtypeStruct((nl, rows, cols), F32)] * 4,
        args=(parts, w, m, v), sem=("parallel", "parallel", "parallel"))


def _from_devices(g, axis):
    nd = g.ndim - 1
    perm = list(range(1, axis + 1)) + [0] + list(range(axis + 1, nd + 1))
    shape = list(g.shape[1:])
    shape[axis] *= N_DEV
    return jnp.transpose(g, perm).reshape(shape)


def _to_devices(a, axis):
    shape = list(a.shape)
    t = a.reshape(shape[:axis] + [N_DEV, shape[axis] // N_DEV] + shape[axis + 1:])
    return jnp.transpose(t, [axis] + list(range(0, axis)) + list(range(axis + 1, t.ndim)))


def _h_row_segments():
    segs = [(O_PI, PI0, IN_COLS - O_PI), (O_AL, AL0, RANK)]
    for h in range(HEADS):
        base = HD0 + h * HEAD_W
        segs += [(O_Q + h * HDK, base, HDK), (O_K + h * HDK, base + HDK, HDK),
                 (O_V + h * HDV, base + 2 * HDK, HDV), (O_GA + h * HDV, base + 2 * HDK + HDV, HDV)]
    return segs


def _h_weight_t(parts):
    return _move_rows(parts.reshape(IN_COLS, D), _h_row_segments(), HP, name="w_in_rows", zero=(AL0 + RANK, AL_W - RANK))


def _w_in_grad_parts_t(dwt):
    g = _move_rows(dwt, [(d0, s0, n) for s0, d0, n in _h_row_segments()], IN_COLS, name="w_in_grad_rows")
    return g.reshape(N_DEV, SHARD, D)


def kernel(x, w_in, w_alpha_up, b_alpha, gla_norm_g, w_pool_grp, pool_scale, b_merge, w_proj_a, w_proj_b, w_out, ln_g, ln_b, loss_target, m_w_in, m_w_alpha_up, m_b_alpha, m_gla_norm_g, m_w_pool_grp, m_pool_scale, m_b_merge, m_w_proj_a, m_w_proj_b, m_w_out, m_ln_g, m_ln_b, v_w_in, v_w_alpha_up, v_b_alpha, v_gla_norm_g, v_w_pool_grp, v_pool_scale, v_b_merge, v_w_proj_a, v_w_proj_b, v_w_out, v_ln_g, v_ln_b):
    s = x.shape[1]
    tt = min(256, s)
    tm = min(512, s)
    tn = HP // 3
    xs = x.reshape(s, D)

    tr3 = lambda a: jnp.transpose(a, (0, 2, 1))
    w_in_s = tr3(w_in).astype(WIRE)
    proj_s = jnp.stack([w_proj_a, w_proj_b, w_out], axis=1).astype(WIRE)
    pool_s = w_pool_grp.astype(WIRE)

    g_in, g_up, g_gn = _gather_first(w_in_s[0], [w_alpha_up.astype(WIRE), gla_norm_g], name="gather_first")
    wup = jnp.pad(_from_devices(g_up, 2), ((0, 0), (0, AL_W - RANK), (0, 0)))
    gn = _from_devices(g_gn, 2).reshape(DEPTH, 1, D)

    saved, wt_all, proj_all, pool_all = [], [], [], []
    cur, cur_b = xs, xs.astype(MXU)
    g_proj = g_pool = None
    for l in range(DEPTH):
        wt = _h_weight_t(g_in)
        nxt_l = l + 1 < DEPTH
        res = _in_proj(cur_b, wt, tm=tm, tn=tn, ride=_Exchange([(w_in_s[l + 1], True)]) if nxt_l else None)
        hh = res[0]
        if nxt_l:
            g_in = res[1]
        layers = ([0] if l == 0 else []) + ([l + 1] if nxt_l else [])
        res = _gla_fwd(hh, wup[l], b_alpha[l:l + 1], gn[l], tt=tt,
                       ride=_Exchange([(a[j], True) for j in layers for a in (proj_s, pool_s)]) if layers else None)
        o, ya, states = res[:3]
        got = {j: res[3 + 2 * t:5 + 2 * t] for t, j in enumerate(layers)}
        if l == 0:
            g_proj, g_pool = got[0]
        proj = _from_devices(g_proj, 1)
        pool = _from_devices(g_pool, 1)
        if nxt_l:
            g_proj, g_pool = got[l + 1]
        wt_all.append(wt), proj_all.append(proj), pool_all.append(pool)
        pooled, yb = _pool_fwd(hh, pool, pool_scale[l:l + 1], tt=tt)
        r, nxt, nxt_b = _merge_fwd(hh, cur, ya, yb, proj[0], proj[1], proj[2],
                                   b_merge[l:l + 1], ln_g[l:l + 1], ln_b[l:l + 1], tt=tt)
        saved.append(dict(xb=cur_b, hh=hh, o=o, ya=ya, states=states, pooled=pooled, yb=yb, r=r))
        cur, cur_b = nxt, nxt_b

    loss_part, dcur = _loss_head(cur, loss_target.reshape(s, D), tt=tt)
    loss = lax.psum(loss_part[0, 0], ("x", "y", "c"))

    small = {k: [None] * DEPTH for k in ("w_up", "b_alpha", "gnorm", "pool_scale", "b_merge", "ln_g", "ln_b")}
    parts = {k: [None] * DEPTH for k in ("w_in", "proj", "pool")}
    for l in range(DEPTH - 1, -1, -1):
        sv = saved[l]
        hh = sv["hh"]
        dh, dr, dpa, dpb, dw_out, dln_g, dln_b, db_merge = _merge_bwd(
            hh, sv["r"], sv["ya"], sv["yb"], dcur, proj_all[l][0], proj_all[l][1], proj_all[l][2], b_merge[l:l + 1], ln_g[l:l + 1], tt=tt)
        dya, dw_pa = _proj_bwd(sv["ya"], dpa, proj_all[l][0], tt=tt, name="proj_a_bwd")
        dyb, dw_pb = _proj_bwd(sv["yb"], dpb, proj_all[l][1], tt=tt, name="proj_b_bwd")
        dh, dw_pool, dscale = _pool_bwd(hh, pool_all[l], pool_scale[l:l + 1], sv["pooled"], dyb, dh, tt=tt)
        ride = _Exchange([(_to_devices(jnp.stack([dw_pa, dw_pb, dw_out]), 1).astype(WIRE), False),
                          (_to_devices(dw_pool, 1).astype(WIRE), False)])
        dh, dz, dgn, db_al, parts["proj"][l], parts["pool"][l] = _gla_bwd(
            hh, wup[l], b_alpha[l:l + 1], gn[l], sv["o"], sv["states"], dya, dh, tt=tt, ride=ride)
        dh = _mm_nt_into(dz, wup[l], dh, AL0 // AL_W, tm=tm, name="alpha_bwd")
        dw_up = _mm_tn(hh, dz, tm=tm, tk=AL_W, name="w_up_grad", a_block=(AL_W, AL0 // AL_W))
        dwt = _mm_tn(dh, sv["xb"], tm=tm, tk=tn, name="w_in_grad", out_dtype=WIRE)
        dcur, parts["w_in"][l] = _in_proj_bwd(dh, wt_all[l], dr, tm=tt, ride=_Exchange([(_w_in_grad_parts_t(dwt), False)]))

        small["w_up"][l] = dw_up[:RANK]
        small["b_alpha"][l] = db_al.reshape(DK)
        small["gnorm"][l] = dgn.reshape(HEADS, HDV)
        small["pool_scale"][l] = dscale[0]
        small["b_merge"][l] = db_merge[0]
        small["ln_g"][l], small["ln_b"][l] = dln_g[0], dln_b[0]
    grad_x = dcur[None]
    sm = {k: jnp.stack(v) for k, v in small.items()}

    rep = (("b_alpha", b_alpha, m_b_alpha, v_b_alpha), ("pool_scale", pool_scale, m_pool_scale, v_pool_scale),
           ("b_merge", b_merge, m_b_merge, v_b_merge), ("ln_g", ln_g, m_ln_g, v_ln_g), ("ln_b", ln_b, m_ln_b, v_ln_b))
    cat = lambda arrs: jnp.concatenate(arrs, axis=1)
    p_up, p_gn, p_rep = _exchange([(_to_devices(sm["w_up"], 2), False), (_to_devices(sm["gnorm"], 2), False),
                                   (cat([sm[nm] for nm, _, _, _ in rep]), True)], name="exchange_small_grads")

    def update(p, w, m, v, tr, name, layered=True, tc=None):
        shape = w.shape
        nl = shape[0] if layered else 1
        cols = shape[-1]
        flat = lambda a: a.reshape(nl, -1, cols)
        outs = _adamw(p.reshape(nl, N_DEV, -1, cols), flat(w), flat(m), flat(v), tr=tr, tc=tc or cols, name=name)
        return [o_.reshape(shape) for o_ in outs]

    res = {}
    res["w_in"] = [tr3(o_) for o_ in update(jnp.stack(parts["w_in"]), tr3(w_in), tr3(m_w_in), tr3(v_w_in), SHARD, "adamw_w_in", tc=256)]
    proj_p = jnp.stack(parts["proj"])
    for j, (nm, w, m, v) in enumerate((("w_proj_a", w_proj_a, m_w_proj_a, v_w_proj_a), ("w_proj_b", w_proj_b, m_w_proj_b, v_w_proj_b),
                                       ("w_out", w_out, m_w_out, v_w_out))):
        res[nm] = update(proj_p[:, :, j], w, m, v, D // N_DEV, "adamw_" + nm)
    res["w_pool_grp"] = update(jnp.stack(parts["pool"]), w_pool_grp, m_w_pool_grp, v_w_pool_grp, 128, "adamw_w_pool")
    res["w_alpha_up"] = update(p_up, w_alpha_up, m_w_alpha_up, v_w_alpha_up, DEPTH * RANK, "adamw_w_up", layered=False)
    res["gla_norm_g"] = update(p_gn, gla_norm_g, m_gla_norm_g, v_gla_norm_g, DEPTH * HEADS, "adamw_gnorm", layered=False)
    rep_out = update(p_rep, cat([w for _, w, _, _ in rep]), cat([m for _, _, m, _ in rep]), cat([v for _, _, _, v in rep]),
                     DEPTH, "adamw_small", layered=False)
    off = 0
    for nm, w, _, _ in rep:
        n = w.shape[1]
        res[nm] = [o_[:, off:off + n] for o_ in rep_out]
        off += n

    order = ("w_in", "w_alpha_up", "b_alpha", "gla_norm_g", "w_pool_grp", "pool_scale", "b_merge", "w_proj_a", "w_proj_b",
             "w_out", "ln_g", "ln_b")
    return (loss, grad_x, *[res[n][0] for n in order], *[res[n][1] for n in order],
            *[res[n][2] for n in order], *[res[n][3] for n in order])
```

```python
import jax
import jax.numpy as jnp
from jax import lax
from jax.experimental import pallas as pl
from jax.experimental.pallas import tpu as pltpu

F32 = jnp.float32
MXU = jnp.bfloat16
WIRE = jnp.bfloat16

N_DEV = 8
DEPTH = 4
D = 1024
HEADS = 4
DK = D // 2
HDK = DK // HEADS
HDV = D // HEADS
RANK = 16
CHUNK = 64
GATE_TAU = 16.0
POOL_WINDOWS = (2, 4, 8, 16)
PG = D // len(POOL_WINDOWS)
HALO = 16
IN_COLS = 7184
SHARD = IN_COLS // N_DEV
ALPHA = (2.0 * DEPTH) ** 0.25
EPS = 1e-5
Q_SCALE = HDK ** -0.5

ADAM_LR, ADAM_B1, ADAM_B2, ADAM_EPS, ADAM_WD, ADAM_STEP = 0.001, 0.9, 0.999, 1e-08, 0.01, 10

PI0, GB0, ML0, AL0, AL_W = 0, D, 2 * D, 4 * D, 512
HD0 = AL0 + AL_W
HEAD_W = 2 * HDK + 2 * HDV
HP = HD0 + HEADS * HEAD_W
HPB = 2
O_Q, O_K, O_V, O_GA, O_AL, O_PI, O_GB, O_ML = 0, DK, 2 * DK, 2 * DK + D, 2 * DK + 2 * D, 2 * DK + 2 * D + RANK, \
    2 * DK + 3 * D + RANK, 2 * DK + 4 * D + RANK

VMEM_BIG = 56 * 1024 * 1024
VMEM_MID = 40 * 1024 * 1024

NN = ((1,), (0,))
NT = ((1,), (1,))
TN = ((0,), (0,))

HBM = pl.BlockSpec(memory_space=pltpu.HBM)
ANY = pl.BlockSpec(memory_space=pl.ANY)


def _dot(a, b, dims):
    return lax.dot_general(a.astype(MXU), b.astype(MXU), (dims, ((), ())), preferred_element_type=F32)


def _params(sem, vmem):
    return pltpu.CompilerParams(dimension_semantics=sem, vmem_limit_bytes=vmem)


def _sigmoid(x):
    return 1.0 / (1.0 + jnp.exp(-x))


def _log_sigmoid(z):
    return jnp.minimum(z, 0.0) - jnp.log(1.0 + jnp.exp(-jnp.abs(z)))


class _Exchange:
    def __init__(self, items):
        self.items = [(s, bool(g)) for s, g in items]
        self.n = len(self.items)
        self.srcs = [s for s, _ in self.items]
        self.in_specs = [HBM] * self.n
        self.out_specs = [HBM] * self.n
        self.out_shape = [jax.ShapeDtypeStruct((N_DEV,) + tuple(s.shape if g else s.shape[1:]), s.dtype) for s, g in self.items]
        self.scratch = [pltpu.SemaphoreType.DMA((self.n * (N_DEV - 1),)), pltpu.SemaphoreType.DMA((self.n * (N_DEV - 1),)),
                        pltpu.SemaphoreType.DMA((self.n,))]

    def copies(self, src_refs, out_refs, send_sems, recv_sems, local_sems):
        x, y, c = lax.axis_index("x"), lax.axis_index("y"), lax.axis_index("c")
        me = 4 * x + 2 * y + c
        copies = []
        for t, (_, gather) in enumerate(self.items):
            src_ref, out_ref = src_refs[t], out_refs[t]
            copies.append(pltpu.make_async_copy(src_ref if gather else src_ref.at[me], out_ref.at[me], local_sems.at[t]))
            for k in range(1, N_DEV):
                px = 1 - x if k & 4 else x
                py = 1 - y if k & 2 else y
                pc = 1 - c if k & 1 else c
                peer = 4 * px + 2 * py + pc
                sem = t * (N_DEV - 1) + k - 1
                copies.append(pltpu.make_async_remote_copy(
                    src_ref=src_ref if gather else src_ref.at[peer],
                    dst_ref=out_ref.at[me],
                    send_sem=send_sems.at[sem],
                    recv_sem=recv_sems.at[sem],
                    device_id=(px, py, pc),
                    device_id_type=pl.DeviceIdType.MESH,
                ))
        return copies


def _grid_ends(grid):
    first = last = None
    for a, n in enumerate(grid):
        f = pl.program_id(a) == 0
        e = pl.program_id(a) == n - 1
        first = f if first is None else first & f
        last = e if last is None else last & e
    return first, last


def _call(body, *, name, grid, in_specs, out_specs, out_shape, args, scratch=(), sem=None, vmem=VMEM_MID, ride=None, aliases=None):
    n_in, n_out, n_scr = len(in_specs), len(out_specs), len(scratch)
    sem = sem or ("arbitrary",) * len(grid)
    if ride is None:
        return pl.pallas_call(body, name=name, grid=grid, in_specs=in_specs, out_specs=out_specs, out_shape=out_shape,
                              scratch_shapes=list(scratch), compiler_params=_params(sem, vmem),
                              input_output_aliases=aliases or {})(*args)
    r = ride.n

    def riding(*refs):
        ins, rsrc = refs[:n_in], refs[n_in:n_in + r]
        outs, rout = refs[n_in + r:n_in + r + n_out], refs[n_in + r + n_out:n_in + 2 * r + n_out]
        scr = refs[n_in + 2 * r + n_out:n_in + 2 * r + n_out + n_scr]
        send_sems, recv_sems, local_sems = refs[n_in + 2 * r + n_out + n_scr:]
        first, last = _grid_ends(grid)
        copies = ride.copies(rsrc, rout, send_sems, recv_sems, local_sems)

        @pl.when(first)
        def _():
            for cp in copies:
                cp.start()

        body(*ins, *outs, *scr)

        @pl.when(last)
        def _():
            for cp in copies:
                cp.wait()

    return pl.pallas_call(riding, name=name, grid=grid, in_specs=list(in_specs) + ride.in_specs,
                          out_specs=list(out_specs) + ride.out_specs, out_shape=list(out_shape) + ride.out_shape,
                          scratch_shapes=list(scratch) + ride.scratch,
                          compiler_params=_params(("arbitrary",) * len(grid), vmem),
                          input_output_aliases=aliases or {})(*args, *ride.srcs)


def _exchange(items, *, name):
    ex = _Exchange(items)

    def body(*refs):
        copies = ex.copies(refs[:ex.n], refs[ex.n:2 * ex.n], *refs[2 * ex.n:])
        for cp in copies:
            cp.start()
        for cp in copies:
            cp.wait()

    return pl.pallas_call(body, name=name, in_specs=ex.in_specs, out_specs=ex.out_specs, out_shape=ex.out_shape,
                          scratch_shapes=ex.scratch)(*ex.srcs)


def _gather_first(big, smalls, *, name):
    ex = _Exchange([(a, True) for a in smalls])

    def body(*refs):
        big_ref, small_src = refs[0], refs[1:1 + ex.n]
        out_ref, small_out = refs[1 + ex.n], refs[2 + ex.n:2 + 2 * ex.n]
        send_sems, recv_sems, local_sem = refs[2 + 2 * ex.n:5 + 2 * ex.n]
        x, y, c = lax.axis_index("x"), lax.axis_index("y"), lax.axis_index("c")
        me, sibling = (x, y, c), (x, y, 1 - c)
        chips = [(1 - x, y), (x, 1 - y), (1 - x, 1 - y)]

        def slot(px, py, pc):
            return out_ref.at[4 * px + 2 * py + pc]

        def copy(k, block, to, src=None):
            return pltpu.make_async_remote_copy(
                src_ref=slot(*block) if src is None else src, dst_ref=slot(*block),
                send_sem=send_sems.at[k], recv_sem=recv_sems.at[k], device_id=to, device_id_type=pl.DeviceIdType.MESH)

        small = ex.copies(small_src, small_out, *refs[5 + 2 * ex.n:])
        mine = pltpu.make_async_copy(big_ref, slot(*me), local_sem)
        mine.start()
        first = [copy(0, me, sibling, src=big_ref)] + [copy(1 + j, me, (*chip, c), src=big_ref) for j, chip in enumerate(chips)]
        for cp in first + small:
            cp.start()
        passed = [copy(4 + j, (*chip, c), sibling) for j, chip in enumerate(chips)]
        for j, chip in enumerate(chips):
            copy(1 + j, (*chip, c), me).wait_recv()
            passed[j].start()
        copy(0, sibling, me).wait_recv()
        for j, chip in enumerate(chips):
            copy(4 + j, (*chip, 1 - c), me).wait_recv()
        for cp in first + passed:
            cp.wait_send()
        mine.wait()
        for cp in small:
            cp.wait()

    return pl.pallas_call(
        body, name=name, in_specs=[HBM] + ex.in_specs, out_specs=[HBM] + ex.out_specs,
        out_shape=[jax.ShapeDtypeStruct((N_DEV,) + tuple(big.shape), big.dtype)] + ex.out_shape,
        scratch_shapes=[pltpu.SemaphoreType.DMA((N_DEV - 1,)), pltpu.SemaphoreType.DMA((N_DEV - 1,)), pltpu.SemaphoreType.DMA]
        + ex.scratch)(big, *ex.srcs)


def _move_rows(src, segs, out_rows, *, name, zero=None):
    cols = src.shape[1]
    step = 256

    def body(src_ref, out_ref):
        for s0, d0, n in segs:
            for r in range(0, n, step):
                m = min(step, n - r)
                out_ref[d0 + r:d0 + r + m, :] = src_ref[s0 + r:s0 + r + m, :]
        if zero is not None:
            out_ref[zero[0]:zero[0] + zero[1], :] = jnp.zeros((zero[1], cols), src.dtype)

    vmem = pl.BlockSpec(memory_space=pltpu.VMEM)
    return pl.pallas_call(
        body, name=name, in_specs=[vmem], out_specs=vmem, out_shape=jax.ShapeDtypeStruct((out_rows, cols), src.dtype),
        compiler_params=pltpu.CompilerParams(vmem_limit_bytes=VMEM_BIG))(src)


def _in_proj(xb, wt, *, tm, tn, ride=None):
    m, k = xb.shape
    n = wt.shape[0]

    def body(x_ref, w_ref, o_ref):
        o_ref[...] = _dot(x_ref[...], w_ref[...], NT)

    return _call(
        body, name="in_proj", grid=(n // tn, m // tm),
        in_specs=[pl.BlockSpec((tm, k), lambda j, i: (i, 0)), pl.BlockSpec((tn, k), lambda j, i: (j, 0))],
        out_specs=[pl.BlockSpec((tm, tn), lambda j, i: (i, j))],
        out_shape=[jax.ShapeDtypeStruct((m, n), F32)],
        args=(xb, wt), sem=("parallel", "parallel"), vmem=VMEM_BIG, ride=ride)


def _in_proj_bwd(dh, wt, dr, *, tm, ride=None):
    m, n = dh.shape
    k = wt.shape[1]

    def body(dh_ref, w_ref, dr_ref, o_ref):
        o_ref[...] = ALPHA * dr_ref[...] + _dot(dh_ref[...], w_ref[...], NN)

    return _call(
        body, name="in_proj_bwd", grid=(m // tm,),
        in_specs=[pl.BlockSpec((tm, n), lambda i: (i, 0)),
                  pl.BlockSpec((n, k), lambda i: (0, 0), pipeline_mode=pl.Buffered(1)),
                  pl.BlockSpec((tm, k), lambda i: (i, 0))],
        out_specs=[pl.BlockSpec((tm, k), lambda i: (i, 0))],
        out_shape=[jax.ShapeDtypeStruct((m, k), F32)],
        args=(dh, wt, dr), sem=("parallel",), vmem=VMEM_BIG, ride=ride)


def _mm_nt_into(dc, w, arr, col, *, tm, name):
    m, n = dc.shape
    k = w.shape[0]

    def body(dc_ref, w_ref, _arr_in, o_ref):
        o_ref[...] = _dot(dc_ref[...], w_ref[...], NT).astype(o_ref.dtype)

    return _call(
        body, name=name, grid=(m // tm,),
        in_specs=[pl.BlockSpec((tm, n), lambda i: (i, 0)), pl.BlockSpec((k, n), lambda i: (0, 0)), ANY],
        out_specs=[pl.BlockSpec((tm, k), lambda i: (i, col))],
        out_shape=[jax.ShapeDtypeStruct(arr.shape, arr.dtype)],
        args=(dc, w, arr), sem=("parallel",), aliases={2: 0})[0]


def _mm_tn(a, dc, *, tm, tk, name, a_block=None, out_dtype=F32):
    m = a.shape[0]
    k, a_col = (a.shape[1], None) if a_block is None else a_block
    n = dc.shape[1]
    ni = m // tm

    def body(a_ref, dc_ref, o_ref, acc):
        i = pl.program_id(1)

        @pl.when(i == 0)
        def _():
            acc[...] = jnp.zeros_like(acc)

        acc[...] += _dot(a_ref[...], dc_ref[...], TN)

        @pl.when(i == ni - 1)
        def _():
            o_ref[...] = acc[...].astype(o_ref.dtype)

    a_map = (lambda j, i: (i, j)) if a_col is None else (lambda j, i: (i, a_col))
    return _call(
        body, name=name, grid=(k // tk, ni),
        in_specs=[pl.BlockSpec((tm, tk), a_map), pl.BlockSpec((tm, n), lambda j, i: (i, 0))],
        out_specs=[pl.BlockSpec((tk, n), lambda j, i: (j, 0))],
        out_shape=[jax.ShapeDtypeStruct((k, n), out_dtype)],
        scratch=[pltpu.VMEM((tk, n), F32)],
        args=(a, dc), sem=("parallel", "arbitrary"), vmem=VMEM_BIG)[0]


def _head_cols(p):
    b = p * HEAD_W
    return (slice(b, b + HDK), slice(b + HDK, b + 2 * HDK), slice(b + 2 * HDK, b + 2 * HDK + HDV),
            slice(b + 2 * HDK + HDV, b + HEAD_W))


def _seg_cumsum(v, rowmod):
    sh = 1
    while sh < CHUNK:
        v = v + jnp.where(rowmod >= sh, pltpu.roll(v, sh, 0), 0.0)
        sh *= 2
    return v


def _seg_rcumsum(v, rowmod):
    t = v.shape[0]
    sh = 1
    while sh < CHUNK:
        v = v + jnp.where(rowmod < CHUNK - sh, pltpu.roll(v, t - sh, 0), 0.0)
        sh *= 2
    return v


def _gla_decay(alpha_ref, wup_ref, b_ref, g_scr):
    z = _dot(alpha_ref[...], wup_ref[...], NN) + b_ref[...]
    rowmod = lax.broadcasted_iota(jnp.int32, z.shape, 0) % CHUNK
    g_scr[...] = _seg_cumsum(_log_sigmoid(z) * (1.0 / GATE_TAU), rowmod)
    return z, rowmod


def _chunk_terms(hd_ref, g_scr, c, p):
    r0 = c * CHUNK
    gc = slice(p * HDK, (p + 1) * HDK)
    qc, kc, _, _ = _head_cols(p)
    g = g_scr[r0:r0 + CHUNK, gc]
    g_first = g_scr[r0:r0 + 1, gc]
    g_last = g_scr[r0 + CHUNK - 1:r0 + CHUNK, gc]
    ref = 0.5 * (g_first + g_last)
    ep = jnp.exp(g - ref)
    em = jnp.exp(ref - g)
    a = jnp.exp(g)
    dl = jnp.exp(g_last - g)
    qs = hd_ref[r0:r0 + CHUNK, qc] * Q_SCALE
    k = hd_ref[r0:r0 + CHUNK, kc]
    op = lambda v: v.astype(MXU)
    return dict(ep=ep, em=em, a=a, dl=dl, egl=jnp.exp(g_last), qs=qs, k=k,
                qe1=op(qs * ep), ke1=op(k * em), qe2=op(qs * em), ke2=op(k * ep), qa=op(qs * a), kd=op(k * dl))


def _scores(t, lower):
    return jnp.where(lower, _dot(t["qe1"], t["ke1"], NT), _dot(t["qe2"], t["ke2"], NT))


def _gla_specs(tt, row):
    return [
        pl.BlockSpec((tt, HPB * HEAD_W), lambda h, i: (row(i), HD0 // (HPB * HEAD_W) + h)),
        pl.BlockSpec((tt, 128), lambda h, i: (row(i), AL0 // 128)),
        pl.BlockSpec((128, HPB * HDK), lambda h, i: (0, h)),
        pl.BlockSpec((1, HPB * HDK), lambda h, i: (0, h)),
        pl.BlockSpec((1, HPB * HDV), lambda h, i: (0, h)),
    ]


def _gla_fwd(hh, wup, b_alpha, gnorm, *, tt, ride=None):
    s = hh.shape[0]
    nt = s // tt
    nct = tt // CHUNK

    def body(hd_ref, al_ref, wup_ref, b_ref, gn_ref, o_ref, ya_ref, st_ref, state, g_scr):
        @pl.when(pl.program_id(1) == 0)
        def _():
            state[...] = jnp.zeros_like(state)

        _gla_decay(al_ref, wup_ref, b_ref, g_scr)
        lower = lax.broadcasted_iota(jnp.int32, (CHUNK, CHUNK), 0) >= lax.broadcasted_iota(jnp.int32, (CHUNK, CHUNK), 1)
        for c in range(nct):
            r0 = c * CHUNK
            for p in range(HPB):
                _, _, vc, _ = _head_cols(p)
                t = _chunk_terms(hd_ref, g_scr, c, p)
                v = hd_ref[r0:r0 + CHUNK, vc]
                st = state[p]
                st_ref[p, c] = st
                o_ref[r0:r0 + CHUNK, p * HDV:(p + 1) * HDV] = _dot(_scores(t, lower), v, NN) + _dot(t["qa"], st, NT)
                state[p] = st * t["egl"] + _dot(v, t["kd"], TN)
        for p in range(HPB):
            oc = slice(p * HDV, (p + 1) * HDV)
            o = o_ref[:, oc]
            ohat = o * lax.rsqrt(jnp.mean(o * o, axis=-1, keepdims=True) + EPS)
            ga = hd_ref[:, _head_cols(p)[3]]
            ya_ref[:, oc] = (ohat * gn_ref[:, oc] * (ga * _sigmoid(ga))).astype(ya_ref.dtype)

    return _call(
        body, name="gla_fwd", grid=(HEADS // HPB, nt),
        in_specs=_gla_specs(tt, lambda i: i),
        out_specs=[
            pl.BlockSpec((tt, HPB * HDV), lambda h, i: (i, h)),
            pl.BlockSpec((tt, HPB * HDV), lambda h, i: (i, h)),
            pl.BlockSpec((HPB, nct, HDV, HDK), lambda h, i: (h, i, 0, 0)),
        ],
        out_shape=[
            jax.ShapeDtypeStruct((s, D), F32),
            jax.ShapeDtypeStruct((s, D), MXU),
            jax.ShapeDtypeStruct((HEADS, s // CHUNK, HDV, HDK), F32),
        ],
        scratch=[pltpu.VMEM((HPB, HDV, HDK), F32), pltpu.VMEM((tt, HPB * HDK), F32)],
        args=(hh, hh, wup, b_alpha, gnorm), ride=ride)


def _gla_bwd(hh, wup, b_alpha, gnorm, o, states, dya, dh, *, tt, ride=None):
    s = hh.shape[0]
    nt = s // tt
    nct = tt // CHUNK

    def body(hd_ref, al_ref, wup_ref, b_ref, gn_ref, o_ref, st_ref, dya_ref, _dh_in,
             dh_ref, dz_ref, dgn_ref, db_ref, dstate, g_scr, dg_scr, do_scr):
        @pl.when(pl.program_id(1) == 0)
        def _():
            dstate[...] = jnp.zeros_like(dstate)
            dgn_ref[...] = jnp.zeros_like(dgn_ref)
            db_ref[...] = jnp.zeros_like(db_ref)

        z, rowmod = _gla_decay(al_ref, wup_ref, b_ref, g_scr)

        for p in range(HPB):
            oc = slice(p * HDV, (p + 1) * HDV)
            gac = _head_cols(p)[3]
            o_t = o_ref[:, oc]
            rstd = lax.rsqrt(jnp.mean(o_t * o_t, axis=-1, keepdims=True) + EPS)
            ohat = o_t * rstd
            ga = hd_ref[:, gac]
            sg = _sigmoid(ga)
            dya_t = dya_ref[:, oc]
            gn = gn_ref[:, oc]
            dh_ref[:, gac] = (dya_t * ohat * gn * (sg * (1.0 + ga * (1.0 - sg)))).astype(dh_ref.dtype)
            don = dya_t * (ga * sg)
            dgn_ref[p] += jnp.sum(don * ohat, axis=0, keepdims=True)
            dohat = don * gn
            do_scr[:, oc] = rstd * (dohat - ohat * jnp.mean(dohat * ohat, axis=-1, keepdims=True))

        lower = lax.broadcasted_iota(jnp.int32, (CHUNK, CHUNK), 0) >= lax.broadcasted_iota(jnp.int32, (CHUNK, CHUNK), 1)
        last_row = lax.broadcasted_iota(jnp.int32, (CHUNK, HDK), 0) == CHUNK - 1
        for c in range(nct - 1, -1, -1):
            r0 = c * CHUNK
            rows = slice(r0, r0 + CHUNK)
            for p in range(HPB):
                qc, kc, vc, _ = _head_cols(p)
                t = _chunk_terms(hd_ref, g_scr, c, p)
                v = hd_ref[rows, vc]
                do = do_scr[rows, p * HDV:(p + 1) * HDV]
                st = st_ref[p, c]
                dst = dstate[p]
                a = _scores(t, lower)
                da = _dot(do, v, NT)
                da1 = jnp.where(lower, da, 0.0)
                da2 = jnp.where(lower, 0.0, da)
                dqe1 = _dot(da1, t["ke1"], NN)
                dke1 = _dot(da1, t["qe1"], TN)
                dqe2 = _dot(da2, t["ke2"], NN)
                dke2 = _dot(da2, t["qe2"], TN)
                dqa = _dot(do, st, NN)
                dkd = _dot(v, dst, NN)
                dh_ref[rows, vc] = (_dot(a, do, TN) + _dot(t["kd"], dst, NT)).astype(dh_ref.dtype)
                p1, p2, p3 = dqe1 * t["ep"], dqe2 * t["em"], dqa * t["a"]
                r1, r2, r3 = dke1 * t["em"], dke2 * t["ep"], dkd * t["dl"]
                dh_ref[rows, qc] = ((p1 + p2 + p3) * Q_SCALE).astype(dh_ref.dtype)
                dh_ref[rows, kc] = (r1 + r2 + r3).astype(dh_ref.dtype)
                r3k = r3 * t["k"]
                dgl = jnp.sum(r3k, axis=0, keepdims=True) + t["egl"] * jnp.sum(dst * st, axis=0, keepdims=True)
                dg = t["qs"] * (p1 - p2 + p3) + t["k"] * (r2 - r1) - r3k
                dg_scr[rows, p * HDK:(p + 1) * HDK] = dg + jnp.where(last_row, dgl, 0.0)
                dstate[p] = dst * t["egl"] + _dot(do, t["qa"], TN)

        dz = _seg_rcumsum(dg_scr[...], rowmod) * _sigmoid(-z) * (1.0 / GATE_TAU)
        dz_ref[...] = dz.astype(dz_ref.dtype)
        for p in range(HPB):
            db_ref[p] += jnp.sum(dz[:, p * HDK:(p + 1) * HDK], axis=0, keepdims=True)

    rev = lambda i: nt - 1 - i
    in_specs = _gla_specs(tt, rev) + [
        pl.BlockSpec((tt, HPB * HDV), lambda h, i: (rev(i), h)),
        pl.BlockSpec((HPB, nct, HDV, HDK), lambda h, i: (h, rev(i), 0, 0)),
        pl.BlockSpec((tt, HPB * HDV), lambda h, i: (rev(i), h)),
        ANY,
    ]
    return _call(
        body, name="gla_bwd", grid=(HEADS // HPB, nt), in_specs=in_specs,
        out_specs=[
            pl.BlockSpec((tt, HPB * HEAD_W), lambda h, i: (rev(i), HD0 // (HPB * HEAD_W) + h)),
            pl.BlockSpec((tt, HPB * HDK), lambda h, i: (rev(i), h)),
            pl.BlockSpec((HPB, 1, HDV), lambda h, i: (h, 0, 0)),
            pl.BlockSpec((HPB, 1, HDK), lambda h, i: (h, 0, 0)),
        ],
        out_shape=[
            jax.ShapeDtypeStruct(dh.shape, dh.dtype),
            jax.ShapeDtypeStruct((s, DK), MXU),
            jax.ShapeDtypeStruct((HEADS, 1, HDV), F32),
            jax.ShapeDtypeStruct((HEADS, 1, HDK), F32),
        ],
        scratch=[pltpu.VMEM((HPB, HDV, HDK), F32), pltpu.VMEM((tt, HPB * HDK), F32), pltpu.VMEM((tt, HPB * HDK), F32),
                 pltpu.VMEM((tt, HPB * HDV), F32)],
        args=(hh, hh, wup, b_alpha, gnorm, o, states, dya, dh), ride=ride, aliases={8: 0})


def _window_count(tile, tt, w):
    pos = tile * tt + lax.broadcasted_iota(jnp.int32, (tt, PG), 0) + 1
    return jnp.minimum(pos, w).astype(F32)


def _pool_fwd(hh, wpool, scale, *, tt):
    s = hh.shape[0]
    nt = s // tt

    def body(ug_ref, w_ref, sc_ref, pooled_ref, yb_ref, halo):
        i = pl.program_id(0)

        @pl.when(i == 0)
        def _():
            halo[...] = jnp.zeros_like(halo)

        for g, w in enumerate(POOL_WINDOWS):
            cols = slice(g * PG, (g + 1) * PG)
            u = ug_ref[:, cols]
            run = jnp.concatenate([halo[:, cols], u], axis=0)
            sh = 1
            while sh < w:
                run = run + pltpu.roll(run, sh, 0)
                sh *= 2
            pooled = run[HALO:, :] / _window_count(i, tt, w) - u
            pooled_ref[:, cols] = pooled.astype(pooled_ref.dtype)
            mixed = _dot(pooled, w_ref[g], NN)
            gb = ug_ref[:, slice(D + g * PG, D + (g + 1) * PG)]
            yb_ref[:, cols] = (mixed * sc_ref[:, cols] * (gb * _sigmoid(gb))).astype(yb_ref.dtype)
        halo[...] = ug_ref[tt - HALO:tt, :D]

    tile = pl.BlockSpec((tt, D), lambda i: (i, 0))
    return _call(
        body, name="pool_fwd", grid=(nt,),
        in_specs=[
            pl.BlockSpec((tt, 2 * D), lambda i: (i, PI0 // (2 * D))),
            pl.BlockSpec((len(POOL_WINDOWS), PG, PG), lambda i: (0, 0, 0)),
            pl.BlockSpec((1, D), lambda i: (0, 0)),
        ],
        out_specs=[tile] * 2,
        out_shape=[jax.ShapeDtypeStruct((s, D), MXU), jax.ShapeDtypeStruct((s, D), MXU)],
        scratch=[pltpu.VMEM((HALO, D), F32)], args=(hh, wpool, scale))


def _pool_bwd(hh, wpool, scale, pooled, dyb, dh, *, tt):
    s = hh.shape[0]
    nt = s // tt

    def body(gb_ref, w_ref, sc_ref, pooled_ref, dyb_ref, _dh_in, dh_ref, dw_ref, dsc_ref, halo):
        i = pl.program_id(0)
        tile = nt - 1 - i

        @pl.when(i == 0)
        def _():
            halo[...] = jnp.zeros_like(halo)
            dw_ref[...] = jnp.zeros_like(dw_ref)
            dsc_ref[...] = jnp.zeros_like(dsc_ref)

        for g, w in enumerate(POOL_WINDOWS):
            cols = slice(g * PG, (g + 1) * PG)
            gcols = slice(D + g * PG, D + (g + 1) * PG)
            gb = gb_ref[:, cols]
            sg = _sigmoid(gb)
            pooled = pooled_ref[:, cols]
            mixed = _dot(pooled, w_ref[g], NN)
            sc = sc_ref[:, cols]
            dyb = dyb_ref[:, cols]
            dh_ref[:, gcols] = (dyb * mixed * sc * (sg * (1.0 + gb * (1.0 - sg)))).astype(dh_ref.dtype)
            dms = dyb * (gb * sg)
            dsc_ref[:, cols] += jnp.sum(dms * mixed, axis=0, keepdims=True)
            dmixed = dms * sc
            dpooled = _dot(dmixed, w_ref[g], NT)
            dw_ref[g] += _dot(pooled, dmixed, TN)
            e = dpooled / _window_count(tile, tt, w)
            run = jnp.concatenate([e, halo[:, cols]], axis=0)
            sh = 1
            while sh < w:
                run = run + pltpu.roll(run, tt + HALO - sh, 0)
                sh *= 2
            dh_ref[:, cols] = (run[:tt, :] - dpooled).astype(dh_ref.dtype)
            halo[:, cols] = e[:HALO, :]

    rev = lambda i: nt - 1 - i
    tile = pl.BlockSpec((tt, D), lambda i: (rev(i), 0))
    wspec = pl.BlockSpec((len(POOL_WINDOWS), PG, PG), lambda i: (0, 0, 0))
    vec = pl.BlockSpec((1, D), lambda i: (0, 0))
    return _call(
        body, name="pool_bwd", grid=(nt,),
        in_specs=[pl.BlockSpec((tt, D), lambda i: (rev(i), GB0 // D)), wspec, vec, tile, tile, ANY],
        out_specs=[pl.BlockSpec((tt, 2 * D), lambda i: (rev(i), PI0 // (2 * D))), wspec, vec],
        out_shape=[jax.ShapeDtypeStruct(dh.shape, dh.dtype), jax.ShapeDtypeStruct((len(POOL_WINDOWS), PG, PG), F32),
                   jax.ShapeDtypeStruct((1, D), F32)],
        scratch=[pltpu.VMEM((HALO, D), F32)], args=(hh, wpool, scale, pooled, dyb, dh), aliases={5: 0})


def _merge_fwd(hh, x, ya, yb, wpa, wpb, wout, b_merge, ln_g, ln_b, *, tt):
    s = x.shape[0]

    def body(ml_ref, x_ref, ya_ref, yb_ref, wpa_ref, wpb_ref, wout_ref, bm_ref, g_ref, b_ref, r_ref, xn_ref, xnb_ref):
        pa = _dot(ya_ref[...], wpa_ref[...], NN)
        pb = _dot(yb_ref[...], wpb_ref[...], NN)
        merged = _sigmoid(ml_ref[:, :D] + bm_ref[:, :D]) * pa + _sigmoid(ml_ref[:, D:] + bm_ref[:, D:]) * pb
        r = ALPHA * x_ref[...] + _dot(merged, wout_ref[...], NN)
        r_ref[...] = r
        mu = jnp.mean(r, axis=-1, keepdims=True)
        xc = r - mu
        var = jnp.mean(xc * xc, axis=-1, keepdims=True)
        xn = xc * lax.rsqrt(var + EPS) * g_ref[...] + b_ref[...]
        xn_ref[...] = xn
        xnb_ref[...] = xn.astype(xnb_ref.dtype)

    tile = pl.BlockSpec((tt, D), lambda i: (i, 0))
    full = pl.BlockSpec((D, D), lambda i: (0, 0))
    vec = pl.BlockSpec((1, D), lambda i: (0, 0))
    return _call(
        body, name="merge_fwd", grid=(s // tt,),
        in_specs=[pl.BlockSpec((tt, 2 * D), lambda i: (i, ML0 // (2 * D))), tile, tile, tile, full, full, full,
                  pl.BlockSpec((1, 2 * D), lambda i: (0, 0)), vec, vec],
        out_specs=[tile] * 3, out_shape=[jax.ShapeDtypeStruct((s, D), F32)] * 2 + [jax.ShapeDtypeStruct((s, D), MXU)],
        args=(hh, x, ya, yb, wpa, wpb, wout, b_merge, ln_g, ln_b), sem=("parallel",), vmem=VMEM_BIG)


def _merge_bwd(hh, r, ya, yb, dout, wpa, wpb, wout, b_merge, ln_g, *, tt):
    s = r.shape[0]

    def body(ml_ref, r_ref, ya_ref, yb_ref, do_ref, wpa_ref, wpb_ref, wout_ref, bm_ref, g_ref,
             dh_ref, dr_ref, dpa_ref, dpb_ref, dwout_ref, dg_ref, db_ref, dbm_ref):
        @pl.when(pl.program_id(0) == 0)
        def _():
            dwout_ref[...] = jnp.zeros_like(dwout_ref)
            dg_ref[...] = jnp.zeros_like(dg_ref)
            db_ref[...] = jnp.zeros_like(db_ref)
            dbm_ref[...] = jnp.zeros_like(dbm_ref)

        rr = r_ref[...]
        mu = jnp.mean(rr, axis=-1, keepdims=True)
        xc = rr - mu
        rstd = lax.rsqrt(jnp.mean(xc * xc, axis=-1, keepdims=True) + EPS)
        xhat = xc * rstd
        do = do_ref[...]
        dg_ref[...] += jnp.sum(do * xhat, axis=0, keepdims=True)
        db_ref[...] += jnp.sum(do, axis=0, keepdims=True)
        dxh = do * g_ref[...]
        dr = rstd * (dxh - jnp.mean(dxh, axis=-1, keepdims=True) - xhat * jnp.mean(dxh * xhat, axis=-1, keepdims=True))
        dr_ref[...] = dr
        g_a = _sigmoid(ml_ref[:, :D] + bm_ref[:, :D])
        g_b = _sigmoid(ml_ref[:, D:] + bm_ref[:, D:])
        pa = _dot(ya_ref[...], wpa_ref[...], NN)
        pb = _dot(yb_ref[...], wpb_ref[...], NN)
        dwout_ref[...] += _dot(g_a * pa + g_b * pb, dr, TN)
        dm = _dot(dr, wout_ref[...], NT)
        dpa_ref[...] = (dm * g_a).astype(dpa_ref.dtype)
        dpb_ref[...] = (dm * g_b).astype(dpb_ref.dtype)
        dml_a = dm * pa * g_a * (1.0 - g_a)
        dml_b = dm * pb * g_b * (1.0 - g_b)
        dh_ref[:, :D] = dml_a.astype(dh_ref.dtype)
        dh_ref[:, D:] = dml_b.astype(dh_ref.dtype)
        dbm_ref[:, :D] += jnp.sum(dml_a, axis=0, keepdims=True)
        dbm_ref[:, D:] += jnp.sum(dml_b, axis=0, keepdims=True)

    tile = pl.BlockSpec((tt, D), lambda i: (i, 0))
    full = pl.BlockSpec((D, D), lambda i: (0, 0))
    vec = pl.BlockSpec((1, D), lambda i: (0, 0))
    vec2 = pl.BlockSpec((1, 2 * D), lambda i: (0, 0))
    mlb = pl.BlockSpec((tt, 2 * D), lambda i: (i, ML0 // (2 * D)))
    return _call(
        body, name="merge_bwd", grid=(s // tt,),
        in_specs=[mlb, tile, tile, tile, tile, full, full, full, vec2, vec],
        out_specs=[mlb, tile, tile, tile, full, vec, vec, vec2],
        out_shape=[
            jax.ShapeDtypeStruct((s, HP), MXU), jax.ShapeDtypeStruct((s, D), F32),
            jax.ShapeDtypeStruct((s, D), MXU), jax.ShapeDtypeStruct((s, D), MXU),
            jax.ShapeDtypeStruct((D, D), F32), jax.ShapeDtypeStruct((1, D), F32),
            jax.ShapeDtypeStruct((1, D), F32), jax.ShapeDtypeStruct((1, 2 * D), F32),
        ],
        args=(hh, r, ya, yb, dout, wpa, wpb, wout, b_merge, ln_g), vmem=VMEM_BIG)


def _proj_bwd(y, dp, w, *, tt, name):
    s = y.shape[0]

    def body(y_ref, dp_ref, w_ref, dy_ref, dw_ref):
        @pl.when(pl.program_id(0) == 0)
        def _():
            dw_ref[...] = jnp.zeros_like(dw_ref)

        dp = dp_ref[...]
        dy_ref[...] = _dot(dp, w_ref[...], NT)
        dw_ref[...] += _dot(y_ref[...], dp, TN)

    tile = pl.BlockSpec((tt, D), lambda i: (i, 0))
    full = pl.BlockSpec((D, D), lambda i: (0, 0))
    return _call(
        body, name=name, grid=(s // tt,), in_specs=[tile, tile, full], out_specs=[tile, full],
        out_shape=[jax.ShapeDtypeStruct((s, D), F32), jax.ShapeDtypeStruct((D, D), F32)], args=(y, dp, w))


def _loss_head(y, target, *, tt):
    s = y.shape[0]

    def body(y_ref, t_ref, loss_ref, dy_ref):
        @pl.when(pl.program_id(0) == 0)
        def _():
            loss_ref[...] = jnp.zeros_like(loss_ref)

        err = y_ref[...] - t_ref[...]
        dy_ref[...] = err * (1.0 / D)
        per_tok = jnp.mean(err * err, axis=-1, keepdims=True)
        loss_ref[...] += 0.5 * jnp.sum(per_tok, axis=0, keepdims=True)

    tile = pl.BlockSpec((tt, D), lambda i: (i, 0))
    return _call(
        body, name="loss_head", grid=(s // tt,), in_specs=[tile, tile],
        out_specs=[pl.BlockSpec((1, 1), lambda i: (0, 0)), tile],
        out_shape=[jax.ShapeDtypeStruct((1, 1), F32), jax.ShapeDtypeStruct((s, D), F32)], args=(y, target))


def _adamw(parts, w, m, v, *, tr, tc, name):
    nl, rows, cols = w.shape

    def body(p_ref, w_ref, m_ref, v_ref, g_ref, d_ref, nm_ref, nv_ref):
        g = p_ref[0, 0].astype(F32)
        for q in range(1, N_DEV):
            g = g + p_ref[0, q].astype(F32)
        g_ref[0] = g
        nm = ADAM_B1 * m_ref[0] + (1.0 - ADAM_B1) * g
        nv = ADAM_B2 * v_ref[0] + (1.0 - ADAM_B2) * (g * g)
        nm_ref[0] = nm
        nv_ref[0] = nv
        m_hat = nm / (1.0 - ADAM_B1 ** ADAM_STEP)
        v_hat = nv / (1.0 - ADAM_B2 ** ADAM_STEP)
        d_ref[0] = -ADAM_LR * (m_hat / (jnp.sqrt(v_hat) + ADAM_EPS) + ADAM_WD * w_ref[0])

    tile = pl.BlockSpec((1, tr, tc), lambda l, i, j: (l, i, j))
    return _call(
        body, name=name, grid=(nl, rows // tr, cols // tc),
        in_specs=[pl.BlockSpec((1, N_DEV, tr, tc), lambda l, i, j: (l, 0, i, j)), tile, tile, tile],
        out_specs=[tile] * 4, out_shape=[jax.ShapeDtypeStruct((nl, rows, cols), F32)] * 4,
        args=(parts, w, m, v), sem=("parallel", "parallel", "parallel"))


def _from_devices(g, axis):
    nd = g.ndim - 1
    perm = list(range(1, axis + 1)) + [0] + list(range(axis + 1, nd + 1))
    shape = list(g.shape[1:])
    shape[axis] *= N_DEV
    return jnp.transpose(g, perm).reshape(shape)


def _to_devices(a, axis):
    shape = list(a.shape)
    t = a.reshape(shape[:axis] + [N_DEV, shape[axis] // N_DEV] + shape[axis + 1:])
    return jnp.transpose(t, [axis] + list(range(0, axis)) + list(range(axis + 1, t.ndim)))


def _h_row_segments():
    segs = [(O_PI, PI0, IN_COLS - O_PI), (O_AL, AL0, RANK)]
    for h in range(HEADS):
        base = HD0 + h * HEAD_W
        segs += [(O_Q + h * HDK, base, HDK), (O_K + h * HDK, base + HDK, HDK),
                 (O_V + h * HDV, base + 2 * HDK, HDV), (O_GA + h * HDV, base + 2 * HDK + HDV, HDV)]
    return segs


def _h_weight_t(parts):
    return _move_rows(parts.reshape(IN_COLS, D), _h_row_segments(), HP, name="w_in_rows", zero=(AL0 + RANK, AL_W - RANK))


def _w_in_grad_parts_t(dwt):
    g = _move_rows(dwt, [(d0, s0, n) for s0, d0, n in _h_row_segments()], IN_COLS, name="w_in_grad_rows")
    return g.reshape(N_DEV, SHARD, D)


def kernel(x, w_in, w_alpha_up, b_alpha, gla_norm_g, w_pool_grp, pool_scale, b_merge, w_proj_a, w_proj_b, w_out, ln_g, ln_b, loss_target, m_w_in, m_w_alpha_up, m_b_alpha, m_gla_norm_g, m_w_pool_grp, m_pool_scale, m_b_merge, m_w_proj_a, m_w_proj_b, m_w_out, m_ln_g, m_ln_b, v_w_in, v_w_alpha_up, v_b_alpha, v_gla_norm_g, v_w_pool_grp, v_pool_scale, v_b_merge, v_w_proj_a, v_w_proj_b, v_w_out, v_ln_g, v_ln_b):
    s = x.shape[1]
    tt = min(256, s)
    tm = min(512, s)
    tb = min(1024, s)
    tn = HP // 3
    xs = x.reshape(s, D)

    tr3 = lambda a: jnp.transpose(a, (0, 2, 1))
    w_in_s = tr3(w_in).astype(WIRE)
    proj_s = jnp.stack([w_proj_a, w_proj_b, w_out], axis=1).astype(WIRE)
    pool_s = w_pool_grp.astype(WIRE)

    g_in, g_up, g_gn = _gather_first(w_in_s[0], [w_alpha_up.astype(WIRE), gla_norm_g], name="gather_first")
    wup = jnp.pad(_from_devices(g_up, 2), ((0, 0), (0, AL_W - RANK), (0, 0)))
    gn = _from_devices(g_gn, 2).reshape(DEPTH, 1, D)

    saved, wt_all, proj_all, pool_all = [], [], [], []
    cur, cur_b = xs, xs.astype(MXU)
    g_proj = g_pool = None
    for l in range(DEPTH):
        wt = _h_weight_t(g_in)
        nxt_l = l + 1 < DEPTH
        res = _in_proj(cur_b, wt, tm=tb, tn=tn, ride=_Exchange([(w_in_s[l + 1], True)]) if nxt_l else None)
        hh = res[0]
        if nxt_l:
            g_in = res[1]
        layers = ([0] if l == 0 else []) + ([l + 1] if nxt_l else [])
        res = _gla_fwd(hh, wup[l], b_alpha[l:l + 1], gn[l], tt=tt,
                       ride=_Exchange([(a[j], True) for j in layers for a in (proj_s, pool_s)]) if layers else None)
        o, ya, states = res[:3]
        got = {j: res[3 + 2 * t:5 + 2 * t] for t, j in enumerate(layers)}
        if l == 0:
            g_proj, g_pool = got[0]
        proj = _from_devices(g_proj, 1)
        pool = _from_devices(g_pool, 1)
        if nxt_l:
            g_proj, g_pool = got[l + 1]
        wt_all.append(wt), proj_all.append(proj), pool_all.append(pool)
        pooled, yb = _pool_fwd(hh, pool, pool_scale[l:l + 1], tt=tt)
        r, nxt, nxt_b = _merge_fwd(hh, cur, ya, yb, proj[0], proj[1], proj[2],
                                   b_merge[l:l + 1], ln_g[l:l + 1], ln_b[l:l + 1], tt=tt)
        saved.append(dict(xb=cur_b, hh=hh, o=o, ya=ya, states=states, pooled=pooled, yb=yb, r=r))
        cur, cur_b = nxt, nxt_b

    loss_part, dcur = _loss_head(cur, loss_target.reshape(s, D), tt=tt)
    loss = lax.psum(loss_part[0, 0], ("x", "y", "c"))

    small = {k: [None] * DEPTH for k in ("w_up", "b_alpha", "gnorm", "pool_scale", "b_merge", "ln_g", "ln_b")}
    parts = {k: [None] * DEPTH for k in ("w_in", "proj", "pool")}
    for l in range(DEPTH - 1, -1, -1):
        sv = saved[l]
        hh = sv["hh"]
        dh, dr, dpa, dpb, dw_out, dln_g, dln_b, db_merge = _merge_bwd(
            hh, sv["r"], sv["ya"], sv["yb"], dcur, proj_all[l][0], proj_all[l][1], proj_all[l][2], b_merge[l:l + 1], ln_g[l:l + 1], tt=tt)
        dya, dw_pa = _proj_bwd(sv["ya"], dpa, proj_all[l][0], tt=tt, name="proj_a_bwd")
        dyb, dw_pb = _proj_bwd(sv["yb"], dpb, proj_all[l][1], tt=tt, name="proj_b_bwd")
        dh, dw_pool, dscale = _pool_bwd(hh, pool_all[l], pool_scale[l:l + 1], sv["pooled"], dyb, dh, tt=tt)
        ride = _Exchange([(_to_devices(jnp.stack([dw_pa, dw_pb, dw_out]), 1).astype(WIRE), False),
                          (_to_devices(dw_pool, 1).astype(WIRE), False)])
        dh, dz, dgn, db_al, parts["proj"][l], parts["pool"][l] = _gla_bwd(
            hh, wup[l], b_alpha[l:l + 1], gn[l], sv["o"], sv["states"], dya, dh, tt=tt, ride=ride)
        dh = _mm_nt_into(dz, wup[l], dh, AL0 // AL_W, tm=tm, name="alpha_bwd")
        dw_up = _mm_tn(hh, dz, tm=tb, tk=128, name="w_up_grad", a_block=(128, AL0 // 128))
        dwt = _mm_tn(dh, sv["xb"], tm=tb, tk=tn, name="w_in_grad", out_dtype=WIRE)
        dcur, parts["w_in"][l] = _in_proj_bwd(dh, wt_all[l], dr, tm=tm, ride=_Exchange([(_w_in_grad_parts_t(dwt), False)]))

        small["w_up"][l] = dw_up[:RANK]
        small["b_alpha"][l] = db_al.reshape(DK)
        small["gnorm"][l] = dgn.reshape(HEADS, HDV)
        small["pool_scale"][l] = dscale[0]
        small["b_merge"][l] = db_merge[0]
        small["ln_g"][l], small["ln_b"][l] = dln_g[0], dln_b[0]
    grad_x = dcur[None]
    sm = {k: jnp.stack(v) for k, v in small.items()}

    rep = (("b_alpha", b_alpha, m_b_alpha, v_b_alpha), ("pool_scale", pool_scale, m_pool_scale, v_pool_scale),
           ("b_merge", b_merge, m_b_merge, v_b_merge), ("ln_g", ln_g, m_ln_g, v_ln_g), ("ln_b", ln_b, m_ln_b, v_ln_b))
    cat = lambda arrs: jnp.concatenate(arrs, axis=1)
    p_up, p_gn, p_rep = _exchange([(_to_devices(sm["w_up"], 2), False), (_to_devices(sm["gnorm"], 2), False),
                                   (cat([sm[nm] for nm, _, _, _ in rep]), True)], name="exchange_small_grads")

    def update(p, w, m, v, tr, name, layered=True, tc=None):
        shape = w.shape
        nl = shape[0] if layered else 1
        cols = shape[-1]
        flat = lambda a: a.reshape(nl, -1, cols)
        outs = _adamw(p.reshape(nl, N_DEV, -1, cols), flat(w), flat(m), flat(v), tr=tr, tc=tc or cols, name=name)
        return [o_.reshape(shape) for o_ in outs]

    res = {}
    res["w_in"] = [tr3(o_) for o_ in update(jnp.stack(parts["w_in"]), tr3(w_in), tr3(m_w_in), tr3(v_w_in), SHARD, "adamw_w_in", tc=256)]
    proj_p = jnp.stack(parts["proj"])
    for j, (nm, w, m, v) in enumerate((("w_proj_a", w_proj_a, m_w_proj_a, v_w_proj_a), ("w_proj_b", w_proj_b, m_w_proj_b, v_w_proj_b),
                                       ("w_out", w_out, m_w_out, v_w_out))):
        res[nm] = update(proj_p[:, :, j], w, m, v, D // N_DEV, "adamw_" + nm)
    res["w_pool_grp"] = update(jnp.stack(parts["pool"]), w_pool_grp, m_w_pool_grp, v_w_pool_grp, 128, "adamw_w_pool")
    res["w_alpha_up"] = update(p_up, w_alpha_up, m_w_alpha_up, v_w_alpha_up, DEPTH * RANK, "adamw_w_up", layered=False)
    res["gla_norm_g"] = update(p_gn, gla_norm_g, m_gla_norm_g, v_gla_norm_g, DEPTH * HEADS, "adamw_gnorm", layered=False)
    rep_out = update(p_rep, cat([w for _, w, _, _ in rep]), cat([m for _, _, m, _ in rep]), cat([v for _, _, _, v in rep]),
                     DEPTH, "adamw_small", layered=False)
    off = 0
    for nm, w, _, _ in rep:
        n = w.shape[1]
        res[nm] = [o_[:, off:off + n] for o_ in rep_out]
        off += n

    order = ("w_in", "w_alpha_up", "b_alpha", "gla_norm_g", "w_pool_grp", "pool_scale", "b_merge", "w_proj_a", "w_proj_b",
             "w_out", "ln_g", "ln_b")
    return (loss, grad_x, *[res[n][0] for n in order], *[res[n][1] for n in order],
            *[res[n][2] for n in order], *[res[n][3] for n in order])
```

```python
import jax
import jax.numpy as jnp
from jax import lax
from jax.experimental import pallas as pl
from jax.experimental.pallas import tpu as pltpu

F32 = jnp.float32
MXU = jnp.bfloat16
WIRE = jnp.bfloat16

N_DEV = 8
DEPTH = 4
D = 1024
HEADS = 4
DK = D // 2
HDK = DK // HEADS
HDV = D // HEADS
RANK = 16
CHUNK = 64
GATE_TAU = 16.0
POOL_WINDOWS = (2, 4, 8, 16)
PG = D // len(POOL_WINDOWS)
HALO = 16
IN_COLS = 7184
SHARD = IN_COLS // N_DEV
ALPHA = (2.0 * DEPTH) ** 0.25
EPS = 1e-5
Q_SCALE = HDK ** -0.5

ADAM_LR, ADAM_B1, ADAM_B2, ADAM_EPS, ADAM_WD, ADAM_STEP = 0.001, 0.9, 0.999, 1e-08, 0.01, 10

PI0, GB0, ML0, AL0, AL_W = 0, D, 2 * D, 4 * D, 512
HD0 = AL0 + AL_W
HEAD_W = 2 * HDK + 2 * HDV
HP = HD0 + HEADS * HEAD_W
HPB = 2
O_Q, O_K, O_V, O_GA, O_AL, O_PI, O_GB, O_ML = 0, DK, 2 * DK, 2 * DK + D, 2 * DK + 2 * D, 2 * DK + 2 * D + RANK, \
    2 * DK + 3 * D + RANK, 2 * DK + 4 * D + RANK

VMEM_BIG = 56 * 1024 * 1024
VMEM_MID = 40 * 1024 * 1024

NN = ((1,), (0,))
NT = ((1,), (1,))
TN = ((0,), (0,))

HBM = pl.BlockSpec(memory_space=pltpu.HBM)
ANY = pl.BlockSpec(memory_space=pl.ANY)


def _dot(a, b, dims):
    return lax.dot_general(a.astype(MXU), b.astype(MXU), (dims, ((), ())), preferred_element_type=F32)


def _params(sem, vmem):
    return pltpu.CompilerParams(dimension_semantics=sem, vmem_limit_bytes=vmem)


def _sigmoid(x):
    return 1.0 / (1.0 + jnp.exp(-x))


def _log_sigmoid(z):
    return jnp.minimum(z, 0.0) - jnp.log(1.0 + jnp.exp(-jnp.abs(z)))


class _Exchange:
    def __init__(self, items):
        self.items = [(s, bool(g)) for s, g in items]
        self.n = len(self.items)
        self.srcs = [s for s, _ in self.items]
        self.in_specs = [HBM] * self.n
        self.out_specs = [HBM] * self.n
        self.out_shape = [jax.ShapeDtypeStruct((N_DEV,) + tuple(s.shape if g else s.shape[1:]), s.dtype) for s, g in self.items]
        self.scratch = [pltpu.SemaphoreType.DMA((self.n * (N_DEV - 1),)), pltpu.SemaphoreType.DMA((self.n * (N_DEV - 1),)),
                        pltpu.SemaphoreType.DMA((self.n,))]

    def copies(self, src_refs, out_refs, send_sems, recv_sems, local_sems):
        x, y, c = lax.axis_index("x"), lax.axis_index("y"), lax.axis_index("c")
        me = 4 * x + 2 * y + c
        copies = []
        for t, (_, gather) in enumerate(self.items):
            src_ref, out_ref = src_refs[t], out_refs[t]
            copies.append(pltpu.make_async_copy(src_ref if gather else src_ref.at[me], out_ref.at[me], local_sems.at[t]))
            for k in range(1, N_DEV):
                px = 1 - x if k & 4 else x
                py = 1 - y if k & 2 else y
                pc = 1 - c if k & 1 else c
                peer = 4 * px + 2 * py + pc
                sem = t * (N_DEV - 1) + k - 1
                copies.append(pltpu.make_async_remote_copy(
                    src_ref=src_ref if gather else src_ref.at[peer],
                    dst_ref=out_ref.at[me],
                    send_sem=send_sems.at[sem],
                    recv_sem=recv_sems.at[sem],
                    device_id=(px, py, pc),
                    device_id_type=pl.DeviceIdType.MESH,
                ))
        return copies


def _grid_ends(grid):
    first = last = None
    for a, n in enumerate(grid):
        f = pl.program_id(a) == 0
        e = pl.program_id(a) == n - 1
        first = f if first is None else first & f
        last = e if last is None else last & e
    return first, last


def _call(body, *, name, grid, in_specs, out_specs, out_shape, args, scratch=(), sem=None, vmem=VMEM_MID, ride=None, aliases=None):
    n_in, n_out, n_scr = len(in_specs), len(out_specs), len(scratch)
    sem = sem or ("arbitrary",) * len(grid)
    if ride is None:
        return pl.pallas_call(body, name=name, grid=grid, in_specs=in_specs, out_specs=out_specs, out_shape=out_shape,
                              scratch_shapes=list(scratch), compiler_params=_params(sem, vmem),
                              input_output_aliases=aliases or {})(*args)
    r = ride.n

    def riding(*refs):
        ins, rsrc = refs[:n_in], refs[n_in:n_in + r]
        outs, rout = refs[n_in + r:n_in + r + n_out], refs[n_in + r + n_out:n_in + 2 * r + n_out]
        scr = refs[n_in + 2 * r + n_out:n_in + 2 * r + n_out + n_scr]
        send_sems, recv_sems, local_sems = refs[n_in + 2 * r + n_out + n_scr:]
        first, last = _grid_ends(grid)

        @pl.when(first)
        def _():
            for cp in ride.copies(rsrc, rout, send_sems, recv_sems, local_sems):
                cp.start()

        body(*ins, *outs, *scr)

        @pl.when(last)
        def _():
            for cp in ride.copies(rsrc, rout, send_sems, recv_sems, local_sems):
                cp.wait()

    return pl.pallas_call(riding, name=name, grid=grid, in_specs=list(in_specs) + ride.in_specs,
                          out_specs=list(out_specs) + ride.out_specs, out_shape=list(out_shape) + ride.out_shape,
                          scratch_shapes=list(scratch) + ride.scratch,
                          compiler_params=_params(("arbitrary",) * len(grid), vmem),
                          input_output_aliases=aliases or {})(*args, *ride.srcs)


def _exchange(items, *, name):
    ex = _Exchange(items)

    def body(*refs):
        copies = ex.copies(refs[:ex.n], refs[ex.n:2 * ex.n], *refs[2 * ex.n:])
        for cp in copies:
            cp.start()
        for cp in copies:
            cp.wait()

    return pl.pallas_call(body, name=name, in_specs=ex.in_specs, out_specs=ex.out_specs, out_shape=ex.out_shape,
                          scratch_shapes=ex.scratch)(*ex.srcs)


def _gather_first(big, smalls, *, name):
    ex = _Exchange([(a, True) for a in smalls])

    def body(*refs):
        big_ref, small_src = refs[0], refs[1:1 + ex.n]
        out_ref, small_out = refs[1 + ex.n], refs[2 + ex.n:2 + 2 * ex.n]
        send_sems, recv_sems, local_sem = refs[2 + 2 * ex.n:5 + 2 * ex.n]
        x, y, c = lax.axis_index("x"), lax.axis_index("y"), lax.axis_index("c")
        me, sibling = (x, y, c), (x, y, 1 - c)
        chips = [(1 - x, y), (x, 1 - y), (1 - x, 1 - y)]

        def slot(px, py, pc):
            return out_ref.at[4 * px + 2 * py + pc]

        def copy(k, block, to, src=None):
            return pltpu.make_async_remote_copy(
                src_ref=slot(*block) if src is None else src, dst_ref=slot(*block),
                send_sem=send_sems.at[k], recv_sem=recv_sems.at[k], device_id=to, device_id_type=pl.DeviceIdType.MESH)

        small = ex.copies(small_src, small_out, *refs[5 + 2 * ex.n:])
        mine = pltpu.make_async_copy(big_ref, slot(*me), local_sem)
        mine.start()
        first = [copy(0, me, sibling, src=big_ref)] + [copy(1 + j, me, (*chip, c), src=big_ref) for j, chip in enumerate(chips)]
        for cp in first + small:
            cp.start()
        passed = [copy(4 + j, (*chip, c), sibling) for j, chip in enumerate(chips)]
        for j, chip in enumerate(chips):
            copy(1 + j, (*chip, c), me).wait_recv()
            passed[j].start()
        copy(0, sibling, me).wait_recv()
        for j, chip in enumerate(chips):
            copy(4 + j, (*chip, 1 - c), me).wait_recv()
        for cp in first + passed:
            cp.wait_send()
        mine.wait()
        for cp in small:
            cp.wait()

    return pl.pallas_call(
        body, name=name, in_specs=[HBM] + ex.in_specs, out_specs=[HBM] + ex.out_specs,
        out_shape=[jax.ShapeDtypeStruct((N_DEV,) + tuple(big.shape), big.dtype)] + ex.out_shape,
        scratch_shapes=[pltpu.SemaphoreType.DMA((N_DEV - 1,)), pltpu.SemaphoreType.DMA((N_DEV - 1,)), pltpu.SemaphoreType.DMA]
        + ex.scratch)(big, *ex.srcs)


def _move_rows(src, segs, out_rows, *, name, zero=None):
    cols = src.shape[1]
    step = 256

    def body(src_ref, out_ref):
        for s0, d0, n in segs:
            for r in range(0, n, step):
                m = min(step, n - r)
                out_ref[d0 + r:d0 + r + m, :] = src_ref[s0 + r:s0 + r + m, :]
        if zero is not None:
            out_ref[zero[0]:zero[0] + zero[1], :] = jnp.zeros((zero[1], cols), src.dtype)

    vmem = pl.BlockSpec(memory_space=pltpu.VMEM)
    return pl.pallas_call(
        body, name=name, in_specs=[vmem], out_specs=vmem, out_shape=jax.ShapeDtypeStruct((out_rows, cols), src.dtype),
        compiler_params=pltpu.CompilerParams(vmem_limit_bytes=VMEM_BIG))(src)


def _in_proj(xb, wt, *, tm, tn, ride=None):
    m, k = xb.shape
    n = wt.shape[0]

    def body(x_ref, w_ref, o_ref):
        o_ref[...] = _dot(x_ref[...], w_ref[...], NT)

    return _call(
        body, name="in_proj", grid=(n // tn, m // tm),
        in_specs=[pl.BlockSpec((tm, k), lambda j, i: (i, 0)), pl.BlockSpec((tn, k), lambda j, i: (j, 0))],
        out_specs=[pl.BlockSpec((tm, tn), lambda j, i: (i, j))],
        out_shape=[jax.ShapeDtypeStruct((m, n), F32)],
        args=(xb, wt), sem=("parallel", "parallel"), vmem=VMEM_BIG, ride=ride)


def _in_proj_bwd(dh, wt, dr, *, tm, ride=None):
    m, n = dh.shape
    k = wt.shape[1]

    def body(dh_ref, w_ref, dr_ref, o_ref):
        o_ref[...] = ALPHA * dr_ref[...] + _dot(dh_ref[...], w_ref[...], NN)

    return _call(
        body, name="in_proj_bwd", grid=(m // tm,),
        in_specs=[pl.BlockSpec((tm, n), lambda i: (i, 0)),
                  pl.BlockSpec((n, k), lambda i: (0, 0), pipeline_mode=pl.Buffered(1)),
                  pl.BlockSpec((tm, k), lambda i: (i, 0))],
        out_specs=[pl.BlockSpec((tm, k), lambda i: (i, 0))],
        out_shape=[jax.ShapeDtypeStruct((m, k), F32)],
        args=(dh, wt, dr), sem=("parallel",), vmem=VMEM_BIG, ride=ride)


def _mm_nt_into(dc, w, arr, col, *, tm, name):
    m, n = dc.shape
    k = w.shape[0]

    def body(dc_ref, w_ref, _arr_in, o_ref):
        o_ref[...] = _dot(dc_ref[...], w_ref[...], NT).astype(o_ref.dtype)

    return _call(
        body, name=name, grid=(m // tm,),
        in_specs=[pl.BlockSpec((tm, n), lambda i: (i, 0)), pl.BlockSpec((k, n), lambda i: (0, 0)), ANY],
        out_specs=[pl.BlockSpec((tm, k), lambda i: (i, col))],
        out_shape=[jax.ShapeDtypeStruct(arr.shape, arr.dtype)],
        args=(dc, w, arr), sem=("parallel",), aliases={2: 0})[0]


def _mm_tn(a, dc, *, tm, tk, name, a_block=None, out_dtype=F32):
    m = a.shape[0]
    k, a_col = (a.shape[1], None) if a_block is None else a_block
    n = dc.shape[1]
    ni = m // tm

    def body(a_ref, dc_ref, o_ref, acc):
        i = pl.program_id(1)

        @pl.when(i == 0)
        def _():
            acc[...] = jnp.zeros_like(acc)

        acc[...] += _dot(a_ref[...], dc_ref[...], TN)

        @pl.when(i == ni - 1)
        def _():
            o_ref[...] = acc[...].astype(o_ref.dtype)

    a_map = (lambda j, i: (i, j)) if a_col is None else (lambda j, i: (i, a_col))
    return _call(
        body, name=name, grid=(k // tk, ni),
        in_specs=[pl.BlockSpec((tm, tk), a_map), pl.BlockSpec((tm, n), lambda j, i: (i, 0))],
        out_specs=[pl.BlockSpec((tk, n), lambda j, i: (j, 0))],
        out_shape=[jax.ShapeDtypeStruct((k, n), out_dtype)],
        scratch=[pltpu.VMEM((tk, n), F32)],
        args=(a, dc), sem=("parallel", "arbitrary"), vmem=VMEM_BIG)[0]


def _head_cols(p):
    b = p * HEAD_W
    return (slice(b, b + HDK), slice(b + HDK, b + 2 * HDK), slice(b + 2 * HDK, b + 2 * HDK + HDV),
            slice(b + 2 * HDK + HDV, b + HEAD_W))


def _seg_cumsum(v, rowmod):
    sh = 1
    while sh < CHUNK:
        v = v + jnp.where(rowmod >= sh, pltpu.roll(v, sh, 0), 0.0)
        sh *= 2
    return v


def _seg_rcumsum(v, rowmod):
    t = v.shape[0]
    sh = 1
    while sh < CHUNK:
        v = v + jnp.where(rowmod < CHUNK - sh, pltpu.roll(v, t - sh, 0), 0.0)
        sh *= 2
    return v


def _gla_decay(alpha_ref, wup_ref, b_ref, g_scr):
    z = _dot(alpha_ref[...], wup_ref[...], NN) + b_ref[...]
    rowmod = lax.broadcasted_iota(jnp.int32, z.shape, 0) % CHUNK
    g_scr[...] = _seg_cumsum(_log_sigmoid(z) * (1.0 / GATE_TAU), rowmod)
    return z, rowmod


QE1, KE1, QE2, KE2, QA, KD = range(6)
EP, EM, EA, EDL = range(4)
GW = HPB * HDK


def _gla_operands(hd_ref, g_scr, opnd_scr, fac_scr=None):
    t = g_scr.shape[0]

    def chunk_row(r):
        return jnp.concatenate([jnp.broadcast_to(g_scr[c * CHUNK + r:c * CHUNK + r + 1, :], (CHUNK, GW))
                                for c in range(t // CHUNK)], axis=0)

    g = g_scr[...]
    g_last = chunk_row(CHUNK - 1)
    ref = 0.5 * (chunk_row(0) + g_last)
    fac = {EP: jnp.exp(g - ref), EM: jnp.exp(ref - g), EA: jnp.exp(g), EDL: jnp.exp(g_last - g)}
    if fac_scr is not None:
        for j, f in fac.items():
            fac_scr[j] = f
    for p in range(HPB):
        qc, kc, _, _ = _head_cols(p)
        gc = slice(p * HDK, (p + 1) * HDK)
        qs = hd_ref[:, qc] * Q_SCALE
        k = hd_ref[:, kc]
        for j, (x, f) in {QE1: (qs, EP), KE1: (k, EM), QE2: (qs, EM), KE2: (k, EP), QA: (qs, EA), KD: (k, EDL)}.items():
            opnd_scr[j, :, gc] = (x * fac[f][:, gc]).astype(opnd_scr.dtype)


def _lower_mask():
    return lax.broadcasted_iota(jnp.int32, (CHUNK, CHUNK), 0) >= lax.broadcasted_iota(jnp.int32, (CHUNK, CHUNK), 1)


def _scores(opnd_scr, rows, gc, lower):
    return jnp.where(lower, _dot(opnd_scr[QE1, rows, gc], opnd_scr[KE1, rows, gc], NT),
                     _dot(opnd_scr[QE2, rows, gc], opnd_scr[KE2, rows, gc], NT))


def _gla_specs(tt, row):
    return [
        pl.BlockSpec((tt, HPB * HEAD_W), lambda h, i: (row(i), HD0 // (HPB * HEAD_W) + h)),
        pl.BlockSpec((tt, 128), lambda h, i: (row(i), AL0 // 128)),
        pl.BlockSpec((128, HPB * HDK), lambda h, i: (0, h)),
        pl.BlockSpec((1, HPB * HDK), lambda h, i: (0, h)),
        pl.BlockSpec((1, HPB * HDV), lambda h, i: (0, h)),
    ]


def _gla_fwd(hh, wup, b_alpha, gnorm, *, tt, ride=None):
    s = hh.shape[0]
    nt = s // tt
    nct = tt // CHUNK

    def body(hd_ref, al_ref, wup_ref, b_ref, gn_ref, o_ref, ya_ref, st_ref, state, g_scr, opnd_scr):
        @pl.when(pl.program_id(1) == 0)
        def _():
            state[...] = jnp.zeros_like(state)

        _gla_decay(al_ref, wup_ref, b_ref, g_scr)
        _gla_operands(hd_ref, g_scr, opnd_scr)
        lower = _lower_mask()
        for c in range(nct):
            rows = slice(c * CHUNK, (c + 1) * CHUNK)
            for p in range(HPB):
                vc = _head_cols(p)[2]
                gc = slice(p * HDK, (p + 1) * HDK)
                v = hd_ref[rows, vc]
                st = state[p]
                st_ref[p, c] = st
                egl = jnp.exp(g_scr[(c + 1) * CHUNK - 1:(c + 1) * CHUNK, gc])
                o_ref[rows, p * HDV:(p + 1) * HDV] = (_dot(_scores(opnd_scr, rows, gc, lower), v, NN)
                                                      + _dot(opnd_scr[QA, rows, gc], st, NT))
                state[p] = st * egl + _dot(v, opnd_scr[KD, rows, gc], TN)
        for p in range(HPB):
            oc = slice(p * HDV, (p + 1) * HDV)
            o = o_ref[:, oc]
            ohat = o * lax.rsqrt(jnp.mean(o * o, axis=-1, keepdims=True) + EPS)
            ga = hd_ref[:, _head_cols(p)[3]]
            ya_ref[:, oc] = (ohat * gn_ref[:, oc] * (ga * _sigmoid(ga))).astype(ya_ref.dtype)

    return _call(
        body, name="gla_fwd", grid=(HEADS // HPB, nt),
        in_specs=_gla_specs(tt, lambda i: i),
        out_specs=[
            pl.BlockSpec((tt, HPB * HDV), lambda h, i: (i, h)),
            pl.BlockSpec((tt, HPB * HDV), lambda h, i: (i, h)),
            pl.BlockSpec((HPB, nct, HDV, HDK), lambda h, i: (h, i, 0, 0)),
        ],
        out_shape=[
            jax.ShapeDtypeStruct((s, D), F32),
            jax.ShapeDtypeStruct((s, D), MXU),
            jax.ShapeDtypeStruct((HEADS, s // CHUNK, HDV, HDK), F32),
        ],
        scratch=[pltpu.VMEM((HPB, HDV, HDK), F32), pltpu.VMEM((tt, GW), F32), pltpu.VMEM((6, tt, GW), MXU)],
        args=(hh, hh, wup, b_alpha, gnorm), ride=ride)


def _gla_bwd(hh, wup, b_alpha, gnorm, o, states, dya, dh, *, tt, ride=None):
    s = hh.shape[0]
    nt = s // tt
    nct = tt // CHUNK

    def body(hd_ref, al_ref, wup_ref, b_ref, gn_ref, o_ref, st_ref, dya_ref, _dh_in,
             dh_ref, dz_ref, dgn_ref, db_ref, dstate, g_scr, dg_scr, do_scr, opnd_scr, fac_scr, res_scr, dgl_scr):
        @pl.when(pl.program_id(1) == 0)
        def _():
            dstate[...] = jnp.zeros_like(dstate)
            dgn_ref[...] = jnp.zeros_like(dgn_ref)
            db_ref[...] = jnp.zeros_like(db_ref)

        z, rowmod = _gla_decay(al_ref, wup_ref, b_ref, g_scr)
        _gla_operands(hd_ref, g_scr, opnd_scr, fac_scr)

        for p in range(HPB):
            oc = slice(p * HDV, (p + 1) * HDV)
            gac = _head_cols(p)[3]
            o_t = o_ref[:, oc]
            rstd = lax.rsqrt(jnp.mean(o_t * o_t, axis=-1, keepdims=True) + EPS)
            ohat = o_t * rstd
            ga = hd_ref[:, gac]
            sg = _sigmoid(ga)
            dya_t = dya_ref[:, oc]
            gn = gn_ref[:, oc]
            dh_ref[:, gac] = (dya_t * ohat * gn * (sg * (1.0 + ga * (1.0 - sg)))).astype(dh_ref.dtype)
            don = dya_t * (ga * sg)
            dgn_ref[p] += jnp.sum(don * ohat, axis=0, keepdims=True)
            dohat = don * gn
            do_scr[:, oc] = rstd * (dohat - ohat * jnp.mean(dohat * ohat, axis=-1, keepdims=True))

        lower = _lower_mask()
        for c in range(nct - 1, -1, -1):
            rows = slice(c * CHUNK, (c + 1) * CHUNK)
            for p in range(HPB):
                vc = _head_cols(p)[2]
                gc = slice(p * HDK, (p + 1) * HDK)
                v = hd_ref[rows, vc]
                do = do_scr[rows, p * HDV:(p + 1) * HDV]
                st = st_ref[p, c]
                dst = dstate[p]
                egl = jnp.exp(g_scr[(c + 1) * CHUNK - 1:(c + 1) * CHUNK, gc])
                a = _scores(opnd_scr, rows, gc, lower)
                da = _dot(do, v, NT)
                da1 = jnp.where(lower, da, 0.0).astype(MXU)
                da2 = jnp.where(lower, 0.0, da).astype(MXU)
                res_scr[0, rows, gc] = _dot(da1, opnd_scr[KE1, rows, gc], NN)
                res_scr[1, rows, gc] = _dot(da1, opnd_scr[QE1, rows, gc], TN)
                res_scr[2, rows, gc] = _dot(da2, opnd_scr[KE2, rows, gc], NN)
                res_scr[3, rows, gc] = _dot(da2, opnd_scr[QE2, rows, gc], TN)
                res_scr[4, rows, gc] = _dot(do, st, NN)
                res_scr[5, rows, gc] = _dot(v, dst, NN)
                dh_ref[rows, vc] = (_dot(a, do, TN) + _dot(opnd_scr[KD, rows, gc], dst, NT)).astype(dh_ref.dtype)
                dgl_scr[c:c + 1, gc] = egl * jnp.sum(dst * st, axis=0, keepdims=True)
                dstate[p] = dst * egl + _dot(do, opnd_scr[QA, rows, gc], TN)

        p1, p2, p3 = res_scr[0] * fac_scr[EP], res_scr[2] * fac_scr[EM], res_scr[4] * fac_scr[EA]
        r1, r2, r3 = res_scr[1] * fac_scr[EM], res_scr[3] * fac_scr[EP], res_scr[5] * fac_scr[EDL]
        dq = (p1 + p2 + p3) * Q_SCALE
        dk = r1 + r2 + r3
        dgq = p1 - p2 + p3
        dgk = r2 - r1
        for p in range(HPB):
            qc, kc, _, _ = _head_cols(p)
            gc = slice(p * HDK, (p + 1) * HDK)
            dh_ref[:, qc] = dq[:, gc].astype(dh_ref.dtype)
            dh_ref[:, kc] = dk[:, gc].astype(dh_ref.dtype)
            k = hd_ref[:, kc]
            r3k = r3[:, gc] * k
            dg_scr[:, gc] = (hd_ref[:, qc] * Q_SCALE) * dgq[:, gc] + k * dgk[:, gc] - r3k
            for c in range(nct):
                last = slice((c + 1) * CHUNK - 1, (c + 1) * CHUNK)
                dg_scr[last, gc] += jnp.sum(r3k[c * CHUNK:(c + 1) * CHUNK], axis=0, keepdims=True) + dgl_scr[c:c + 1, gc]

        dz = _seg_rcumsum(dg_scr[...], rowmod) * _sigmoid(-z) * (1.0 / GATE_TAU)
        dz_ref[...] = dz.astype(dz_ref.dtype)
        for p in range(HPB):
            db_ref[p] += jnp.sum(dz[:, p * HDK:(p + 1) * HDK], axis=0, keepdims=True)

    rev = lambda i: nt - 1 - i
    in_specs = _gla_specs(tt, rev) + [
        pl.BlockSpec((tt, HPB * HDV), lambda h, i: (rev(i), h)),
        pl.BlockSpec((HPB, nct, HDV, HDK), lambda h, i: (h, rev(i), 0, 0)),
        pl.BlockSpec((tt, HPB * HDV), lambda h, i: (rev(i), h)),
        ANY,
    ]
    return _call(
        body, name="gla_bwd", grid=(HEADS // HPB, nt), in_specs=in_specs,
        out_specs=[
            pl.BlockSpec((tt, HPB * HEAD_W), lambda h, i: (rev(i), HD0 // (HPB * HEAD_W) + h)),
            pl.BlockSpec((tt, HPB * HDK), lambda h, i: (rev(i), h)),
            pl.BlockSpec((HPB, 1, HDV), lambda h, i: (h, 0, 0)),
            pl.BlockSpec((HPB, 1, HDK), lambda h, i: (h, 0, 0)),
        ],
        out_shape=[
            jax.ShapeDtypeStruct(dh.shape, dh.dtype),
            jax.ShapeDtypeStruct((s, DK), MXU),
            jax.ShapeDtypeStruct((HEADS, 1, HDV), F32),
            jax.ShapeDtypeStruct((HEADS, 1, HDK), F32),
        ],
        scratch=[pltpu.VMEM((HPB, HDV, HDK), F32), pltpu.VMEM((tt, GW), F32), pltpu.VMEM((tt, GW), F32),
                 pltpu.VMEM((tt, HPB * HDV), F32), pltpu.VMEM((6, tt, GW), MXU), pltpu.VMEM((4, tt, GW), F32),
                 pltpu.VMEM((6, tt, GW), F32), pltpu.VMEM((max(nct, 8), GW), F32)],
        args=(hh, hh, wup, b_alpha, gnorm, o, states, dya, dh), ride=ride, aliases={8: 0})


def _window_count(tile, tt, w):
    pos = tile * tt + lax.broadcasted_iota(jnp.int32, (tt, PG), 0) + 1
    return jnp.minimum(pos, w).astype(F32)


def _pool_fwd(hh, wpool, scale, *, tt):
    s = hh.shape[0]
    nt = s // tt

    def body(ug_ref, w_ref, sc_ref, pooled_ref, yb_ref, halo):
        i = pl.program_id(0)

        @pl.when(i == 0)
        def _():
            halo[...] = jnp.zeros_like(halo)

        for g, w in enumerate(POOL_WINDOWS):
            cols = slice(g * PG, (g + 1) * PG)
            u = ug_ref[:, cols]
            run = jnp.concatenate([halo[:, cols], u], axis=0)
            sh = 1
            while sh < w:
                run = run + pltpu.roll(run, sh, 0)
                sh *= 2
            pooled = run[HALO:, :] / _window_count(i, tt, w) - u
            pooled_ref[:, cols] = pooled.astype(pooled_ref.dtype)
            mixed = _dot(pooled, w_ref[g], NN)
            gb = ug_ref[:, slice(D + g * PG, D + (g + 1) * PG)]
            yb_ref[:, cols] = (mixed * sc_ref[:, cols] * (gb * _sigmoid(gb))).astype(yb_ref.dtype)
        halo[...] = ug_ref[tt - HALO:tt, :D]

    tile = pl.BlockSpec((tt, D), lambda i: (i, 0))
    return _call(
        body, name="pool_fwd", grid=(nt,),
        in_specs=[
            pl.BlockSpec((tt, 2 * D), lambda i: (i, PI0 // (2 * D))),
            pl.BlockSpec((len(POOL_WINDOWS), PG, PG), lambda i: (0, 0, 0)),
            pl.BlockSpec((1, D), lambda i: (0, 0)),
        ],
        out_specs=[tile] * 2,
        out_shape=[jax.ShapeDtypeStruct((s, D), MXU), jax.ShapeDtypeStruct((s, D), MXU)],
        scratch=[pltpu.VMEM((HALO, D), F32)], args=(hh, wpool, scale))


def _pool_bwd(hh, wpool, scale, pooled, dyb, dh, *, tt):
    s = hh.shape[0]
    nt = s // tt

    def body(gb_ref, w_ref, sc_ref, pooled_ref, dyb_ref, _dh_in, dh_ref, dw_ref, dsc_ref, halo):
        i = pl.program_id(0)
        tile = nt - 1 - i

        @pl.when(i == 0)
        def _():
            halo[...] = jnp.zeros_like(halo)
            dw_ref[...] = jnp.zeros_like(dw_ref)
            dsc_ref[...] = jnp.zeros_like(dsc_ref)

        for g, w in enumerate(POOL_WINDOWS):
            cols = slice(g * PG, (g + 1) * PG)
            gcols = slice(D + g * PG, D + (g + 1) * PG)
            gb = gb_ref[:, cols]
            sg = _sigmoid(gb)
            pooled = pooled_ref[:, cols]
            mixed = _dot(pooled, w_ref[g], NN)
            sc = sc_ref[:, cols]
            dyb = dyb_ref[:, cols]
            dh_ref[:, gcols] = (dyb * mixed * sc * (sg * (1.0 + gb * (1.0 - sg)))).astype(dh_ref.dtype)
            dms = dyb * (gb * sg)
            dsc_ref[:, cols] += jnp.sum(dms * mixed, axis=0, keepdims=True)
            dmixed = dms * sc
            dpooled = _dot(dmixed, w_ref[g], NT)
            dw_ref[g] += _dot(pooled, dmixed, TN)
            e = dpooled / _window_count(tile, tt, w)
            run = jnp.concatenate([e, halo[:, cols]], axis=0)
            sh = 1
            while sh < w:
                run = run + pltpu.roll(run, tt + HALO - sh, 0)
                sh *= 2
            dh_ref[:, cols] = (run[:tt, :] - dpooled).astype(dh_ref.dtype)
            halo[:, cols] = e[:HALO, :]

    rev = lambda i: nt - 1 - i
    tile = pl.BlockSpec((tt, D), lambda i: (rev(i), 0))
    wspec = pl.BlockSpec((len(POOL_WINDOWS), PG, PG), lambda i: (0, 0, 0))
    vec = pl.BlockSpec((1, D), lambda i: (0, 0))
    return _call(
        body, name="pool_bwd", grid=(nt,),
        in_specs=[pl.BlockSpec((tt, D), lambda i: (rev(i), GB0 // D)), wspec, vec, tile, tile, ANY],
        out_specs=[pl.BlockSpec((tt, 2 * D), lambda i: (rev(i), PI0 // (2 * D))), wspec, vec],
        out_shape=[jax.ShapeDtypeStruct(dh.shape, dh.dtype), jax.ShapeDtypeStruct((len(POOL_WINDOWS), PG, PG), F32),
                   jax.ShapeDtypeStruct((1, D), F32)],
        scratch=[pltpu.VMEM((HALO, D), F32)], args=(hh, wpool, scale, pooled, dyb, dh), aliases={5: 0})


def _merge_fwd(hh, x, ya, yb, wpa, wpb, wout, b_merge, ln_g, ln_b, *, tt):
    s = x.shape[0]

    def body(ml_ref, x_ref, ya_ref, yb_ref, wpa_ref, wpb_ref, wout_ref, bm_ref, g_ref, b_ref, r_ref, xn_ref, xnb_ref):
        pa = _dot(ya_ref[...], wpa_ref[...], NN)
        pb = _dot(yb_ref[...], wpb_ref[...], NN)
        merged = _sigmoid(ml_ref[:, :D] + bm_ref[:, :D]) * pa + _sigmoid(ml_ref[:, D:] + bm_ref[:, D:]) * pb
        r = ALPHA * x_ref[...] + _dot(merged, wout_ref[...], NN)
        r_ref[...] = r
        mu = jnp.mean(r, axis=-1, keepdims=True)
        xc = r - mu
        var = jnp.mean(xc * xc, axis=-1, keepdims=True)
        xn = xc * lax.rsqrt(var + EPS) * g_ref[...] + b_ref[...]
        xn_ref[...] = xn
        xnb_ref[...] = xn.astype(xnb_ref.dtype)

    tile = pl.BlockSpec((tt, D), lambda i: (i, 0))
    full = pl.BlockSpec((D, D), lambda i: (0, 0))
    vec = pl.BlockSpec((1, D), lambda i: (0, 0))
    return _call(
        body, name="merge_fwd", grid=(s // tt,),
        in_specs=[pl.BlockSpec((tt, 2 * D), lambda i: (i, ML0 // (2 * D))), tile, tile, tile, full, full, full,
                  pl.BlockSpec((1, 2 * D), lambda i: (0, 0)), vec, vec],
        out_specs=[tile] * 3, out_shape=[jax.ShapeDtypeStruct((s, D), F32)] * 2 + [jax.ShapeDtypeStruct((s, D), MXU)],
        args=(hh, x, ya, yb, wpa, wpb, wout, b_merge, ln_g, ln_b), sem=("parallel",), vmem=VMEM_BIG)


def _merge_bwd(hh, r, ya, yb, dout, wpa, wpb, wout, b_merge, ln_g, *, tt):
    s = r.shape[0]

    def body(ml_ref, r_ref, ya_ref, yb_ref, do_ref, wpa_ref, wpb_ref, wout_ref, bm_ref, g_ref,
             dh_ref, dr_ref, dpa_ref, dpb_ref, dwout_ref, dg_ref, db_ref, dbm_ref):
        @pl.when(pl.program_id(0) == 0)
        def _():
            dwout_ref[...] = jnp.zeros_like(dwout_ref)
            dg_ref[...] = jnp.zeros_like(dg_ref)
            db_ref[...] = jnp.zeros_like(db_ref)
            dbm_ref[...] = jnp.zeros_like(dbm_ref)

        rr = r_ref[...]
        mu = jnp.mean(rr, axis=-1, keepdims=True)
        xc = rr - mu
        rstd = lax.rsqrt(jnp.mean(xc * xc, axis=-1, keepdims=True) + EPS)
        xhat = xc * rstd
        do = do_ref[...]
        dg_ref[...] += jnp.sum(do * xhat, axis=0, keepdims=True)
        db_ref[...] += jnp.sum(do, axis=0, keepdims=True)
        dxh = do * g_ref[...]
        dr = rstd * (dxh - jnp.mean(dxh, axis=-1, keepdims=True) - xhat * jnp.mean(dxh * xhat, axis=-1, keepdims=True))
        dr_ref[...] = dr
        g_a = _sigmoid(ml_ref[:, :D] + bm_ref[:, :D])
        g_b = _sigmoid(ml_ref[:, D:] + bm_ref[:, D:])
        pa = _dot(ya_ref[...], wpa_ref[...], NN)
        pb = _dot(yb_ref[...], wpb_ref[...], NN)
        dwout_ref[...] += _dot(g_a * pa + g_b * pb, dr, TN)
        dm = _dot(dr, wout_ref[...], NT)
        dpa_ref[...] = (dm * g_a).astype(dpa_ref.dtype)
        dpb_ref[...] = (dm * g_b).astype(dpb_ref.dtype)
        dml_a = dm * pa * g_a * (1.0 - g_a)
        dml_b = dm * pb * g_b * (1.0 - g_b)
        dh_ref[:, :D] = dml_a.astype(dh_ref.dtype)
        dh_ref[:, D:] = dml_b.astype(dh_ref.dtype)
        dbm_ref[:, :D] += jnp.sum(dml_a, axis=0, keepdims=True)
        dbm_ref[:, D:] += jnp.sum(dml_b, axis=0, keepdims=True)

    tile = pl.BlockSpec((tt, D), lambda i: (i, 0))
    full = pl.BlockSpec((D, D), lambda i: (0, 0))
    vec = pl.BlockSpec((1, D), lambda i: (0, 0))
    vec2 = pl.BlockSpec((1, 2 * D), lambda i: (0, 0))
    mlb = pl.BlockSpec((tt, 2 * D), lambda i: (i, ML0 // (2 * D)))
    return _call(
        body, name="merge_bwd", grid=(s // tt,),
        in_specs=[mlb, tile, tile, tile, tile, full, full, full, vec2, vec],
        out_specs=[mlb, tile, tile, tile, full, vec, vec, vec2],
        out_shape=[
            jax.ShapeDtypeStruct((s, HP), MXU), jax.ShapeDtypeStruct((s, D), F32),
            jax.ShapeDtypeStruct((s, D), MXU), jax.ShapeDtypeStruct((s, D), MXU),
            jax.ShapeDtypeStruct((D, D), F32), jax.ShapeDtypeStruct((1, D), F32),
            jax.ShapeDtypeStruct((1, D), F32), jax.ShapeDtypeStruct((1, 2 * D), F32),
        ],
        args=(hh, r, ya, yb, dout, wpa, wpb, wout, b_merge, ln_g), vmem=VMEM_BIG)


def _proj_bwd(y, dp, w, *, tt, name):
    s = y.shape[0]

    def body(y_ref, dp_ref, w_ref, dy_ref, dw_ref):
        @pl.when(pl.program_id(0) == 0)
        def _():
            dw_ref[...] = jnp.zeros_like(dw_ref)

        dp = dp_ref[...]
        dy_ref[...] = _dot(dp, w_ref[...], NT)
        dw_ref[...] += _dot(y_ref[...], dp, TN)

    tile = pl.BlockSpec((tt, D), lambda i: (i, 0))
    full = pl.BlockSpec((D, D), lambda i: (0, 0))
    return _call(
        body, name=name, grid=(s // tt,), in_specs=[tile, tile, full], out_specs=[tile, full],
        out_shape=[jax.ShapeDtypeStruct((s, D), F32), jax.ShapeDtypeStruct((D, D), F32)], args=(y, dp, w))


def _loss_head(y, target, *, tt):
    s = y.shape[0]

    def body(y_ref, t_ref, loss_ref, dy_ref):
        @pl.when(pl.program_id(0) == 0)
        def _():
            loss_ref[...] = jnp.zeros_like(loss_ref)

        err = y_ref[...] - t_ref[...]
        dy_ref[...] = err * (1.0 / D)
        per_tok = jnp.mean(err * err, axis=-1, keepdims=True)
        loss_ref[...] += 0.5 * jnp.sum(per_tok, axis=0, keepdims=True)

    tile = pl.BlockSpec((tt, D), lambda i: (i, 0))
    return _call(
        body, name="loss_head", grid=(s // tt,), in_specs=[tile, tile],
        out_specs=[pl.BlockSpec((1, 1), lambda i: (0, 0)), tile],
        out_shape=[jax.ShapeDtypeStruct((1, 1), F32), jax.ShapeDtypeStruct((s, D), F32)], args=(y, target))


def _adamw(parts, w, m, v, *, tr, tc, name):
    nl, rows, cols = w.shape

    def body(p_ref, w_ref, m_ref, v_ref, g_ref, d_ref, nm_ref, nv_ref):
        g = p_ref[0, 0].astype(F32)
        for q in range(1, N_DEV):
            g = g + p_ref[0, q].astype(F32)
        g_ref[0] = g
        nm = ADAM_B1 * m_ref[0] + (1.0 - ADAM_B1) * g
        nv = ADAM_B2 * v_ref[0] + (1.0 - ADAM_B2) * (g * g)
        nm_ref[0] = nm
        nv_ref[0] = nv
        m_hat = nm / (1.0 - ADAM_B1 ** ADAM_STEP)
        v_hat = nv / (1.0 - ADAM_B2 ** ADAM_STEP)
        d_ref[0] = -ADAM_LR * (m_hat / (jnp.sqrt(v_hat) + ADAM_EPS) + ADAM_WD * w_ref[0])

    tile = pl.BlockSpec((1, tr, tc), lambda l, i, j: (l, i, j))
    return _call(
        body, name=name, grid=(nl, rows // tr, cols // tc),
        in_specs=[pl.BlockSpec((1, N_DEV, tr, tc), lambda l, i, j: (l, 0, i, j)), tile, tile, tile],
        out_specs=[tile] * 4, out_shape=[jax.ShapeDtypeStruct((nl, rows, cols), F32)] * 4,
        args=(parts, w, m, v), sem=("parallel", "parallel", "parallel"))


def _from_devices(g, axis):
    nd = g.ndim - 1
    perm = list(range(1, axis + 1)) + [0] + list(range(axis + 1, nd + 1))
    shape = list(g.shape[1:])
    shape[axis] *= N_DEV
    return jnp.transpose(g, perm).reshape(shape)


def _to_devices(a, axis):
    shape = list(a.shape)
    t = a.reshape(shape[:axis] + [N_DEV, shape[axis] // N_DEV] + shape[axis + 1:])
    return jnp.transpose(t, [axis] + list(range(0, axis)) + list(range(axis + 1, t.ndim)))


def _h_row_segments():
    segs = [(O_PI, PI0, IN_COLS - O_PI), (O_AL, AL0, RANK)]
    for h in range(HEADS):
        base = HD0 + h * HEAD_W
        segs += [(O_Q + h * HDK, base, HDK), (O_K + h * HDK, base + HDK, HDK),
                 (O_V + h * HDV, base + 2 * HDK, HDV), (O_GA + h * HDV, base + 2 * HDK + HDV, HDV)]
    return segs


def _h_weight_t(parts):
    return _move_rows(parts.reshape(IN_COLS, D), _h_row_segments(), HP, name="w_in_rows", zero=(AL0 + RANK, AL_W - RANK))


def _w_in_grad_parts_t(dwt):
    g = _move_rows(dwt, [(d0, s0, n) for s0, d0, n in _h_row_segments()], IN_COLS, name="w_in_grad_rows")
    return g.reshape(N_DEV, SHARD, D)


def kernel(x, w_in, w_alpha_up, b_alpha, gla_norm_g, w_pool_grp, pool_scale, b_merge, w_proj_a, w_proj_b, w_out, ln_g, ln_b, loss_target, m_w_in, m_w_alpha_up, m_b_alpha, m_gla_norm_g, m_w_pool_grp, m_pool_scale, m_b_merge, m_w_proj_a, m_w_proj_b, m_w_out, m_ln_g, m_ln_b, v_w_in, v_w_alpha_up, v_b_alpha, v_gla_norm_g, v_w_pool_grp, v_pool_scale, v_b_merge, v_w_proj_a, v_w_proj_b, v_w_out, v_ln_g, v_ln_b):
    s = x.shape[1]
    tt = min(256, s)
    tm = min(512, s)
    tb = min(1024, s)
    tn = HP // 3
    xs = x.reshape(s, D)

    tr3 = lambda a: jnp.transpose(a, (0, 2, 1))
    w_in_s = tr3(w_in).astype(WIRE)
    proj_s = jnp.stack([w_proj_a, w_proj_b, w_out], axis=1).astype(WIRE)
    pool_s = w_pool_grp.astype(WIRE)

    g_in, g_up, g_gn = _gather_first(w_in_s[0], [w_alpha_up.astype(WIRE), gla_norm_g], name="gather_first")
    wup = jnp.pad(_from_devices(g_up, 2), ((0, 0), (0, AL_W - RANK), (0, 0)))
    gn = _from_devices(g_gn, 2).reshape(DEPTH, 1, D)

    saved, wt_all, proj_all, pool_all = [], [], [], []
    cur, cur_b = xs, xs.astype(MXU)
    g_proj = g_pool = None
    for l in range(DEPTH):
        wt = _h_weight_t(g_in)
        nxt_l = l + 1 < DEPTH
        res = _in_proj(cur_b, wt, tm=tb, tn=tn, ride=_Exchange([(w_in_s[l + 1], True)]) if nxt_l else None)
        hh = res[0]
        if nxt_l:
            g_in = res[1]
        layers = ([0] if l == 0 else []) + ([l + 1] if nxt_l else [])
        res = _gla_fwd(hh, wup[l], b_alpha[l:l + 1], gn[l], tt=tt,
                       ride=_Exchange([(a[j], True) for j in layers for a in (proj_s, pool_s)]) if layers else None)
        o, ya, states = res[:3]
        got = {j: res[3 + 2 * t:5 + 2 * t] for t, j in enumerate(layers)}
        if l == 0:
            g_proj, g_pool = got[0]
        proj = _from_devices(g_proj, 1)
        pool = _from_devices(g_pool, 1)
        if nxt_l:
            g_proj, g_pool = got[l + 1]
        wt_all.append(wt), proj_all.append(proj), pool_all.append(pool)
        pooled, yb = _pool_fwd(hh, pool, pool_scale[l:l + 1], tt=tt)
        r, nxt, nxt_b = _merge_fwd(hh, cur, ya, yb, proj[0], proj[1], proj[2],
                                   b_merge[l:l + 1], ln_g[l:l + 1], ln_b[l:l + 1], tt=tt)
        saved.append(dict(xb=cur_b, hh=hh, o=o, ya=ya, states=states, pooled=pooled, yb=yb, r=r))
        cur, cur_b = nxt, nxt_b

    loss_part, dcur = _loss_head(cur, loss_target.reshape(s, D), tt=tt)
    loss = lax.psum(loss_part[0, 0], ("x", "y", "c"))

    small = {k: [None] * DEPTH for k in ("w_up", "b_alpha", "gnorm", "pool_scale", "b_merge", "ln_g", "ln_b")}
    parts = {k: [None] * DEPTH for k in ("w_in", "proj", "pool")}
    for l in range(DEPTH - 1, -1, -1):
        sv = saved[l]
        hh = sv["hh"]
        dh, dr, dpa, dpb, dw_out, dln_g, dln_b, db_merge = _merge_bwd(
            hh, sv["r"], sv["ya"], sv["yb"], dcur, proj_all[l][0], proj_all[l][1], proj_all[l][2], b_merge[l:l + 1], ln_g[l:l + 1], tt=tt)
        dya, dw_pa = _proj_bwd(sv["ya"], dpa, proj_all[l][0], tt=tm, name="proj_a_bwd")
        dyb, dw_pb = _proj_bwd(sv["yb"], dpb, proj_all[l][1], tt=tm, name="proj_b_bwd")
        dh, dw_pool, dscale = _pool_bwd(hh, pool_all[l], pool_scale[l:l + 1], sv["pooled"], dyb, dh, tt=tt)
        ride = _Exchange([(_to_devices(jnp.stack([dw_pa, dw_pb, dw_out]), 1).astype(WIRE), False),
                          (_to_devices(dw_pool, 1).astype(WIRE), False)])
        dh, dz, dgn, db_al, parts["proj"][l], parts["pool"][l] = _gla_bwd(
            hh, wup[l], b_alpha[l:l + 1], gn[l], sv["o"], sv["states"], dya, dh, tt=tt, ride=ride)
        dh = _mm_nt_into(dz, wup[l], dh, AL0 // AL_W, tm=tm, name="alpha_bwd")
        dw_up = _mm_tn(hh, dz, tm=tb, tk=128, name="w_up_grad", a_block=(128, AL0 // 128))
        dwt = _mm_tn(dh, sv["xb"], tm=tb, tk=tn, name="w_in_grad", out_dtype=WIRE)
        dcur, parts["w_in"][l] = _in_proj_bwd(dh, wt_all[l], dr, tm=tm, ride=_Exchange([(_w_in_grad_parts_t(dwt), False)]))

        small["w_up"][l] = dw_up[:RANK]
        small["b_alpha"][l] = db_al.reshape(DK)
        small["gnorm"][l] = dgn.reshape(HEADS, HDV)
        small["pool_scale"][l] = dscale[0]
        small["b_merge"][l] = db_merge[0]
        small["ln_g"][l], small["ln_b"][l] = dln_g[0], dln_b[0]
    grad_x = dcur[None]
    sm = {k: jnp.stack(v) for k, v in small.items()}

    rep = (("b_alpha", b_alpha, m_b_alpha, v_b_alpha), ("pool_scale", pool_scale, m_pool_scale, v_pool_scale),
           ("b_merge", b_merge, m_b_merge, v_b_merge), ("ln_g", ln_g, m_ln_g, v_ln_g), ("ln_b", ln_b, m_ln_b, v_ln_b))
    cat = lambda arrs: jnp.concatenate(arrs, axis=1)
    p_up, p_gn, p_rep = _exchange([(_to_devices(sm["w_up"], 2), False), (_to_devices(sm["gnorm"], 2), False),
                                   (cat([sm[nm] for nm, _, _, _ in rep]), True)], name="exchange_small_grads")

    def update(p, w, m, v, tr, name, layered=True, tc=None):
        shape = w.shape
        nl = shape[0] if layered else 1
        cols = shape[-1]
        flat = lambda a: a.reshape(nl, -1, cols)
        outs = _adamw(p.reshape(nl, N_DEV, -1, cols), flat(w), flat(m), flat(v), tr=tr, tc=tc or cols, name=name)
        return [o_.reshape(shape) for o_ in outs]

    res = {}
    res["w_in"] = [tr3(o_) for o_ in update(jnp.stack(parts["w_in"]), tr3(w_in), tr3(m_w_in), tr3(v_w_in), SHARD, "adamw_w_in", tc=256)]
    proj_p = jnp.stack(parts["proj"])
    for j, (nm, w, m, v) in enumerate((("w_proj_a", w_proj_a, m_w_proj_a, v_w_proj_a), ("w_proj_b", w_proj_b, m_w_proj_b, v_w_proj_b),
                                       ("w_out", w_out, m_w_out, v_w_out))):
        res[nm] = update(proj_p[:, :, j], w, m, v, D // N_DEV, "adamw_" + nm)
    res["w_pool_grp"] = update(jnp.stack(parts["pool"]), w_pool_grp, m_w_pool_grp, v_w_pool_grp, 128, "adamw_w_pool")
    res["w_alpha_up"] = update(p_up, w_alpha_up, m_w_alpha_up, v_w_alpha_up, DEPTH * RANK, "adamw_w_up", layered=False)
    res["gla_norm_g"] = update(p_gn, gla_norm_g, m_gla_norm_g, v_gla_norm_g, DEPTH * HEADS, "adamw_gnorm", layered=False)
    rep_out = update(p_rep, cat([w for _, w, _, _ in rep]), cat([m for _, _, m, _ in rep]), cat([v for _, _, _, v in rep]),
                     DEPTH, "adamw_small", layered=False)
    off = 0
    for nm, w, _, _ in rep:
        n = w.shape[1]
        res[nm] = [o_[:, off:off + n] for o_ in rep_out]
        off += n

    order = ("w_in", "w_alpha_up", "b_alpha", "gla_norm_g", "w_pool_grp", "pool_scale", "b_merge", "w_proj_a", "w_proj_b",
             "w_out", "ln_g", "ln_b")
    return (loss, grad_x, *[res[n][0] for n in order], *[res[n][1] for n in order],
            *[res[n][2] for n in order], *[res[n][3] for n in order])
```

```python
import jax
import jax.numpy as jnp
from jax import lax
from jax.experimental import pallas as pl
from jax.experimental.pallas import tpu as pltpu

F32 = jnp.float32
MXU = jnp.bfloat16
WIRE = jnp.bfloat16

N_DEV = 8
DEPTH = 4
D = 1024
HEADS = 4
DK = D // 2
HDK = DK // HEADS
HDV = D // HEADS
RANK = 16
CHUNK = 64
GATE_TAU = 16.0
POOL_WINDOWS = (2, 4, 8, 16)
PG = D // len(POOL_WINDOWS)
HALO = 16
IN_COLS = 7184
SHARD = IN_COLS // N_DEV
ALPHA = (2.0 * DEPTH) ** 0.25
EPS = 1e-5
Q_SCALE = HDK ** -0.5

ADAM_LR, ADAM_B1, ADAM_B2, ADAM_EPS, ADAM_WD, ADAM_STEP = 0.001, 0.9, 0.999, 1e-08, 0.01, 10

PI0, GB0, ML0, AL0, AL_W = 0, D, 2 * D, 4 * D, 512
HD0 = AL0 + AL_W
HEAD_W = 2 * HDK + 2 * HDV
HP = HD0 + HEADS * HEAD_W
HPB = 2
O_Q, O_K, O_V, O_GA, O_AL, O_PI, O_GB, O_ML = 0, DK, 2 * DK, 2 * DK + D, 2 * DK + 2 * D, 2 * DK + 2 * D + RANK, \
    2 * DK + 3 * D + RANK, 2 * DK + 4 * D + RANK

VMEM_BIG = 56 * 1024 * 1024
VMEM_MID = 40 * 1024 * 1024

NN = ((1,), (0,))
NT = ((1,), (1,))
TN = ((0,), (0,))

HBM = pl.BlockSpec(memory_space=pltpu.HBM)
ANY = pl.BlockSpec(memory_space=pl.ANY)


def _dot(a, b, dims):
    return lax.dot_general(a.astype(MXU), b.astype(MXU), (dims, ((), ())), preferred_element_type=F32)


def _params(sem, vmem):
    return pltpu.CompilerParams(dimension_semantics=sem, vmem_limit_bytes=vmem)


def _sigmoid(x):
    return 1.0 / (1.0 + jnp.exp(-x))


def _log_sigmoid(z):
    return jnp.minimum(z, 0.0) - jnp.log(1.0 + jnp.exp(-jnp.abs(z)))


class _Exchange:
    def __init__(self, items):
        self.items = [(s, bool(g)) for s, g in items]
        self.n = len(self.items)
        self.srcs = [s for s, _ in self.items]
        self.in_specs = [HBM] * self.n
        self.out_specs = [HBM] * self.n
        self.out_shape = [jax.ShapeDtypeStruct((N_DEV,) + tuple(s.shape if g else s.shape[1:]), s.dtype) for s, g in self.items]
        self.scratch = [pltpu.SemaphoreType.DMA((self.n * (N_DEV - 1),)), pltpu.SemaphoreType.DMA((self.n * (N_DEV - 1),)),
                        pltpu.SemaphoreType.DMA((self.n,))]

    def copies(self, src_refs, out_refs, send_sems, recv_sems, local_sems):
        x, y, c = lax.axis_index("x"), lax.axis_index("y"), lax.axis_index("c")
        me = 4 * x + 2 * y + c
        copies = []
        for t, (_, gather) in enumerate(self.items):
            src_ref, out_ref = src_refs[t], out_refs[t]
            copies.append(pltpu.make_async_copy(src_ref if gather else src_ref.at[me], out_ref.at[me], local_sems.at[t]))
            for k in range(1, N_DEV):
                px = 1 - x if k & 4 else x
                py = 1 - y if k & 2 else y
                pc = 1 - c if k & 1 else c
                peer = 4 * px + 2 * py + pc
                sem = t * (N_DEV - 1) + k - 1
                copies.append(pltpu.make_async_remote_copy(
                    src_ref=src_ref if gather else src_ref.at[peer],
                    dst_ref=out_ref.at[me],
                    send_sem=send_sems.at[sem],
                    recv_sem=recv_sems.at[sem],
                    device_id=(px, py, pc),
                    device_id_type=pl.DeviceIdType.MESH,
                ))
        return copies


    mid_step = None

    def start(self, *refs):
        for cp in self.copies(*refs):
            cp.start()

    def finish(self, *refs):
        for cp in self.copies(*refs):
            cp.wait()


class _ChipGather:
    def __init__(self, src, mid_step):
        self.n = 1
        self.srcs = [src]
        self.mid_step = mid_step
        self.in_specs = [HBM]
        self.out_specs = [HBM]
        self.out_shape = [jax.ShapeDtypeStruct((N_DEV,) + tuple(src.shape), src.dtype)]
        self.scratch = [pltpu.SemaphoreType.DMA((N_DEV - 1,)), pltpu.SemaphoreType.DMA((N_DEV - 1,)), pltpu.SemaphoreType.DMA((1,))]

    def _plan(self, src_refs, out_refs, send_sems, recv_sems, local_sems):
        src_ref, out_ref = src_refs[0], out_refs[0]
        x, y, c = lax.axis_index("x"), lax.axis_index("y"), lax.axis_index("c")
        me, sibling = (x, y, c), (x, y, 1 - c)
        chips = [(1 - x, y), (x, 1 - y), (1 - x, 1 - y)]

        def copy(k, block, to, src=None):
            slot = out_ref.at[4 * block[0] + 2 * block[1] + block[2]]
            return pltpu.make_async_remote_copy(
                src_ref=slot if src is None else src, dst_ref=slot, send_sem=send_sems.at[k], recv_sem=recv_sems.at[k],
                device_id=to, device_id_type=pl.DeviceIdType.MESH)

        mine = pltpu.make_async_copy(src_ref, out_ref.at[4 * x + 2 * y + c], local_sems.at[0])
        first = [copy(0, me, sibling, src=src_ref)] + [copy(1 + j, me, (*chip, c), src=src_ref) for j, chip in enumerate(chips)]
        landed = [copy(1 + j, (*chip, c), me) for j, chip in enumerate(chips)]
        passed = [copy(4 + j, (*chip, c), sibling) for j, chip in enumerate(chips)]
        from_sibling = [copy(0, sibling, me)] + [copy(4 + j, (*chip, 1 - c), me) for j, chip in enumerate(chips)]
        return mine, first, landed, passed, from_sibling

    def start(self, *refs):
        mine, first, _, _, _ = self._plan(*refs)
        mine.start()
        for cp in first:
            cp.start()

    def mid(self, *refs):
        _, _, landed, passed, _ = self._plan(*refs)
        for got, fwd in zip(landed, passed):
            got.wait_recv()
            fwd.start()

    def finish(self, *refs):
        mine, first, _, passed, from_sibling = self._plan(*refs)
        for cp in from_sibling:
            cp.wait_recv()
        for cp in first + passed:
            cp.wait_send()
        mine.wait()


def _grid_ends(grid):
    first = last = step = None
    for a, n in enumerate(grid):
        f = pl.program_id(a) == 0
        e = pl.program_id(a) == n - 1
        first = f if first is None else first & f
        last = e if last is None else last & e
        step = pl.program_id(a) if step is None else step * n + pl.program_id(a)
    return first, last, step


def _call(body, *, name, grid, in_specs, out_specs, out_shape, args, scratch=(), sem=None, vmem=VMEM_MID, ride=None, aliases=None):
    n_in, n_out, n_scr = len(in_specs), len(out_specs), len(scratch)
    sem = sem or ("arbitrary",) * len(grid)
    if ride is None:
        return pl.pallas_call(body, name=name, grid=grid, in_specs=in_specs, out_specs=out_specs, out_shape=out_shape,
                              scratch_shapes=list(scratch), compiler_params=_params(sem, vmem),
                              input_output_aliases=aliases or {})(*args)
    r = ride.n

    def riding(*refs):
        ins, rsrc = refs[:n_in], refs[n_in:n_in + r]
        outs, rout = refs[n_in + r:n_in + r + n_out], refs[n_in + r + n_out:n_in + 2 * r + n_out]
        scr = refs[n_in + 2 * r + n_out:n_in + 2 * r + n_out + n_scr]
        send_sems, recv_sems, local_sems = refs[n_in + 2 * r + n_out + n_scr:]
        first, last, step = _grid_ends(grid)
        comm = (rsrc, rout, send_sems, recv_sems, local_sems)

        @pl.when(first)
        def _():
            ride.start(*comm)

        body(*ins, *outs, *scr)

        if ride.mid_step is not None:
            @pl.when(step == ride.mid_step)
            def _():
                ride.mid(*comm)

        @pl.when(last)
        def _():
            ride.finish(*comm)

    return pl.pallas_call(riding, name=name, grid=grid, in_specs=list(in_specs) + ride.in_specs,
                          out_specs=list(out_specs) + ride.out_specs, out_shape=list(out_shape) + ride.out_shape,
                          scratch_shapes=list(scratch) + ride.scratch,
                          compiler_params=_params(("arbitrary",) * len(grid), vmem),
                          input_output_aliases=aliases or {})(*args, *ride.srcs)


def _exchange(items, *, name):
    ex = _Exchange(items)

    def body(*refs):
        copies = ex.copies(refs[:ex.n], refs[ex.n:2 * ex.n], *refs[2 * ex.n:])
        for cp in copies:
            cp.start()
        for cp in copies:
            cp.wait()

    return pl.pallas_call(body, name=name, in_specs=ex.in_specs, out_specs=ex.out_specs, out_shape=ex.out_shape,
                          scratch_shapes=ex.scratch)(*ex.srcs)


def _gather_first(big, smalls, *, name):
    ex = _Exchange([(a, True) for a in smalls])

    def body(*refs):
        big_ref, small_src = refs[0], refs[1:1 + ex.n]
        out_ref, small_out = refs[1 + ex.n], refs[2 + ex.n:2 + 2 * ex.n]
        send_sems, recv_sems, local_sem = refs[2 + 2 * ex.n:5 + 2 * ex.n]
        x, y, c = lax.axis_index("x"), lax.axis_index("y"), lax.axis_index("c")
        me, sibling = (x, y, c), (x, y, 1 - c)
        chips = [(1 - x, y), (x, 1 - y), (1 - x, 1 - y)]

        def slot(px, py, pc):
            return out_ref.at[4 * px + 2 * py + pc]

        def copy(k, block, to, src=None):
            return pltpu.make_async_remote_copy(
                src_ref=slot(*block) if src is None else src, dst_ref=slot(*block),
                send_sem=send_sems.at[k], recv_sem=recv_sems.at[k], device_id=to, device_id_type=pl.DeviceIdType.MESH)

        small = ex.copies(small_src, small_out, *refs[5 + 2 * ex.n:])
        mine = pltpu.make_async_copy(big_ref, slot(*me), local_sem)
        mine.start()
        first = [copy(0, me, sibling, src=big_ref)] + [copy(1 + j, me, (*chip, c), src=big_ref) for j, chip in enumerate(chips)]
        for cp in first + small:
            cp.start()
        passed = [copy(4 + j, (*chip, c), sibling) for j, chip in enumerate(chips)]
        for j, chip in enumerate(chips):
            copy(1 + j, (*chip, c), me).wait_recv()
            passed[j].start()
        copy(0, sibling, me).wait_recv()
        for j, chip in enumerate(chips):
            copy(4 + j, (*chip, 1 - c), me).wait_recv()
        for cp in first + passed:
            cp.wait_send()
        mine.wait()
        for cp in small:
            cp.wait()

    return pl.pallas_call(
        body, name=name, in_specs=[HBM] + ex.in_specs, out_specs=[HBM] + ex.out_specs,
        out_shape=[jax.ShapeDtypeStruct((N_DEV,) + tuple(big.shape), big.dtype)] + ex.out_shape,
        scratch_shapes=[pltpu.SemaphoreType.DMA((N_DEV - 1,)), pltpu.SemaphoreType.DMA((N_DEV - 1,)), pltpu.SemaphoreType.DMA]
        + ex.scratch)(big, *ex.srcs)


def _move_rows(src, segs, out_rows, *, name, zero=None):
    cols = src.shape[1]
    step = 256

    def body(src_ref, out_ref):
        for s0, d0, n in segs:
            for r in range(0, n, step):
                m = min(step, n - r)
                out_ref[d0 + r:d0 + r + m, :] = src_ref[s0 + r:s0 + r + m, :]
        if zero is not None:
            out_ref[zero[0]:zero[0] + zero[1], :] = jnp.zeros((zero[1], cols), src.dtype)

    vmem = pl.BlockSpec(memory_space=pltpu.VMEM)
    return pl.pallas_call(
        body, name=name, in_specs=[vmem], out_specs=vmem, out_shape=jax.ShapeDtypeStruct((out_rows, cols), src.dtype),
        compiler_params=pltpu.CompilerParams(vmem_limit_bytes=VMEM_BIG))(src)


def _in_proj(xb, wt, *, tm, tn, ride=None):
    m, k = xb.shape
    n = wt.shape[0]

    def body(x_ref, w_ref, o_ref):
        o_ref[...] = _dot(x_ref[...], w_ref[...], NT)

    return _call(
        body, name="in_proj", grid=(n // tn, m // tm),
        in_specs=[pl.BlockSpec((tm, k), lambda j, i: (i, 0)), pl.BlockSpec((tn, k), lambda j, i: (j, 0))],
        out_specs=[pl.BlockSpec((tm, tn), lambda j, i: (i, j))],
        out_shape=[jax.ShapeDtypeStruct((m, n), F32)],
        args=(xb, wt), sem=("parallel", "parallel"), vmem=VMEM_BIG, ride=ride)


def _in_proj_bwd(dh, wt, dr, *, tm, ride=None):
    m, n = dh.shape
    k = wt.shape[1]

    def body(dh_ref, w_ref, dr_ref, o_ref):
        o_ref[...] = ALPHA * dr_ref[...] + _dot(dh_ref[...], w_ref[...], NN)

    return _call(
        body, name="in_proj_bwd", grid=(m // tm,),
        in_specs=[pl.BlockSpec((tm, n), lambda i: (i, 0)),
                  pl.BlockSpec((n, k), lambda i: (0, 0), pipeline_mode=pl.Buffered(1)),
                  pl.BlockSpec((tm, k), lambda i: (i, 0))],
        out_specs=[pl.BlockSpec((tm, k), lambda i: (i, 0))],
        out_shape=[jax.ShapeDtypeStruct((m, k), F32)],
        args=(dh, wt, dr), sem=("parallel",), vmem=VMEM_BIG, ride=ride)


def _mm_nt_into(dc, w, arr, col, *, tm, name):
    m, n = dc.shape
    k = w.shape[0]

    def body(dc_ref, w_ref, _arr_in, o_ref):
        o_ref[...] = _dot(dc_ref[...], w_ref[...], NT).astype(o_ref.dtype)

    return _call(
        body, name=name, grid=(m // tm,),
        in_specs=[pl.BlockSpec((tm, n), lambda i: (i, 0)), pl.BlockSpec((k, n), lambda i: (0, 0)), ANY],
        out_specs=[pl.BlockSpec((tm, k), lambda i: (i, col))],
        out_shape=[jax.ShapeDtypeStruct(arr.shape, arr.dtype)],
        args=(dc, w, arr), sem=("parallel",), aliases={2: 0})[0]


def _mm_tn(a, dc, *, tm, tk, name, a_block=None, out_dtype=F32):
    m = a.shape[0]
    k, a_col = (a.shape[1], None) if a_block is None else a_block
    n = dc.shape[1]
    ni = m // tm

    def body(a_ref, dc_ref, o_ref, acc):
        i = pl.program_id(1)

        @pl.when(i == 0)
        def _():
            acc[...] = jnp.zeros_like(acc)

        acc[...] += _dot(a_ref[...], dc_ref[...], TN)

        @pl.when(i == ni - 1)
        def _():
            o_ref[...] = acc[...].astype(o_ref.dtype)

    a_map = (lambda j, i: (i, j)) if a_col is None else (lambda j, i: (i, a_col))
    return _call(
        body, name=name, grid=(k // tk, ni),
        in_specs=[pl.BlockSpec((tm, tk), a_map), pl.BlockSpec((tm, n), lambda j, i: (i, 0))],
        out_specs=[pl.BlockSpec((tk, n), lambda j, i: (j, 0))],
        out_shape=[jax.ShapeDtypeStruct((k, n), out_dtype)],
        scratch=[pltpu.VMEM((tk, n), F32)],
        args=(a, dc), sem=("parallel", "arbitrary"), vmem=VMEM_BIG)[0]


def _head_cols(p):
    b = p * HEAD_W
    return (slice(b, b + HDK), slice(b + HDK, b + 2 * HDK), slice(b + 2 * HDK, b + 2 * HDK + HDV),
            slice(b + 2 * HDK + HDV, b + HEAD_W))


def _seg_cumsum(v, rowmod):
    sh = 1
    while sh < CHUNK:
        v = v + jnp.where(rowmod >= sh, pltpu.roll(v, sh, 0), 0.0)
        sh *= 2
    return v


def _seg_rcumsum(v, rowmod):
    t = v.shape[0]
    sh = 1
    while sh < CHUNK:
        v = v + jnp.where(rowmod < CHUNK - sh, pltpu.roll(v, t - sh, 0), 0.0)
        sh *= 2
    return v


def _gla_decay(alpha_ref, wup_ref, b_ref, g_scr):
    z = _dot(alpha_ref[...], wup_ref[...], NN) + b_ref[...]
    rowmod = lax.broadcasted_iota(jnp.int32, z.shape, 0) % CHUNK
    g_scr[...] = _seg_cumsum(_log_sigmoid(z) * (1.0 / GATE_TAU), rowmod)
    return z, rowmod


QE1, KE1, QE2, KE2, QA, KD = range(6)
EP, EM, EA, EDL = range(4)
GW = HPB * HDK


def _gla_operands(hd_ref, g_scr, opnd_scr, fac_scr=None):
    t = g_scr.shape[0]

    def chunk_row(r):
        return jnp.concatenate([jnp.broadcast_to(g_scr[c * CHUNK + r:c * CHUNK + r + 1, :], (CHUNK, GW))
                                for c in range(t // CHUNK)], axis=0)

    g = g_scr[...]
    g_last = chunk_row(CHUNK - 1)
    ref = 0.5 * (chunk_row(0) + g_last)
    fac = {EP: jnp.exp(g - ref), EM: jnp.exp(ref - g), EA: jnp.exp(g), EDL: jnp.exp(g_last - g)}
    if fac_scr is not None:
        for j, f in fac.items():
            fac_scr[j] = f
    for p in range(HPB):
        qc, kc, _, _ = _head_cols(p)
        gc = slice(p * HDK, (p + 1) * HDK)
        qs = hd_ref[:, qc] * Q_SCALE
        k = hd_ref[:, kc]
        for j, (x, f) in {QE1: (qs, EP), KE1: (k, EM), QE2: (qs, EM), KE2: (k, EP), QA: (qs, EA), KD: (k, EDL)}.items():
            opnd_scr[j, :, gc] = (x * fac[f][:, gc]).astype(opnd_scr.dtype)


def _lower_mask():
    return lax.broadcasted_iota(jnp.int32, (CHUNK, CHUNK), 0) >= lax.broadcasted_iota(jnp.int32, (CHUNK, CHUNK), 1)


def _scores(opnd_scr, rows, gc, lower):
    return jnp.where(lower, _dot(opnd_scr[QE1, rows, gc], opnd_scr[KE1, rows, gc], NT),
                     _dot(opnd_scr[QE2, rows, gc], opnd_scr[KE2, rows, gc], NT))


def _gla_specs(tt, row):
    return [
        pl.BlockSpec((tt, HPB * HEAD_W), lambda h, i: (row(i), HD0 // (HPB * HEAD_W) + h)),
        pl.BlockSpec((tt, 128), lambda h, i: (row(i), AL0 // 128)),
        pl.BlockSpec((128, HPB * HDK), lambda h, i: (0, h)),
        pl.BlockSpec((1, HPB * HDK), lambda h, i: (0, h)),
        pl.BlockSpec((1, HPB * HDV), lambda h, i: (0, h)),
    ]


def _gla_fwd(hh, wup, b_alpha, gnorm, *, tt, ride=None):
    s = hh.shape[0]
    nt = s // tt
    nct = tt // CHUNK

    def body(hd_ref, al_ref, wup_ref, b_ref, gn_ref, o_ref, ya_ref, st_ref, state, g_scr, opnd_scr):
        @pl.when(pl.program_id(1) == 0)
        def _():
            state[...] = jnp.zeros_like(state)

        _gla_decay(al_ref, wup_ref, b_ref, g_scr)
        _gla_operands(hd_ref, g_scr, opnd_scr)
        lower = _lower_mask()
        for c in range(nct):
            rows = slice(c * CHUNK, (c + 1) * CHUNK)
            for p in range(HPB):
                vc = _head_cols(p)[2]
                gc = slice(p * HDK, (p + 1) * HDK)
                v = hd_ref[rows, vc]
                st = state[p]
                st_ref[p, c] = st
                egl = jnp.exp(g_scr[(c + 1) * CHUNK - 1:(c + 1) * CHUNK, gc])
                o_ref[rows, p * HDV:(p + 1) * HDV] = (_dot(_scores(opnd_scr, rows, gc, lower), v, NN)
                                                      + _dot(opnd_scr[QA, rows, gc], st, NT))
                state[p] = st * egl + _dot(v, opnd_scr[KD, rows, gc], TN)
        for p in range(HPB):
            oc = slice(p * HDV, (p + 1) * HDV)
            o = o_ref[:, oc]
            ohat = o * lax.rsqrt(jnp.mean(o * o, axis=-1, keepdims=True) + EPS)
            ga = hd_ref[:, _head_cols(p)[3]]
            ya_ref[:, oc] = (ohat * gn_ref[:, oc] * (ga * _sigmoid(ga))).astype(ya_ref.dtype)

    return _call(
        body, name="gla_fwd", grid=(HEADS // HPB, nt),
        in_specs=_gla_specs(tt, lambda i: i),
        out_specs=[
            pl.BlockSpec((tt, HPB * HDV), lambda h, i: (i, h)),
            pl.BlockSpec((tt, HPB * HDV), lambda h, i: (i, h)),
            pl.BlockSpec((HPB, nct, HDV, HDK), lambda h, i: (h, i, 0, 0)),
        ],
        out_shape=[
            jax.ShapeDtypeStruct((s, D), F32),
            jax.ShapeDtypeStruct((s, D), MXU),
            jax.ShapeDtypeStruct((HEADS, s // CHUNK, HDV, HDK), F32),
        ],
        scratch=[pltpu.VMEM((HPB, HDV, HDK), F32), pltpu.VMEM((tt, GW), F32), pltpu.VMEM((6, tt, GW), MXU)],
        args=(hh, hh, wup, b_alpha, gnorm), ride=ride)


def _gla_bwd(hh, wup, b_alpha, gnorm, o, states, dya, dh, *, tt, ride=None):
    s = hh.shape[0]
    nt = s // tt
    nct = tt // CHUNK

    def body(hd_ref, al_ref, wup_ref, b_ref, gn_ref, o_ref, st_ref, dya_ref, _dh_in,
             dh_ref, dz_ref, dgn_ref, db_ref, dstate, g_scr, dg_scr, do_scr, opnd_scr, fac_scr, res_scr, dgl_scr):
        @pl.when(pl.program_id(1) == 0)
        def _():
            dstate[...] = jnp.zeros_like(dstate)
            dgn_ref[...] = jnp.zeros_like(dgn_ref)
            db_ref[...] = jnp.zeros_like(db_ref)

        z, rowmod = _gla_decay(al_ref, wup_ref, b_ref, g_scr)
        _gla_operands(hd_ref, g_scr, opnd_scr, fac_scr)

        for p in range(HPB):
            oc = slice(p * HDV, (p + 1) * HDV)
            gac = _head_cols(p)[3]
            o_t = o_ref[:, oc]
            rstd = lax.rsqrt(jnp.mean(o_t * o_t, axis=-1, keepdims=True) + EPS)
            ohat = o_t * rstd
            ga = hd_ref[:, gac]
            sg = _sigmoid(ga)
            dya_t = dya_ref[:, oc]
            gn = gn_ref[:, oc]
            dh_ref[:, gac] = (dya_t * ohat * gn * (sg * (1.0 + ga * (1.0 - sg)))).astype(dh_ref.dtype)
            don = dya_t * (ga * sg)
            dgn_ref[p] += jnp.sum(don * ohat, axis=0, keepdims=True)
            dohat = don * gn
            do_scr[:, oc] = rstd * (dohat - ohat * jnp.mean(dohat * ohat, axis=-1, keepdims=True))

        lower = _lower_mask()
        for c in range(nct - 1, -1, -1):
            rows = slice(c * CHUNK, (c + 1) * CHUNK)
            for p in range(HPB):
                vc = _head_cols(p)[2]
                gc = slice(p * HDK, (p + 1) * HDK)
                v = hd_ref[rows, vc]
                do = do_scr[rows, p * HDV:(p + 1) * HDV]
                st = st_ref[p, c]
                dst = dstate[p]
                egl = jnp.exp(g_scr[(c + 1) * CHUNK - 1:(c + 1) * CHUNK, gc])
                a = _scores(opnd_scr, rows, gc, lower)
                da = _dot(do, v, NT)
                da1 = jnp.where(lower, da, 0.0).astype(MXU)
                da2 = jnp.where(lower, 0.0, da).astype(MXU)
                res_scr[0, rows, gc] = _dot(da1, opnd_scr[KE1, rows, gc], NN)
                res_scr[1, rows, gc] = _dot(da1, opnd_scr[QE1, rows, gc], TN)
                res_scr[2, rows, gc] = _dot(da2, opnd_scr[KE2, rows, gc], NN)
                res_scr[3, rows, gc] = _dot(da2, opnd_scr[QE2, rows, gc], TN)
                res_scr[4, rows, gc] = _dot(do, st, NN)
                res_scr[5, rows, gc] = _dot(v, dst, NN)
                dh_ref[rows, vc] = (_dot(a, do, TN) + _dot(opnd_scr[KD, rows, gc], dst, NT)).astype(dh_ref.dtype)
                dgl_scr[c:c + 1, gc] = egl * jnp.sum(dst * st, axis=0, keepdims=True)
                dstate[p] = dst * egl + _dot(do, opnd_scr[QA, rows, gc], TN)

        p1, p2, p3 = res_scr[0] * fac_scr[EP], res_scr[2] * fac_scr[EM], res_scr[4] * fac_scr[EA]
        r1, r2, r3 = res_scr[1] * fac_scr[EM], res_scr[3] * fac_scr[EP], res_scr[5] * fac_scr[EDL]
        dq = (p1 + p2 + p3) * Q_SCALE
        dk = r1 + r2 + r3
        dgq = p1 - p2 + p3
        dgk = r2 - r1
        for p in range(HPB):
            qc, kc, _, _ = _head_cols(p)
            gc = slice(p * HDK, (p + 1) * HDK)
            dh_ref[:, qc] = dq[:, gc].astype(dh_ref.dtype)
            dh_ref[:, kc] = dk[:, gc].astype(dh_ref.dtype)
            k = hd_ref[:, kc]
            r3k = r3[:, gc] * k
            dg_scr[:, gc] = (hd_ref[:, qc] * Q_SCALE) * dgq[:, gc] + k * dgk[:, gc] - r3k
            for c in range(nct):
                last = slice((c + 1) * CHUNK - 1, (c + 1) * CHUNK)
                dg_scr[last, gc] += jnp.sum(r3k[c * CHUNK:(c + 1) * CHUNK], axis=0, keepdims=True) + dgl_scr[c:c + 1, gc]

        dz = _seg_rcumsum(dg_scr[...], rowmod) * _sigmoid(-z) * (1.0 / GATE_TAU)
        dz_ref[...] = dz.astype(dz_ref.dtype)
        for p in range(HPB):
            db_ref[p] += jnp.sum(dz[:, p * HDK:(p + 1) * HDK], axis=0, keepdims=True)

    rev = lambda i: nt - 1 - i
    in_specs = _gla_specs(tt, rev) + [
        pl.BlockSpec((tt, HPB * HDV), lambda h, i: (rev(i), h)),
        pl.BlockSpec((HPB, nct, HDV, HDK), lambda h, i: (h, rev(i), 0, 0)),
        pl.BlockSpec((tt, HPB * HDV), lambda h, i: (rev(i), h)),
        ANY,
    ]
    return _call(
        body, name="gla_bwd", grid=(HEADS // HPB, nt), in_specs=in_specs,
        out_specs=[
            pl.BlockSpec((tt, HPB * HEAD_W), lambda h, i: (rev(i), HD0 // (HPB * HEAD_W) + h)),
            pl.BlockSpec((tt, HPB * HDK), lambda h, i: (rev(i), h)),
            pl.BlockSpec((HPB, 1, HDV), lambda h, i: (h, 0, 0)),
            pl.BlockSpec((HPB, 1, HDK), lambda h, i: (h, 0, 0)),
        ],
        out_shape=[
            jax.ShapeDtypeStruct(dh.shape, dh.dtype),
            jax.ShapeDtypeStruct((s, DK), MXU),
            jax.ShapeDtypeStruct((HEADS, 1, HDV), F32),
            jax.ShapeDtypeStruct((HEADS, 1, HDK), F32),
        ],
        scratch=[pltpu.VMEM((HPB, HDV, HDK), F32), pltpu.VMEM((tt, GW), F32), pltpu.VMEM((tt, GW), F32),
                 pltpu.VMEM((tt, HPB * HDV), F32), pltpu.VMEM((6, tt, GW), MXU), pltpu.VMEM((4, tt, GW), F32),
                 pltpu.VMEM((6, tt, GW), F32), pltpu.VMEM((max(nct, 8), GW), F32)],
        args=(hh, hh, wup, b_alpha, gnorm, o, states, dya, dh), ride=ride, aliases={8: 0})


def _window_count(tile, tt, w):
    pos = tile * tt + lax.broadcasted_iota(jnp.int32, (tt, PG), 0) + 1
    return jnp.minimum(pos, w).astype(F32)


def _pool_fwd(hh, wpool, scale, *, tt):
    s = hh.shape[0]
    nt = s // tt

    def body(ug_ref, w_ref, sc_ref, pooled_ref, yb_ref, halo):
        i = pl.program_id(0)

        @pl.when(i == 0)
        def _():
            halo[...] = jnp.zeros_like(halo)

        for g, w in enumerate(POOL_WINDOWS):
            cols = slice(g * PG, (g + 1) * PG)
            u = ug_ref[:, cols]
            run = jnp.concatenate([halo[:, cols], u], axis=0)
            sh = 1
            while sh < w:
                run = run + pltpu.roll(run, sh, 0)
                sh *= 2
            pooled = run[HALO:, :] / _window_count(i, tt, w) - u
            pooled_ref[:, cols] = pooled.astype(pooled_ref.dtype)
            mixed = _dot(pooled, w_ref[g], NN)
            gb = ug_ref[:, slice(D + g * PG, D + (g + 1) * PG)]
            yb_ref[:, cols] = (mixed * sc_ref[:, cols] * (gb * _sigmoid(gb))).astype(yb_ref.dtype)
        halo[...] = ug_ref[tt - HALO:tt, :D]

    tile = pl.BlockSpec((tt, D), lambda i: (i, 0))
    return _call(
        body, name="pool_fwd", grid=(nt,),
        in_specs=[
            pl.BlockSpec((tt, 2 * D), lambda i: (i, PI0 // (2 * D))),
            pl.BlockSpec((len(POOL_WINDOWS), PG, PG), lambda i: (0, 0, 0)),
            pl.BlockSpec((1, D), lambda i: (0, 0)),
        ],
        out_specs=[tile] * 2,
        out_shape=[jax.ShapeDtypeStruct((s, D), MXU), jax.ShapeDtypeStruct((s, D), MXU)],
        scratch=[pltpu.VMEM((HALO, D), F32)], args=(hh, wpool, scale))


def _pool_bwd(hh, wpool, scale, pooled, dyb, dh, *, tt):
    s = hh.shape[0]
    nt = s // tt

    def body(gb_ref, w_ref, sc_ref, pooled_ref, dyb_ref, _dh_in, dh_ref, dw_ref, dsc_ref, halo):
        i = pl.program_id(0)
        tile = nt - 1 - i

        @pl.when(i == 0)
        def _():
            halo[...] = jnp.zeros_like(halo)
            dw_ref[...] = jnp.zeros_like(dw_ref)
            dsc_ref[...] = jnp.zeros_like(dsc_ref)

        for g, w in enumerate(POOL_WINDOWS):
            cols = slice(g * PG, (g + 1) * PG)
            gcols = slice(D + g * PG, D + (g + 1) * PG)
            gb = gb_ref[:, cols]
            sg = _sigmoid(gb)
            pooled = pooled_ref[:, cols]
            mixed = _dot(pooled, w_ref[g], NN)
            sc = sc_ref[:, cols]
            dyb = dyb_ref[:, cols]
            dh_ref[:, gcols] = (dyb * mixed * sc * (sg * (1.0 + gb * (1.0 - sg)))).astype(dh_ref.dtype)
            dms = dyb * (gb * sg)
            dsc_ref[:, cols] += jnp.sum(dms * mixed, axis=0, keepdims=True)
            dmixed = dms * sc
            dpooled = _dot(dmixed, w_ref[g], NT)
            dw_ref[g] += _dot(pooled, dmixed, TN)
            e = dpooled / _window_count(tile, tt, w)
            run = jnp.concatenate([e, halo[:, cols]], axis=0)
            sh = 1
            while sh < w:
                run = run + pltpu.roll(run, tt + HALO - sh, 0)
                sh *= 2
            dh_ref[:, cols] = (run[:tt, :] - dpooled).astype(dh_ref.dtype)
            halo[:, cols] = e[:HALO, :]

    rev = lambda i: nt - 1 - i
    tile = pl.BlockSpec((tt, D), lambda i: (rev(i), 0))
    wspec = pl.BlockSpec((len(POOL_WINDOWS), PG, PG), lambda i: (0, 0, 0))
    vec = pl.BlockSpec((1, D), lambda i: (0, 0))
    return _call(
        body, name="pool_bwd", grid=(nt,),
        in_specs=[pl.BlockSpec((tt, D), lambda i: (rev(i), GB0 // D)), wspec, vec, tile, tile, ANY],
        out_specs=[pl.BlockSpec((tt, 2 * D), lambda i: (rev(i), PI0 // (2 * D))), wspec, vec],
        out_shape=[jax.ShapeDtypeStruct(dh.shape, dh.dtype), jax.ShapeDtypeStruct((len(POOL_WINDOWS), PG, PG), F32),
                   jax.ShapeDtypeStruct((1, D), F32)],
        scratch=[pltpu.VMEM((HALO, D), F32)], args=(hh, wpool, scale, pooled, dyb, dh), aliases={5: 0})


def _merge_fwd(hh, x, ya, yb, wpa, wpb, wout, b_merge, ln_g, ln_b, *, tt):
    s = x.shape[0]

    def body(ml_ref, x_ref, ya_ref, yb_ref, wpa_ref, wpb_ref, wout_ref, bm_ref, g_ref, b_ref, r_ref, xn_ref, xnb_ref):
        pa = _dot(ya_ref[...], wpa_ref[...], NN)
        pb = _dot(yb_ref[...], wpb_ref[...], NN)
        merged = _sigmoid(ml_ref[:, :D] + bm_ref[:, :D]) * pa + _sigmoid(ml_ref[:, D:] + bm_ref[:, D:]) * pb
        r = ALPHA * x_ref[...] + _dot(merged, wout_ref[...], NN)
        r_ref[...] = r
        mu = jnp.mean(r, axis=-1, keepdims=True)
        xc = r - mu
        var = jnp.mean(xc * xc, axis=-1, keepdims=True)
        xn = xc * lax.rsqrt(var + EPS) * g_ref[...] + b_ref[...]
        xn_ref[...] = xn
        xnb_ref[...] = xn.astype(xnb_ref.dtype)

    tile = pl.BlockSpec((tt, D), lambda i: (i, 0))
    full = pl.BlockSpec((D, D), lambda i: (0, 0), pipeline_mode=pl.Buffered(1))
    vec = pl.BlockSpec((1, D), lambda i: (0, 0))
    return _call(
        body, name="merge_fwd", grid=(s // tt,),
        in_specs=[pl.BlockSpec((tt, 2 * D), lambda i: (i, ML0 // (2 * D))), tile, tile, tile, full, full, full,
                  pl.BlockSpec((1, 2 * D), lambda i: (0, 0)), vec, vec],
        out_specs=[tile] * 3, out_shape=[jax.ShapeDtypeStruct((s, D), F32)] * 2 + [jax.ShapeDtypeStruct((s, D), MXU)],
        args=(hh, x, ya, yb, wpa, wpb, wout, b_merge, ln_g, ln_b), sem=("parallel",), vmem=VMEM_BIG)


def _merge_bwd(hh, r, ya, yb, dout, wpa, wpb, wout, b_merge, ln_g, *, tt):
    s = r.shape[0]

    def body(ml_ref, r_ref, ya_ref, yb_ref, do_ref, wpa_ref, wpb_ref, wout_ref, bm_ref, g_ref,
             dh_ref, dr_ref, dpa_ref, dpb_ref, dwout_ref, dg_ref, db_ref, dbm_ref):
        @pl.when(pl.program_id(0) == 0)
        def _():
            dwout_ref[...] = jnp.zeros_like(dwout_ref)
            dg_ref[...] = jnp.zeros_like(dg_ref)
            db_ref[...] = jnp.zeros_like(db_ref)
            dbm_ref[...] = jnp.zeros_like(dbm_ref)

        rr = r_ref[...]
        mu = jnp.mean(rr, axis=-1, keepdims=True)
        xc = rr - mu
        rstd = lax.rsqrt(jnp.mean(xc * xc, axis=-1, keepdims=True) + EPS)
        xhat = xc * rstd
        do = do_ref[...]
        dg_ref[...] += jnp.sum(do * xhat, axis=0, keepdims=True)
        db_ref[...] += jnp.sum(do, axis=0, keepdims=True)
        dxh = do * g_ref[...]
        dr = rstd * (dxh - jnp.mean(dxh, axis=-1, keepdims=True) - xhat * jnp.mean(dxh * xhat, axis=-1, keepdims=True))
        dr_ref[...] = dr
        g_a = _sigmoid(ml_ref[:, :D] + bm_ref[:, :D])
        g_b = _sigmoid(ml_ref[:, D:] + bm_ref[:, D:])
        pa = _dot(ya_ref[...], wpa_ref[...], NN)
        pb = _dot(yb_ref[...], wpb_ref[...], NN)
        dwout_ref[...] += _dot(g_a * pa + g_b * pb, dr, TN)
        dm = _dot(dr, wout_ref[...], NT)
        dpa_ref[...] = (dm * g_a).astype(dpa_ref.dtype)
        dpb_ref[...] = (dm * g_b).astype(dpb_ref.dtype)
        dml_a = dm * pa * g_a * (1.0 - g_a)
        dml_b = dm * pb * g_b * (1.0 - g_b)
        dh_ref[:, :D] = dml_a.astype(dh_ref.dtype)
        dh_ref[:, D:] = dml_b.astype(dh_ref.dtype)
        dbm_ref[:, :D] += jnp.sum(dml_a, axis=0, keepdims=True)
        dbm_ref[:, D:] += jnp.sum(dml_b, axis=0, keepdims=True)

    tile = pl.BlockSpec((tt, D), lambda i: (i, 0))
    full = pl.BlockSpec((D, D), lambda i: (0, 0))
    vec = pl.BlockSpec((1, D), lambda i: (0, 0))
    vec2 = pl.BlockSpec((1, 2 * D), lambda i: (0, 0))
    mlb = pl.BlockSpec((tt, 2 * D), lambda i: (i, ML0 // (2 * D)))
    return _call(
        body, name="merge_bwd", grid=(s // tt,),
        in_specs=[mlb, tile, tile, tile, tile, full, full, full, vec2, vec],
        out_specs=[mlb, tile, tile, tile, full, vec, vec, vec2],
        out_shape=[
            jax.ShapeDtypeStruct((s, HP), MXU), jax.ShapeDtypeStruct((s, D), F32),
            jax.ShapeDtypeStruct((s, D), MXU), jax.ShapeDtypeStruct((s, D), MXU),
            jax.ShapeDtypeStruct((D, D), F32), jax.ShapeDtypeStruct((1, D), F32),
            jax.ShapeDtypeStruct((1, D), F32), jax.ShapeDtypeStruct((1, 2 * D), F32),
        ],
        args=(hh, r, ya, yb, dout, wpa, wpb, wout, b_merge, ln_g), vmem=VMEM_BIG)


def _proj_bwd(y, dp, w, *, tt, name):
    s = y.shape[0]

    def body(y_ref, dp_ref, w_ref, dy_ref, dw_ref):
        @pl.when(pl.program_id(0) == 0)
        def _():
            dw_ref[...] = jnp.zeros_like(dw_ref)

        dp = dp_ref[...]
        dy_ref[...] = _dot(dp, w_ref[...], NT)
        dw_ref[...] += _dot(y_ref[...], dp, TN)

    tile = pl.BlockSpec((tt, D), lambda i: (i, 0))
    full = pl.BlockSpec((D, D), lambda i: (0, 0))
    return _call(
        body, name=name, grid=(s // tt,), in_specs=[tile, tile, full], out_specs=[tile, full],
        out_shape=[jax.ShapeDtypeStruct((s, D), F32), jax.ShapeDtypeStruct((D, D), F32)], args=(y, dp, w))


def _loss_head(y, target, *, tt):
    s = y.shape[0]

    def body(y_ref, t_ref, loss_ref, dy_ref):
        @pl.when(pl.program_id(0) == 0)
        def _():
            loss_ref[...] = jnp.zeros_like(loss_ref)

        err = y_ref[...] - t_ref[...]
        dy_ref[...] = err * (1.0 / D)
        per_tok = jnp.mean(err * err, axis=-1, keepdims=True)
        loss_ref[...] += 0.5 * jnp.sum(per_tok, axis=0, keepdims=True)

    tile = pl.BlockSpec((tt, D), lambda i: (i, 0))
    return _call(
        body, name="loss_head", grid=(s // tt,), in_specs=[tile, tile],
        out_specs=[pl.BlockSpec((1, 1), lambda i: (0, 0)), tile],
        out_shape=[jax.ShapeDtypeStruct((1, 1), F32), jax.ShapeDtypeStruct((s, D), F32)], args=(y, target))


def _adamw(parts, w, m, v, *, tr, tc, name):
    nl, rows, cols = w.shape

    def body(p_ref, w_ref, m_ref, v_ref, g_ref, d_ref, nm_ref, nv_ref):
        g = p_ref[0, 0].astype(F32)
        for q in range(1, N_DEV):
            g = g + p_ref[0, q].astype(F32)
        g_ref[0] = g
        nm = ADAM_B1 * m_ref[0] + (1.0 - ADAM_B1) * g
        nv = ADAM_B2 * v_ref[0] + (1.0 - ADAM_B2) * (g * g)
        nm_ref[0] = nm
        nv_ref[0] = nv
        m_hat = nm / (1.0 - ADAM_B1 ** ADAM_STEP)
        v_hat = nv / (1.0 - ADAM_B2 ** ADAM_STEP)
        d_ref[0] = -ADAM_LR * (m_hat / (jnp.sqrt(v_hat) + ADAM_EPS) + ADAM_WD * w_ref[0])

    tile = pl.BlockSpec((1, tr, tc), lambda l, i, j: (l, i, j))
    return _call(
        body, name=name, grid=(nl, rows // tr, cols // tc),
        in_specs=[pl.BlockSpec((1, N_DEV, tr, tc), lambda l, i, j: (l, 0, i, j)), tile, tile, tile],
        out_specs=[tile] * 4, out_shape=[jax.ShapeDtypeStruct((nl, rows, cols), F32)] * 4,
        args=(parts, w, m, v), sem=("parallel", "parallel", "parallel"))


def _from_devices(g, axis):
    nd = g.ndim - 1
    perm = list(range(1, axis + 1)) + [0] + list(range(axis + 1, nd + 1))
    shape = list(g.shape[1:])
    shape[axis] *= N_DEV
    return jnp.transpose(g, perm).reshape(shape)


def _to_devices(a, axis):
    shape = list(a.shape)
    t = a.reshape(shape[:axis] + [N_DEV, shape[axis] // N_DEV] + shape[axis + 1:])
    return jnp.transpose(t, [axis] + list(range(0, axis)) + list(range(axis + 1, t.ndim)))


def _h_row_segments():
    segs = [(O_PI, PI0, IN_COLS - O_PI), (O_AL, AL0, RANK)]
    for h in range(HEADS):
        base = HD0 + h * HEAD_W
        segs += [(O_Q + h * HDK, base, HDK), (O_K + h * HDK, base + HDK, HDK),
                 (O_V + h * HDV, base + 2 * HDK, HDV), (O_GA + h * HDV, base + 2 * HDK + HDV, HDV)]
    return segs


def _h_weight_t(parts):
    return _move_rows(parts.reshape(IN_COLS, D), _h_row_segments(), HP, name="w_in_rows", zero=(AL0 + RANK, AL_W - RANK))


def _w_in_grad_parts_t(dwt):
    g = _move_rows(dwt, [(d0, s0, n) for s0, d0, n in _h_row_segments()], IN_COLS, name="w_in_grad_rows")
    return g.reshape(N_DEV, SHARD, D)


def kernel(x, w_in, w_alpha_up, b_alpha, gla_norm_g, w_pool_grp, pool_scale, b_merge, w_proj_a, w_proj_b, w_out, ln_g, ln_b, loss_target, m_w_in, m_w_alpha_up, m_b_alpha, m_gla_norm_g, m_w_pool_grp, m_pool_scale, m_b_merge, m_w_proj_a, m_w_proj_b, m_w_out, m_ln_g, m_ln_b, v_w_in, v_w_alpha_up, v_b_alpha, v_gla_norm_g, v_w_pool_grp, v_pool_scale, v_b_merge, v_w_proj_a, v_w_proj_b, v_w_out, v_ln_g, v_ln_b):
    s = x.shape[1]
    tt = min(256, s)
    tm = min(512, s)
    tb = min(1024, s)
    tn = HP // 3
    xs = x.reshape(s, D)

    tr3 = lambda a: jnp.transpose(a, (0, 2, 1))
    w_in_s = tr3(w_in).astype(WIRE)
    proj_s = jnp.stack([w_proj_a, w_proj_b, w_out], axis=1).astype(WIRE)
    pool_s = w_pool_grp.astype(WIRE)

    g_in, g_up, g_gn = _gather_first(w_in_s[0], [w_alpha_up.astype(WIRE), gla_norm_g], name="gather_first")
    wup = jnp.pad(_from_devices(g_up, 2), ((0, 0), (0, AL_W - RANK), (0, 0)))
    gn = _from_devices(g_gn, 2).reshape(DEPTH, 1, D)

    saved, wt_all, proj_all, pool_all = [], [], [], []
    cur, cur_b = xs, xs.astype(MXU)
    g_proj = g_pool = None
    for l in range(DEPTH):
        wt = _h_weight_t(g_in)
        nxt_l = l + 1 < DEPTH
        steps = (HP // tn) * (s // tb)
        res = _in_proj(cur_b, wt, tm=tb, tn=tn, ride=_ChipGather(w_in_s[l + 1], (2 * steps) // 3) if nxt_l else None)
        hh = res[0]
        if nxt_l:
            g_in = res[1]
        layers = ([0] if l == 0 else []) + ([l + 1] if nxt_l else [])
        res = _gla_fwd(hh, wup[l], b_alpha[l:l + 1], gn[l], tt=tm,
                       ride=_Exchange([(a[j], True) for j in layers for a in (proj_s, pool_s)]) if layers else None)
        o, ya, states = res[:3]
        got = {j: res[3 + 2 * t:5 + 2 * t] for t, j in enumerate(layers)}
        if l == 0:
            g_proj, g_pool = got[0]
        proj = _from_devices(g_proj, 1)
        pool = _from_devices(g_pool, 1)
        if nxt_l:
            g_proj, g_pool = got[l + 1]
        wt_all.append(wt), proj_all.append(proj), pool_all.append(pool)
        pooled, yb = _pool_fwd(hh, pool, pool_scale[l:l + 1], tt=tm)
        r, nxt, nxt_b = _merge_fwd(hh, cur, ya, yb, proj[0], proj[1], proj[2],
                                   b_merge[l:l + 1], ln_g[l:l + 1], ln_b[l:l + 1], tt=tm)
        saved.append(dict(xb=cur_b, hh=hh, o=o, ya=ya, states=states, pooled=pooled, yb=yb, r=r))
        cur, cur_b = nxt, nxt_b

    loss_part, dcur = _loss_head(cur, loss_target.reshape(s, D), tt=tm)
    loss = lax.psum(loss_part[0, 0], ("x", "y", "c"))

    small = {k: [None] * DEPTH for k in ("w_up", "b_alpha", "gnorm", "pool_scale", "b_merge", "ln_g", "ln_b")}
    parts = {k: [None] * DEPTH for k in ("w_in", "proj", "pool")}
    for l in range(DEPTH - 1, -1, -1):
        sv = saved[l]
        hh = sv["hh"]
        dh, dr, dpa, dpb, dw_out, dln_g, dln_b, db_merge = _merge_bwd(
            hh, sv["r"], sv["ya"], sv["yb"], dcur, proj_all[l][0], proj_all[l][1], proj_all[l][2], b_merge[l:l + 1], ln_g[l:l + 1], tt=tt)
        dya, dw_pa = _proj_bwd(sv["ya"], dpa, proj_all[l][0], tt=tm, name="proj_a_bwd")
        dyb, dw_pb = _proj_bwd(sv["yb"], dpb, proj_all[l][1], tt=tm, name="proj_b_bwd")
        dh, dw_pool, dscale = _pool_bwd(hh, pool_all[l], pool_scale[l:l + 1], sv["pooled"], dyb, dh, tt=tm)
        ride = _Exchange([(_to_devices(jnp.stack([dw_pa, dw_pb, dw_out]), 1).astype(WIRE), False),
                          (_to_devices(dw_pool, 1).astype(WIRE), False)])
        dh, dz, dgn, db_al, parts["proj"][l], parts["pool"][l] = _gla_bwd(
            hh, wup[l], b_alpha[l:l + 1], gn[l], sv["o"], sv["states"], dya, dh, tt=tm, ride=ride)
        dh = _mm_nt_into(dz, wup[l], dh, AL0 // AL_W, tm=tm, name="alpha_bwd")
        dw_up = _mm_tn(hh, dz, tm=tb, tk=128, name="w_up_grad", a_block=(128, AL0 // 128))
        dwt = _mm_tn(dh, sv["xb"], tm=tb, tk=tn, name="w_in_grad", out_dtype=WIRE)
        dcur, parts["w_in"][l] = _in_proj_bwd(dh, wt_all[l], dr, tm=tm, ride=_Exchange([(_w_in_grad_parts_t(dwt), False)]))

        small["w_up"][l] = dw_up[:RANK]
        small["b_alpha"][l] = db_al.reshape(DK)
        small["gnorm"][l] = dgn.reshape(HEADS, HDV)
        small["pool_scale"][l] = dscale[0]
        small["b_merge"][l] = db_merge[0]
        small["ln_g"][l], small["ln_b"][l] = dln_g[0], dln_b[0]
    grad_x = dcur[None]
    sm = {k: jnp.stack(v) for k, v in small.items()}

    rep = (("b_alpha", b_alpha, m_b_alpha, v_b_alpha), ("pool_scale", pool_scale, m_pool_scale, v_pool_scale),
           ("b_merge", b_merge, m_b_merge, v_b_merge), ("ln_g", ln_g, m_ln_g, v_ln_g), ("ln_b", ln_b, m_ln_b, v_ln_b))
    cat = lambda arrs: jnp.concatenate(arrs, axis=1)
    p_up, p_gn, p_rep = _exchange([(_to_devices(sm["w_up"], 2), False), (_to_devices(sm["gnorm"], 2), False),
                                   (cat([sm[nm] for nm, _, _, _ in rep]), True)], name="exchange_small_grads")

    def update(p, w, m, v, tr, name, layered=True, tc=None):
        shape = w.shape
        nl = shape[0] if layered else 1
        cols = shape[-1]
        flat = lambda a: a.reshape(nl, -1, cols)
        outs = _adamw(p.reshape(nl, N_DEV, -1, cols), flat(w), flat(m), flat(v), tr=tr, tc=tc or cols, name=name)
        return [o_.reshape(shape) for o_ in outs]

    res = {}
    res["w_in"] = [tr3(o_) for o_ in update(jnp.stack(parts["w_in"]), tr3(w_in), tr3(m_w_in), tr3(v_w_in), SHARD, "adamw_w_in", tc=256)]
    proj_p = jnp.stack(parts["proj"])
    for j, (nm, w, m, v) in enumerate((("w_proj_a", w_proj_a, m_w_proj_a, v_w_proj_a), ("w_proj_b", w_proj_b, m_w_proj_b, v_w_proj_b),
                                       ("w_out", w_out, m_w_out, v_w_out))):
        res[nm] = update(proj_p[:, :, j], w, m, v, D // N_DEV, "adamw_" + nm)
    res["w_pool_grp"] = update(jnp.stack(parts["pool"]), w_pool_grp, m_w_pool_grp, v_w_pool_grp, 128, "adamw_w_pool")
    res["w_alpha_up"] = update(p_up, w_alpha_up, m_w_alpha_up, v_w_alpha_up, DEPTH * RANK, "adamw_w_up", layered=False)
    res["gla_norm_g"] = update(p_gn, gla_norm_g, m_gla_norm_g, v_gla_norm_g, DEPTH * HEADS, "adamw_gnorm", layered=False)
    rep_out = update(p_rep, cat([w for _, w, _, _ in rep]), cat([m for _, _, m, _ in rep]), cat([v for _, _, _, v in rep]),
                     DEPTH, "adamw_small", layered=False)
    off = 0
    for nm, w, _, _ in rep:
        n = w.shape[1]
        res[nm] = [o_[:, off:off + n] for o_ in rep_out]
        off += n

    order = ("w_in", "w_alpha_up", "b_alpha", "gla_norm_g", "w_pool_grp", "pool_scale", "b_merge", "w_proj_a", "w_proj_b",
             "w_out", "ln_g", "ln_b")
    return (loss, grad_x, *[res[n][0] for n in order], *[res[n][1] for n in order],
            *[res[n][2] for n in order], *[res[n][3] for n in order])
```

```python
import jax
import jax.numpy as jnp
from jax import lax
from jax.experimental import pallas as pl
from jax.experimental.pallas import tpu as pltpu

F32 = jnp.float32
MXU = jnp.bfloat16
WIRE = jnp.bfloat16

N_DEV = 8
DEPTH = 4
D = 1024
HEADS = 4
DK = D // 2
HDK = DK // HEADS
HDV = D // HEADS
RANK = 16
CHUNK = 64
GATE_TAU = 16.0
POOL_WINDOWS = (2, 4, 8, 16)
PG = D // len(POOL_WINDOWS)
HALO = 16
IN_COLS = 7184
SHARD = IN_COLS // N_DEV
ALPHA = (2.0 * DEPTH) ** 0.25
EPS = 1e-5
Q_SCALE = HDK ** -0.5

ADAM_LR, ADAM_B1, ADAM_B2, ADAM_EPS, ADAM_WD, ADAM_STEP = 0.001, 0.9, 0.999, 1e-08, 0.01, 10

PI0, GB0, ML0, AL0, AL_W = 0, D, 2 * D, 4 * D, 512
HD0 = AL0 + AL_W
HEAD_W = 2 * HDK + 2 * HDV
HP = HD0 + HEADS * HEAD_W
HPB = 2
O_Q, O_K, O_V, O_GA, O_AL, O_PI, O_GB, O_ML = 0, DK, 2 * DK, 2 * DK + D, 2 * DK + 2 * D, 2 * DK + 2 * D + RANK, \
    2 * DK + 3 * D + RANK, 2 * DK + 4 * D + RANK

VMEM_BIG = 56 * 1024 * 1024
VMEM_MID = 40 * 1024 * 1024

NN = ((1,), (0,))
NT = ((1,), (1,))
TN = ((0,), (0,))

HBM = pl.BlockSpec(memory_space=pltpu.HBM)
ANY = pl.BlockSpec(memory_space=pl.ANY)


def _dot(a, b, dims):
    return lax.dot_general(a.astype(MXU), b.astype(MXU), (dims, ((), ())), preferred_element_type=F32)


def _params(sem, vmem):
    return pltpu.CompilerParams(dimension_semantics=sem, vmem_limit_bytes=vmem)


def _sigmoid(x):
    return 1.0 / (1.0 + jnp.exp(-x))


def _log_sigmoid(z):
    return jnp.minimum(z, 0.0) - jnp.log(1.0 + jnp.exp(-jnp.abs(z)))


class _Exchange:
    def __init__(self, items):
        self.items = [(s, bool(g)) for s, g in items]
        self.n = len(self.items)
        self.srcs = [s for s, _ in self.items]
        self.in_specs = [HBM] * self.n
        self.out_specs = [HBM] * self.n
        self.out_shape = [jax.ShapeDtypeStruct((N_DEV,) + tuple(s.shape if g else s.shape[1:]), s.dtype) for s, g in self.items]
        self.scratch = [pltpu.SemaphoreType.DMA((self.n * (N_DEV - 1),)), pltpu.SemaphoreType.DMA((self.n * (N_DEV - 1),)),
                        pltpu.SemaphoreType.DMA((self.n,))]

    def copies(self, src_refs, out_refs, send_sems, recv_sems, local_sems):
        x, y, c = lax.axis_index("x"), lax.axis_index("y"), lax.axis_index("c")
        me = 4 * x + 2 * y + c
        copies = []
        for t, (_, gather) in enumerate(self.items):
            src_ref, out_ref = src_refs[t], out_refs[t]
            copies.append(pltpu.make_async_copy(src_ref if gather else src_ref.at[me], out_ref.at[me], local_sems.at[t]))
            for k in range(1, N_DEV):
                px = 1 - x if k & 4 else x
                py = 1 - y if k & 2 else y
                pc = 1 - c if k & 1 else c
                peer = 4 * px + 2 * py + pc
                sem = t * (N_DEV - 1) + k - 1
                copies.append(pltpu.make_async_remote_copy(
                    src_ref=src_ref if gather else src_ref.at[peer],
                    dst_ref=out_ref.at[me],
                    send_sem=send_sems.at[sem],
                    recv_sem=recv_sems.at[sem],
                    device_id=(px, py, pc),
                    device_id_type=pl.DeviceIdType.MESH,
                ))
        return copies


    mid_step = None

    def start(self, *refs):
        for cp in self.copies(*refs):
            cp.start()

    def finish(self, *refs):
        for cp in self.copies(*refs):
            cp.wait()


class _ChipGather:
    def __init__(self, src, mid_step):
        self.n = 1
        self.srcs = [src]
        self.mid_step = mid_step
        self.in_specs = [HBM]
        self.out_specs = [HBM]
        self.out_shape = [jax.ShapeDtypeStruct((N_DEV,) + tuple(src.shape), src.dtype)]
        self.scratch = [pltpu.SemaphoreType.DMA((N_DEV - 1,)), pltpu.SemaphoreType.DMA((N_DEV - 1,)), pltpu.SemaphoreType.DMA((1,))]

    def _plan(self, src_refs, out_refs, send_sems, recv_sems, local_sems):
        src_ref, out_ref = src_refs[0], out_refs[0]
        x, y, c = lax.axis_index("x"), lax.axis_index("y"), lax.axis_index("c")
        me, sibling = (x, y, c), (x, y, 1 - c)
        chips = [(1 - x, y), (x, 1 - y), (1 - x, 1 - y)]

        def copy(k, block, to, src=None):
            slot = out_ref.at[4 * block[0] + 2 * block[1] + block[2]]
            return pltpu.make_async_remote_copy(
                src_ref=slot if src is None else src, dst_ref=slot, send_sem=send_sems.at[k], recv_sem=recv_sems.at[k],
                device_id=to, device_id_type=pl.DeviceIdType.MESH)

        mine = pltpu.make_async_copy(src_ref, out_ref.at[4 * x + 2 * y + c], local_sems.at[0])
        first = [copy(0, me, sibling, src=src_ref)] + [copy(1 + j, me, (*chip, c), src=src_ref) for j, chip in enumerate(chips)]
        landed = [copy(1 + j, (*chip, c), me) for j, chip in enumerate(chips)]
        passed = [copy(4 + j, (*chip, c), sibling) for j, chip in enumerate(chips)]
        from_sibling = [copy(0, sibling, me)] + [copy(4 + j, (*chip, 1 - c), me) for j, chip in enumerate(chips)]
        return mine, first, landed, passed, from_sibling

    def start(self, *refs):
        mine, first, _, _, _ = self._plan(*refs)
        mine.start()
        for cp in first:
            cp.start()

    def mid(self, *refs):
        _, _, landed, passed, _ = self._plan(*refs)
        for got, fwd in zip(landed, passed):
            got.wait_recv()
            fwd.start()

    def finish(self, *refs):
        mine, first, _, passed, from_sibling = self._plan(*refs)
        for cp in from_sibling:
            cp.wait_recv()
        for cp in first + passed:
            cp.wait_send()
        mine.wait()


def _grid_ends(grid):
    first = last = step = None
    for a, n in enumerate(grid):
        f = pl.program_id(a) == 0
        e = pl.program_id(a) == n - 1
        first = f if first is None else first & f
        last = e if last is None else last & e
        step = pl.program_id(a) if step is None else step * n + pl.program_id(a)
    return first, last, step


def _call(body, *, name, grid, in_specs, out_specs, out_shape, args, scratch=(), sem=None, vmem=VMEM_MID, ride=None, aliases=None):
    n_in, n_out, n_scr = len(in_specs), len(out_specs), len(scratch)
    sem = sem or ("arbitrary",) * len(grid)
    if ride is None:
        return pl.pallas_call(body, name=name, grid=grid, in_specs=in_specs, out_specs=out_specs, out_shape=out_shape,
                              scratch_shapes=list(scratch), compiler_params=_params(sem, vmem),
                              input_output_aliases=aliases or {})(*args)
    r = ride.n

    def riding(*refs):
        ins, rsrc = refs[:n_in], refs[n_in:n_in + r]
        outs, rout = refs[n_in + r:n_in + r + n_out], refs[n_in + r + n_out:n_in + 2 * r + n_out]
        scr = refs[n_in + 2 * r + n_out:n_in + 2 * r + n_out + n_scr]
        send_sems, recv_sems, local_sems = refs[n_in + 2 * r + n_out + n_scr:]
        first, last, step = _grid_ends(grid)
        comm = (rsrc, rout, send_sems, recv_sems, local_sems)

        @pl.when(first)
        def _():
            ride.start(*comm)

        body(*ins, *outs, *scr)

        if ride.mid_step is not None:
            @pl.when(step == ride.mid_step)
            def _():
                ride.mid(*comm)

        @pl.when(last)
        def _():
            ride.finish(*comm)

    return pl.pallas_call(riding, name=name, grid=grid, in_specs=list(in_specs) + ride.in_specs,
                          out_specs=list(out_specs) + ride.out_specs, out_shape=list(out_shape) + ride.out_shape,
                          scratch_shapes=list(scratch) + ride.scratch,
                          compiler_params=_params(("arbitrary",) * len(grid), vmem),
                          input_output_aliases=aliases or {})(*args, *ride.srcs)


def _exchange(items, *, name):
    ex = _Exchange(items)

    def body(*refs):
        copies = ex.copies(refs[:ex.n], refs[ex.n:2 * ex.n], *refs[2 * ex.n:])
        for cp in copies:
            cp.start()
        for cp in copies:
            cp.wait()

    return pl.pallas_call(body, name=name, in_specs=ex.in_specs, out_specs=ex.out_specs, out_shape=ex.out_shape,
                          scratch_shapes=ex.scratch)(*ex.srcs)


def _gather_first(big, smalls, *, name):
    ex = _Exchange([(a, True) for a in smalls])

    def body(*refs):
        big_ref, small_src = refs[0], refs[1:1 + ex.n]
        out_ref, small_out = refs[1 + ex.n], refs[2 + ex.n:2 + 2 * ex.n]
        send_sems, recv_sems, local_sem = refs[2 + 2 * ex.n:5 + 2 * ex.n]
        x, y, c = lax.axis_index("x"), lax.axis_index("y"), lax.axis_index("c")
        me, sibling = (x, y, c), (x, y, 1 - c)
        chips = [(1 - x, y), (x, 1 - y), (1 - x, 1 - y)]

        def slot(px, py, pc):
            return out_ref.at[4 * px + 2 * py + pc]

        def copy(k, block, to, src=None):
            return pltpu.make_async_remote_copy(
                src_ref=slot(*block) if src is None else src, dst_ref=slot(*block),
                send_sem=send_sems.at[k], recv_sem=recv_sems.at[k], device_id=to, device_id_type=pl.DeviceIdType.MESH)

        small = ex.copies(small_src, small_out, *refs[5 + 2 * ex.n:])
        mine = pltpu.make_async_copy(big_ref, slot(*me), local_sem)
        mine.start()
        first = [copy(0, me, sibling, src=big_ref)] + [copy(1 + j, me, (*chip, c), src=big_ref) for j, chip in enumerate(chips)]
        for cp in first + small:
            cp.start()
        passed = [copy(4 + j, (*chip, c), sibling) for j, chip in enumerate(chips)]
        for j, chip in enumerate(chips):
            copy(1 + j, (*chip, c), me).wait_recv()
            passed[j].start()
        copy(0, sibling, me).wait_recv()
        for j, chip in enumerate(chips):
            copy(4 + j, (*chip, 1 - c), me).wait_recv()
        for cp in first + passed:
            cp.wait_send()
        mine.wait()
        for cp in small:
            cp.wait()

    return pl.pallas_call(
        body, name=name, in_specs=[HBM] + ex.in_specs, out_specs=[HBM] + ex.out_specs,
        out_shape=[jax.ShapeDtypeStruct((N_DEV,) + tuple(big.shape), big.dtype)] + ex.out_shape,
        scratch_shapes=[pltpu.SemaphoreType.DMA((N_DEV - 1,)), pltpu.SemaphoreType.DMA((N_DEV - 1,)), pltpu.SemaphoreType.DMA]
        + ex.scratch)(big, *ex.srcs)


def _move_rows(src, segs, out_rows, *, name, zero=None):
    cols = src.shape[1]
    step = 256

    def body(src_ref, out_ref):
        for s0, d0, n in segs:
            for r in range(0, n, step):
                m = min(step, n - r)
                out_ref[d0 + r:d0 + r + m, :] = src_ref[s0 + r:s0 + r + m, :]
        if zero is not None:
            out_ref[zero[0]:zero[0] + zero[1], :] = jnp.zeros((zero[1], cols), src.dtype)

    vmem = pl.BlockSpec(memory_space=pltpu.VMEM)
    return pl.pallas_call(
        body, name=name, in_specs=[vmem], out_specs=vmem, out_shape=jax.ShapeDtypeStruct((out_rows, cols), src.dtype),
        compiler_params=pltpu.CompilerParams(vmem_limit_bytes=VMEM_BIG))(src)


def _in_proj(xb, wt, *, tm, tn, ride=None):
    m, k = xb.shape
    n = wt.shape[0]

    def body(x_ref, w_ref, o_ref):
        o_ref[...] = _dot(x_ref[...], w_ref[...], NT)

    return _call(
        body, name="in_proj", grid=(n // tn, m // tm),
        in_specs=[pl.BlockSpec((tm, k), lambda j, i: (i, 0)), pl.BlockSpec((tn, k), lambda j, i: (j, 0))],
        out_specs=[pl.BlockSpec((tm, tn), lambda j, i: (i, j))],
        out_shape=[jax.ShapeDtypeStruct((m, n), F32)],
        args=(xb, wt), sem=("parallel", "parallel"), vmem=VMEM_BIG, ride=ride)


def _in_proj_bwd(dh, wt, dr, *, tm, ride=None):
    m, n = dh.shape
    k = wt.shape[1]

    def body(dh_ref, w_ref, dr_ref, o_ref):
        o_ref[...] = ALPHA * dr_ref[...] + _dot(dh_ref[...], w_ref[...], NN)

    return _call(
        body, name="in_proj_bwd", grid=(m // tm,),
        in_specs=[pl.BlockSpec((tm, n), lambda i: (i, 0)),
                  pl.BlockSpec((n, k), lambda i: (0, 0), pipeline_mode=pl.Buffered(1)),
                  pl.BlockSpec((tm, k), lambda i: (i, 0))],
        out_specs=[pl.BlockSpec((tm, k), lambda i: (i, 0))],
        out_shape=[jax.ShapeDtypeStruct((m, k), F32)],
        args=(dh, wt, dr), sem=("parallel",), vmem=VMEM_BIG, ride=ride)


def _mm_nt_into(dc, w, arr, col, *, tm, name):
    m, n = dc.shape
    k = w.shape[0]

    def body(dc_ref, w_ref, _arr_in, o_ref):
        o_ref[...] = _dot(dc_ref[...], w_ref[...], NT).astype(o_ref.dtype)

    return _call(
        body, name=name, grid=(m // tm,),
        in_specs=[pl.BlockSpec((tm, n), lambda i: (i, 0)), pl.BlockSpec((k, n), lambda i: (0, 0)), ANY],
        out_specs=[pl.BlockSpec((tm, k), lambda i: (i, col))],
        out_shape=[jax.ShapeDtypeStruct(arr.shape, arr.dtype)],
        args=(dc, w, arr), sem=("parallel",), aliases={2: 0})[0]


def _mm_tn(a, dc, *, tm, tk, name, a_block=None, out_dtype=F32):
    m = a.shape[0]
    k, a_col = (a.shape[1], None) if a_block is None else a_block
    n = dc.shape[1]
    ni = m // tm

    def body(a_ref, dc_ref, o_ref, acc):
        i = pl.program_id(1)

        @pl.when(i == 0)
        def _():
            acc[...] = jnp.zeros_like(acc)

        acc[...] += _dot(a_ref[...], dc_ref[...], TN)

        @pl.when(i == ni - 1)
        def _():
            o_ref[...] = acc[...].astype(o_ref.dtype)

    a_map = (lambda j, i: (i, j)) if a_col is None else (lambda j, i: (i, a_col))
    return _call(
        body, name=name, grid=(k // tk, ni),
        in_specs=[pl.BlockSpec((tm, tk), a_map), pl.BlockSpec((tm, n), lambda j, i: (i, 0))],
        out_specs=[pl.BlockSpec((tk, n), lambda j, i: (j, 0))],
        out_shape=[jax.ShapeDtypeStruct((k, n), out_dtype)],
        scratch=[pltpu.VMEM((tk, n), F32)],
        args=(a, dc), sem=("parallel", "arbitrary"), vmem=VMEM_BIG)[0]


def _head_cols(p):
    b = p * HEAD_W
    return (slice(b, b + HDK), slice(b + HDK, b + 2 * HDK), slice(b + 2 * HDK, b + 2 * HDK + HDV),
            slice(b + 2 * HDK + HDV, b + HEAD_W))


def _seg_cumsum(v, reverse=False):
    t, w = v.shape
    hi = v.astype(MXU)
    lo = (v - hi.astype(F32)).astype(MXU)
    terms = jnp.concatenate([hi, lo], axis=1)
    row = lax.broadcasted_iota(jnp.int32, (CHUNK, CHUNK), 0)
    col = lax.broadcasted_iota(jnp.int32, (CHUNK, CHUNK), 1)
    ones = jnp.where((row <= col) if reverse else (row >= col), 1.0, 0.0).astype(MXU)
    out = []
    for c in range(t // CHUNK):
        y = _dot(ones, terms[c * CHUNK:(c + 1) * CHUNK], NN)
        out.append(y[:, :w] + y[:, w:])
    return jnp.concatenate(out, axis=0)


def _seg_rcumsum_rolls(v):
    t = v.shape[0]
    rowmod = lax.broadcasted_iota(jnp.int32, v.shape, 0) % CHUNK
    sh = 1
    while sh < CHUNK:
        v = v + jnp.where(rowmod < CHUNK - sh, pltpu.roll(v, t - sh, 0), 0.0)
        sh *= 2
    return v


def _gla_decay(alpha_ref, wup_ref, b_ref, g_scr):
    z = _dot(alpha_ref[...], wup_ref[...], NN) + b_ref[...]
    g_scr[...] = _seg_cumsum(_log_sigmoid(z) * (1.0 / GATE_TAU))
    return z


QE1, KE1, QE2, KE2, QA, KD = range(6)
EP, EM, EA, EDL = range(4)
GW = HPB * HDK


def _gla_operands(hd_ref, g_scr, opnd_scr, fac_scr=None):
    t = g_scr.shape[0]

    def chunk_row(r):
        return jnp.concatenate([jnp.broadcast_to(g_scr[c * CHUNK + r:c * CHUNK + r + 1, :], (CHUNK, GW))
                                for c in range(t // CHUNK)], axis=0)

    g = g_scr[...]
    g_last = chunk_row(CHUNK - 1)
    ref = 0.5 * (chunk_row(0) + g_last)
    fac = {EP: jnp.exp(g - ref), EM: jnp.exp(ref - g), EA: jnp.exp(g), EDL: jnp.exp(g_last - g)}
    if fac_scr is not None:
        for j, f in fac.items():
            fac_scr[j] = f
    for p in range(HPB):
        qc, kc, _, _ = _head_cols(p)
        gc = slice(p * HDK, (p + 1) * HDK)
        qs = hd_ref[:, qc] * Q_SCALE
        k = hd_ref[:, kc]
        for j, (x, f) in {QE1: (qs, EP), KE1: (k, EM), QE2: (qs, EM), KE2: (k, EP), QA: (qs, EA), KD: (k, EDL)}.items():
            opnd_scr[j, :, gc] = (x * fac[f][:, gc]).astype(opnd_scr.dtype)


def _lower_mask():
    return lax.broadcasted_iota(jnp.int32, (CHUNK, CHUNK), 0) >= lax.broadcasted_iota(jnp.int32, (CHUNK, CHUNK), 1)


def _scores(opnd_scr, rows, gc, lower):
    return jnp.where(lower, _dot(opnd_scr[QE1, rows, gc], opnd_scr[KE1, rows, gc], NT),
                     _dot(opnd_scr[QE2, rows, gc], opnd_scr[KE2, rows, gc], NT))


def _gla_specs(tt, row):
    return [
        pl.BlockSpec((tt, HPB * HEAD_W), lambda h, i: (row(i), HD0 // (HPB * HEAD_W) + h)),
        pl.BlockSpec((tt, 128), lambda h, i: (row(i), AL0 // 128)),
        pl.BlockSpec((128, HPB * HDK), lambda h, i: (0, h)),
        pl.BlockSpec((1, HPB * HDK), lambda h, i: (0, h)),
        pl.BlockSpec((1, HPB * HDV), lambda h, i: (0, h)),
    ]


def _gla_fwd(hh, wup, b_alpha, gnorm, *, tt, ride=None):
    s = hh.shape[0]
    nt = s // tt
    nct = tt // CHUNK

    def body(hd_ref, al_ref, wup_ref, b_ref, gn_ref, o_ref, ya_ref, st_ref, state, g_scr, opnd_scr):
        @pl.when(pl.program_id(1) == 0)
        def _():
            state[...] = jnp.zeros_like(state)

        _gla_decay(al_ref, wup_ref, b_ref, g_scr)
        _gla_operands(hd_ref, g_scr, opnd_scr)
        lower = _lower_mask()
        for c in range(nct):
            rows = slice(c * CHUNK, (c + 1) * CHUNK)
            for p in range(HPB):
                vc = _head_cols(p)[2]
                gc = slice(p * HDK, (p + 1) * HDK)
                v = hd_ref[rows, vc]
                st = state[p]
                st_ref[p, c] = st
                egl = jnp.exp(g_scr[(c + 1) * CHUNK - 1:(c + 1) * CHUNK, gc])
                o_ref[rows, p * HDV:(p + 1) * HDV] = (_dot(_scores(opnd_scr, rows, gc, lower), v, NN)
                                                      + _dot(opnd_scr[QA, rows, gc], st, NT))
                state[p] = st * egl + _dot(v, opnd_scr[KD, rows, gc], TN)
        for p in range(HPB):
            oc = slice(p * HDV, (p + 1) * HDV)
            o = o_ref[:, oc]
            ohat = o * lax.rsqrt(jnp.mean(o * o, axis=-1, keepdims=True) + EPS)
            ga = hd_ref[:, _head_cols(p)[3]]
            ya_ref[:, oc] = (ohat * gn_ref[:, oc] * (ga * _sigmoid(ga))).astype(ya_ref.dtype)

    return _call(
        body, name="gla_fwd", grid=(HEADS // HPB, nt),
        in_specs=_gla_specs(tt, lambda i: i),
        out_specs=[
            pl.BlockSpec((tt, HPB * HDV), lambda h, i: (i, h)),
            pl.BlockSpec((tt, HPB * HDV), lambda h, i: (i, h)),
            pl.BlockSpec((HPB, nct, HDV, HDK), lambda h, i: (h, i, 0, 0)),
        ],
        out_shape=[
            jax.ShapeDtypeStruct((s, D), F32),
            jax.ShapeDtypeStruct((s, D), MXU),
            jax.ShapeDtypeStruct((HEADS, s // CHUNK, HDV, HDK), F32),
        ],
        scratch=[pltpu.VMEM((HPB, HDV, HDK), F32), pltpu.VMEM((tt, GW), F32), pltpu.VMEM((6, tt, GW), MXU)],
        args=(hh, hh, wup, b_alpha, gnorm), ride=ride)


def _gla_bwd(hh, wup, b_alpha, gnorm, o, states, dya, dh, *, tt, ride=None):
    s = hh.shape[0]
    nt = s // tt
    nct = tt // CHUNK

    def body(hd_ref, al_ref, wup_ref, b_ref, gn_ref, o_ref, st_ref, dya_ref, _dh_in,
             dh_ref, dz_ref, dgn_ref, db_ref, dstate, g_scr, dg_scr, do_scr, opnd_scr, fac_scr, res_scr, dgl_scr):
        @pl.when(pl.program_id(1) == 0)
        def _():
            dstate[...] = jnp.zeros_like(dstate)
            dgn_ref[...] = jnp.zeros_like(dgn_ref)
            db_ref[...] = jnp.zeros_like(db_ref)

        z = _gla_decay(al_ref, wup_ref, b_ref, g_scr)
        _gla_operands(hd_ref, g_scr, opnd_scr, fac_scr)

        for p in range(HPB):
            oc = slice(p * HDV, (p + 1) * HDV)
            gac = _head_cols(p)[3]
            o_t = o_ref[:, oc]
            rstd = lax.rsqrt(jnp.mean(o_t * o_t, axis=-1, keepdims=True) + EPS)
            ohat = o_t * rstd
            ga = hd_ref[:, gac]
            sg = _sigmoid(ga)
            dya_t = dya_ref[:, oc]
            gn = gn_ref[:, oc]
            dh_ref[:, gac] = (dya_t * ohat * gn * (sg * (1.0 + ga * (1.0 - sg)))).astype(dh_ref.dtype)
            don = dya_t * (ga * sg)
            dgn_ref[p] += jnp.sum(don * ohat, axis=0, keepdims=True)
            dohat = don * gn
            do_scr[:, oc] = rstd * (dohat - ohat * jnp.mean(dohat * ohat, axis=-1, keepdims=True))

        lower = _lower_mask()
        for c in range(nct - 1, -1, -1):
            rows = slice(c * CHUNK, (c + 1) * CHUNK)
            for p in range(HPB):
                vc = _head_cols(p)[2]
                gc = slice(p * HDK, (p + 1) * HDK)
                v = hd_ref[rows, vc]
                do = do_scr[rows, p * HDV:(p + 1) * HDV]
                st = st_ref[p, c]
                dst = dstate[p]
                egl = jnp.exp(g_scr[(c + 1) * CHUNK - 1:(c + 1) * CHUNK, gc])
                a = _scores(opnd_scr, rows, gc, lower)
                da = _dot(do, v, NT)
                da1 = jnp.where(lower, da, 0.0).astype(MXU)
                da2 = jnp.where(lower, 0.0, da).astype(MXU)
                res_scr[0, rows, gc] = _dot(da1, opnd_scr[KE1, rows, gc], NN)
                res_scr[1, rows, gc] = _dot(da1, opnd_scr[QE1, rows, gc], TN)
                res_scr[2, rows, gc] = _dot(da2, opnd_scr[KE2, rows, gc], NN)
                res_scr[3, rows, gc] = _dot(da2, opnd_scr[QE2, rows, gc], TN)
                res_scr[4, rows, gc] = _dot(do, st, NN)
                res_scr[5, rows, gc] = _dot(v, dst, NN)
                dh_ref[rows, vc] = (_dot(a, do, TN) + _dot(opnd_scr[KD, rows, gc], dst, NT)).astype(dh_ref.dtype)
                dgl_scr[c:c + 1, gc] = egl * jnp.sum(dst * st, axis=0, keepdims=True)
                dstate[p] = dst * egl + _dot(do, opnd_scr[QA, rows, gc], TN)

        p1, p2, p3 = res_scr[0] * fac_scr[EP], res_scr[2] * fac_scr[EM], res_scr[4] * fac_scr[EA]
        r1, r2, r3 = res_scr[1] * fac_scr[EM], res_scr[3] * fac_scr[EP], res_scr[5] * fac_scr[EDL]
        dq = (p1 + p2 + p3) * Q_SCALE
        dk = r1 + r2 + r3
        dgq = p1 - p2 + p3
        dgk = r2 - r1
        for p in range(HPB):
            qc, kc, _, _ = _head_cols(p)
            gc = slice(p * HDK, (p + 1) * HDK)
            dh_ref[:, qc] = dq[:, gc].astype(dh_ref.dtype)
            dh_ref[:, kc] = dk[:, gc].astype(dh_ref.dtype)
            k = hd_ref[:, kc]
            r3k = r3[:, gc] * k
            dg_scr[:, gc] = (hd_ref[:, qc] * Q_SCALE) * dgq[:, gc] + k * dgk[:, gc] - r3k
            for c in range(nct):
                last = slice((c + 1) * CHUNK - 1, (c + 1) * CHUNK)
                dg_scr[last, gc] += jnp.sum(r3k[c * CHUNK:(c + 1) * CHUNK], axis=0, keepdims=True) + dgl_scr[c:c + 1, gc]

        dz = _seg_rcumsum_rolls(dg_scr[...]) * _sigmoid(-z) * (1.0 / GATE_TAU)
        dz_ref[...] = dz.astype(dz_ref.dtype)
        for p in range(HPB):
            db_ref[p] += jnp.sum(dz[:, p * HDK:(p + 1) * HDK], axis=0, keepdims=True)

    rev = lambda i: nt - 1 - i
    in_specs = _gla_specs(tt, rev) + [
        pl.BlockSpec((tt, HPB * HDV), lambda h, i: (rev(i), h)),
        pl.BlockSpec((HPB, nct, HDV, HDK), lambda h, i: (h, rev(i), 0, 0)),
        pl.BlockSpec((tt, HPB * HDV), lambda h, i: (rev(i), h)),
        ANY,
    ]
    return _call(
        body, name="gla_bwd", grid=(HEADS // HPB, nt), in_specs=in_specs,
        out_specs=[
            pl.BlockSpec((tt, HPB * HEAD_W), lambda h, i: (rev(i), HD0 // (HPB * HEAD_W) + h)),
            pl.BlockSpec((tt, HPB * HDK), lambda h, i: (rev(i), h)),
            pl.BlockSpec((HPB, 1, HDV), lambda h, i: (h, 0, 0)),
            pl.BlockSpec((HPB, 1, HDK), lambda h, i: (h, 0, 0)),
        ],
        out_shape=[
            jax.ShapeDtypeStruct(dh.shape, dh.dtype),
            jax.ShapeDtypeStruct((s, DK), MXU),
            jax.ShapeDtypeStruct((HEADS, 1, HDV), F32),
            jax.ShapeDtypeStruct((HEADS, 1, HDK), F32),
        ],
        scratch=[pltpu.VMEM((HPB, HDV, HDK), F32), pltpu.VMEM((tt, GW), F32), pltpu.VMEM((tt, GW), F32),
                 pltpu.VMEM((tt, HPB * HDV), F32), pltpu.VMEM((6, tt, GW), MXU), pltpu.VMEM((4, tt, GW), F32),
                 pltpu.VMEM((6, tt, GW), F32), pltpu.VMEM((max(nct, 8), GW), F32)],
        args=(hh, hh, wup, b_alpha, gnorm, o, states, dya, dh), ride=ride, aliases={8: 0})


def _window_count(tile, tt, w):
    pos = tile * tt + lax.broadcasted_iota(jnp.int32, (tt, PG), 0) + 1
    return jnp.minimum(pos, w).astype(F32)


def _pool_fwd(hh, wpool, scale, *, tt):
    s = hh.shape[0]
    nt = s // tt

    def body(ug_ref, w_ref, sc_ref, pooled_ref, yb_ref, halo):
        i = pl.program_id(0)

        @pl.when(i == 0)
        def _():
            halo[...] = jnp.zeros_like(halo)

        for g, w in enumerate(POOL_WINDOWS):
            cols = slice(g * PG, (g + 1) * PG)
            u = ug_ref[:, cols]
            run = jnp.concatenate([halo[:, cols], u], axis=0)
            sh = 1
            while sh < w:
                run = run + pltpu.roll(run, sh, 0)
                sh *= 2
            pooled = run[HALO:, :] / _window_count(i, tt, w) - u
            pooled_ref[:, cols] = pooled.astype(pooled_ref.dtype)
            mixed = _dot(pooled, w_ref[g], NN)
            gb = ug_ref[:, slice(D + g * PG, D + (g + 1) * PG)]
            yb_ref[:, cols] = (mixed * sc_ref[:, cols] * (gb * _sigmoid(gb))).astype(yb_ref.dtype)
        halo[...] = ug_ref[tt - HALO:tt, :D]

    tile = pl.BlockSpec((tt, D), lambda i: (i, 0))
    return _call(
        body, name="pool_fwd", grid=(nt,),
        in_specs=[
            pl.BlockSpec((tt, 2 * D), lambda i: (i, PI0 // (2 * D))),
            pl.BlockSpec((len(POOL_WINDOWS), PG, PG), lambda i: (0, 0, 0)),
            pl.BlockSpec((1, D), lambda i: (0, 0)),
        ],
        out_specs=[tile] * 2,
        out_shape=[jax.ShapeDtypeStruct((s, D), MXU), jax.ShapeDtypeStruct((s, D), MXU)],
        scratch=[pltpu.VMEM((HALO, D), F32)], args=(hh, wpool, scale))


def _pool_bwd(hh, wpool, scale, pooled, dyb, dh, *, tt):
    s = hh.shape[0]
    nt = s // tt

    def body(gb_ref, w_ref, sc_ref, pooled_ref, dyb_ref, _dh_in, dh_ref, dw_ref, dsc_ref, halo):
        i = pl.program_id(0)
        tile = nt - 1 - i

        @pl.when(i == 0)
        def _():
            halo[...] = jnp.zeros_like(halo)
            dw_ref[...] = jnp.zeros_like(dw_ref)
            dsc_ref[...] = jnp.zeros_like(dsc_ref)

        for g, w in enumerate(POOL_WINDOWS):
            cols = slice(g * PG, (g + 1) * PG)
            gcols = slice(D + g * PG, D + (g + 1) * PG)
            gb = gb_ref[:, cols]
            sg = _sigmoid(gb)
            pooled = pooled_ref[:, cols]
            mixed = _dot(pooled, w_ref[g], NN)
            sc = sc_ref[:, cols]
            dyb = dyb_ref[:, cols]
            dh_ref[:, gcols] = (dyb * mixed * sc * (sg * (1.0 + gb * (1.0 - sg)))).astype(dh_ref.dtype)
            dms = dyb * (gb * sg)
            dsc_ref[:, cols] += jnp.sum(dms * mixed, axis=0, keepdims=True)
            dmixed = dms * sc
            dpooled = _dot(dmixed, w_ref[g], NT)
            dw_ref[g] += _dot(pooled, dmixed, TN)
            e = dpooled / _window_count(tile, tt, w)
            run = jnp.concatenate([e, halo[:, cols]], axis=0)
            sh = 1
            while sh < w:
                run = run + pltpu.roll(run, tt + HALO - sh, 0)
                sh *= 2
            dh_ref[:, cols] = (run[:tt, :] - dpooled).astype(dh_ref.dtype)
            halo[:, cols] = e[:HALO, :]

    rev = lambda i: nt - 1 - i
    tile = pl.BlockSpec((tt, D), lambda i: (rev(i), 0))
    wspec = pl.BlockSpec((len(POOL_WINDOWS), PG, PG), lambda i: (0, 0, 0))
    vec = pl.BlockSpec((1, D), lambda i: (0, 0))
    return _call(
        body, name="pool_bwd", grid=(nt,),
        in_specs=[pl.BlockSpec((tt, D), lambda i: (rev(i), GB0 // D)), wspec, vec, tile, tile, ANY],
        out_specs=[pl.BlockSpec((tt, 2 * D), lambda i: (rev(i), PI0 // (2 * D))), wspec, vec],
        out_shape=[jax.ShapeDtypeStruct(dh.shape, dh.dtype), jax.ShapeDtypeStruct((len(POOL_WINDOWS), PG, PG), F32),
                   jax.ShapeDtypeStruct((1, D), F32)],
        scratch=[pltpu.VMEM((HALO, D), F32)], args=(hh, wpool, scale, pooled, dyb, dh), aliases={5: 0})


def _merge_fwd(hh, x, ya, yb, wpa, wpb, wout, b_merge, ln_g, ln_b, *, tt):
    s = x.shape[0]

    def body(ml_ref, x_ref, ya_ref, yb_ref, wpa_ref, wpb_ref, wout_ref, bm_ref, g_ref, b_ref, r_ref, xn_ref, xnb_ref):
        pa = _dot(ya_ref[...], wpa_ref[...], NN)
        pb = _dot(yb_ref[...], wpb_ref[...], NN)
        merged = _sigmoid(ml_ref[:, :D] + bm_ref[:, :D]) * pa + _sigmoid(ml_ref[:, D:] + bm_ref[:, D:]) * pb
        r = ALPHA * x_ref[...] + _dot(merged, wout_ref[...], NN)
        r_ref[...] = r
        mu = jnp.mean(r, axis=-1, keepdims=True)
        xc = r - mu
        var = jnp.mean(xc * xc, axis=-1, keepdims=True)
        xn = xc * lax.rsqrt(var + EPS) * g_ref[...] + b_ref[...]
        xn_ref[...] = xn
        xnb_ref[...] = xn.astype(xnb_ref.dtype)

    tile = pl.BlockSpec((tt, D), lambda i: (i, 0))
    full = pl.BlockSpec((D, D), lambda i: (0, 0), pipeline_mode=pl.Buffered(1))
    vec = pl.BlockSpec((1, D), lambda i: (0, 0))
    return _call(
        body, name="merge_fwd", grid=(s // tt,),
        in_specs=[pl.BlockSpec((tt, 2 * D), lambda i: (i, ML0 // (2 * D))), tile, tile, tile, full, full, full,
                  pl.BlockSpec((1, 2 * D), lambda i: (0, 0)), vec, vec],
        out_specs=[tile] * 3, out_shape=[jax.ShapeDtypeStruct((s, D), F32)] * 2 + [jax.ShapeDtypeStruct((s, D), MXU)],
        args=(hh, x, ya, yb, wpa, wpb, wout, b_merge, ln_g, ln_b), sem=("parallel",), vmem=VMEM_BIG)


def _merge_bwd(hh, r, ya, yb, dout, wpa, wpb, wout, b_merge, ln_g, *, tt):
    s = r.shape[0]

    def body(ml_ref, r_ref, ya_ref, yb_ref, do_ref, wpa_ref, wpb_ref, wout_ref, bm_ref, g_ref,
             dh_ref, dr_ref, dpa_ref, dpb_ref, dwout_ref, dg_ref, db_ref, dbm_ref):
        @pl.when(pl.program_id(0) == 0)
        def _():
            dwout_ref[...] = jnp.zeros_like(dwout_ref)
            dg_ref[...] = jnp.zeros_like(dg_ref)
            db_ref[...] = jnp.zeros_like(db_ref)
            dbm_ref[...] = jnp.zeros_like(dbm_ref)

        rr = r_ref[...]
        mu = jnp.mean(rr, axis=-1, keepdims=True)
        xc = rr - mu
        rstd = lax.rsqrt(jnp.mean(xc * xc, axis=-1, keepdims=True) + EPS)
        xhat = xc * rstd
        do = do_ref[...]
        dg_ref[...] += jnp.sum(do * xhat, axis=0, keepdims=True)
        db_ref[...] += jnp.sum(do, axis=0, keepdims=True)
        dxh = do * g_ref[...]
        dr = rstd * (dxh - jnp.mean(dxh, axis=-1, keepdims=True) - xhat * jnp.mean(dxh * xhat, axis=-1, keepdims=True))
        dr_ref[...] = dr
        g_a = _sigmoid(ml_ref[:, :D] + bm_ref[:, :D])
        g_b = _sigmoid(ml_ref[:, D:] + bm_ref[:, D:])
        pa = _dot(ya_ref[...], wpa_ref[...], NN)
        pb = _dot(yb_ref[...], wpb_ref[...], NN)
        dwout_ref[...] += _dot(g_a * pa + g_b * pb, dr, TN)
        dm = _dot(dr, wout_ref[...], NT)
        dpa_ref[...] = (dm * g_a).astype(dpa_ref.dtype)
        dpb_ref[...] = (dm * g_b).astype(dpb_ref.dtype)
        dml_a = dm * pa * g_a * (1.0 - g_a)
        dml_b = dm * pb * g_b * (1.0 - g_b)
        dh_ref[:, :D] = dml_a.astype(dh_ref.dtype)
        dh_ref[:, D:] = dml_b.astype(dh_ref.dtype)
        dbm_ref[:, :D] += jnp.sum(dml_a, axis=0, keepdims=True)
        dbm_ref[:, D:] += jnp.sum(dml_b, axis=0, keepdims=True)

    tile = pl.BlockSpec((tt, D), lambda i: (i, 0))
    full = pl.BlockSpec((D, D), lambda i: (0, 0))
    vec = pl.BlockSpec((1, D), lambda i: (0, 0))
    vec2 = pl.BlockSpec((1, 2 * D), lambda i: (0, 0))
    mlb = pl.BlockSpec((tt, 2 * D), lambda i: (i, ML0 // (2 * D)))
    return _call(
        body, name="merge_bwd", grid=(s // tt,),
        in_specs=[mlb, tile, tile, tile, tile, full, full, full, vec2, vec],
        out_specs=[mlb, tile, tile, tile, full, vec, vec, vec2],
        out_shape=[
            jax.ShapeDtypeStruct((s, HP), MXU), jax.ShapeDtypeStruct((s, D), F32),
            jax.ShapeDtypeStruct((s, D), MXU), jax.ShapeDtypeStruct((s, D), MXU),
            jax.ShapeDtypeStruct((D, D), F32), jax.ShapeDtypeStruct((1, D), F32),
            jax.ShapeDtypeStruct((1, D), F32), jax.ShapeDtypeStruct((1, 2 * D), F32),
        ],
        args=(hh, r, ya, yb, dout, wpa, wpb, wout, b_merge, ln_g), vmem=VMEM_BIG)


def _proj_bwd(y, dp, w, *, tt, name):
    s = y.shape[0]

    def body(y_ref, dp_ref, w_ref, dy_ref, dw_ref):
        @pl.when(pl.program_id(0) == 0)
        def _():
            dw_ref[...] = jnp.zeros_like(dw_ref)

        dp = dp_ref[...]
        dy_ref[...] = _dot(dp, w_ref[...], NT)
        dw_ref[...] += _dot(y_ref[...], dp, TN)

    tile = pl.BlockSpec((tt, D), lambda i: (i, 0))
    full = pl.BlockSpec((D, D), lambda i: (0, 0))
    return _call(
        body, name=name, grid=(s // tt,), in_specs=[tile, tile, full], out_specs=[tile, full],
        out_shape=[jax.ShapeDtypeStruct((s, D), F32), jax.ShapeDtypeStruct((D, D), F32)], args=(y, dp, w))


def _loss_head(y, target, *, tt):
    s = y.shape[0]

    def body(y_ref, t_ref, loss_ref, dy_ref):
        @pl.when(pl.program_id(0) == 0)
        def _():
            loss_ref[...] = jnp.zeros_like(loss_ref)

        err = y_ref[...] - t_ref[...]
        dy_ref[...] = err * (1.0 / D)
        per_tok = jnp.mean(err * err, axis=-1, keepdims=True)
        loss_ref[...] += 0.5 * jnp.sum(per_tok, axis=0, keepdims=True)

    tile = pl.BlockSpec((tt, D), lambda i: (i, 0))
    return _call(
        body, name="loss_head", grid=(s // tt,), in_specs=[tile, tile],
        out_specs=[pl.BlockSpec((1, 1), lambda i: (0, 0)), tile],
        out_shape=[jax.ShapeDtypeStruct((1, 1), F32), jax.ShapeDtypeStruct((s, D), F32)], args=(y, target))


def _adamw_math(share, w_ref, m_ref, v_ref, g_ref, d_ref, nm_ref, nv_ref):
    g = share(0).astype(F32)
    for q in range(1, N_DEV):
        g = g + share(q).astype(F32)
    g_ref[0] = g
    nm = ADAM_B1 * m_ref[0] + (1.0 - ADAM_B1) * g
    nv = ADAM_B2 * v_ref[0] + (1.0 - ADAM_B2) * (g * g)
    nm_ref[0] = nm
    nv_ref[0] = nv
    m_hat = nm / (1.0 - ADAM_B1 ** ADAM_STEP)
    v_hat = nv / (1.0 - ADAM_B2 ** ADAM_STEP)
    d_ref[0] = -ADAM_LR * (m_hat / (jnp.sqrt(v_hat) + ADAM_EPS) + ADAM_WD * w_ref[0])


def _adamw_layers(parts, w, m, v, *, tc, name):
    nl, rows, cols = w.shape
    nc = cols // tc

    def body(*refs):
        p_refs, rest = refs[:nl], refs[nl:]
        for j in range(nl):
            @pl.when(pl.program_id(0) == j)
            def _(p_ref=p_refs[j]):
                _adamw_math(lambda q: p_ref[q], *rest)

    def part_spec(j):
        return pl.BlockSpec((N_DEV, rows, tc), lambda l, c: (0, 0, jnp.where(l == j, c, jnp.where(l < j, 0, nc - 1))))

    tile = pl.BlockSpec((1, rows, tc), lambda l, c: (l, 0, c))
    return _call(
        body, name=name, grid=(nl, nc),
        in_specs=[part_spec(j) for j in range(nl)] + [tile, tile, tile],
        out_specs=[tile] * 4, out_shape=[jax.ShapeDtypeStruct((nl, rows, cols), F32)] * 4,
        args=(*parts, w, m, v))


def _adamw(parts, w, m, v, *, tr, tc, name):
    nl, rows, cols = w.shape

    def body(p_ref, *rest):
        _adamw_math(lambda q: p_ref[0, q], *rest)

    tile = pl.BlockSpec((1, tr, tc), lambda l, i, j: (l, i, j))
    return _call(
        body, name=name, grid=(nl, rows // tr, cols // tc),
        in_specs=[pl.BlockSpec((1, N_DEV, tr, tc), lambda l, i, j: (l, 0, i, j)), tile, tile, tile],
        out_specs=[tile] * 4, out_shape=[jax.ShapeDtypeStruct((nl, rows, cols), F32)] * 4,
        args=(parts, w, m, v), sem=("parallel", "parallel", "parallel"))


def _from_devices(g, axis):
    nd = g.ndim - 1
    perm = list(range(1, axis + 1)) + [0] + list(range(axis + 1, nd + 1))
    shape = list(g.shape[1:])
    shape[axis] *= N_DEV
    return jnp.transpose(g, perm).reshape(shape)


def _to_devices(a, axis):
    shape = list(a.shape)
    t = a.reshape(shape[:axis] + [N_DEV, shape[axis] // N_DEV] + shape[axis + 1:])
    return jnp.transpose(t, [axis] + list(range(0, axis)) + list(range(axis + 1, t.ndim)))


def _h_row_segments():
    segs = [(O_PI, PI0, IN_COLS - O_PI), (O_AL, AL0, RANK)]
    for h in range(HEADS):
        base = HD0 + h * HEAD_W
        segs += [(O_Q + h * HDK, base, HDK), (O_K + h * HDK, base + HDK, HDK),
                 (O_V + h * HDV, base + 2 * HDK, HDV), (O_GA + h * HDV, base + 2 * HDK + HDV, HDV)]
    return segs


def _h_weight_t(parts):
    return _move_rows(parts.reshape(IN_COLS, D), _h_row_segments(), HP, name="w_in_rows", zero=(AL0 + RANK, AL_W - RANK))


def _w_in_grad_parts_t(dwt):
    g = _move_rows(dwt, [(d0, s0, n) for s0, d0, n in _h_row_segments()], IN_COLS, name="w_in_grad_rows")
    return g.reshape(N_DEV, SHARD, D)


def kernel(x, w_in, w_alpha_up, b_alpha, gla_norm_g, w_pool_grp, pool_scale, b_merge, w_proj_a, w_proj_b, w_out, ln_g, ln_b, loss_target, m_w_in, m_w_alpha_up, m_b_alpha, m_gla_norm_g, m_w_pool_grp, m_pool_scale, m_b_merge, m_w_proj_a, m_w_proj_b, m_w_out, m_ln_g, m_ln_b, v_w_in, v_w_alpha_up, v_b_alpha, v_gla_norm_g, v_w_pool_grp, v_pool_scale, v_b_merge, v_w_proj_a, v_w_proj_b, v_w_out, v_ln_g, v_ln_b):
    s = x.shape[1]
    tt = min(256, s)
    tm = min(512, s)
    tb = min(1024, s)
    tn = HP // 3
    xs = x.reshape(s, D)

    tr3 = lambda a: jnp.transpose(a, (0, 2, 1))
    w_in_s = tr3(w_in).astype(WIRE)
    proj_s = jnp.stack([w_proj_a, w_proj_b, w_out], axis=1).astype(WIRE)
    pool_s = w_pool_grp.astype(WIRE)

    g_in, g_up, g_gn = _gather_first(w_in_s[0], [w_alpha_up.astype(WIRE), gla_norm_g], name="gather_first")
    wup = jnp.pad(_from_devices(g_up, 2), ((0, 0), (0, AL_W - RANK), (0, 0)))
    gn = _from_devices(g_gn, 2).reshape(DEPTH, 1, D)

    saved, wt_all, proj_all, pool_all = [], [], [], []
    cur, cur_b = xs, xs.astype(MXU)
    g_proj = g_pool = None
    for l in range(DEPTH):
        wt = _h_weight_t(g_in)
        nxt_l = l + 1 < DEPTH
        steps = (HP // tn) * (s // tb)
        res = _in_proj(cur_b, wt, tm=tb, tn=tn, ride=_ChipGather(w_in_s[l + 1], (2 * steps) // 3) if nxt_l else None)
        hh = res[0]
        if nxt_l:
            g_in = res[1]
        layers = ([0] if l == 0 else []) + ([l + 1] if nxt_l else [])
        res = _gla_fwd(hh, wup[l], b_alpha[l:l + 1], gn[l], tt=tm,
                       ride=_Exchange([(a[j], True) for j in layers for a in (proj_s, pool_s)]) if layers else None)
        o, ya, states = res[:3]
        got = {j: res[3 + 2 * t:5 + 2 * t] for t, j in enumerate(layers)}
        if l == 0:
            g_proj, g_pool = got[0]
        proj = _from_devices(g_proj, 1)
        pool = _from_devices(g_pool, 1)
        if nxt_l:
            g_proj, g_pool = got[l + 1]
        wt_all.append(wt), proj_all.append(proj), pool_all.append(pool)
        pooled, yb = _pool_fwd(hh, pool, pool_scale[l:l + 1], tt=tm)
        r, nxt, nxt_b = _merge_fwd(hh, cur, ya, yb, proj[0], proj[1], proj[2],
                                   b_merge[l:l + 1], ln_g[l:l + 1], ln_b[l:l + 1], tt=tm)
        saved.append(dict(xb=cur_b, hh=hh, o=o, ya=ya, states=states, pooled=pooled, yb=yb, r=r))
        cur, cur_b = nxt, nxt_b

    loss_part, dcur = _loss_head(cur, loss_target.reshape(s, D), tt=tm)
    loss = lax.psum(loss_part[0, 0], ("x", "y", "c"))

    small = {k: [None] * DEPTH for k in ("w_up", "b_alpha", "gnorm", "pool_scale", "b_merge", "ln_g", "ln_b")}
    parts = {k: [None] * DEPTH for k in ("w_in", "proj", "pool")}
    for l in range(DEPTH - 1, -1, -1):
        sv = saved[l]
        hh = sv["hh"]
        dh, dr, dpa, dpb, dw_out, dln_g, dln_b, db_merge = _merge_bwd(
            hh, sv["r"], sv["ya"], sv["yb"], dcur, proj_all[l][0], proj_all[l][1], proj_all[l][2], b_merge[l:l + 1], ln_g[l:l + 1], tt=tt)
        dya, dw_pa = _proj_bwd(sv["ya"], dpa, proj_all[l][0], tt=tm, name="proj_a_bwd")
        dyb, dw_pb = _proj_bwd(sv["yb"], dpb, proj_all[l][1], tt=tm, name="proj_b_bwd")
        dh, dw_pool, dscale = _pool_bwd(hh, pool_all[l], pool_scale[l:l + 1], sv["pooled"], dyb, dh, tt=tm)
        ride = _Exchange([(_to_devices(jnp.stack([dw_pa, dw_pb, dw_out]), 1).astype(WIRE), False),
                          (_to_devices(dw_pool, 1).astype(WIRE), False)])
        dh, dz, dgn, db_al, parts["proj"][l], parts["pool"][l] = _gla_bwd(
            hh, wup[l], b_alpha[l:l + 1], gn[l], sv["o"], sv["states"], dya, dh, tt=tm, ride=ride)
        dh = _mm_nt_into(dz, wup[l], dh, AL0 // AL_W, tm=tm, name="alpha_bwd")
        dw_up = _mm_tn(hh, dz, tm=tb, tk=128, name="w_up_grad", a_block=(128, AL0 // 128))
        dwt = _mm_tn(dh, sv["xb"], tm=tb, tk=tn, name="w_in_grad", out_dtype=WIRE)
        dcur, parts["w_in"][l] = _in_proj_bwd(dh, wt_all[l], dr, tm=tt, ride=_Exchange([(_w_in_grad_parts_t(dwt), False)]))

        small["w_up"][l] = dw_up[:RANK]
        small["b_alpha"][l] = db_al.reshape(DK)
        small["gnorm"][l] = dgn.reshape(HEADS, HDV)
        small["pool_scale"][l] = dscale[0]
        small["b_merge"][l] = db_merge[0]
        small["ln_g"][l], small["ln_b"][l] = dln_g[0], dln_b[0]
    grad_x = dcur[None]
    sm = {k: jnp.stack(v) for k, v in small.items()}

    rep = (("b_alpha", b_alpha, m_b_alpha, v_b_alpha), ("pool_scale", pool_scale, m_pool_scale, v_pool_scale),
           ("b_merge", b_merge, m_b_merge, v_b_merge), ("ln_g", ln_g, m_ln_g, v_ln_g), ("ln_b", ln_b, m_ln_b, v_ln_b))
    cat = lambda arrs: jnp.concatenate(arrs, axis=1)
    p_up, p_gn, p_rep = _exchange([(_to_devices(sm["w_up"], 2), False), (_to_devices(sm["gnorm"], 2), False),
                                   (cat([sm[nm] for nm, _, _, _ in rep]), True)], name="exchange_small_grads")

    def update(p, w, m, v, tr, name, layered=True, tc=None):
        shape = w.shape
        nl = shape[0] if layered else 1
        cols = shape[-1]
        flat = lambda a: a.reshape(nl, -1, cols)
        outs = _adamw(p.reshape(nl, N_DEV, -1, cols), flat(w), flat(m), flat(v), tr=tr, tc=tc or cols, name=name)
        return [o_.reshape(shape) for o_ in outs]

    res = {}
    res["w_in"] = [tr3(o_) for o_ in _adamw_layers(parts["w_in"], tr3(w_in), tr3(m_w_in), tr3(v_w_in), tc=128, name="adamw_w_in")]
    proj_p = jnp.stack(parts["proj"])
    for j, (nm, w, m, v) in enumerate((("w_proj_a", w_proj_a, m_w_proj_a, v_w_proj_a), ("w_proj_b", w_proj_b, m_w_proj_b, v_w_proj_b),
                                       ("w_out", w_out, m_w_out, v_w_out))):
        res[nm] = update(proj_p[:, :, j], w, m, v, D // N_DEV, "adamw_" + nm)
    res["w_pool_grp"] = update(jnp.stack(parts["pool"]), w_pool_grp, m_w_pool_grp, v_w_pool_grp, 128, "adamw_w_pool")
    res["w_alpha_up"] = update(p_up, w_alpha_up, m_w_alpha_up, v_w_alpha_up, DEPTH * RANK, "adamw_w_up", layered=False)
    res["gla_norm_g"] = update(p_gn, gla_norm_g, m_gla_norm_g, v_gla_norm_g, DEPTH * HEADS, "adamw_gnorm", layered=False)
    rep_out = update(p_rep, cat([w for _, w, _, _ in rep]), cat([m for _, _, m, _ in rep]), cat([v for _, _, _, v in rep]),
                     DEPTH, "adamw_small", layered=False)
    off = 0
    for nm, w, _, _ in rep:
        n = w.shape[1]
        res[nm] = [o_[:, off:off + n] for o_ in rep_out]
        off += n

    order = ("w_in", "w_alpha_up", "b_alpha", "gla_norm_g", "w_pool_grp", "pool_scale", "b_merge", "w_proj_a", "w_proj_b",
             "w_out", "ln_g", "ln_b")
    return (loss, grad_x, *[res[n][0] for n in order], *[res[n][1] for n in order],
            *[res[n][2] for n in order], *[res[n][3] for n in order])
```

```python
import jax
import jax.numpy as jnp
from jax import lax
from jax.experimental import pallas as pl
from jax.experimental.pallas import tpu as pltpu

F32 = jnp.float32
MXU = jnp.bfloat16
WIRE = jnp.bfloat16

N_DEV = 8
DEPTH = 4
D = 1024
HEADS = 4
DK = D // 2
HDK = DK // HEADS
HDV = D // HEADS
RANK = 16
CHUNK = 64
GATE_TAU = 16.0
POOL_WINDOWS = (2, 4, 8, 16)
PG = D // len(POOL_WINDOWS)
HALO = 16
IN_COLS = 7184
SHARD = IN_COLS // N_DEV
ALPHA = (2.0 * DEPTH) ** 0.25
EPS = 1e-5
Q_SCALE = HDK ** -0.5

ADAM_LR, ADAM_B1, ADAM_B2, ADAM_EPS, ADAM_WD, ADAM_STEP = 0.001, 0.9, 0.999, 1e-08, 0.01, 10

PI0, GB0, ML0, AL0, AL_W = 0, D, 2 * D, 4 * D, 512
HD0 = AL0 + AL_W
HEAD_W = 2 * HDK + 2 * HDV
HP = HD0 + HEADS * HEAD_W
HPB = 2
O_Q, O_K, O_V, O_GA, O_AL, O_PI, O_GB, O_ML = 0, DK, 2 * DK, 2 * DK + D, 2 * DK + 2 * D, 2 * DK + 2 * D + RANK, \
    2 * DK + 3 * D + RANK, 2 * DK + 4 * D + RANK

VMEM_BIG = 56 * 1024 * 1024
VMEM_MID = 40 * 1024 * 1024

NN = ((1,), (0,))
NT = ((1,), (1,))
TN = ((0,), (0,))

HBM = pl.BlockSpec(memory_space=pltpu.HBM)
ANY = pl.BlockSpec(memory_space=pl.ANY)


def _dot(a, b, dims):
    return lax.dot_general(a.astype(MXU), b.astype(MXU), (dims, ((), ())), preferred_element_type=F32)


def _params(sem, vmem):
    return pltpu.CompilerParams(dimension_semantics=sem, vmem_limit_bytes=vmem)


def _sigmoid(x):
    return 1.0 / (1.0 + jnp.exp(-x))


def _log_sigmoid(z):
    return jnp.minimum(z, 0.0) - jnp.log(1.0 + jnp.exp(-jnp.abs(z)))


class _Exchange:
    def __init__(self, items):
        self.items = [(s, bool(g)) for s, g in items]
        self.n = len(self.items)
        self.srcs = [s for s, _ in self.items]
        self.in_specs = [HBM] * self.n
        self.out_specs = [HBM] * self.n
        self.out_shape = [jax.ShapeDtypeStruct((N_DEV,) + tuple(s.shape if g else s.shape[1:]), s.dtype) for s, g in self.items]
        self.scratch = [pltpu.SemaphoreType.DMA((self.n * (N_DEV - 1),)), pltpu.SemaphoreType.DMA((self.n * (N_DEV - 1),)),
                        pltpu.SemaphoreType.DMA((self.n,))]

    def copies(self, src_refs, out_refs, send_sems, recv_sems, local_sems):
        x, y, c = lax.axis_index("x"), lax.axis_index("y"), lax.axis_index("c")
        me = 4 * x + 2 * y + c
        copies = []
        for t, (_, gather) in enumerate(self.items):
            src_ref, out_ref = src_refs[t], out_refs[t]
            copies.append(pltpu.make_async_copy(src_ref if gather else src_ref.at[me], out_ref.at[me], local_sems.at[t]))
            for k in range(1, N_DEV):
                px = 1 - x if k & 4 else x
                py = 1 - y if k & 2 else y
                pc = 1 - c if k & 1 else c
                peer = 4 * px + 2 * py + pc
                sem = t * (N_DEV - 1) + k - 1
                copies.append(pltpu.make_async_remote_copy(
                    src_ref=src_ref if gather else src_ref.at[peer],
                    dst_ref=out_ref.at[me],
                    send_sem=send_sems.at[sem],
                    recv_sem=recv_sems.at[sem],
                    device_id=(px, py, pc),
                    device_id_type=pl.DeviceIdType.MESH,
                ))
        return copies


    mid_step = None

    def start(self, *refs):
        for cp in self.copies(*refs):
            cp.start()

    def finish(self, *refs):
        for cp in self.copies(*refs):
            cp.wait()


class _ChipGather:
    def __init__(self, src, mid_step):
        self.n = 1
        self.srcs = [src]
        self.mid_step = mid_step
        self.in_specs = [HBM]
        self.out_specs = [HBM]
        self.out_shape = [jax.ShapeDtypeStruct((N_DEV,) + tuple(src.shape), src.dtype)]
        self.scratch = [pltpu.SemaphoreType.DMA((N_DEV - 1,)), pltpu.SemaphoreType.DMA((N_DEV - 1,)), pltpu.SemaphoreType.DMA((1,))]

    def _plan(self, src_refs, out_refs, send_sems, recv_sems, local_sems):
        src_ref, out_ref = src_refs[0], out_refs[0]
        x, y, c = lax.axis_index("x"), lax.axis_index("y"), lax.axis_index("c")
        me, sibling = (x, y, c), (x, y, 1 - c)
        chips = [(1 - x, y), (x, 1 - y), (1 - x, 1 - y)]

        def copy(k, block, to, src=None):
            slot = out_ref.at[4 * block[0] + 2 * block[1] + block[2]]
            return pltpu.make_async_remote_copy(
                src_ref=slot if src is None else src, dst_ref=slot, send_sem=send_sems.at[k], recv_sem=recv_sems.at[k],
                device_id=to, device_id_type=pl.DeviceIdType.MESH)

        mine = pltpu.make_async_copy(src_ref, out_ref.at[4 * x + 2 * y + c], local_sems.at[0])
        first = [copy(0, me, sibling, src=src_ref)] + [copy(1 + j, me, (*chip, c), src=src_ref) for j, chip in enumerate(chips)]
        landed = [copy(1 + j, (*chip, c), me) for j, chip in enumerate(chips)]
        passed = [copy(4 + j, (*chip, c), sibling) for j, chip in enumerate(chips)]
        from_sibling = [copy(0, sibling, me)] + [copy(4 + j, (*chip, 1 - c), me) for j, chip in enumerate(chips)]
        return mine, first, landed, passed, from_sibling

    def start(self, *refs):
        mine, first, _, _, _ = self._plan(*refs)
        mine.start()
        for cp in first:
            cp.start()

    def mid(self, *refs):
        _, _, landed, passed, _ = self._plan(*refs)
        for got, fwd in zip(landed, passed):
            got.wait_recv()
            fwd.start()

    def finish(self, *refs):
        mine, first, _, passed, from_sibling = self._plan(*refs)
        for cp in from_sibling:
            cp.wait_recv()
        for cp in first + passed:
            cp.wait_send()
        mine.wait()


def _grid_ends(grid):
    first = last = step = None
    for a, n in enumerate(grid):
        f = pl.program_id(a) == 0
        e = pl.program_id(a) == n - 1
        first = f if first is None else first & f
        last = e if last is None else last & e
        step = pl.program_id(a) if step is None else step * n + pl.program_id(a)
    return first, last, step


def _call(body, *, name, grid, in_specs, out_specs, out_shape, args, scratch=(), sem=None, vmem=VMEM_MID, ride=None, aliases=None):
    n_in, n_out, n_scr = len(in_specs), len(out_specs), len(scratch)
    sem = sem or ("arbitrary",) * len(grid)
    if ride is None:
        return pl.pallas_call(body, name=name, grid=grid, in_specs=in_specs, out_specs=out_specs, out_shape=out_shape,
                              scratch_shapes=list(scratch), compiler_params=_params(sem, vmem),
                              input_output_aliases=aliases or {})(*args)
    r = ride.n

    def riding(*refs):
        ins, rsrc = refs[:n_in], refs[n_in:n_in + r]
        outs, rout = refs[n_in + r:n_in + r + n_out], refs[n_in + r + n_out:n_in + 2 * r + n_out]
        scr = refs[n_in + 2 * r + n_out:n_in + 2 * r + n_out + n_scr]
        send_sems, recv_sems, local_sems = refs[n_in + 2 * r + n_out + n_scr:]
        first, last, step = _grid_ends(grid)
        comm = (rsrc, rout, send_sems, recv_sems, local_sems)

        @pl.when(first)
        def _():
            ride.start(*comm)

        body(*ins, *outs, *scr)

        if ride.mid_step is not None:
            @pl.when(step == ride.mid_step)
            def _():
                ride.mid(*comm)

        @pl.when(last)
        def _():
            ride.finish(*comm)

    return pl.pallas_call(riding, name=name, grid=grid, in_specs=list(in_specs) + ride.in_specs,
                          out_specs=list(out_specs) + ride.out_specs, out_shape=list(out_shape) + ride.out_shape,
                          scratch_shapes=list(scratch) + ride.scratch,
                          compiler_params=_params(("arbitrary",) * len(grid), vmem),
                          input_output_aliases=aliases or {})(*args, *ride.srcs)


def _exchange(items, *, name):
    ex = _Exchange(items)

    def body(*refs):
        copies = ex.copies(refs[:ex.n], refs[ex.n:2 * ex.n], *refs[2 * ex.n:])
        for cp in copies:
            cp.start()
        for cp in copies:
            cp.wait()

    return pl.pallas_call(body, name=name, in_specs=ex.in_specs, out_specs=ex.out_specs, out_shape=ex.out_shape,
                          scratch_shapes=ex.scratch)(*ex.srcs)


def _gather_first(big, smalls, *, name):
    ex = _Exchange([(a, True) for a in smalls])

    def body(*refs):
        big_ref, small_src = refs[0], refs[1:1 + ex.n]
        out_ref, small_out = refs[1 + ex.n], refs[2 + ex.n:2 + 2 * ex.n]
        send_sems, recv_sems, local_sem = refs[2 + 2 * ex.n:5 + 2 * ex.n]
        x, y, c = lax.axis_index("x"), lax.axis_index("y"), lax.axis_index("c")
        me, sibling = (x, y, c), (x, y, 1 - c)
        chips = [(1 - x, y), (x, 1 - y), (1 - x, 1 - y)]

        def slot(px, py, pc):
            return out_ref.at[4 * px + 2 * py + pc]

        def copy(k, block, to, src=None):
            return pltpu.make_async_remote_copy(
                src_ref=slot(*block) if src is None else src, dst_ref=slot(*block),
                send_sem=send_sems.at[k], recv_sem=recv_sems.at[k], device_id=to, device_id_type=pl.DeviceIdType.MESH)

        small = ex.copies(small_src, small_out, *refs[5 + 2 * ex.n:])
        mine = pltpu.make_async_copy(big_ref, slot(*me), local_sem)
        mine.start()
        first = [copy(0, me, sibling, src=big_ref)] + [copy(1 + j, me, (*chip, c), src=big_ref) for j, chip in enumerate(chips)]
        for cp in first + small:
            cp.start()
        passed = [copy(4 + j, (*chip, c), sibling) for j, chip in enumerate(chips)]
        for j, chip in enumerate(chips):
            copy(1 + j, (*chip, c), me).wait_recv()
            passed[j].start()
        copy(0, sibling, me).wait_recv()
        for j, chip in enumerate(chips):
            copy(4 + j, (*chip, 1 - c), me).wait_recv()
        for cp in first + passed:
            cp.wait_send()
        mine.wait()
        for cp in small:
            cp.wait()

    return pl.pallas_call(
        body, name=name, in_specs=[HBM] + ex.in_specs, out_specs=[HBM] + ex.out_specs,
        out_shape=[jax.ShapeDtypeStruct((N_DEV,) + tuple(big.shape), big.dtype)] + ex.out_shape,
        scratch_shapes=[pltpu.SemaphoreType.DMA((N_DEV - 1,)), pltpu.SemaphoreType.DMA((N_DEV - 1,)), pltpu.SemaphoreType.DMA]
        + ex.scratch)(big, *ex.srcs)


def _move_rows(src, segs, out_shape, *, name, zero=None):
    cols = src.shape[-1]
    step = 256

    def at(shape, row, m):
        return (slice(row, row + m),) if len(shape) == 2 else (row // SHARD, slice(row % SHARD, row % SHARD + m))

    def room(shape, row):
        return step if len(shape) == 2 else SHARD - row % SHARD

    def body(src_ref, out_ref):
        for s0, d0, n in segs:
            r = 0
            while r < n:
                m = min(step, n - r, room(src.shape, s0 + r), room(out_shape, d0 + r))
                out_ref[(*at(out_shape, d0 + r, m), slice(None))] = src_ref[(*at(src.shape, s0 + r, m), slice(None))]
                r += m
        if zero is not None:
            out_ref[zero[0]:zero[0] + zero[1], :] = jnp.zeros((zero[1], cols), src.dtype)

    vmem = pl.BlockSpec(memory_space=pltpu.VMEM)
    return pl.pallas_call(
        body, name=name, in_specs=[vmem], out_specs=vmem, out_shape=jax.ShapeDtypeStruct(tuple(out_shape), src.dtype),
        compiler_params=pltpu.CompilerParams(vmem_limit_bytes=VMEM_BIG))(src)


def _in_proj(xb, wt, *, tm, tn, ride=None):
    m, k = xb.shape
    n = wt.shape[0]

    def body(x_ref, w_ref, o_ref):
        o_ref[...] = _dot(x_ref[...], w_ref[...], NT)

    return _call(
        body, name="in_proj", grid=(n // tn, m // tm),
        in_specs=[pl.BlockSpec((tm, k), lambda j, i: (i, 0)), pl.BlockSpec((tn, k), lambda j, i: (j, 0))],
        out_specs=[pl.BlockSpec((tm, tn), lambda j, i: (i, j))],
        out_shape=[jax.ShapeDtypeStruct((m, n), F32)],
        args=(xb, wt), sem=("parallel", "parallel"), vmem=VMEM_BIG, ride=ride)


def _in_proj_bwd(dh, wt, dr, *, tm, ride=None):
    m, n = dh.shape
    k = wt.shape[1]

    def body(dh_ref, w_ref, dr_ref, o_ref):
        o_ref[...] = ALPHA * dr_ref[...] + _dot(dh_ref[...], w_ref[...], NN)

    return _call(
        body, name="in_proj_bwd", grid=(m // tm,),
        in_specs=[pl.BlockSpec((tm, n), lambda i: (i, 0)),
                  pl.BlockSpec((n, k), lambda i: (0, 0), pipeline_mode=pl.Buffered(1)),
                  pl.BlockSpec((tm, k), lambda i: (i, 0))],
        out_specs=[pl.BlockSpec((tm, k), lambda i: (i, 0))],
        out_shape=[jax.ShapeDtypeStruct((m, k), F32)],
        args=(dh, wt, dr), sem=("parallel",), vmem=VMEM_BIG, ride=ride)


def _mm_nt_into(dc, w, arr, col, *, tm, name):
    m, n = dc.shape
    k = w.shape[0]

    def body(dc_ref, w_ref, _arr_in, o_ref):
        o_ref[...] = _dot(dc_ref[...], w_ref[...], NT).astype(o_ref.dtype)

    return _call(
        body, name=name, grid=(m // tm,),
        in_specs=[pl.BlockSpec((tm, n), lambda i: (i, 0)), pl.BlockSpec((k, n), lambda i: (0, 0)), ANY],
        out_specs=[pl.BlockSpec((tm, k), lambda i: (i, col))],
        out_shape=[jax.ShapeDtypeStruct(arr.shape, arr.dtype)],
        args=(dc, w, arr), sem=("parallel",), aliases={2: 0})[0]


def _mm_tn(a, dc, *, tm, tk, name, a_block=None, out_dtype=F32):
    m = a.shape[0]
    k, a_col = (a.shape[1], None) if a_block is None else a_block
    n = dc.shape[1]
    ni = m // tm

    def body(a_ref, dc_ref, o_ref, acc):
        i = pl.program_id(1)

        @pl.when(i == 0)
        def _():
            acc[...] = jnp.zeros_like(acc)

        acc[...] += _dot(a_ref[...], dc_ref[...], TN)

        @pl.when(i == ni - 1)
        def _():
            o_ref[...] = acc[...].astype(o_ref.dtype)

    a_map = (lambda j, i: (i, j)) if a_col is None else (lambda j, i: (i, a_col))
    return _call(
        body, name=name, grid=(k // tk, ni),
        in_specs=[pl.BlockSpec((tm, tk), a_map), pl.BlockSpec((tm, n), lambda j, i: (i, 0))],
        out_specs=[pl.BlockSpec((tk, n), lambda j, i: (j, 0))],
        out_shape=[jax.ShapeDtypeStruct((k, n), out_dtype)],
        scratch=[pltpu.VMEM((tk, n), F32)],
        args=(a, dc), sem=("parallel", "arbitrary"), vmem=VMEM_BIG)[0]


def _head_cols(p):
    b = p * HEAD_W
    return (slice(b, b + HDK), slice(b + HDK, b + 2 * HDK), slice(b + 2 * HDK, b + 2 * HDK + HDV),
            slice(b + 2 * HDK + HDV, b + HEAD_W))


def _seg_cumsum(v, reverse=False):
    t, w = v.shape
    hi = v.astype(MXU)
    lo = (v - hi.astype(F32)).astype(MXU)
    terms = jnp.concatenate([hi, lo], axis=1)
    row = lax.broadcasted_iota(jnp.int32, (CHUNK, CHUNK), 0)
    col = lax.broadcasted_iota(jnp.int32, (CHUNK, CHUNK), 1)
    ones = jnp.where((row <= col) if reverse else (row >= col), 1.0, 0.0).astype(MXU)
    out = []
    for c in range(t // CHUNK):
        y = _dot(ones, terms[c * CHUNK:(c + 1) * CHUNK], NN)
        out.append(y[:, :w] + y[:, w:])
    return jnp.concatenate(out, axis=0)


def _seg_rcumsum_rolls(v):
    t = v.shape[0]
    rowmod = lax.broadcasted_iota(jnp.int32, v.shape, 0) % CHUNK
    sh = 1
    while sh < CHUNK:
        v = v + jnp.where(rowmod < CHUNK - sh, pltpu.roll(v, t - sh, 0), 0.0)
        sh *= 2
    return v


def _gla_decay(alpha_ref, wup_ref, b_ref, g_scr):
    z = _dot(alpha_ref[...], wup_ref[...], NN) + b_ref[...]
    g_scr[...] = _seg_cumsum(_log_sigmoid(z) * (1.0 / GATE_TAU))
    return z


QE1, KE1, QE2, KE2, QA, KD = range(6)
EP, EM, EA, EDL = range(4)
GW = HPB * HDK


def _gla_operands(hd_ref, g_scr, opnd_scr, fac_scr=None):
    t = g_scr.shape[0]

    def chunk_row(r):
        return jnp.concatenate([jnp.broadcast_to(g_scr[c * CHUNK + r:c * CHUNK + r + 1, :], (CHUNK, GW))
                                for c in range(t // CHUNK)], axis=0)

    g = g_scr[...]
    g_last = chunk_row(CHUNK - 1)
    ref = 0.5 * (chunk_row(0) + g_last)
    fac = {EP: jnp.exp(g - ref), EM: jnp.exp(ref - g), EA: jnp.exp(g), EDL: jnp.exp(g_last - g)}
    if fac_scr is not None:
        for j, f in fac.items():
            fac_scr[j] = f
    for p in range(HPB):
        qc, kc, _, _ = _head_cols(p)
        gc = slice(p * HDK, (p + 1) * HDK)
        qs = hd_ref[:, qc] * Q_SCALE
        k = hd_ref[:, kc]
        for j, (x, f) in {QE1: (qs, EP), KE1: (k, EM), QE2: (qs, EM), KE2: (k, EP), QA: (qs, EA), KD: (k, EDL)}.items():
            opnd_scr[j, :, gc] = (x * fac[f][:, gc]).astype(opnd_scr.dtype)


def _lower_mask():
    return lax.broadcasted_iota(jnp.int32, (CHUNK, CHUNK), 0) >= lax.broadcasted_iota(jnp.int32, (CHUNK, CHUNK), 1)


def _scores(opnd_scr, rows, gc, lower):
    return jnp.where(lower, _dot(opnd_scr[QE1, rows, gc], opnd_scr[KE1, rows, gc], NT),
                     _dot(opnd_scr[QE2, rows, gc], opnd_scr[KE2, rows, gc], NT))


def _gla_specs(tt, row):
    return [
        pl.BlockSpec((tt, HPB * HEAD_W), lambda h, i: (row(i), HD0 // (HPB * HEAD_W) + h)),
        pl.BlockSpec((tt, 128), lambda h, i: (row(i), AL0 // 128)),
        pl.BlockSpec((128, HPB * HDK), lambda h, i: (0, h)),
        pl.BlockSpec((1, HPB * HDK), lambda h, i: (0, h)),
        pl.BlockSpec((1, HPB * HDV), lambda h, i: (0, h)),
    ]


def _gla_fwd(hh, wup, b_alpha, gnorm, *, tt, ride=None):
    s = hh.shape[0]
    nt = s // tt
    nct = tt // CHUNK

    def body(hd_ref, al_ref, wup_ref, b_ref, gn_ref, o_ref, ya_ref, st_ref, state, g_scr, opnd_scr):
        @pl.when(pl.program_id(1) == 0)
        def _():
            state[...] = jnp.zeros_like(state)

        _gla_decay(al_ref, wup_ref, b_ref, g_scr)
        _gla_operands(hd_ref, g_scr, opnd_scr)
        lower = _lower_mask()
        for c in range(nct):
            rows = slice(c * CHUNK, (c + 1) * CHUNK)
            for p in range(HPB):
                vc = _head_cols(p)[2]
                gc = slice(p * HDK, (p + 1) * HDK)
                v = hd_ref[rows, vc]
                st = state[p]
                st_ref[p, c] = st
                egl = jnp.exp(g_scr[(c + 1) * CHUNK - 1:(c + 1) * CHUNK, gc])
                o_ref[rows, p * HDV:(p + 1) * HDV] = (_dot(_scores(opnd_scr, rows, gc, lower), v, NN)
                                                      + _dot(opnd_scr[QA, rows, gc], st, NT))
                state[p] = st * egl + _dot(v, opnd_scr[KD, rows, gc], TN)
        for p in range(HPB):
            oc = slice(p * HDV, (p + 1) * HDV)
            o = o_ref[:, oc]
            ohat = o * lax.rsqrt(jnp.mean(o * o, axis=-1, keepdims=True) + EPS)
            ga = hd_ref[:, _head_cols(p)[3]]
            ya_ref[:, oc] = (ohat * gn_ref[:, oc] * (ga * _sigmoid(ga))).astype(ya_ref.dtype)

    return _call(
        body, name="gla_fwd", grid=(HEADS // HPB, nt),
        in_specs=_gla_specs(tt, lambda i: i),
        out_specs=[
            pl.BlockSpec((tt, HPB * HDV), lambda h, i: (i, h)),
            pl.BlockSpec((tt, HPB * HDV), lambda h, i: (i, h)),
            pl.BlockSpec((HPB, nct, HDV, HDK), lambda h, i: (h, i, 0, 0)),
        ],
        out_shape=[
            jax.ShapeDtypeStruct((s, D), F32),
            jax.ShapeDtypeStruct((s, D), MXU),
            jax.ShapeDtypeStruct((HEADS, s // CHUNK, HDV, HDK), F32),
        ],
        scratch=[pltpu.VMEM((HPB, HDV, HDK), F32), pltpu.VMEM((tt, GW), F32), pltpu.VMEM((6, tt, GW), MXU)],
        args=(hh, hh, wup, b_alpha, gnorm), ride=ride)


def _gla_bwd(hh, wup, b_alpha, gnorm, o, states, dya, dh, *, tt, ride=None):
    s = hh.shape[0]
    nt = s // tt
    nct = tt // CHUNK

    def body(hd_ref, al_ref, wup_ref, b_ref, gn_ref, o_ref, st_ref, dya_ref, _dh_in,
             dh_ref, dz_ref, dgn_ref, db_ref, dstate, g_scr, dg_scr, do_scr, opnd_scr, fac_scr, res_scr, dgl_scr):
        @pl.when(pl.program_id(1) == 0)
        def _():
            dstate[...] = jnp.zeros_like(dstate)
            dgn_ref[...] = jnp.zeros_like(dgn_ref)
            db_ref[...] = jnp.zeros_like(db_ref)

        z = _gla_decay(al_ref, wup_ref, b_ref, g_scr)
        _gla_operands(hd_ref, g_scr, opnd_scr, fac_scr)

        for p in range(HPB):
            oc = slice(p * HDV, (p + 1) * HDV)
            gac = _head_cols(p)[3]
            o_t = o_ref[:, oc]
            rstd = lax.rsqrt(jnp.mean(o_t * o_t, axis=-1, keepdims=True) + EPS)
            ohat = o_t * rstd
            ga = hd_ref[:, gac]
            sg = _sigmoid(ga)
            dya_t = dya_ref[:, oc]
            gn = gn_ref[:, oc]
            dh_ref[:, gac] = (dya_t * ohat * gn * (sg * (1.0 + ga * (1.0 - sg)))).astype(dh_ref.dtype)
            don = dya_t * (ga * sg)
            dgn_ref[p] += jnp.sum(don * ohat, axis=0, keepdims=True)
            dohat = don * gn
            do_scr[:, oc] = rstd * (dohat - ohat * jnp.mean(dohat * ohat, axis=-1, keepdims=True))

        lower = _lower_mask()
        for c in range(nct - 1, -1, -1):
            rows = slice(c * CHUNK, (c + 1) * CHUNK)
            for p in range(HPB):
                vc = _head_cols(p)[2]
                gc = slice(p * HDK, (p + 1) * HDK)
                v = hd_ref[rows, vc]
                do = do_scr[rows, p * HDV:(p + 1) * HDV]
                st = st_ref[p, c]
                dst = dstate[p]
                egl = jnp.exp(g_scr[(c + 1) * CHUNK - 1:(c + 1) * CHUNK, gc])
                a = _scores(opnd_scr, rows, gc, lower)
                da = _dot(do, v, NT)
                da1 = jnp.where(lower, da, 0.0).astype(MXU)
                da2 = jnp.where(lower, 0.0, da).astype(MXU)
                res_scr[0, rows, gc] = _dot(da1, opnd_scr[KE1, rows, gc], NN)
                res_scr[1, rows, gc] = _dot(da1, opnd_scr[QE1, rows, gc], TN)
                res_scr[2, rows, gc] = _dot(da2, opnd_scr[KE2, rows, gc], NN)
                res_scr[3, rows, gc] = _dot(da2, opnd_scr[QE2, rows, gc], TN)
                res_scr[4, rows, gc] = _dot(do, st, NN)
                res_scr[5, rows, gc] = _dot(v, dst, NN)
                dh_ref[rows, vc] = (_dot(a, do, TN) + _dot(opnd_scr[KD, rows, gc], dst, NT)).astype(dh_ref.dtype)
                dgl_scr[c:c + 1, gc] = egl * jnp.sum(dst * st, axis=0, keepdims=True)
                dstate[p] = dst * egl + _dot(do, opnd_scr[QA, rows, gc], TN)

        p1, p2, p3 = res_scr[0] * fac_scr[EP], res_scr[2] * fac_scr[EM], res_scr[4] * fac_scr[EA]
        r1, r2, r3 = res_scr[1] * fac_scr[EM], res_scr[3] * fac_scr[EP], res_scr[5] * fac_scr[EDL]
        dq = (p1 + p2 + p3) * Q_SCALE
        dk = r1 + r2 + r3
        dgq = p1 - p2 + p3
        dgk = r2 - r1
        for p in range(HPB):
            qc, kc, _, _ = _head_cols(p)
            gc = slice(p * HDK, (p + 1) * HDK)
            dh_ref[:, qc] = dq[:, gc].astype(dh_ref.dtype)
            dh_ref[:, kc] = dk[:, gc].astype(dh_ref.dtype)
            k = hd_ref[:, kc]
            r3k = r3[:, gc] * k
            dg_scr[:, gc] = (hd_ref[:, qc] * Q_SCALE) * dgq[:, gc] + k * dgk[:, gc] - r3k
            for c in range(nct):
                last = slice((c + 1) * CHUNK - 1, (c + 1) * CHUNK)
                dg_scr[last, gc] += jnp.sum(r3k[c * CHUNK:(c + 1) * CHUNK], axis=0, keepdims=True) + dgl_scr[c:c + 1, gc]

        dz = _seg_rcumsum_rolls(dg_scr[...]) * _sigmoid(-z) * (1.0 / GATE_TAU)
        dz_ref[...] = dz.astype(dz_ref.dtype)
        for p in range(HPB):
            db_ref[p] += jnp.sum(dz[:, p * HDK:(p + 1) * HDK], axis=0, keepdims=True)

    rev = lambda i: nt - 1 - i
    in_specs = _gla_specs(tt, rev) + [
        pl.BlockSpec((tt, HPB * HDV), lambda h, i: (rev(i), h)),
        pl.BlockSpec((HPB, nct, HDV, HDK), lambda h, i: (h, rev(i), 0, 0)),
        pl.BlockSpec((tt, HPB * HDV), lambda h, i: (rev(i), h)),
        ANY,
    ]
    return _call(
        body, name="gla_bwd", grid=(HEADS // HPB, nt), in_specs=in_specs,
        out_specs=[
            pl.BlockSpec((tt, HPB * HEAD_W), lambda h, i: (rev(i), HD0 // (HPB * HEAD_W) + h)),
            pl.BlockSpec((tt, HPB * HDK), lambda h, i: (rev(i), h)),
            pl.BlockSpec((HPB, 1, HDV), lambda h, i: (h, 0, 0)),
            pl.BlockSpec((HPB, 1, HDK), lambda h, i: (h, 0, 0)),
        ],
        out_shape=[
            jax.ShapeDtypeStruct(dh.shape, dh.dtype),
            jax.ShapeDtypeStruct((s, DK), MXU),
            jax.ShapeDtypeStruct((HEADS, 1, HDV), F32),
            jax.ShapeDtypeStruct((HEADS, 1, HDK), F32),
        ],
        scratch=[pltpu.VMEM((HPB, HDV, HDK), F32), pltpu.VMEM((tt, GW), F32), pltpu.VMEM((tt, GW), F32),
                 pltpu.VMEM((tt, HPB * HDV), F32), pltpu.VMEM((6, tt, GW), MXU), pltpu.VMEM((4, tt, GW), F32),
                 pltpu.VMEM((6, tt, GW), F32), pltpu.VMEM((max(nct, 8), GW), F32)],
        args=(hh, hh, wup, b_alpha, gnorm, o, states, dya, dh), ride=ride, aliases={8: 0})


def _window_count(tile, tt, w):
    pos = tile * tt + lax.broadcasted_iota(jnp.int32, (tt, PG), 0) + 1
    return jnp.minimum(pos, w).astype(F32)


def _pool_fwd(hh, wpool, scale, *, tt):
    s = hh.shape[0]
    nt = s // tt

    def body(ug_ref, w_ref, sc_ref, pooled_ref, yb_ref, halo):
        i = pl.program_id(0)

        @pl.when(i == 0)
        def _():
            halo[...] = jnp.zeros_like(halo)

        for g, w in enumerate(POOL_WINDOWS):
            cols = slice(g * PG, (g + 1) * PG)
            u = ug_ref[:, cols]
            run = jnp.concatenate([halo[:, cols], u], axis=0)
            sh = 1
            while sh < w:
                run = run + pltpu.roll(run, sh, 0)
                sh *= 2
            pooled = run[HALO:, :] / _window_count(i, tt, w) - u
            pooled_ref[:, cols] = pooled.astype(pooled_ref.dtype)
            mixed = _dot(pooled, w_ref[g], NN)
            gb = ug_ref[:, slice(D + g * PG, D + (g + 1) * PG)]
            yb_ref[:, cols] = (mixed * sc_ref[:, cols] * (gb * _sigmoid(gb))).astype(yb_ref.dtype)
        halo[...] = ug_ref[tt - HALO:tt, :D]

    tile = pl.BlockSpec((tt, D), lambda i: (i, 0))
    return _call(
        body, name="pool_fwd", grid=(nt,),
        in_specs=[
            pl.BlockSpec((tt, 2 * D), lambda i: (i, PI0 // (2 * D))),
            pl.BlockSpec((len(POOL_WINDOWS), PG, PG), lambda i: (0, 0, 0)),
            pl.BlockSpec((1, D), lambda i: (0, 0)),
        ],
        out_specs=[tile] * 2,
        out_shape=[jax.ShapeDtypeStruct((s, D), MXU), jax.ShapeDtypeStruct((s, D), MXU)],
        scratch=[pltpu.VMEM((HALO, D), F32)], args=(hh, wpool, scale))


def _pool_bwd(hh, wpool, scale, pooled, dyb, dh, *, tt):
    s = hh.shape[0]
    nt = s // tt

    def body(gb_ref, w_ref, sc_ref, pooled_ref, dyb_ref, _dh_in, dh_ref, dw_ref, dsc_ref, halo):
        i = pl.program_id(0)
        tile = nt - 1 - i

        @pl.when(i == 0)
        def _():
            halo[...] = jnp.zeros_like(halo)
            dw_ref[...] = jnp.zeros_like(dw_ref)
            dsc_ref[...] = jnp.zeros_like(dsc_ref)

        for g, w in enumerate(POOL_WINDOWS):
            cols = slice(g * PG, (g + 1) * PG)
            gcols = slice(D + g * PG, D + (g + 1) * PG)
            gb = gb_ref[:, cols]
            sg = _sigmoid(gb)
            pooled = pooled_ref[:, cols]
            mixed = _dot(pooled, w_ref[g], NN)
            sc = sc_ref[:, cols]
            dyb = dyb_ref[:, cols]
            dh_ref[:, gcols] = (dyb * mixed * sc * (sg * (1.0 + gb * (1.0 - sg)))).astype(dh_ref.dtype)
            dms = dyb * (gb * sg)
            dsc_ref[:, cols] += jnp.sum(dms * mixed, axis=0, keepdims=True)
            dmixed = dms * sc
            dpooled = _dot(dmixed, w_ref[g], NT)
            dw_ref[g] += _dot(pooled, dmixed, TN)
            e = dpooled / _window_count(tile, tt, w)
            run = jnp.concatenate([e, halo[:, cols]], axis=0)
            sh = 1
            while sh < w:
                run = run + pltpu.roll(run, tt + HALO - sh, 0)
                sh *= 2
            dh_ref[:, cols] = (run[:tt, :] - dpooled).astype(dh_ref.dtype)
            halo[:, cols] = e[:HALO, :]

    rev = lambda i: nt - 1 - i
    tile = pl.BlockSpec((tt, D), lambda i: (rev(i), 0))
    wspec = pl.BlockSpec((len(POOL_WINDOWS), PG, PG), lambda i: (0, 0, 0))
    vec = pl.BlockSpec((1, D), lambda i: (0, 0))
    return _call(
        body, name="pool_bwd", grid=(nt,),
        in_specs=[pl.BlockSpec((tt, D), lambda i: (rev(i), GB0 // D)), wspec, vec, tile, tile, ANY],
        out_specs=[pl.BlockSpec((tt, 2 * D), lambda i: (rev(i), PI0 // (2 * D))), wspec, vec],
        out_shape=[jax.ShapeDtypeStruct(dh.shape, dh.dtype), jax.ShapeDtypeStruct((len(POOL_WINDOWS), PG, PG), F32),
                   jax.ShapeDtypeStruct((1, D), F32)],
        scratch=[pltpu.VMEM((HALO, D), F32)], args=(hh, wpool, scale, pooled, dyb, dh), aliases={5: 0})


def _merge_fwd(hh, x, ya, yb, wpa, wpb, wout, b_merge, ln_g, ln_b, *, tt):
    s = x.shape[0]

    def body(ml_ref, x_ref, ya_ref, yb_ref, wpa_ref, wpb_ref, wout_ref, bm_ref, g_ref, b_ref, r_ref, xn_ref, xnb_ref):
        pa = _dot(ya_ref[...], wpa_ref[...], NN)
        pb = _dot(yb_ref[...], wpb_ref[...], NN)
        merged = _sigmoid(ml_ref[:, :D] + bm_ref[:, :D]) * pa + _sigmoid(ml_ref[:, D:] + bm_ref[:, D:]) * pb
        r = ALPHA * x_ref[...] + _dot(merged, wout_ref[...], NN)
        r_ref[...] = r
        mu = jnp.mean(r, axis=-1, keepdims=True)
        xc = r - mu
        var = jnp.mean(xc * xc, axis=-1, keepdims=True)
        xn = xc * lax.rsqrt(var + EPS) * g_ref[...] + b_ref[...]
        xn_ref[...] = xn
        xnb_ref[...] = xn.astype(xnb_ref.dtype)

    tile = pl.BlockSpec((tt, D), lambda i: (i, 0))
    full = pl.BlockSpec((D, D), lambda i: (0, 0), pipeline_mode=pl.Buffered(1))
    vec = pl.BlockSpec((1, D), lambda i: (0, 0))
    return _call(
        body, name="merge_fwd", grid=(s // tt,),
        in_specs=[pl.BlockSpec((tt, 2 * D), lambda i: (i, ML0 // (2 * D))), tile, tile, tile, full, full, full,
                  pl.BlockSpec((1, 2 * D), lambda i: (0, 0)), vec, vec],
        out_specs=[tile] * 3, out_shape=[jax.ShapeDtypeStruct((s, D), F32)] * 2 + [jax.ShapeDtypeStruct((s, D), MXU)],
        args=(hh, x, ya, yb, wpa, wpb, wout, b_merge, ln_g, ln_b), sem=("parallel",), vmem=VMEM_BIG)


def _merge_bwd(hh, r, ya, yb, dout, wpa, wpb, wout, b_merge, ln_g, *, tt):
    s = r.shape[0]

    def body(ml_ref, r_ref, ya_ref, yb_ref, do_ref, wpa_ref, wpb_ref, wout_ref, bm_ref, g_ref,
             dh_ref, dr_ref, dpa_ref, dpb_ref, dwout_ref, dg_ref, db_ref, dbm_ref):
        @pl.when(pl.program_id(0) == 0)
        def _():
            dwout_ref[...] = jnp.zeros_like(dwout_ref)
            dg_ref[...] = jnp.zeros_like(dg_ref)
            db_ref[...] = jnp.zeros_like(db_ref)
            dbm_ref[...] = jnp.zeros_like(dbm_ref)

        rr = r_ref[...]
        mu = jnp.mean(rr, axis=-1, keepdims=True)
        xc = rr - mu
        rstd = lax.rsqrt(jnp.mean(xc * xc, axis=-1, keepdims=True) + EPS)
        xhat = xc * rstd
        do = do_ref[...]
        dg_ref[...] += jnp.sum(do * xhat, axis=0, keepdims=True)
        db_ref[...] += jnp.sum(do, axis=0, keepdims=True)
        dxh = do * g_ref[...]
        dr = rstd * (dxh - jnp.mean(dxh, axis=-1, keepdims=True) - xhat * jnp.mean(dxh * xhat, axis=-1, keepdims=True))
        dr_ref[...] = dr
        g_a = _sigmoid(ml_ref[:, :D] + bm_ref[:, :D])
        g_b = _sigmoid(ml_ref[:, D:] + bm_ref[:, D:])
        pa = _dot(ya_ref[...], wpa_ref[...], NN)
        pb = _dot(yb_ref[...], wpb_ref[...], NN)
        dwout_ref[...] += _dot(g_a * pa + g_b * pb, dr, TN)
        dm = _dot(dr, wout_ref[...], NT)
        dpa_ref[...] = (dm * g_a).astype(dpa_ref.dtype)
        dpb_ref[...] = (dm * g_b).astype(dpb_ref.dtype)
        dml_a = dm * pa * g_a * (1.0 - g_a)
        dml_b = dm * pb * g_b * (1.0 - g_b)
        dh_ref[:, :D] = dml_a.astype(dh_ref.dtype)
        dh_ref[:, D:] = dml_b.astype(dh_ref.dtype)
        dbm_ref[:, :D] += jnp.sum(dml_a, axis=0, keepdims=True)
        dbm_ref[:, D:] += jnp.sum(dml_b, axis=0, keepdims=True)

    tile = pl.BlockSpec((tt, D), lambda i: (i, 0))
    full = pl.BlockSpec((D, D), lambda i: (0, 0))
    vec = pl.BlockSpec((1, D), lambda i: (0, 0))
    vec2 = pl.BlockSpec((1, 2 * D), lambda i: (0, 0))
    mlb = pl.BlockSpec((tt, 2 * D), lambda i: (i, ML0 // (2 * D)))
    return _call(
        body, name="merge_bwd", grid=(s // tt,),
        in_specs=[mlb, tile, tile, tile, tile, full, full, full, vec2, vec],
        out_specs=[mlb, tile, tile, tile, full, vec, vec, vec2],
        out_shape=[
            jax.ShapeDtypeStruct((s, HP), MXU), jax.ShapeDtypeStruct((s, D), F32),
            jax.ShapeDtypeStruct((s, D), MXU), jax.ShapeDtypeStruct((s, D), MXU),
            jax.ShapeDtypeStruct((D, D), F32), jax.ShapeDtypeStruct((1, D), F32),
            jax.ShapeDtypeStruct((1, D), F32), jax.ShapeDtypeStruct((1, 2 * D), F32),
        ],
        args=(hh, r, ya, yb, dout, wpa, wpb, wout, b_merge, ln_g), vmem=VMEM_BIG)


def _proj_bwd(y, dp, w, *, tt, name):
    s = y.shape[0]

    def body(y_ref, dp_ref, w_ref, dy_ref, dw_ref):
        @pl.when(pl.program_id(0) == 0)
        def _():
            dw_ref[...] = jnp.zeros_like(dw_ref)

        dp = dp_ref[...]
        dy_ref[...] = _dot(dp, w_ref[...], NT)
        dw_ref[...] += _dot(y_ref[...], dp, TN)

    tile = pl.BlockSpec((tt, D), lambda i: (i, 0))
    full = pl.BlockSpec((D, D), lambda i: (0, 0))
    return _call(
        body, name=name, grid=(s // tt,), in_specs=[tile, tile, full], out_specs=[tile, full],
        out_shape=[jax.ShapeDtypeStruct((s, D), F32), jax.ShapeDtypeStruct((D, D), F32)], args=(y, dp, w))


def _loss_head(y, target, *, tt):
    s = y.shape[0]

    def body(y_ref, t_ref, loss_ref, dy_ref):
        @pl.when(pl.program_id(0) == 0)
        def _():
            loss_ref[...] = jnp.zeros_like(loss_ref)

        err = y_ref[...] - t_ref[...]
        dy_ref[...] = err * (1.0 / D)
        per_tok = jnp.mean(err * err, axis=-1, keepdims=True)
        loss_ref[...] += 0.5 * jnp.sum(per_tok, axis=0, keepdims=True)

    tile = pl.BlockSpec((tt, D), lambda i: (i, 0))
    return _call(
        body, name="loss_head", grid=(s // tt,), in_specs=[tile, tile],
        out_specs=[pl.BlockSpec((1, 1), lambda i: (0, 0)), tile],
        out_shape=[jax.ShapeDtypeStruct((1, 1), F32), jax.ShapeDtypeStruct((s, D), F32)], args=(y, target))


def _adamw_math(share, w_ref, m_ref, v_ref, g_ref, d_ref, nm_ref, nv_ref):
    g = share(0).astype(F32)
    for q in range(1, N_DEV):
        g = g + share(q).astype(F32)
    g_ref[0] = g
    nm = ADAM_B1 * m_ref[0] + (1.0 - ADAM_B1) * g
    nv = ADAM_B2 * v_ref[0] + (1.0 - ADAM_B2) * (g * g)
    nm_ref[0] = nm
    nv_ref[0] = nv
    m_hat = nm / (1.0 - ADAM_B1 ** ADAM_STEP)
    v_hat = nv / (1.0 - ADAM_B2 ** ADAM_STEP)
    d_ref[0] = -ADAM_LR * (m_hat / (jnp.sqrt(v_hat) + ADAM_EPS) + ADAM_WD * w_ref[0])


def _adamw_layers(parts, w, m, v, *, tc, name):
    nl, rows, cols = w.shape
    nc = cols // tc

    def body(*refs):
        p_refs, rest = refs[:nl], refs[nl:]
        for j in range(nl):
            @pl.when(pl.program_id(0) == j)
            def _(p_ref=p_refs[j]):
                _adamw_math(lambda q: p_ref[q], *rest)

    def part_spec(j):
        return pl.BlockSpec((N_DEV, rows, tc), lambda l, c: (0, 0, jnp.where(l == j, c, jnp.where(l < j, 0, nc - 1))))

    tile = pl.BlockSpec((1, rows, tc), lambda l, c: (l, 0, c))
    return _call(
        body, name=name, grid=(nl, nc),
        in_specs=[part_spec(j) for j in range(nl)] + [tile, tile, tile],
        out_specs=[tile] * 4, out_shape=[jax.ShapeDtypeStruct((nl, rows, cols), F32)] * 4,
        args=(*parts, w, m, v))


def _adamw(parts, w, m, v, *, tr, tc, name):
    nl, rows, cols = w.shape

    def body(p_ref, *rest):
        _adamw_math(lambda q: p_ref[0, q], *rest)

    tile = pl.BlockSpec((1, tr, tc), lambda l, i, j: (l, i, j))
    return _call(
        body, name=name, grid=(nl, rows // tr, cols // tc),
        in_specs=[pl.BlockSpec((1, N_DEV, tr, tc), lambda l, i, j: (l, 0, i, j)), tile, tile, tile],
        out_specs=[tile] * 4, out_shape=[jax.ShapeDtypeStruct((nl, rows, cols), F32)] * 4,
        args=(parts, w, m, v), sem=("parallel", "parallel", "parallel"))


def _from_devices(g, axis):
    nd = g.ndim - 1
    perm = list(range(1, axis + 1)) + [0] + list(range(axis + 1, nd + 1))
    shape = list(g.shape[1:])
    shape[axis] *= N_DEV
    return jnp.transpose(g, perm).reshape(shape)


def _to_devices(a, axis):
    shape = list(a.shape)
    t = a.reshape(shape[:axis] + [N_DEV, shape[axis] // N_DEV] + shape[axis + 1:])
    return jnp.transpose(t, [axis] + list(range(0, axis)) + list(range(axis + 1, t.ndim)))


def _h_row_segments():
    segs = [(O_PI, PI0, IN_COLS - O_PI), (O_AL, AL0, RANK)]
    for h in range(HEADS):
        base = HD0 + h * HEAD_W
        segs += [(O_Q + h * HDK, base, HDK), (O_K + h * HDK, base + HDK, HDK),
                 (O_V + h * HDV, base + 2 * HDK, HDV), (O_GA + h * HDV, base + 2 * HDK + HDV, HDV)]
    return segs


def _h_weight_t(parts):
    return _move_rows(parts, _h_row_segments(), (HP, D), name="w_in_rows", zero=(AL0 + RANK, AL_W - RANK))


def _w_in_grad_parts_t(dwt):
    return _move_rows(dwt, [(d0, s0, n) for s0, d0, n in _h_row_segments()], (N_DEV, SHARD, D), name="w_in_grad_rows")


def kernel(x, w_in, w_alpha_up, b_alpha, gla_norm_g, w_pool_grp, pool_scale, b_merge, w_proj_a, w_proj_b, w_out, ln_g, ln_b, loss_target, m_w_in, m_w_alpha_up, m_b_alpha, m_gla_norm_g, m_w_pool_grp, m_pool_scale, m_b_merge, m_w_proj_a, m_w_proj_b, m_w_out, m_ln_g, m_ln_b, v_w_in, v_w_alpha_up, v_b_alpha, v_gla_norm_g, v_w_pool_grp, v_pool_scale, v_b_merge, v_w_proj_a, v_w_proj_b, v_w_out, v_ln_g, v_ln_b):
    s = x.shape[1]
    tt = min(256, s)
    tm = min(512, s)
    tb = min(1024, s)
    tn = HP // 3
    xs = x.reshape(s, D)

    tr3 = lambda a: jnp.transpose(a, (0, 2, 1))
    w_in_s = tr3(w_in).astype(WIRE)
    proj_s = jnp.stack([w_proj_a, w_proj_b, w_out], axis=1).astype(WIRE)
    pool_s = w_pool_grp.astype(WIRE)

    g_in, g_up, g_gn = _gather_first(w_in_s[0], [w_alpha_up.astype(WIRE), gla_norm_g], name="gather_first")
    wup = jnp.pad(_from_devices(g_up, 2), ((0, 0), (0, AL_W - RANK), (0, 0)))
    gn = _from_devices(g_gn, 2).reshape(DEPTH, 1, D)

    saved, wt_all, proj_all, pool_all = [], [], [], []
    cur, cur_b = xs, xs.astype(MXU)
    g_proj = g_pool = None
    for l in range(DEPTH):
        wt = _h_weight_t(g_in)
        nxt_l = l + 1 < DEPTH
        steps = (HP // tn) * (s // tb)
        res = _in_proj(cur_b, wt, tm=tb, tn=tn, ride=_ChipGather(w_in_s[l + 1], (2 * steps) // 3) if nxt_l else None)
        hh = res[0]
        if nxt_l:
            g_in = res[1]
        layers = ([0] if l == 0 else []) + ([l + 1] if nxt_l else [])
        res = _gla_fwd(hh, wup[l], b_alpha[l:l + 1], gn[l], tt=tm,
                       ride=_Exchange([(a[j], True) for j in layers for a in (proj_s, pool_s)]) if layers else None)
        o, ya, states = res[:3]
        got = {j: res[3 + 2 * t:5 + 2 * t] for t, j in enumerate(layers)}
        if l == 0:
            g_proj, g_pool = got[0]
        proj = _from_devices(g_proj, 1)
        pool = _from_devices(g_pool, 1)
        if nxt_l:
            g_proj, g_pool = got[l + 1]
        wt_all.append(wt), proj_all.append(proj), pool_all.append(pool)
        pooled, yb = _pool_fwd(hh, pool, pool_scale[l:l + 1], tt=tm)
        r, nxt, nxt_b = _merge_fwd(hh, cur, ya, yb, proj[0], proj[1], proj[2],
                                   b_merge[l:l + 1], ln_g[l:l + 1], ln_b[l:l + 1], tt=tm)
        saved.append(dict(xb=cur_b, hh=hh, o=o, ya=ya, states=states, pooled=pooled, yb=yb, r=r))
        cur, cur_b = nxt, nxt_b

    loss_part, dcur = _loss_head(cur, loss_target.reshape(s, D), tt=tm)
    loss = lax.psum(loss_part[0, 0], ("x", "y", "c"))

    small = {k: [None] * DEPTH for k in ("w_up", "b_alpha", "gnorm", "pool_scale", "b_merge", "ln_g", "ln_b")}
    parts = {k: [None] * DEPTH for k in ("w_in", "proj", "pool")}
    for l in range(DEPTH - 1, -1, -1):
        sv = saved[l]
        hh = sv["hh"]
        dh, dr, dpa, dpb, dw_out, dln_g, dln_b, db_merge = _merge_bwd(
            hh, sv["r"], sv["ya"], sv["yb"], dcur, proj_all[l][0], proj_all[l][1], proj_all[l][2], b_merge[l:l + 1], ln_g[l:l + 1], tt=tt)
        dya, dw_pa = _proj_bwd(sv["ya"], dpa, proj_all[l][0], tt=tm, name="proj_a_bwd")
        dyb, dw_pb = _proj_bwd(sv["yb"], dpb, proj_all[l][1], tt=tm, name="proj_b_bwd")
        dh, dw_pool, dscale = _pool_bwd(hh, pool_all[l], pool_scale[l:l + 1], sv["pooled"], dyb, dh, tt=tm)
        ride = _Exchange([(_to_devices(jnp.stack([dw_pa, dw_pb, dw_out]), 1).astype(WIRE), False),
                          (_to_devices(dw_pool, 1).astype(WIRE), False)])
        dh, dz, dgn, db_al, parts["proj"][l], parts["pool"][l] = _gla_bwd(
            hh, wup[l], b_alpha[l:l + 1], gn[l], sv["o"], sv["states"], dya, dh, tt=tm, ride=ride)
        dh = _mm_nt_into(dz, wup[l], dh, AL0 // AL_W, tm=tm, name="alpha_bwd")
        dw_up = _mm_tn(hh, dz, tm=tb, tk=128, name="w_up_grad", a_block=(128, AL0 // 128))
        dwt = _mm_tn(dh, sv["xb"], tm=tb, tk=tn, name="w_in_grad", out_dtype=WIRE)
        dcur, parts["w_in"][l] = _in_proj_bwd(dh, wt_all[l], dr, tm=tt, ride=_Exchange([(_w_in_grad_parts_t(dwt), False)]))

        small["w_up"][l] = dw_up[:RANK]
        small["b_alpha"][l] = db_al.reshape(DK)
        small["gnorm"][l] = dgn.reshape(HEADS, HDV)
        small["pool_scale"][l] = dscale[0]
        small["b_merge"][l] = db_merge[0]
        small["ln_g"][l], small["ln_b"][l] = dln_g[0], dln_b[0]
    grad_x = dcur[None]
    sm = {k: jnp.stack(v) for k, v in small.items()}

    rep = (("b_alpha", b_alpha, m_b_alpha, v_b_alpha), ("pool_scale", pool_scale, m_pool_scale, v_pool_scale),
           ("b_merge", b_merge, m_b_merge, v_b_merge), ("ln_g", ln_g, m_ln_g, v_ln_g), ("ln_b", ln_b, m_ln_b, v_ln_b))
    cat = lambda arrs: jnp.concatenate(arrs, axis=1)
    p_up, p_gn, p_rep = _exchange([(_to_devices(sm["w_up"], 2), False), (_to_devices(sm["gnorm"], 2), False),
                                   (cat([sm[nm] for nm, _, _, _ in rep]), True)], name="exchange_small_grads")

    def update(p, w, m, v, tr, name, layered=True, tc=None):
        shape = w.shape
        nl = shape[0] if layered else 1
        cols = shape[-1]
        flat = lambda a: a.reshape(nl, -1, cols)
        outs = _adamw(p.reshape(nl, N_DEV, -1, cols), flat(w), flat(m), flat(v), tr=tr, tc=tc or cols, name=name)
        return [o_.reshape(shape) for o_ in outs]

    res = {}
    res["w_in"] = [tr3(o_) for o_ in _adamw_layers(parts["w_in"], tr3(w_in), tr3(m_w_in), tr3(v_w_in), tc=128, name="adamw_w_in")]
    proj_p = jnp.stack(parts["proj"])
    for j, (nm, w, m, v) in enumerate((("w_proj_a", w_proj_a, m_w_proj_a, v_w_proj_a), ("w_proj_b", w_proj_b, m_w_proj_b, v_w_proj_b),
                                       ("w_out", w_out, m_w_out, v_w_out))):
        res[nm] = update(proj_p[:, :, j], w, m, v, D // N_DEV, "adamw_" + nm)
    res["w_pool_grp"] = update(jnp.stack(parts["pool"]), w_pool_grp, m_w_pool_grp, v_w_pool_grp, 128, "adamw_w_pool")
    res["w_alpha_up"] = update(p_up, w_alpha_up, m_w_alpha_up, v_w_alpha_up, DEPTH * RANK, "adamw_w_up", layered=False)
    res["gla_norm_g"] = update(p_gn, gla_norm_g, m_gla_norm_g, v_gla_norm_g, DEPTH * HEADS, "adamw_gnorm", layered=False)
    rep_out = update(p_rep, cat([w for _, w, _, _ in rep]), cat([m for _, _, m, _ in rep]), cat([v for _, _, _, v in rep]),
                     DEPTH, "adamw_small", layered=False)
    off = 0
    for nm, w, _, _ in rep:
        n = w.shape[1]
        res[nm] = [o_[:, off:off + n] for o_ in rep_out]
        off += n

    order = ("w_in", "w_alpha_up", "b_alpha", "gla_norm_g", "w_pool_grp", "pool_scale", "b_merge", "w_proj_a", "w_proj_b",
             "w_out", "ln_g", "ln_b")
    return (loss, grad_x, *[res[n][0] for n in order], *[res[n][1] for n in order],
            *[res[n][2] for n in order], *[res[n][3] for n in order])
```

```python
import jax
import jax.numpy as jnp
from jax import lax
from jax.experimental import pallas as pl
from jax.experimental.pallas import tpu as pltpu

F32 = jnp.float32
MXU = jnp.bfloat16
WIRE = jnp.bfloat16

N_DEV = 8
DEPTH = 4
D = 1024
HEADS = 4
DK = D // 2
HDK = DK // HEADS
HDV = D // HEADS
RANK = 16
CHUNK = 64
GATE_TAU = 16.0
POOL_WINDOWS = (2, 4, 8, 16)
PG = D // len(POOL_WINDOWS)
HALO = 16
IN_COLS = 7184
SHARD = IN_COLS // N_DEV
ALPHA = (2.0 * DEPTH) ** 0.25
EPS = 1e-5
Q_SCALE = HDK ** -0.5

ADAM_LR, ADAM_B1, ADAM_B2, ADAM_EPS, ADAM_WD, ADAM_STEP = 0.001, 0.9, 0.999, 1e-08, 0.01, 10

PI0, GB0, ML0, AL0, AL_W = 0, D, 2 * D, 4 * D, 512
HD0 = AL0 + AL_W
HEAD_W = 2 * HDK + 2 * HDV
HP = HD0 + HEADS * HEAD_W
HPB = 2
O_Q, O_K, O_V, O_GA, O_AL, O_PI, O_GB, O_ML = 0, DK, 2 * DK, 2 * DK + D, 2 * DK + 2 * D, 2 * DK + 2 * D + RANK, \
    2 * DK + 3 * D + RANK, 2 * DK + 4 * D + RANK

VMEM_BIG = 56 * 1024 * 1024
VMEM_MID = 40 * 1024 * 1024

NN = ((1,), (0,))
NT = ((1,), (1,))
TN = ((0,), (0,))

HBM = pl.BlockSpec(memory_space=pltpu.HBM)
ANY = pl.BlockSpec(memory_space=pl.ANY)


def _dot(a, b, dims):
    return lax.dot_general(a.astype(MXU), b.astype(MXU), (dims, ((), ())), preferred_element_type=F32)


def _params(sem, vmem):
    return pltpu.CompilerParams(dimension_semantics=sem, vmem_limit_bytes=vmem)


def _sigmoid(x):
    return 1.0 / (1.0 + jnp.exp(-x))


def _log_sigmoid(z):
    return jnp.minimum(z, 0.0) - jnp.log(1.0 + jnp.exp(-jnp.abs(z)))


class _Exchange:
    def __init__(self, items):
        self.items = [(s, bool(g)) for s, g in items]
        self.n = len(self.items)
        self.srcs = [s for s, _ in self.items]
        self.in_specs = [HBM] * self.n
        self.out_specs = [HBM] * self.n
        self.out_shape = [jax.ShapeDtypeStruct((N_DEV,) + tuple(s.shape if g else s.shape[1:]), s.dtype) for s, g in self.items]
        self.scratch = [pltpu.SemaphoreType.DMA((self.n * (N_DEV - 1),)), pltpu.SemaphoreType.DMA((self.n * (N_DEV - 1),)),
                        pltpu.SemaphoreType.DMA((self.n,))]

    def copies(self, src_refs, out_refs, send_sems, recv_sems, local_sems):
        x, y, c = lax.axis_index("x"), lax.axis_index("y"), lax.axis_index("c")
        me = 4 * x + 2 * y + c
        copies = []
        for t, (_, gather) in enumerate(self.items):
            src_ref, out_ref = src_refs[t], out_refs[t]
            copies.append(pltpu.make_async_copy(src_ref if gather else src_ref.at[me], out_ref.at[me], local_sems.at[t]))
            for k in range(1, N_DEV):
                px = 1 - x if k & 4 else x
                py = 1 - y if k & 2 else y
                pc = 1 - c if k & 1 else c
                peer = 4 * px + 2 * py + pc
                sem = t * (N_DEV - 1) + k - 1
                copies.append(pltpu.make_async_remote_copy(
                    src_ref=src_ref if gather else src_ref.at[peer],
                    dst_ref=out_ref.at[me],
                    send_sem=send_sems.at[sem],
                    recv_sem=recv_sems.at[sem],
                    device_id=(px, py, pc),
                    device_id_type=pl.DeviceIdType.MESH,
                ))
        return copies


    mid_step = None

    def start(self, *refs):
        for cp in self.copies(*refs):
            cp.start()

    def finish(self, *refs):
        for cp in self.copies(*refs):
            cp.wait()


class _ChipGather:
    def __init__(self, src, mid_step):
        self.n = 1
        self.srcs = [src]
        self.mid_step = mid_step
        self.in_specs = [HBM]
        self.out_specs = [HBM]
        self.out_shape = [jax.ShapeDtypeStruct((N_DEV,) + tuple(src.shape), src.dtype)]
        self.scratch = [pltpu.SemaphoreType.DMA((N_DEV - 1,)), pltpu.SemaphoreType.DMA((N_DEV - 1,)), pltpu.SemaphoreType.DMA((1,))]

    def _plan(self, src_refs, out_refs, send_sems, recv_sems, local_sems):
        src_ref, out_ref = src_refs[0], out_refs[0]
        x, y, c = lax.axis_index("x"), lax.axis_index("y"), lax.axis_index("c")
        me, sibling = (x, y, c), (x, y, 1 - c)
        chips = [(1 - x, y), (x, 1 - y), (1 - x, 1 - y)]

        def copy(k, block, to, src=None):
            slot = out_ref.at[4 * block[0] + 2 * block[1] + block[2]]
            return pltpu.make_async_remote_copy(
                src_ref=slot if src is None else src, dst_ref=slot, send_sem=send_sems.at[k], recv_sem=recv_sems.at[k],
                device_id=to, device_id_type=pl.DeviceIdType.MESH)

        mine = pltpu.make_async_copy(src_ref, out_ref.at[4 * x + 2 * y + c], local_sems.at[0])
        first = [copy(0, me, sibling, src=src_ref)] + [copy(1 + j, me, (*chip, c), src=src_ref) for j, chip in enumerate(chips)]
        landed = [copy(1 + j, (*chip, c), me) for j, chip in enumerate(chips)]
        passed = [copy(4 + j, (*chip, c), sibling) for j, chip in enumerate(chips)]
        from_sibling = [copy(0, sibling, me)] + [copy(4 + j, (*chip, 1 - c), me) for j, chip in enumerate(chips)]
        return mine, first, landed, passed, from_sibling

    def start(self, *refs):
        mine, first, _, _, _ = self._plan(*refs)
        mine.start()
        for cp in first:
            cp.start()

    def mid(self, *refs):
        _, _, landed, passed, _ = self._plan(*refs)
        for got, fwd in zip(landed, passed):
            got.wait_recv()
            fwd.start()

    def finish(self, *refs):
        mine, first, _, passed, from_sibling = self._plan(*refs)
        for cp in from_sibling:
            cp.wait_recv()
        for cp in first + passed:
            cp.wait_send()
        mine.wait()


def _grid_ends(grid):
    first = last = step = None
    for a, n in enumerate(grid):
        f = pl.program_id(a) == 0
        e = pl.program_id(a) == n - 1
        first = f if first is None else first & f
        last = e if last is None else last & e
        step = pl.program_id(a) if step is None else step * n + pl.program_id(a)
    return first, last, step


def _call(body, *, name, grid, in_specs, out_specs, out_shape, args, scratch=(), sem=None, vmem=VMEM_MID, ride=None, aliases=None):
    n_in, n_out, n_scr = len(in_specs), len(out_specs), len(scratch)
    sem = sem or ("arbitrary",) * len(grid)
    if ride is None:
        return pl.pallas_call(body, name=name, grid=grid, in_specs=in_specs, out_specs=out_specs, out_shape=out_shape,
                              scratch_shapes=list(scratch), compiler_params=_params(sem, vmem),
                              input_output_aliases=aliases or {})(*args)
    r = ride.n

    def riding(*refs):
        ins, rsrc = refs[:n_in], refs[n_in:n_in + r]
        outs, rout = refs[n_in + r:n_in + r + n_out], refs[n_in + r + n_out:n_in + 2 * r + n_out]
        scr = refs[n_in + 2 * r + n_out:n_in + 2 * r + n_out + n_scr]
        send_sems, recv_sems, local_sems = refs[n_in + 2 * r + n_out + n_scr:]
        first, last, step = _grid_ends(grid)
        comm = (rsrc, rout, send_sems, recv_sems, local_sems)

        @pl.when(first)
        def _():
            ride.start(*comm)

        body(*ins, *outs, *scr)

        if ride.mid_step is not None:
            @pl.when(step == ride.mid_step)
            def _():
                ride.mid(*comm)

        @pl.when(last)
        def _():
            ride.finish(*comm)

    return pl.pallas_call(riding, name=name, grid=grid, in_specs=list(in_specs) + ride.in_specs,
                          out_specs=list(out_specs) + ride.out_specs, out_shape=list(out_shape) + ride.out_shape,
                          scratch_shapes=list(scratch) + ride.scratch,
                          compiler_params=_params(("arbitrary",) * len(grid), vmem),
                          input_output_aliases=aliases or {})(*args, *ride.srcs)


def _exchange(items, *, name):
    ex = _Exchange(items)

    def body(*refs):
        copies = ex.copies(refs[:ex.n], refs[ex.n:2 * ex.n], *refs[2 * ex.n:])
        for cp in copies:
            cp.start()
        for cp in copies:
            cp.wait()

    return pl.pallas_call(body, name=name, in_specs=ex.in_specs, out_specs=ex.out_specs, out_shape=ex.out_shape,
                          scratch_shapes=ex.scratch)(*ex.srcs)


def _gather_first(big, smalls, *, name):
    ex = _Exchange([(a, True) for a in smalls])

    def body(*refs):
        big_ref, small_src = refs[0], refs[1:1 + ex.n]
        out_ref, small_out = refs[1 + ex.n], refs[2 + ex.n:2 + 2 * ex.n]
        send_sems, recv_sems, local_sem = refs[2 + 2 * ex.n:5 + 2 * ex.n]
        x, y, c = lax.axis_index("x"), lax.axis_index("y"), lax.axis_index("c")
        me, sibling = (x, y, c), (x, y, 1 - c)
        chips = [(1 - x, y), (x, 1 - y), (1 - x, 1 - y)]

        def slot(px, py, pc):
            return out_ref.at[4 * px + 2 * py + pc]

        def copy(k, block, to, src=None):
            return pltpu.make_async_remote_copy(
                src_ref=slot(*block) if src is None else src, dst_ref=slot(*block),
                send_sem=send_sems.at[k], recv_sem=recv_sems.at[k], device_id=to, device_id_type=pl.DeviceIdType.MESH)

        small = ex.copies(small_src, small_out, *refs[5 + 2 * ex.n:])
        mine = pltpu.make_async_copy(big_ref, slot(*me), local_sem)
        mine.start()
        first = [copy(0, me, sibling, src=big_ref)] + [copy(1 + j, me, (*chip, c), src=big_ref) for j, chip in enumerate(chips)]
        for cp in first + small:
            cp.start()
        passed = [copy(4 + j, (*chip, c), sibling) for j, chip in enumerate(chips)]
        for j, chip in enumerate(chips):
            copy(1 + j, (*chip, c), me).wait_recv()
            passed[j].start()
        copy(0, sibling, me).wait_recv()
        for j, chip in enumerate(chips):
            copy(4 + j, (*chip, 1 - c), me).wait_recv()
        for cp in first + passed:
            cp.wait_send()
        mine.wait()
        for cp in small:
            cp.wait()

    return pl.pallas_call(
        body, name=name, in_specs=[HBM] + ex.in_specs, out_specs=[HBM] + ex.out_specs,
        out_shape=[jax.ShapeDtypeStruct((N_DEV,) + tuple(big.shape), big.dtype)] + ex.out_shape,
        scratch_shapes=[pltpu.SemaphoreType.DMA((N_DEV - 1,)), pltpu.SemaphoreType.DMA((N_DEV - 1,)), pltpu.SemaphoreType.DMA]
        + ex.scratch)(big, *ex.srcs)


def _move_rows(src, segs, out_shape, *, name, zero=None):
    cols = src.shape[-1]
    step = 256

    def at(shape, row, m):
        return (slice(row, row + m),) if len(shape) == 2 else (row // SHARD, slice(row % SHARD, row % SHARD + m))

    def room(shape, row):
        return step if len(shape) == 2 else SHARD - row % SHARD

    def body(src_ref, out_ref):
        for s0, d0, n in segs:
            r = 0
            while r < n:
                m = min(step, n - r, room(src.shape, s0 + r), room(out_shape, d0 + r))
                out_ref[(*at(out_shape, d0 + r, m), slice(None))] = src_ref[(*at(src.shape, s0 + r, m), slice(None))]
                r += m
        if zero is not None:
            out_ref[zero[0]:zero[0] + zero[1], :] = jnp.zeros((zero[1], cols), src.dtype)

    vmem = pl.BlockSpec(memory_space=pltpu.VMEM)
    return pl.pallas_call(
        body, name=name, in_specs=[vmem], out_specs=vmem, out_shape=jax.ShapeDtypeStruct(tuple(out_shape), src.dtype),
        compiler_params=pltpu.CompilerParams(vmem_limit_bytes=VMEM_BIG))(src)


def _in_proj(xb, wt, *, tm, tn, ride=None):
    m, k = xb.shape
    n = wt.shape[0]

    def body(x_ref, w_ref, o_ref):
        o_ref[...] = _dot(x_ref[...], w_ref[...], NT)

    return _call(
        body, name="in_proj", grid=(n // tn, m // tm),
        in_specs=[pl.BlockSpec((tm, k), lambda j, i: (i, 0)), pl.BlockSpec((tn, k), lambda j, i: (j, 0))],
        out_specs=[pl.BlockSpec((tm, tn), lambda j, i: (i, j))],
        out_shape=[jax.ShapeDtypeStruct((m, n), F32)],
        args=(xb, wt), sem=("parallel", "parallel"), vmem=VMEM_BIG, ride=ride)


def _in_proj_bwd(dh, wt, dr, *, tm, ride=None):
    m, n = dh.shape
    k = wt.shape[1]

    def body(dh_ref, w_ref, dr_ref, o_ref):
        o_ref[...] = ALPHA * dr_ref[...] + _dot(dh_ref[...], w_ref[...], NN)

    return _call(
        body, name="in_proj_bwd", grid=(m // tm,),
        in_specs=[pl.BlockSpec((tm, n), lambda i: (i, 0)),
                  pl.BlockSpec((n, k), lambda i: (0, 0), pipeline_mode=pl.Buffered(1)),
                  pl.BlockSpec((tm, k), lambda i: (i, 0))],
        out_specs=[pl.BlockSpec((tm, k), lambda i: (i, 0))],
        out_shape=[jax.ShapeDtypeStruct((m, k), F32)],
        args=(dh, wt, dr), sem=("parallel",), vmem=VMEM_BIG, ride=ride)


def _mm_tn(a, dc, *, tm, tk, name, out_dtype=F32):
    m, k = a.shape
    n = dc.shape[1]
    ni = m // tm

    def body(a_ref, dc_ref, o_ref, acc):
        i = pl.program_id(1)

        @pl.when(i == 0)
        def _():
            acc[...] = jnp.zeros_like(acc)

        acc[...] += _dot(a_ref[...], dc_ref[...], TN)

        @pl.when(i == ni - 1)
        def _():
            o_ref[...] = acc[...].astype(o_ref.dtype)

    return _call(
        body, name=name, grid=(k // tk, ni),
        in_specs=[pl.BlockSpec((tm, tk), lambda j, i: (i, j)), pl.BlockSpec((tm, n), lambda j, i: (i, 0))],
        out_specs=[pl.BlockSpec((tk, n), lambda j, i: (j, 0))],
        out_shape=[jax.ShapeDtypeStruct((k, n), out_dtype)],
        scratch=[pltpu.VMEM((tk, n), F32)],
        args=(a, dc), sem=("parallel", "arbitrary"), vmem=VMEM_BIG)[0]


def _head_cols(p):
    b = p * HEAD_W
    return (slice(b, b + HDK), slice(b + HDK, b + 2 * HDK), slice(b + 2 * HDK, b + 2 * HDK + HDV),
            slice(b + 2 * HDK + HDV, b + HEAD_W))


def _seg_cumsum(v, reverse=False):
    t, w = v.shape
    hi = v.astype(MXU)
    lo = (v - hi.astype(F32)).astype(MXU)
    terms = jnp.concatenate([hi, lo], axis=1)
    row = lax.broadcasted_iota(jnp.int32, (CHUNK, CHUNK), 0)
    col = lax.broadcasted_iota(jnp.int32, (CHUNK, CHUNK), 1)
    ones = jnp.where((row <= col) if reverse else (row >= col), 1.0, 0.0).astype(MXU)
    out = []
    for c in range(t // CHUNK):
        y = _dot(ones, terms[c * CHUNK:(c + 1) * CHUNK], NN)
        out.append(y[:, :w] + y[:, w:])
    return jnp.concatenate(out, axis=0)


def _seg_rcumsum_rolls(v):
    t = v.shape[0]
    rowmod = lax.broadcasted_iota(jnp.int32, v.shape, 0) % CHUNK
    sh = 1
    while sh < CHUNK:
        v = v + jnp.where(rowmod < CHUNK - sh, pltpu.roll(v, t - sh, 0), 0.0)
        sh *= 2
    return v


def _gla_decay(alpha_ref, wup_ref, b_ref, g_scr):
    z = _dot(alpha_ref[...], wup_ref[...], NN) + b_ref[...]
    g_scr[...] = _seg_cumsum(_log_sigmoid(z) * (1.0 / GATE_TAU))
    return z


QE1, KE1, QE2, KE2, QA, KD = range(6)
EP, EM, EA, EDL = range(4)
GW = HPB * HDK


def _gla_operands(hd_ref, g_scr, opnd_scr, fac_scr=None):
    t = g_scr.shape[0]

    def chunk_row(r):
        return jnp.concatenate([jnp.broadcast_to(g_scr[c * CHUNK + r:c * CHUNK + r + 1, :], (CHUNK, GW))
                                for c in range(t // CHUNK)], axis=0)

    g = g_scr[...]
    g_last = chunk_row(CHUNK - 1)
    ref = 0.5 * (chunk_row(0) + g_last)
    fac = {EP: jnp.exp(g - ref), EM: jnp.exp(ref - g), EA: jnp.exp(g), EDL: jnp.exp(g_last - g)}
    if fac_scr is not None:
        for j, f in fac.items():
            fac_scr[j] = f
    for p in range(HPB):
        qc, kc, _, _ = _head_cols(p)
        gc = slice(p * HDK, (p + 1) * HDK)
        qs = hd_ref[:, qc] * Q_SCALE
        k = hd_ref[:, kc]
        for j, (x, f) in {QE1: (qs, EP), KE1: (k, EM), QE2: (qs, EM), KE2: (k, EP), QA: (qs, EA), KD: (k, EDL)}.items():
            opnd_scr[j, :, gc] = (x * fac[f][:, gc]).astype(opnd_scr.dtype)


def _lower_mask():
    return lax.broadcasted_iota(jnp.int32, (CHUNK, CHUNK), 0) >= lax.broadcasted_iota(jnp.int32, (CHUNK, CHUNK), 1)


def _scores(opnd_scr, rows, gc, lower):
    return jnp.where(lower, _dot(opnd_scr[QE1, rows, gc], opnd_scr[KE1, rows, gc], NT),
                     _dot(opnd_scr[QE2, rows, gc], opnd_scr[KE2, rows, gc], NT))


def _gla_specs(tt, row):
    return [
        pl.BlockSpec((tt, HPB * HEAD_W), lambda h, i: (row(i), HD0 // (HPB * HEAD_W) + h)),
        pl.BlockSpec((tt, 128), lambda h, i: (row(i), AL0 // 128)),
        pl.BlockSpec((128, HPB * HDK), lambda h, i: (0, h)),
        pl.BlockSpec((1, HPB * HDK), lambda h, i: (0, h)),
        pl.BlockSpec((1, HPB * HDV), lambda h, i: (0, h)),
    ]


def _gla_fwd(hh, wup, b_alpha, gnorm, *, tt, ride=None):
    s = hh.shape[0]
    nt = s // tt
    nct = tt // CHUNK

    def body(hd_ref, al_ref, wup_ref, b_ref, gn_ref, o_ref, ya_ref, st_ref, state, g_scr, opnd_scr):
        @pl.when(pl.program_id(1) == 0)
        def _():
            state[...] = jnp.zeros_like(state)

        _gla_decay(al_ref, wup_ref, b_ref, g_scr)
        _gla_operands(hd_ref, g_scr, opnd_scr)
        lower = _lower_mask()
        for c in range(nct):
            rows = slice(c * CHUNK, (c + 1) * CHUNK)
            for p in range(HPB):
                vc = _head_cols(p)[2]
                gc = slice(p * HDK, (p + 1) * HDK)
                v = hd_ref[rows, vc]
                st = state[p]
                st_ref[p, c] = st
                egl = jnp.exp(g_scr[(c + 1) * CHUNK - 1:(c + 1) * CHUNK, gc])
                o_ref[rows, p * HDV:(p + 1) * HDV] = (_dot(_scores(opnd_scr, rows, gc, lower), v, NN)
                                                      + _dot(opnd_scr[QA, rows, gc], st, NT))
                state[p] = st * egl + _dot(v, opnd_scr[KD, rows, gc], TN)
        for p in range(HPB):
            oc = slice(p * HDV, (p + 1) * HDV)
            o = o_ref[:, oc]
            ohat = o * lax.rsqrt(jnp.mean(o * o, axis=-1, keepdims=True) + EPS)
            ga = hd_ref[:, _head_cols(p)[3]]
            ya_ref[:, oc] = (ohat * gn_ref[:, oc] * (ga * _sigmoid(ga))).astype(ya_ref.dtype)

    return _call(
        body, name="gla_fwd", grid=(HEADS // HPB, nt),
        in_specs=_gla_specs(tt, lambda i: i),
        out_specs=[
            pl.BlockSpec((tt, HPB * HDV), lambda h, i: (i, h)),
            pl.BlockSpec((tt, HPB * HDV), lambda h, i: (i, h)),
            pl.BlockSpec((HPB, nct, HDV, HDK), lambda h, i: (h, i, 0, 0)),
        ],
        out_shape=[
            jax.ShapeDtypeStruct((s, D), F32),
            jax.ShapeDtypeStruct((s, D), MXU),
            jax.ShapeDtypeStruct((HEADS, s // CHUNK, HDV, HDK), F32),
        ],
        scratch=[pltpu.VMEM((HPB, HDV, HDK), F32), pltpu.VMEM((tt, GW), F32), pltpu.VMEM((6, tt, GW), MXU)],
        args=(hh, hh, wup, b_alpha, gnorm), ride=ride)


def _gla_bwd(hh, wup, b_alpha, gnorm, o, states, dya, dh, *, tt, ride=None):
    s = hh.shape[0]
    nt = s // tt
    nct = tt // CHUNK

    def body(hd_ref, al_ref, wup_ref, b_ref, gn_ref, o_ref, st_ref, dya_ref, _dh_in,
             dh_ref, dz_ref, dgn_ref, db_ref, dstate, g_scr, dg_scr, do_scr, opnd_scr, fac_scr, res_scr, dgl_scr):
        @pl.when(pl.program_id(1) == 0)
        def _():
            dstate[...] = jnp.zeros_like(dstate)
            dgn_ref[...] = jnp.zeros_like(dgn_ref)
            db_ref[...] = jnp.zeros_like(db_ref)

        z = _gla_decay(al_ref, wup_ref, b_ref, g_scr)
        _gla_operands(hd_ref, g_scr, opnd_scr, fac_scr)

        for p in range(HPB):
            oc = slice(p * HDV, (p + 1) * HDV)
            gac = _head_cols(p)[3]
            o_t = o_ref[:, oc]
            rstd = lax.rsqrt(jnp.mean(o_t * o_t, axis=-1, keepdims=True) + EPS)
            ohat = o_t * rstd
            ga = hd_ref[:, gac]
            sg = _sigmoid(ga)
            dya_t = dya_ref[:, oc]
            gn = gn_ref[:, oc]
            dh_ref[:, gac] = (dya_t * ohat * gn * (sg * (1.0 + ga * (1.0 - sg)))).astype(dh_ref.dtype)
            don = dya_t * (ga * sg)
            dgn_ref[p] += jnp.sum(don * ohat, axis=0, keepdims=True)
            dohat = don * gn
            do_scr[:, oc] = rstd * (dohat - ohat * jnp.mean(dohat * ohat, axis=-1, keepdims=True))

        lower = _lower_mask()
        for c in range(nct - 1, -1, -1):
            rows = slice(c * CHUNK, (c + 1) * CHUNK)
            for p in range(HPB):
                vc = _head_cols(p)[2]
                gc = slice(p * HDK, (p + 1) * HDK)
                v = hd_ref[rows, vc]
                do = do_scr[rows, p * HDV:(p + 1) * HDV]
                st = st_ref[p, c]
                dst = dstate[p]
                egl = jnp.exp(g_scr[(c + 1) * CHUNK - 1:(c + 1) * CHUNK, gc])
                a = _scores(opnd_scr, rows, gc, lower)
                da = _dot(do, v, NT)
                da1 = jnp.where(lower, da, 0.0).astype(MXU)
                da2 = jnp.where(lower, 0.0, da).astype(MXU)
                res_scr[0, rows, gc] = _dot(da1, opnd_scr[KE1, rows, gc], NN)
                res_scr[1, rows, gc] = _dot(da1, opnd_scr[QE1, rows, gc], TN)
                res_scr[2, rows, gc] = _dot(da2, opnd_scr[KE2, rows, gc], NN)
                res_scr[3, rows, gc] = _dot(da2, opnd_scr[QE2, rows, gc], TN)
                res_scr[4, rows, gc] = _dot(do, st, NN)
                res_scr[5, rows, gc] = _dot(v, dst, NN)
                dh_ref[rows, vc] = (_dot(a, do, TN) + _dot(opnd_scr[KD, rows, gc], dst, NT)).astype(dh_ref.dtype)
                dgl_scr[c:c + 1, gc] = egl * jnp.sum(dst * st, axis=0, keepdims=True)
                dstate[p] = dst * egl + _dot(do, opnd_scr[QA, rows, gc], TN)

        p1, p2, p3 = res_scr[0] * fac_scr[EP], res_scr[2] * fac_scr[EM], res_scr[4] * fac_scr[EA]
        r1, r2, r3 = res_scr[1] * fac_scr[EM], res_scr[3] * fac_scr[EP], res_scr[5] * fac_scr[EDL]
        dq = (p1 + p2 + p3) * Q_SCALE
        dk = r1 + r2 + r3
        dgq = p1 - p2 + p3
        dgk = r2 - r1
        for p in range(HPB):
            qc, kc, _, _ = _head_cols(p)
            gc = slice(p * HDK, (p + 1) * HDK)
            dh_ref[:, qc] = dq[:, gc].astype(dh_ref.dtype)
            dh_ref[:, kc] = dk[:, gc].astype(dh_ref.dtype)
            k = hd_ref[:, kc]
            r3k = r3[:, gc] * k
            dg_scr[:, gc] = (hd_ref[:, qc] * Q_SCALE) * dgq[:, gc] + k * dgk[:, gc] - r3k
            for c in range(nct):
                last = slice((c + 1) * CHUNK - 1, (c + 1) * CHUNK)
                dg_scr[last, gc] += jnp.sum(r3k[c * CHUNK:(c + 1) * CHUNK], axis=0, keepdims=True) + dgl_scr[c:c + 1, gc]

        dz = _seg_rcumsum_rolls(dg_scr[...]) * _sigmoid(-z) * (1.0 / GATE_TAU)
        dz_ref[...] = dz.astype(dz_ref.dtype)
        for p in range(HPB):
            db_ref[p] += jnp.sum(dz[:, p * HDK:(p + 1) * HDK], axis=0, keepdims=True)

    rev = lambda i: nt - 1 - i
    in_specs = _gla_specs(tt, rev) + [
        pl.BlockSpec((tt, HPB * HDV), lambda h, i: (rev(i), h)),
        pl.BlockSpec((HPB, nct, HDV, HDK), lambda h, i: (h, rev(i), 0, 0)),
        pl.BlockSpec((tt, HPB * HDV), lambda h, i: (rev(i), h)),
        ANY,
    ]
    return _call(
        body, name="gla_bwd", grid=(HEADS // HPB, nt), in_specs=in_specs,
        out_specs=[
            pl.BlockSpec((tt, HPB * HEAD_W), lambda h, i: (rev(i), HD0 // (HPB * HEAD_W) + h)),
            pl.BlockSpec((tt, HPB * HDK), lambda h, i: (rev(i), h)),
            pl.BlockSpec((HPB, 1, HDV), lambda h, i: (h, 0, 0)),
            pl.BlockSpec((HPB, 1, HDK), lambda h, i: (h, 0, 0)),
        ],
        out_shape=[
            jax.ShapeDtypeStruct(dh.shape, dh.dtype),
            jax.ShapeDtypeStruct((s, DK), MXU),
            jax.ShapeDtypeStruct((HEADS, 1, HDV), F32),
            jax.ShapeDtypeStruct((HEADS, 1, HDK), F32),
        ],
        scratch=[pltpu.VMEM((HPB, HDV, HDK), F32), pltpu.VMEM((tt, GW), F32), pltpu.VMEM((tt, GW), F32),
                 pltpu.VMEM((tt, HPB * HDV), F32), pltpu.VMEM((6, tt, GW), MXU), pltpu.VMEM((4, tt, GW), F32),
                 pltpu.VMEM((6, tt, GW), F32), pltpu.VMEM((max(nct, 8), GW), F32)],
        args=(hh, hh, wup, b_alpha, gnorm, o, states, dya, dh), ride=ride, aliases={8: 0})


def _window_count(tile, tt, w):
    pos = tile * tt + lax.broadcasted_iota(jnp.int32, (tt, PG), 0) + 1
    return jnp.minimum(pos, w).astype(F32)


def _pool_fwd(hh, wpool, scale, *, tt):
    s = hh.shape[0]
    nt = s // tt

    def body(ug_ref, w_ref, sc_ref, pooled_ref, yb_ref, halo):
        i = pl.program_id(0)

        @pl.when(i == 0)
        def _():
            halo[...] = jnp.zeros_like(halo)

        for g, w in enumerate(POOL_WINDOWS):
            cols = slice(g * PG, (g + 1) * PG)
            u = ug_ref[:, cols]
            run = jnp.concatenate([halo[:, cols], u], axis=0)
            sh = 1
            while sh < w:
                run = run + pltpu.roll(run, sh, 0)
                sh *= 2
            pooled = run[HALO:, :] / _window_count(i, tt, w) - u
            pooled_ref[:, cols] = pooled.astype(pooled_ref.dtype)
            mixed = _dot(pooled, w_ref[g], NN)
            gb = ug_ref[:, slice(D + g * PG, D + (g + 1) * PG)]
            yb_ref[:, cols] = (mixed * sc_ref[:, cols] * (gb * _sigmoid(gb))).astype(yb_ref.dtype)
        halo[...] = ug_ref[tt - HALO:tt, :D]

    tile = pl.BlockSpec((tt, D), lambda i: (i, 0))
    return _call(
        body, name="pool_fwd", grid=(nt,),
        in_specs=[
            pl.BlockSpec((tt, 2 * D), lambda i: (i, PI0 // (2 * D))),
            pl.BlockSpec((len(POOL_WINDOWS), PG, PG), lambda i: (0, 0, 0)),
            pl.BlockSpec((1, D), lambda i: (0, 0)),
        ],
        out_specs=[tile] * 2,
        out_shape=[jax.ShapeDtypeStruct((s, D), MXU), jax.ShapeDtypeStruct((s, D), MXU)],
        scratch=[pltpu.VMEM((HALO, D), F32)], args=(hh, wpool, scale))


def _pool_bwd(hh, wpool, scale, pooled, dyb, dh, *, tt):
    s = hh.shape[0]
    nt = s // tt

    def body(gb_ref, w_ref, sc_ref, pooled_ref, dyb_ref, _dh_in, dh_ref, dw_ref, dsc_ref, halo):
        i = pl.program_id(0)
        tile = nt - 1 - i

        @pl.when(i == 0)
        def _():
            halo[...] = jnp.zeros_like(halo)
            dw_ref[...] = jnp.zeros_like(dw_ref)
            dsc_ref[...] = jnp.zeros_like(dsc_ref)

        for g, w in enumerate(POOL_WINDOWS):
            cols = slice(g * PG, (g + 1) * PG)
            gcols = slice(D + g * PG, D + (g + 1) * PG)
            gb = gb_ref[:, cols]
            sg = _sigmoid(gb)
            pooled = pooled_ref[:, cols]
            mixed = _dot(pooled, w_ref[g], NN)
            sc = sc_ref[:, cols]
            dyb = dyb_ref[:, cols]
            dh_ref[:, gcols] = (dyb * mixed * sc * (sg * (1.0 + gb * (1.0 - sg)))).astype(dh_ref.dtype)
            dms = dyb * (gb * sg)
            dsc_ref[:, cols] += jnp.sum(dms * mixed, axis=0, keepdims=True)
            dmixed = dms * sc
            dpooled = _dot(dmixed, w_ref[g], NT)
            dw_ref[g] += _dot(pooled, dmixed, TN)
            e = dpooled / _window_count(tile, tt, w)
            run = jnp.concatenate([e, halo[:, cols]], axis=0)
            sh = 1
            while sh < w:
                run = run + pltpu.roll(run, tt + HALO - sh, 0)
                sh *= 2
            dh_ref[:, cols] = (run[:tt, :] - dpooled).astype(dh_ref.dtype)
            halo[:, cols] = e[:HALO, :]

    rev = lambda i: nt - 1 - i
    tile = pl.BlockSpec((tt, D), lambda i: (rev(i), 0))
    wspec = pl.BlockSpec((len(POOL_WINDOWS), PG, PG), lambda i: (0, 0, 0))
    vec = pl.BlockSpec((1, D), lambda i: (0, 0))
    return _call(
        body, name="pool_bwd", grid=(nt,),
        in_specs=[pl.BlockSpec((tt, D), lambda i: (rev(i), GB0 // D)), wspec, vec, tile, tile, ANY],
        out_specs=[pl.BlockSpec((tt, 2 * D), lambda i: (rev(i), PI0 // (2 * D))), wspec, vec],
        out_shape=[jax.ShapeDtypeStruct(dh.shape, dh.dtype), jax.ShapeDtypeStruct((len(POOL_WINDOWS), PG, PG), F32),
                   jax.ShapeDtypeStruct((1, D), F32)],
        scratch=[pltpu.VMEM((HALO, D), F32)], args=(hh, wpool, scale, pooled, dyb, dh), aliases={5: 0})


def _merge_fwd(hh, x, ya, yb, wpa, wpb, wout, b_merge, ln_g, ln_b, *, tt):
    s = x.shape[0]

    def body(ml_ref, x_ref, ya_ref, yb_ref, wpa_ref, wpb_ref, wout_ref, bm_ref, g_ref, b_ref, r_ref, xn_ref, xnb_ref):
        pa = _dot(ya_ref[...], wpa_ref[...], NN)
        pb = _dot(yb_ref[...], wpb_ref[...], NN)
        merged = _sigmoid(ml_ref[:, :D] + bm_ref[:, :D]) * pa + _sigmoid(ml_ref[:, D:] + bm_ref[:, D:]) * pb
        r = ALPHA * x_ref[...] + _dot(merged, wout_ref[...], NN)
        r_ref[...] = r
        mu = jnp.mean(r, axis=-1, keepdims=True)
        xc = r - mu
        var = jnp.mean(xc * xc, axis=-1, keepdims=True)
        xn = xc * lax.rsqrt(var + EPS) * g_ref[...] + b_ref[...]
        xn_ref[...] = xn
        xnb_ref[...] = xn.astype(xnb_ref.dtype)

    tile = pl.BlockSpec((tt, D), lambda i: (i, 0))
    full = pl.BlockSpec((D, D), lambda i: (0, 0), pipeline_mode=pl.Buffered(1))
    vec = pl.BlockSpec((1, D), lambda i: (0, 0))
    return _call(
        body, name="merge_fwd", grid=(s // tt,),
        in_specs=[pl.BlockSpec((tt, 2 * D), lambda i: (i, ML0 // (2 * D))), tile, tile, tile, full, full, full,
                  pl.BlockSpec((1, 2 * D), lambda i: (0, 0)), vec, vec],
        out_specs=[tile] * 3, out_shape=[jax.ShapeDtypeStruct((s, D), F32)] * 2 + [jax.ShapeDtypeStruct((s, D), MXU)],
        args=(hh, x, ya, yb, wpa, wpb, wout, b_merge, ln_g, ln_b), sem=("parallel",), vmem=VMEM_BIG)


def _merge_bwd(hh, r, ya, yb, dout, wpa, wpb, wout, b_merge, ln_g, *, tt):
    s = r.shape[0]

    def body(ml_ref, r_ref, ya_ref, yb_ref, do_ref, wpa_ref, wpb_ref, wout_ref, bm_ref, g_ref,
             dh_ref, dr_ref, dpa_ref, dpb_ref, dwout_ref, dg_ref, db_ref, dbm_ref):
        @pl.when(pl.program_id(0) == 0)
        def _():
            dwout_ref[...] = jnp.zeros_like(dwout_ref)
            dg_ref[...] = jnp.zeros_like(dg_ref)
            db_ref[...] = jnp.zeros_like(db_ref)
            dbm_ref[...] = jnp.zeros_like(dbm_ref)

        rr = r_ref[...]
        mu = jnp.mean(rr, axis=-1, keepdims=True)
        xc = rr - mu
        rstd = lax.rsqrt(jnp.mean(xc * xc, axis=-1, keepdims=True) + EPS)
        xhat = xc * rstd
        do = do_ref[...]
        dg_ref[...] += jnp.sum(do * xhat, axis=0, keepdims=True)
        db_ref[...] += jnp.sum(do, axis=0, keepdims=True)
        dxh = do * g_ref[...]
        dr = rstd * (dxh - jnp.mean(dxh, axis=-1, keepdims=True) - xhat * jnp.mean(dxh * xhat, axis=-1, keepdims=True))
        dr_ref[...] = dr
        g_a = _sigmoid(ml_ref[:, :D] + bm_ref[:, :D])
        g_b = _sigmoid(ml_ref[:, D:] + bm_ref[:, D:])
        pa = _dot(ya_ref[...], wpa_ref[...], NN)
        pb = _dot(yb_ref[...], wpb_ref[...], NN)
        dwout_ref[...] += _dot(g_a * pa + g_b * pb, dr, TN)
        dm = _dot(dr, wout_ref[...], NT)
        dpa_ref[...] = (dm * g_a).astype(dpa_ref.dtype)
        dpb_ref[...] = (dm * g_b).astype(dpb_ref.dtype)
        dml_a = dm * pa * g_a * (1.0 - g_a)
        dml_b = dm * pb * g_b * (1.0 - g_b)
        dh_ref[:, :D] = dml_a.astype(dh_ref.dtype)
        dh_ref[:, D:] = dml_b.astype(dh_ref.dtype)
        dbm_ref[:, :D] += jnp.sum(dml_a, axis=0, keepdims=True)
        dbm_ref[:, D:] += jnp.sum(dml_b, axis=0, keepdims=True)

    tile = pl.BlockSpec((tt, D), lambda i: (i, 0))
    full = pl.BlockSpec((D, D), lambda i: (0, 0))
    vec = pl.BlockSpec((1, D), lambda i: (0, 0))
    vec2 = pl.BlockSpec((1, 2 * D), lambda i: (0, 0))
    mlb = pl.BlockSpec((tt, 2 * D), lambda i: (i, ML0 // (2 * D)))
    return _call(
        body, name="merge_bwd", grid=(s // tt,),
        in_specs=[mlb, tile, tile, tile, tile, full, full, full, vec2, vec],
        out_specs=[mlb, tile, tile, tile, full, vec, vec, vec2],
        out_shape=[
            jax.ShapeDtypeStruct((s, HP), MXU), jax.ShapeDtypeStruct((s, D), F32),
            jax.ShapeDtypeStruct((s, D), MXU), jax.ShapeDtypeStruct((s, D), MXU),
            jax.ShapeDtypeStruct((D, D), F32), jax.ShapeDtypeStruct((1, D), F32),
            jax.ShapeDtypeStruct((1, D), F32), jax.ShapeDtypeStruct((1, 2 * D), F32),
        ],
        args=(hh, r, ya, yb, dout, wpa, wpb, wout, b_merge, ln_g), vmem=VMEM_BIG)


def _proj_bwd(ya, yb, dpa, dpb, wpa, wpb, *, tt):
    s = ya.shape[0]

    def body(ya_ref, yb_ref, dpa_ref, dpb_ref, wpa_ref, wpb_ref, dya_ref, dyb_ref, dwa_ref, dwb_ref):
        @pl.when(pl.program_id(0) == 0)
        def _():
            dwa_ref[...] = jnp.zeros_like(dwa_ref)
            dwb_ref[...] = jnp.zeros_like(dwb_ref)

        for y_ref, dp_ref, w_ref, dy_ref, dw_ref in ((ya_ref, dpa_ref, wpa_ref, dya_ref, dwa_ref),
                                                     (yb_ref, dpb_ref, wpb_ref, dyb_ref, dwb_ref)):
            dp = dp_ref[...]
            dy_ref[...] = _dot(dp, w_ref[...], NT)
            dw_ref[...] += _dot(y_ref[...], dp, TN)

    tile = pl.BlockSpec((tt, D), lambda i: (i, 0))
    full = pl.BlockSpec((D, D), lambda i: (0, 0))
    return _call(
        body, name="proj_bwd", grid=(s // tt,), in_specs=[tile] * 4 + [full] * 2, out_specs=[tile, tile, full, full],
        out_shape=[jax.ShapeDtypeStruct((s, D), F32)] * 2 + [jax.ShapeDtypeStruct((D, D), F32)] * 2,
        args=(ya, yb, dpa, dpb, wpa, wpb), vmem=VMEM_BIG)


def _alpha_grads(dz, wup, hh, dh, *, tm):
    s = dz.shape[0]

    def body(dz_ref, w_ref, al_ref, _dh_in, dh_ref, dw_ref):
        @pl.when(pl.program_id(0) == 0)
        def _():
            dw_ref[...] = jnp.zeros_like(dw_ref)

        dz_t = dz_ref[...]
        dh_ref[...] = _dot(dz_t, w_ref[...], NT).astype(dh_ref.dtype)
        dw_ref[...] += _dot(al_ref[...], dz_t, TN)

    return _call(
        body, name="alpha_grads", grid=(s // tm,),
        in_specs=[pl.BlockSpec((tm, DK), lambda i: (i, 0)), pl.BlockSpec((AL_W, DK), lambda i: (0, 0)),
                  pl.BlockSpec((tm, 128), lambda i: (i, AL0 // 128)), ANY],
        out_specs=[pl.BlockSpec((tm, AL_W), lambda i: (i, AL0 // AL_W)), pl.BlockSpec((128, DK), lambda i: (0, 0))],
        out_shape=[jax.ShapeDtypeStruct(dh.shape, dh.dtype), jax.ShapeDtypeStruct((128, DK), F32)],
        args=(dz, wup, hh, dh), aliases={3: 0})


def _loss_head(y, target, *, tt):
    s = y.shape[0]

    def body(y_ref, t_ref, loss_ref, dy_ref):
        @pl.when(pl.program_id(0) == 0)
        def _():
            loss_ref[...] = jnp.zeros_like(loss_ref)

        err = y_ref[...] - t_ref[...]
        dy_ref[...] = err * (1.0 / D)
        per_tok = jnp.mean(err * err, axis=-1, keepdims=True)
        loss_ref[...] += 0.5 * jnp.sum(per_tok, axis=0, keepdims=True)

    tile = pl.BlockSpec((tt, D), lambda i: (i, 0))
    return _call(
        body, name="loss_head", grid=(s // tt,), in_specs=[tile, tile],
        out_specs=[pl.BlockSpec((1, 1), lambda i: (0, 0)), tile],
        out_shape=[jax.ShapeDtypeStruct((1, 1), F32), jax.ShapeDtypeStruct((s, D), F32)], args=(y, target))


def _adamw_math(share, w_ref, m_ref, v_ref, g_ref, d_ref, nm_ref, nv_ref):
    g = share(0).astype(F32)
    for q in range(1, N_DEV):
        g = g + share(q).astype(F32)
    g_ref[0] = g
    nm = ADAM_B1 * m_ref[0] + (1.0 - ADAM_B1) * g
    nv = ADAM_B2 * v_ref[0] + (1.0 - ADAM_B2) * (g * g)
    nm_ref[0] = nm
    nv_ref[0] = nv
    m_hat = nm / (1.0 - ADAM_B1 ** ADAM_STEP)
    v_hat = nv / (1.0 - ADAM_B2 ** ADAM_STEP)
    d_ref[0] = -ADAM_LR * (m_hat / (jnp.sqrt(v_hat) + ADAM_EPS) + ADAM_WD * w_ref[0])


def _adamw_layers(parts, w, m, v, *, tc, name):
    nl, rows, cols = w.shape
    nc = cols // tc

    def body(*refs):
        p_refs, rest = refs[:nl], refs[nl:]
        for j in range(nl):
            @pl.when(pl.program_id(0) == j)
            def _(p_ref=p_refs[j]):
                _adamw_math(lambda q: p_ref[q], *rest)

    def part_spec(j):
        return pl.BlockSpec((N_DEV, rows, tc), lambda l, c: (0, 0, jnp.where(l == j, c, jnp.where(l < j, 0, nc - 1))))

    tile = pl.BlockSpec((1, rows, tc), lambda l, c: (l, 0, c))
    return _call(
        body, name=name, grid=(nl, nc),
        in_specs=[part_spec(j) for j in range(nl)] + [tile, tile, tile],
        out_specs=[tile] * 4, out_shape=[jax.ShapeDtypeStruct((nl, rows, cols), F32)] * 4,
        args=(*parts, w, m, v))


def _adamw(parts, w, m, v, *, tr, tc, name):
    nl, rows, cols = w.shape

    def body(p_ref, *rest):
        _adamw_math(lambda q: p_ref[0, q], *rest)

    tile = pl.BlockSpec((1, tr, tc), lambda l, i, j: (l, i, j))
    return _call(
        body, name=name, grid=(nl, rows // tr, cols // tc),
        in_specs=[pl.BlockSpec((1, N_DEV, tr, tc), lambda l, i, j: (l, 0, i, j)), tile, tile, tile],
        out_specs=[tile] * 4, out_shape=[jax.ShapeDtypeStruct((nl, rows, cols), F32)] * 4,
        args=(parts, w, m, v), sem=("parallel", "parallel", "parallel"))


def _from_devices(g, axis):
    nd = g.ndim - 1
    perm = list(range(1, axis + 1)) + [0] + list(range(axis + 1, nd + 1))
    shape = list(g.shape[1:])
    shape[axis] *= N_DEV
    return jnp.transpose(g, perm).reshape(shape)


def _to_devices(a, axis):
    shape = list(a.shape)
    t = a.reshape(shape[:axis] + [N_DEV, shape[axis] // N_DEV] + shape[axis + 1:])
    return jnp.transpose(t, [axis] + list(range(0, axis)) + list(range(axis + 1, t.ndim)))


def _h_row_segments():
    segs = [(O_PI, PI0, IN_COLS - O_PI), (O_AL, AL0, RANK)]
    for h in range(HEADS):
        base = HD0 + h * HEAD_W
        segs += [(O_Q + h * HDK, base, HDK), (O_K + h * HDK, base + HDK, HDK),
                 (O_V + h * HDV, base + 2 * HDK, HDV), (O_GA + h * HDV, base + 2 * HDK + HDV, HDV)]
    return segs


def _h_weight_t(parts):
    return _move_rows(parts, _h_row_segments(), (HP, D), name="w_in_rows", zero=(AL0 + RANK, AL_W - RANK))


def _w_in_grad_parts_t(dwt):
    return _move_rows(dwt, [(d0, s0, n) for s0, d0, n in _h_row_segments()], (N_DEV, SHARD, D), name="w_in_grad_rows")


def kernel(x, w_in, w_alpha_up, b_alpha, gla_norm_g, w_pool_grp, pool_scale, b_merge, w_proj_a, w_proj_b, w_out, ln_g, ln_b, loss_target, m_w_in, m_w_alpha_up, m_b_alpha, m_gla_norm_g, m_w_pool_grp, m_pool_scale, m_b_merge, m_w_proj_a, m_w_proj_b, m_w_out, m_ln_g, m_ln_b, v_w_in, v_w_alpha_up, v_b_alpha, v_gla_norm_g, v_w_pool_grp, v_pool_scale, v_b_merge, v_w_proj_a, v_w_proj_b, v_w_out, v_ln_g, v_ln_b):
    s = x.shape[1]
    tt = min(256, s)
    tm = min(512, s)
    tb = min(1024, s)
    tn = HP // 3
    xs = x.reshape(s, D)

    tr3 = lambda a: jnp.transpose(a, (0, 2, 1))
    w_in_s = tr3(w_in).astype(WIRE)
    proj_s = jnp.stack([w_proj_a, w_proj_b, w_out], axis=1).astype(WIRE)
    pool_s = w_pool_grp.astype(WIRE)

    g_in, g_up, g_gn = _gather_first(w_in_s[0], [w_alpha_up.astype(WIRE), gla_norm_g], name="gather_first")
    wup = jnp.pad(_from_devices(g_up, 2), ((0, 0), (0, AL_W - RANK), (0, 0)))
    gn = _from_devices(g_gn, 2).reshape(DEPTH, 1, D)

    saved, wt_all, proj_all, pool_all = [], [], [], []
    cur, cur_b = xs, xs.astype(MXU)
    g_proj = g_pool = None
    for l in range(DEPTH):
        wt = _h_weight_t(g_in)
        nxt_l = l + 1 < DEPTH
        steps = (HP // tn) * (s // tb)
        res = _in_proj(cur_b, wt, tm=tb, tn=tn, ride=_ChipGather(w_in_s[l + 1], (2 * steps) // 3) if nxt_l else None)
        hh = res[0]
        if nxt_l:
            g_in = res[1]
        layers = ([0] if l == 0 else []) + ([l + 1] if nxt_l else [])
        res = _gla_fwd(hh, wup[l], b_alpha[l:l + 1], gn[l], tt=tm,
                       ride=_Exchange([(a[j], True) for j in layers for a in (proj_s, pool_s)]) if layers else None)
        o, ya, states = res[:3]
        got = {j: res[3 + 2 * t:5 + 2 * t] for t, j in enumerate(layers)}
        if l == 0:
            g_proj, g_pool = got[0]
        proj = _from_devices(g_proj, 1)
        pool = _from_devices(g_pool, 1)
        if nxt_l:
            g_proj, g_pool = got[l + 1]
        wt_all.append(wt), proj_all.append(proj), pool_all.append(pool)
        pooled, yb = _pool_fwd(hh, pool, pool_scale[l:l + 1], tt=tm)
        r, nxt, nxt_b = _merge_fwd(hh, cur, ya, yb, proj[0], proj[1], proj[2],
                                   b_merge[l:l + 1], ln_g[l:l + 1], ln_b[l:l + 1], tt=tm)
        saved.append(dict(xb=cur_b, hh=hh, o=o, ya=ya, states=states, pooled=pooled, yb=yb, r=r))
        cur, cur_b = nxt, nxt_b

    loss_part, dcur = _loss_head(cur, loss_target.reshape(s, D), tt=tm)
    loss = lax.psum(loss_part[0, 0], ("x", "y", "c"))

    small = {k: [None] * DEPTH for k in ("w_up", "b_alpha", "gnorm", "pool_scale", "b_merge", "ln_g", "ln_b")}
    parts = {k: [None] * DEPTH for k in ("w_in", "proj", "pool")}
    for l in range(DEPTH - 1, -1, -1):
        sv = saved[l]
        hh = sv["hh"]
        dh, dr, dpa, dpb, dw_out, dln_g, dln_b, db_merge = _merge_bwd(
            hh, sv["r"], sv["ya"], sv["yb"], dcur, proj_all[l][0], proj_all[l][1], proj_all[l][2], b_merge[l:l + 1], ln_g[l:l + 1], tt=tt)
        dya, dyb, dw_pa, dw_pb = _proj_bwd(sv["ya"], sv["yb"], dpa, dpb, proj_all[l][0], proj_all[l][1], tt=tm)
        dh, dw_pool, dscale = _pool_bwd(hh, pool_all[l], pool_scale[l:l + 1], sv["pooled"], dyb, dh, tt=tm)
        ride = _Exchange([(_to_devices(jnp.stack([dw_pa, dw_pb, dw_out]), 1).astype(WIRE), False),
                          (_to_devices(dw_pool, 1).astype(WIRE), False)])
        dh, dz, dgn, db_al, parts["proj"][l], parts["pool"][l] = _gla_bwd(
            hh, wup[l], b_alpha[l:l + 1], gn[l], sv["o"], sv["states"], dya, dh, tt=tm, ride=ride)
        dh, dw_up = _alpha_grads(dz, wup[l], hh, dh, tm=tb)
        dwt = _mm_tn(dh, sv["xb"], tm=tb, tk=tn, name="w_in_grad", out_dtype=WIRE)
        dcur, parts["w_in"][l] = _in_proj_bwd(dh, wt_all[l], dr, tm=tt, ride=_Exchange([(_w_in_grad_parts_t(dwt), False)]))

        small["w_up"][l] = dw_up[:RANK]
        small["b_alpha"][l] = db_al.reshape(DK)
        small["gnorm"][l] = dgn.reshape(HEADS, HDV)
        small["pool_scale"][l] = dscale[0]
        small["b_merge"][l] = db_merge[0]
        small["ln_g"][l], small["ln_b"][l] = dln_g[0], dln_b[0]
    grad_x = dcur[None]
    sm = {k: jnp.stack(v) for k, v in small.items()}

    rep = (("b_alpha", b_alpha, m_b_alpha, v_b_alpha), ("pool_scale", pool_scale, m_pool_scale, v_pool_scale),
           ("b_merge", b_merge, m_b_merge, v_b_merge), ("ln_g", ln_g, m_ln_g, v_ln_g), ("ln_b", ln_b, m_ln_b, v_ln_b))
    cat = lambda arrs: jnp.concatenate(arrs, axis=1)
    p_up, p_gn, p_rep = _exchange([(_to_devices(sm["w_up"], 2), False), (_to_devices(sm["gnorm"], 2), False),
                                   (cat([sm[nm] for nm, _, _, _ in rep]), True)], name="exchange_small_grads")

    def update(p, w, m, v, tr, name, layered=True, tc=None):
        shape = w.shape
        nl = shape[0] if layered else 1
        cols = shape[-1]
        flat = lambda a: a.reshape(nl, -1, cols)
        outs = _adamw(p.reshape(nl, N_DEV, -1, cols), flat(w), flat(m), flat(v), tr=tr, tc=tc or cols, name=name)
        return [o_.reshape(shape) for o_ in outs]

    res = {}
    res["w_in"] = [tr3(o_) for o_ in _adamw_layers(parts["w_in"], tr3(w_in), tr3(m_w_in), tr3(v_w_in), tc=128, name="adamw_w_in")]
    proj_p = jnp.stack(parts["proj"])
    for j, (nm, w, m, v) in enumerate((("w_proj_a", w_proj_a, m_w_proj_a, v_w_proj_a), ("w_proj_b", w_proj_b, m_w_proj_b, v_w_proj_b),
                                       ("w_out", w_out, m_w_out, v_w_out))):
        res[nm] = update(proj_p[:, :, j], w, m, v, D // N_DEV, "adamw_" + nm)
    res["w_pool_grp"] = update(jnp.stack(parts["pool"]), w_pool_grp, m_w_pool_grp, v_w_pool_grp, 128, "adamw_w_pool")
    res["w_alpha_up"] = update(p_up, w_alpha_up, m_w_alpha_up, v_w_alpha_up, DEPTH * RANK, "adamw_w_up", layered=False)
    res["gla_norm_g"] = update(p_gn, gla_norm_g, m_gla_norm_g, v_gla_norm_g, DEPTH * HEADS, "adamw_gnorm", layered=False)
    rep_out = update(p_rep, cat([w for _, w, _, _ in rep]), cat([m for _, _, m, _ in rep]), cat([v for _, _, _, v in rep]),
                     DEPTH, "adamw_small", layered=False)
    off = 0
    for nm, w, _, _ in rep:
        n = w.shape[1]
        res[nm] = [o_[:, off:off + n] for o_ in rep_out]
        off += n

    order = ("w_in", "w_alpha_up", "b_alpha", "gla_norm_g", "w_pool_grp", "pool_scale", "b_merge", "w_proj_a", "w_proj_b",
             "w_out", "ln_g", "ln_b")
    return (loss, grad_x, *[res[n][0] for n in order], *[res[n][1] for n in order],
            *[res[n][2] for n in order], *[res[n][3] for n in order])
```

```python
import jax
import jax.numpy as jnp
from jax import lax
from jax.experimental import pallas as pl
from jax.experimental.pallas import tpu as pltpu

F32 = jnp.float32
MXU = jnp.bfloat16
WIRE = jnp.bfloat16

N_DEV = 8
DEPTH = 4
D = 1024
HEADS = 4
DK = D // 2
HDK = DK // HEADS
HDV = D // HEADS
RANK = 16
CHUNK = 64
GATE_TAU = 16.0
POOL_WINDOWS = (2, 4, 8, 16)
PG = D // len(POOL_WINDOWS)
HALO = 16
IN_COLS = 7184
SHARD = IN_COLS // N_DEV
ALPHA = (2.0 * DEPTH) ** 0.25
EPS = 1e-5
Q_SCALE = HDK ** -0.5

ADAM_LR, ADAM_B1, ADAM_B2, ADAM_EPS, ADAM_WD, ADAM_STEP = 0.001, 0.9, 0.999, 1e-08, 0.01, 10

PI0, GB0, ML0, AL0, AL_W = 0, D, 2 * D, 4 * D, 512
HD0 = AL0 + AL_W
HEAD_W = 2 * HDK + 2 * HDV
HP = HD0 + HEADS * HEAD_W
HPB = 2
O_Q, O_K, O_V, O_GA, O_AL, O_PI, O_GB, O_ML = 0, DK, 2 * DK, 2 * DK + D, 2 * DK + 2 * D, 2 * DK + 2 * D + RANK, \
    2 * DK + 3 * D + RANK, 2 * DK + 4 * D + RANK

VMEM_BIG = 56 * 1024 * 1024
VMEM_MID = 40 * 1024 * 1024

NN = ((1,), (0,))
NT = ((1,), (1,))
TN = ((0,), (0,))

HBM = pl.BlockSpec(memory_space=pltpu.HBM)
ANY = pl.BlockSpec(memory_space=pl.ANY)


def _dot(a, b, dims):
    return lax.dot_general(a.astype(MXU), b.astype(MXU), (dims, ((), ())), preferred_element_type=F32)


def _params(sem, vmem):
    return pltpu.CompilerParams(dimension_semantics=sem, vmem_limit_bytes=vmem)


def _sigmoid(x):
    return 1.0 / (1.0 + jnp.exp(-x))


def _log_sigmoid(z):
    return jnp.minimum(z, 0.0) - jnp.log(1.0 + jnp.exp(-jnp.abs(z)))


class _Exchange:
    def __init__(self, items):
        self.items = [(s, bool(g)) for s, g in items]
        self.n = len(self.items)
        self.srcs = [s for s, _ in self.items]
        self.in_specs = [HBM] * self.n
        self.out_specs = [HBM] * self.n
        self.out_shape = [jax.ShapeDtypeStruct((N_DEV,) + tuple(s.shape if g else s.shape[1:]), s.dtype) for s, g in self.items]
        self.scratch = [pltpu.SemaphoreType.DMA((self.n * (N_DEV - 1),)), pltpu.SemaphoreType.DMA((self.n * (N_DEV - 1),)),
                        pltpu.SemaphoreType.DMA((self.n,))]

    def copies(self, src_refs, out_refs, send_sems, recv_sems, local_sems):
        x, y, c = lax.axis_index("x"), lax.axis_index("y"), lax.axis_index("c")
        me = 4 * x + 2 * y + c
        copies = []
        for t, (_, gather) in enumerate(self.items):
            src_ref, out_ref = src_refs[t], out_refs[t]
            copies.append(pltpu.make_async_copy(src_ref if gather else src_ref.at[me], out_ref.at[me], local_sems.at[t]))
            for k in range(1, N_DEV):
                px = 1 - x if k & 4 else x
                py = 1 - y if k & 2 else y
                pc = 1 - c if k & 1 else c
                peer = 4 * px + 2 * py + pc
                sem = t * (N_DEV - 1) + k - 1
                copies.append(pltpu.make_async_remote_copy(
                    src_ref=src_ref if gather else src_ref.at[peer],
                    dst_ref=out_ref.at[me],
                    send_sem=send_sems.at[sem],
                    recv_sem=recv_sems.at[sem],
                    device_id=(px, py, pc),
                    device_id_type=pl.DeviceIdType.MESH,
                ))
        return copies


    mid_step = None

    def start(self, *refs):
        for cp in self.copies(*refs):
            cp.start()

    def finish(self, *refs):
        for cp in self.copies(*refs):
            cp.wait()


class _ChipGather:
    def __init__(self, src, mid_step):
        self.n = 1
        self.srcs = [src]
        self.mid_step = mid_step
        self.in_specs = [HBM]
        self.out_specs = [HBM]
        self.out_shape = [jax.ShapeDtypeStruct((N_DEV,) + tuple(src.shape), src.dtype)]
        self.scratch = [pltpu.SemaphoreType.DMA((N_DEV - 1,)), pltpu.SemaphoreType.DMA((N_DEV - 1,)), pltpu.SemaphoreType.DMA((1,))]

    def _plan(self, src_refs, out_refs, send_sems, recv_sems, local_sems):
        src_ref, out_ref = src_refs[0], out_refs[0]
        x, y, c = lax.axis_index("x"), lax.axis_index("y"), lax.axis_index("c")
        me, sibling = (x, y, c), (x, y, 1 - c)
        chips = [(1 - x, y), (x, 1 - y), (1 - x, 1 - y)]

        def copy(k, block, to, src=None):
            slot = out_ref.at[4 * block[0] + 2 * block[1] + block[2]]
            return pltpu.make_async_remote_copy(
                src_ref=slot if src is None else src, dst_ref=slot, send_sem=send_sems.at[k], recv_sem=recv_sems.at[k],
                device_id=to, device_id_type=pl.DeviceIdType.MESH)

        mine = pltpu.make_async_copy(src_ref, out_ref.at[4 * x + 2 * y + c], local_sems.at[0])
        first = [copy(0, me, sibling, src=src_ref)] + [copy(1 + j, me, (*chip, c), src=src_ref) for j, chip in enumerate(chips)]
        landed = [copy(1 + j, (*chip, c), me) for j, chip in enumerate(chips)]
        passed = [copy(4 + j, (*chip, c), sibling) for j, chip in enumerate(chips)]
        from_sibling = [copy(0, sibling, me)] + [copy(4 + j, (*chip, 1 - c), me) for j, chip in enumerate(chips)]
        return mine, first, landed, passed, from_sibling

    def start(self, *refs):
        mine, first, _, _, _ = self._plan(*refs)
        mine.start()
        for cp in first:
            cp.start()

    def mid(self, *refs):
        _, _, landed, passed, _ = self._plan(*refs)
        for got, fwd in zip(landed, passed):
            got.wait_recv()
            fwd.start()

    def finish(self, *refs):
        mine, first, _, passed, from_sibling = self._plan(*refs)
        for cp in from_sibling:
            cp.wait_recv()
        for cp in first + passed:
            cp.wait_send()
        mine.wait()


def _grid_ends(grid):
    first = last = step = None
    for a, n in enumerate(grid):
        f = pl.program_id(a) == 0
        e = pl.program_id(a) == n - 1
        first = f if first is None else first & f
        last = e if last is None else last & e
        step = pl.program_id(a) if step is None else step * n + pl.program_id(a)
    return first, last, step


def _call(body, *, name, grid, in_specs, out_specs, out_shape, args, scratch=(), sem=None, vmem=VMEM_MID, ride=None, aliases=None):
    n_in, n_out, n_scr = len(in_specs), len(out_specs), len(scratch)
    sem = sem or ("arbitrary",) * len(grid)
    if ride is None:
        return pl.pallas_call(body, name=name, grid=grid, in_specs=in_specs, out_specs=out_specs, out_shape=out_shape,
                              scratch_shapes=list(scratch), compiler_params=_params(sem, vmem),
                              input_output_aliases=aliases or {})(*args)
    r = ride.n

    def riding(*refs):
        ins, rsrc = refs[:n_in], refs[n_in:n_in + r]
        outs, rout = refs[n_in + r:n_in + r + n_out], refs[n_in + r + n_out:n_in + 2 * r + n_out]
        scr = refs[n_in + 2 * r + n_out:n_in + 2 * r + n_out + n_scr]
        send_sems, recv_sems, local_sems = refs[n_in + 2 * r + n_out + n_scr:]
        first, last, step = _grid_ends(grid)
        comm = (rsrc, rout, send_sems, recv_sems, local_sems)

        @pl.when(first)
        def _():
            ride.start(*comm)

        body(*ins, *outs, *scr)

        if ride.mid_step is not None:
            @pl.when(step == ride.mid_step)
            def _():
                ride.mid(*comm)

        @pl.when(last)
        def _():
            ride.finish(*comm)

    return pl.pallas_call(riding, name=name, grid=grid, in_specs=list(in_specs) + ride.in_specs,
                          out_specs=list(out_specs) + ride.out_specs, out_shape=list(out_shape) + ride.out_shape,
                          scratch_shapes=list(scratch) + ride.scratch,
                          compiler_params=_params(("arbitrary",) * len(grid), vmem),
                          input_output_aliases=aliases or {})(*args, *ride.srcs)


def _exchange(items, *, name):
    ex = _Exchange(items)

    def body(*refs):
        copies = ex.copies(refs[:ex.n], refs[ex.n:2 * ex.n], *refs[2 * ex.n:])
        for cp in copies:
            cp.start()
        for cp in copies:
            cp.wait()

    return pl.pallas_call(body, name=name, in_specs=ex.in_specs, out_specs=ex.out_specs, out_shape=ex.out_shape,
                          scratch_shapes=ex.scratch)(*ex.srcs)


def _gather_first(big, smalls, *, name):
    ex = _Exchange([(a, True) for a in smalls])

    def body(*refs):
        big_ref, small_src = refs[0], refs[1:1 + ex.n]
        out_ref, small_out = refs[1 + ex.n], refs[2 + ex.n:2 + 2 * ex.n]
        send_sems, recv_sems, local_sem = refs[2 + 2 * ex.n:5 + 2 * ex.n]
        x, y, c = lax.axis_index("x"), lax.axis_index("y"), lax.axis_index("c")
        me, sibling = (x, y, c), (x, y, 1 - c)
        chips = [(1 - x, y), (x, 1 - y), (1 - x, 1 - y)]

        def slot(px, py, pc):
            return out_ref.at[4 * px + 2 * py + pc]

        def copy(k, block, to, src=None):
            return pltpu.make_async_remote_copy(
                src_ref=slot(*block) if src is None else src, dst_ref=slot(*block),
                send_sem=send_sems.at[k], recv_sem=recv_sems.at[k], device_id=to, device_id_type=pl.DeviceIdType.MESH)

        small = ex.copies(small_src, small_out, *refs[5 + 2 * ex.n:])
        mine = pltpu.make_async_copy(big_ref, slot(*me), local_sem)
        mine.start()
        first = [copy(0, me, sibling, src=big_ref)] + [copy(1 + j, me, (*chip, c), src=big_ref) for j, chip in enumerate(chips)]
        for cp in first + small:
            cp.start()
        passed = [copy(4 + j, (*chip, c), sibling) for j, chip in enumerate(chips)]
        for j, chip in enumerate(chips):
            copy(1 + j, (*chip, c), me).wait_recv()
            passed[j].start()
        copy(0, sibling, me).wait_recv()
        for j, chip in enumerate(chips):
            copy(4 + j, (*chip, 1 - c), me).wait_recv()
        for cp in first + passed:
            cp.wait_send()
        mine.wait()
        for cp in small:
            cp.wait()

    return pl.pallas_call(
        body, name=name, in_specs=[HBM] + ex.in_specs, out_specs=[HBM] + ex.out_specs,
        out_shape=[jax.ShapeDtypeStruct((N_DEV,) + tuple(big.shape), big.dtype)] + ex.out_shape,
        scratch_shapes=[pltpu.SemaphoreType.DMA((N_DEV - 1,)), pltpu.SemaphoreType.DMA((N_DEV - 1,)), pltpu.SemaphoreType.DMA]
        + ex.scratch)(big, *ex.srcs)


def _move_rows(src, segs, out_shape, *, name, zero=None):
    cols = src.shape[-1]
    step = 256

    def at(shape, row, m):
        return (slice(row, row + m),) if len(shape) == 2 else (row // SHARD, slice(row % SHARD, row % SHARD + m))

    def room(shape, row):
        return step if len(shape) == 2 else SHARD - row % SHARD

    def body(src_ref, out_ref):
        for s0, d0, n in segs:
            r = 0
            while r < n:
                m = min(step, n - r, room(src.shape, s0 + r), room(out_shape, d0 + r))
                out_ref[(*at(out_shape, d0 + r, m), slice(None))] = src_ref[(*at(src.shape, s0 + r, m), slice(None))]
                r += m
        if zero is not None:
            out_ref[zero[0]:zero[0] + zero[1], :] = jnp.zeros((zero[1], cols), src.dtype)

    vmem = pl.BlockSpec(memory_space=pltpu.VMEM)
    return pl.pallas_call(
        body, name=name, in_specs=[vmem], out_specs=vmem, out_shape=jax.ShapeDtypeStruct(tuple(out_shape), src.dtype),
        compiler_params=pltpu.CompilerParams(vmem_limit_bytes=VMEM_BIG))(src)


def _in_proj(xb, wt, *, tm, tn, ride=None):
    m, k = xb.shape
    n = wt.shape[0]

    def body(x_ref, w_ref, o_ref):
        o_ref[...] = _dot(x_ref[...], w_ref[...], NT)

    return _call(
        body, name="in_proj", grid=(n // tn, m // tm),
        in_specs=[pl.BlockSpec((tm, k), lambda j, i: (i, 0)), pl.BlockSpec((tn, k), lambda j, i: (j, 0))],
        out_specs=[pl.BlockSpec((tm, tn), lambda j, i: (i, j))],
        out_shape=[jax.ShapeDtypeStruct((m, n), F32)],
        args=(xb, wt), sem=("parallel", "parallel"), vmem=VMEM_BIG, ride=ride)


def _in_proj_bwd(dh, wt, dr, *, tm, ride=None):
    m, n = dh.shape
    k = wt.shape[1]

    def body(dh_ref, w_ref, dr_ref, o_ref):
        o_ref[...] = ALPHA * dr_ref[...] + _dot(dh_ref[...], w_ref[...], NN)

    return _call(
        body, name="in_proj_bwd", grid=(m // tm,),
        in_specs=[pl.BlockSpec((tm, n), lambda i: (i, 0)),
                  pl.BlockSpec((n, k), lambda i: (0, 0), pipeline_mode=pl.Buffered(1)),
                  pl.BlockSpec((tm, k), lambda i: (i, 0))],
        out_specs=[pl.BlockSpec((tm, k), lambda i: (i, 0))],
        out_shape=[jax.ShapeDtypeStruct((m, k), F32)],
        args=(dh, wt, dr), sem=("parallel",), vmem=VMEM_BIG, ride=ride)


def _mm_tn(a, dc, *, tm, tk, name, out_dtype=F32):
    m, k = a.shape
    n = dc.shape[1]
    ni = m // tm

    def body(a_ref, dc_ref, o_ref, acc):
        i = pl.program_id(1)

        @pl.when(i == 0)
        def _():
            acc[...] = jnp.zeros_like(acc)

        acc[...] += _dot(a_ref[...], dc_ref[...], TN)

        @pl.when(i == ni - 1)
        def _():
            o_ref[...] = acc[...].astype(o_ref.dtype)

    return _call(
        body, name=name, grid=(k // tk, ni),
        in_specs=[pl.BlockSpec((tm, tk), lambda j, i: (i, j)), pl.BlockSpec((tm, n), lambda j, i: (i, 0))],
        out_specs=[pl.BlockSpec((tk, n), lambda j, i: (j, 0))],
        out_shape=[jax.ShapeDtypeStruct((k, n), out_dtype)],
        scratch=[pltpu.VMEM((tk, n), F32)],
        args=(a, dc), sem=("parallel", "arbitrary"), vmem=VMEM_BIG)[0]


def _head_cols(p):
    b = p * HEAD_W
    return (slice(b, b + HDK), slice(b + HDK, b + 2 * HDK), slice(b + 2 * HDK, b + 2 * HDK + HDV),
            slice(b + 2 * HDK + HDV, b + HEAD_W))


def _seg_cumsum(v, reverse=False):
    t, w = v.shape
    hi = v.astype(MXU)
    lo = (v - hi.astype(F32)).astype(MXU)
    terms = jnp.concatenate([hi, lo], axis=1)
    row = lax.broadcasted_iota(jnp.int32, (CHUNK, CHUNK), 0)
    col = lax.broadcasted_iota(jnp.int32, (CHUNK, CHUNK), 1)
    ones = jnp.where((row <= col) if reverse else (row >= col), 1.0, 0.0).astype(MXU)
    out = []
    for c in range(t // CHUNK):
        y = _dot(ones, terms[c * CHUNK:(c + 1) * CHUNK], NN)
        out.append(y[:, :w] + y[:, w:])
    return jnp.concatenate(out, axis=0)


def _seg_rcumsum_rolls(v):
    t = v.shape[0]
    rowmod = lax.broadcasted_iota(jnp.int32, v.shape, 0) % CHUNK
    sh = 1
    while sh < CHUNK:
        v = v + jnp.where(rowmod < CHUNK - sh, pltpu.roll(v, t - sh, 0), 0.0)
        sh *= 2
    return v


def _gla_decay(alpha_ref, wup_ref, b_ref, g_scr):
    z = _dot(alpha_ref[...], wup_ref[...], NN) + b_ref[...]
    g_scr[...] = _seg_cumsum(_log_sigmoid(z) * (1.0 / GATE_TAU))
    return z


QE1, KE1, QE2, KE2, QA, KD = range(6)
EP, EM, EA, EDL = range(4)
GW = HPB * HDK


def _gla_operands(hd_ref, g_scr, opnd_scr, fac_scr=None):
    t = g_scr.shape[0]

    def chunk_row(r):
        return jnp.concatenate([jnp.broadcast_to(g_scr[c * CHUNK + r:c * CHUNK + r + 1, :], (CHUNK, GW))
                                for c in range(t // CHUNK)], axis=0)

    g = g_scr[...]
    g_last = chunk_row(CHUNK - 1)
    ref = 0.5 * (chunk_row(0) + g_last)
    fac = {EP: jnp.exp(g - ref), EM: jnp.exp(ref - g), EA: jnp.exp(g), EDL: jnp.exp(g_last - g)}
    if fac_scr is not None:
        for j, f in fac.items():
            fac_scr[j] = f
    for p in range(HPB):
        qc, kc, _, _ = _head_cols(p)
        gc = slice(p * HDK, (p + 1) * HDK)
        qs = hd_ref[:, qc] * Q_SCALE
        k = hd_ref[:, kc]
        for j, (x, f) in {QE1: (qs, EP), KE1: (k, EM), QE2: (qs, EM), KE2: (k, EP), QA: (qs, EA), KD: (k, EDL)}.items():
            opnd_scr[j, :, gc] = (x * fac[f][:, gc]).astype(opnd_scr.dtype)


def _lower_mask():
    return lax.broadcasted_iota(jnp.int32, (CHUNK, CHUNK), 0) >= lax.broadcasted_iota(jnp.int32, (CHUNK, CHUNK), 1)


def _scores(opnd_scr, rows, gc, lower):
    return jnp.where(lower, _dot(opnd_scr[QE1, rows, gc], opnd_scr[KE1, rows, gc], NT),
                     _dot(opnd_scr[QE2, rows, gc], opnd_scr[KE2, rows, gc], NT))


def _gla_specs(tt, row):
    return [
        pl.BlockSpec((tt, HPB * HEAD_W), lambda h, i: (row(i), HD0 // (HPB * HEAD_W) + h)),
        pl.BlockSpec((tt, 128), lambda h, i: (row(i), AL0 // 128)),
        pl.BlockSpec((128, HPB * HDK), lambda h, i: (0, h)),
        pl.BlockSpec((1, HPB * HDK), lambda h, i: (0, h)),
        pl.BlockSpec((1, HPB * HDV), lambda h, i: (0, h)),
    ]


def _gla_fwd(hh, wup, b_alpha, gnorm, *, tt, ride=None):
    s = hh.shape[0]
    nt = s // tt
    nct = tt // CHUNK

    def body(hd_ref, al_ref, wup_ref, b_ref, gn_ref, o_ref, ya_ref, st_ref, state, g_scr, opnd_scr):
        @pl.when(pl.program_id(1) == 0)
        def _():
            state[...] = jnp.zeros_like(state)

        _gla_decay(al_ref, wup_ref, b_ref, g_scr)
        _gla_operands(hd_ref, g_scr, opnd_scr)
        lower = _lower_mask()
        for c in range(nct):
            rows = slice(c * CHUNK, (c + 1) * CHUNK)
            for p in range(HPB):
                vc = _head_cols(p)[2]
                gc = slice(p * HDK, (p + 1) * HDK)
                v = hd_ref[rows, vc]
                st = state[p]
                st_ref[p, c] = st
                egl = jnp.exp(g_scr[(c + 1) * CHUNK - 1:(c + 1) * CHUNK, gc])
                o_ref[rows, p * HDV:(p + 1) * HDV] = (_dot(_scores(opnd_scr, rows, gc, lower), v, NN)
                                                      + _dot(opnd_scr[QA, rows, gc], st, NT))
                state[p] = st * egl + _dot(v, opnd_scr[KD, rows, gc], TN)
        for p in range(HPB):
            oc = slice(p * HDV, (p + 1) * HDV)
            o = o_ref[:, oc]
            ohat = o * lax.rsqrt(jnp.mean(o * o, axis=-1, keepdims=True) + EPS)
            ga = hd_ref[:, _head_cols(p)[3]]
            ya_ref[:, oc] = (ohat * gn_ref[:, oc] * (ga * _sigmoid(ga))).astype(ya_ref.dtype)

    return _call(
        body, name="gla_fwd", grid=(HEADS // HPB, nt),
        in_specs=_gla_specs(tt, lambda i: i),
        out_specs=[
            pl.BlockSpec((tt, HPB * HDV), lambda h, i: (i, h)),
            pl.BlockSpec((tt, HPB * HDV), lambda h, i: (i, h)),
            pl.BlockSpec((HPB, nct, HDV, HDK), lambda h, i: (h, i, 0, 0)),
        ],
        out_shape=[
            jax.ShapeDtypeStruct((s, D), F32),
            jax.ShapeDtypeStruct((s, D), MXU),
            jax.ShapeDtypeStruct((HEADS, s // CHUNK, HDV, HDK), F32),
        ],
        scratch=[pltpu.VMEM((HPB, HDV, HDK), F32), pltpu.VMEM((tt, GW), F32), pltpu.VMEM((6, tt, GW), MXU)],
        args=(hh, hh, wup, b_alpha, gnorm), ride=ride, vmem=VMEM_BIG)


def _gla_bwd(hh, wup, b_alpha, gnorm, o, states, dya, dh, *, tt, ride=None):
    s = hh.shape[0]
    nt = s // tt
    nct = tt // CHUNK

    def body(hd_ref, al_ref, wup_ref, b_ref, gn_ref, o_ref, st_ref, dya_ref, _dh_in,
             dh_ref, dz_ref, dgn_ref, db_ref, dstate, g_scr, dg_scr, do_scr, opnd_scr, fac_scr, res_scr, dgl_scr):
        @pl.when(pl.program_id(1) == 0)
        def _():
            dstate[...] = jnp.zeros_like(dstate)
            dgn_ref[...] = jnp.zeros_like(dgn_ref)
            db_ref[...] = jnp.zeros_like(db_ref)

        z = _gla_decay(al_ref, wup_ref, b_ref, g_scr)
        _gla_operands(hd_ref, g_scr, opnd_scr, fac_scr)

        for p in range(HPB):
            oc = slice(p * HDV, (p + 1) * HDV)
            gac = _head_cols(p)[3]
            o_t = o_ref[:, oc]
            rstd = lax.rsqrt(jnp.mean(o_t * o_t, axis=-1, keepdims=True) + EPS)
            ohat = o_t * rstd
            ga = hd_ref[:, gac]
            sg = _sigmoid(ga)
            dya_t = dya_ref[:, oc]
            gn = gn_ref[:, oc]
            dh_ref[:, gac] = (dya_t * ohat * gn * (sg * (1.0 + ga * (1.0 - sg)))).astype(dh_ref.dtype)
            don = dya_t * (ga * sg)
            dgn_ref[p] += jnp.sum(don * ohat, axis=0, keepdims=True)
            dohat = don * gn
            do_scr[:, oc] = rstd * (dohat - ohat * jnp.mean(dohat * ohat, axis=-1, keepdims=True))

        lower = _lower_mask()
        for c in range(nct - 1, -1, -1):
            rows = slice(c * CHUNK, (c + 1) * CHUNK)
            for p in range(HPB):
                vc = _head_cols(p)[2]
                gc = slice(p * HDK, (p + 1) * HDK)
                v = hd_ref[rows, vc]
                do = do_scr[rows, p * HDV:(p + 1) * HDV]
                st = st_ref[p, c]
                dst = dstate[p]
                egl = jnp.exp(g_scr[(c + 1) * CHUNK - 1:(c + 1) * CHUNK, gc])
                a = _scores(opnd_scr, rows, gc, lower)
                da = _dot(do, v, NT)
                da1 = jnp.where(lower, da, 0.0).astype(MXU)
                da2 = jnp.where(lower, 0.0, da).astype(MXU)
                res_scr[0, rows, gc] = _dot(da1, opnd_scr[KE1, rows, gc], NN)
                res_scr[1, rows, gc] = _dot(da1, opnd_scr[QE1, rows, gc], TN)
                res_scr[2, rows, gc] = _dot(da2, opnd_scr[KE2, rows, gc], NN)
                res_scr[3, rows, gc] = _dot(da2, opnd_scr[QE2, rows, gc], TN)
                res_scr[4, rows, gc] = _dot(do, st, NN)
                res_scr[5, rows, gc] = _dot(v, dst, NN)
                dh_ref[rows, vc] = (_dot(a, do, TN) + _dot(opnd_scr[KD, rows, gc], dst, NT)).astype(dh_ref.dtype)
                dgl_scr[c:c + 1, gc] = egl * jnp.sum(dst * st, axis=0, keepdims=True)
                dstate[p] = dst * egl + _dot(do, opnd_scr[QA, rows, gc], TN)

        p1, p2, p3 = res_scr[0] * fac_scr[EP], res_scr[2] * fac_scr[EM], res_scr[4] * fac_scr[EA]
        r1, r2, r3 = res_scr[1] * fac_scr[EM], res_scr[3] * fac_scr[EP], res_scr[5] * fac_scr[EDL]
        dq = (p1 + p2 + p3) * Q_SCALE
        dk = r1 + r2 + r3
        dgq = p1 - p2 + p3
        dgk = r2 - r1
        for p in range(HPB):
            qc, kc, _, _ = _head_cols(p)
            gc = slice(p * HDK, (p + 1) * HDK)
            dh_ref[:, qc] = dq[:, gc].astype(dh_ref.dtype)
            dh_ref[:, kc] = dk[:, gc].astype(dh_ref.dtype)
            k = hd_ref[:, kc]
            r3k = r3[:, gc] * k
            dg_scr[:, gc] = (hd_ref[:, qc] * Q_SCALE) * dgq[:, gc] + k * dgk[:, gc] - r3k
            for c in range(nct):
                last = slice((c + 1) * CHUNK - 1, (c + 1) * CHUNK)
                dg_scr[last, gc] += jnp.sum(r3k[c * CHUNK:(c + 1) * CHUNK], axis=0, keepdims=True) + dgl_scr[c:c + 1, gc]

        dz = _seg_rcumsum_rolls(dg_scr[...]) * _sigmoid(-z) * (1.0 / GATE_TAU)
        dz_ref[...] = dz.astype(dz_ref.dtype)
        for p in range(HPB):
            db_ref[p] += jnp.sum(dz[:, p * HDK:(p + 1) * HDK], axis=0, keepdims=True)

    rev = lambda i: nt - 1 - i
    in_specs = _gla_specs(tt, rev) + [
        pl.BlockSpec((tt, HPB * HDV), lambda h, i: (rev(i), h)),
        pl.BlockSpec((HPB, nct, HDV, HDK), lambda h, i: (h, rev(i), 0, 0)),
        pl.BlockSpec((tt, HPB * HDV), lambda h, i: (rev(i), h)),
        ANY,
    ]
    return _call(
        body, name="gla_bwd", grid=(HEADS // HPB, nt), in_specs=in_specs,
        out_specs=[
            pl.BlockSpec((tt, HPB * HEAD_W), lambda h, i: (rev(i), HD0 // (HPB * HEAD_W) + h)),
            pl.BlockSpec((tt, HPB * HDK), lambda h, i: (rev(i), h)),
            pl.BlockSpec((HPB, 1, HDV), lambda h, i: (h, 0, 0)),
            pl.BlockSpec((HPB, 1, HDK), lambda h, i: (h, 0, 0)),
        ],
        out_shape=[
            jax.ShapeDtypeStruct(dh.shape, dh.dtype),
            jax.ShapeDtypeStruct((s, DK), MXU),
            jax.ShapeDtypeStruct((HEADS, 1, HDV), F32),
            jax.ShapeDtypeStruct((HEADS, 1, HDK), F32),
        ],
        scratch=[pltpu.VMEM((HPB, HDV, HDK), F32), pltpu.VMEM((tt, GW), F32), pltpu.VMEM((tt, GW), F32),
                 pltpu.VMEM((tt, HPB * HDV), F32), pltpu.VMEM((6, tt, GW), MXU), pltpu.VMEM((4, tt, GW), F32),
                 pltpu.VMEM((6, tt, GW), F32), pltpu.VMEM((max(nct, 8), GW), F32)],
        args=(hh, hh, wup, b_alpha, gnorm, o, states, dya, dh), ride=ride, aliases={8: 0}, vmem=VMEM_BIG)


def _window_count(tile, tt, w):
    pos = tile * tt + lax.broadcasted_iota(jnp.int32, (tt, PG), 0) + 1
    return jnp.minimum(pos, w).astype(F32)


def _pool_fwd(hh, wpool, scale, *, tt):
    s = hh.shape[0]
    nt = s // tt

    def body(ug_ref, w_ref, sc_ref, pooled_ref, yb_ref, halo):
        i = pl.program_id(0)

        @pl.when(i == 0)
        def _():
            halo[...] = jnp.zeros_like(halo)

        for g, w in enumerate(POOL_WINDOWS):
            cols = slice(g * PG, (g + 1) * PG)
            u = ug_ref[:, cols]
            run = jnp.concatenate([halo[:, cols], u], axis=0)
            sh = 1
            while sh < w:
                run = run + pltpu.roll(run, sh, 0)
                sh *= 2
            pooled = run[HALO:, :] / _window_count(i, tt, w) - u
            pooled_ref[:, cols] = pooled.astype(pooled_ref.dtype)
            mixed = _dot(pooled, w_ref[g], NN)
            gb = ug_ref[:, slice(D + g * PG, D + (g + 1) * PG)]
            yb_ref[:, cols] = (mixed * sc_ref[:, cols] * (gb * _sigmoid(gb))).astype(yb_ref.dtype)
        halo[...] = ug_ref[tt - HALO:tt, :D]

    tile = pl.BlockSpec((tt, D), lambda i: (i, 0))
    return _call(
        body, name="pool_fwd", grid=(nt,),
        in_specs=[
            pl.BlockSpec((tt, 2 * D), lambda i: (i, PI0 // (2 * D))),
            pl.BlockSpec((len(POOL_WINDOWS), PG, PG), lambda i: (0, 0, 0)),
            pl.BlockSpec((1, D), lambda i: (0, 0)),
        ],
        out_specs=[tile] * 2,
        out_shape=[jax.ShapeDtypeStruct((s, D), MXU), jax.ShapeDtypeStruct((s, D), MXU)],
        scratch=[pltpu.VMEM((HALO, D), F32)], args=(hh, wpool, scale))


def _pool_bwd(hh, wpool, scale, pooled, dyb, dh, *, tt):
    s = hh.shape[0]
    nt = s // tt

    def body(gb_ref, w_ref, sc_ref, pooled_ref, dyb_ref, _dh_in, dh_ref, dw_ref, dsc_ref, halo):
        i = pl.program_id(0)
        tile = nt - 1 - i

        @pl.when(i == 0)
        def _():
            halo[...] = jnp.zeros_like(halo)
            dw_ref[...] = jnp.zeros_like(dw_ref)
            dsc_ref[...] = jnp.zeros_like(dsc_ref)

        for g, w in enumerate(POOL_WINDOWS):
            cols = slice(g * PG, (g + 1) * PG)
            gcols = slice(D + g * PG, D + (g + 1) * PG)
            gb = gb_ref[:, cols]
            sg = _sigmoid(gb)
            pooled = pooled_ref[:, cols]
            mixed = _dot(pooled, w_ref[g], NN)
            sc = sc_ref[:, cols]
            dyb = dyb_ref[:, cols]
            dh_ref[:, gcols] = (dyb * mixed * sc * (sg * (1.0 + gb * (1.0 - sg)))).astype(dh_ref.dtype)
            dms = dyb * (gb * sg)
            dsc_ref[:, cols] += jnp.sum(dms * mixed, axis=0, keepdims=True)
            dmixed = dms * sc
            dpooled = _dot(dmixed, w_ref[g], NT)
            dw_ref[g] += _dot(pooled, dmixed, TN)
            e = dpooled / _window_count(tile, tt, w)
            run = jnp.concatenate([e, halo[:, cols]], axis=0)
            sh = 1
            while sh < w:
                run = run + pltpu.roll(run, tt + HALO - sh, 0)
                sh *= 2
            dh_ref[:, cols] = (run[:tt, :] - dpooled).astype(dh_ref.dtype)
            halo[:, cols] = e[:HALO, :]

    rev = lambda i: nt - 1 - i
    tile = pl.BlockSpec((tt, D), lambda i: (rev(i), 0))
    wspec = pl.BlockSpec((len(POOL_WINDOWS), PG, PG), lambda i: (0, 0, 0))
    vec = pl.BlockSpec((1, D), lambda i: (0, 0))
    return _call(
        body, name="pool_bwd", grid=(nt,),
        in_specs=[pl.BlockSpec((tt, D), lambda i: (rev(i), GB0 // D)), wspec, vec, tile, tile, ANY],
        out_specs=[pl.BlockSpec((tt, 2 * D), lambda i: (rev(i), PI0 // (2 * D))), wspec, vec],
        out_shape=[jax.ShapeDtypeStruct(dh.shape, dh.dtype), jax.ShapeDtypeStruct((len(POOL_WINDOWS), PG, PG), F32),
                   jax.ShapeDtypeStruct((1, D), F32)],
        scratch=[pltpu.VMEM((HALO, D), F32)], args=(hh, wpool, scale, pooled, dyb, dh), aliases={5: 0})


def _merge_fwd(hh, x, ya, yb, wpa, wpb, wout, b_merge, ln_g, ln_b, *, tt):
    s = x.shape[0]

    def body(ml_ref, x_ref, ya_ref, yb_ref, wpa_ref, wpb_ref, wout_ref, bm_ref, g_ref, b_ref, r_ref, xn_ref, xnb_ref):
        pa = _dot(ya_ref[...], wpa_ref[...], NN)
        pb = _dot(yb_ref[...], wpb_ref[...], NN)
        merged = _sigmoid(ml_ref[:, :D] + bm_ref[:, :D]) * pa + _sigmoid(ml_ref[:, D:] + bm_ref[:, D:]) * pb
        r = ALPHA * x_ref[...] + _dot(merged, wout_ref[...], NN)
        r_ref[...] = r
        mu = jnp.mean(r, axis=-1, keepdims=True)
        xc = r - mu
        var = jnp.mean(xc * xc, axis=-1, keepdims=True)
        xn = xc * lax.rsqrt(var + EPS) * g_ref[...] + b_ref[...]
        xn_ref[...] = xn
        xnb_ref[...] = xn.astype(xnb_ref.dtype)

    tile = pl.BlockSpec((tt, D), lambda i: (i, 0))
    full = pl.BlockSpec((D, D), lambda i: (0, 0), pipeline_mode=pl.Buffered(1))
    vec = pl.BlockSpec((1, D), lambda i: (0, 0))
    return _call(
        body, name="merge_fwd", grid=(s // tt,),
        in_specs=[pl.BlockSpec((tt, 2 * D), lambda i: (i, ML0 // (2 * D))), tile, tile, tile, full, full, full,
                  pl.BlockSpec((1, 2 * D), lambda i: (0, 0)), vec, vec],
        out_specs=[tile] * 3, out_shape=[jax.ShapeDtypeStruct((s, D), F32)] * 2 + [jax.ShapeDtypeStruct((s, D), MXU)],
        args=(hh, x, ya, yb, wpa, wpb, wout, b_merge, ln_g, ln_b), sem=("parallel",), vmem=VMEM_BIG)


def _merge_bwd(hh, r, ya, yb, dout, wpa, wpb, wout, b_merge, ln_g, *, tt):
    s = r.shape[0]

    def body(ml_ref, r_ref, ya_ref, yb_ref, do_ref, wpa_ref, wpb_ref, wout_ref, bm_ref, g_ref,
             dh_ref, dr_ref, dpa_ref, dpb_ref, dwout_ref, dg_ref, db_ref, dbm_ref):
        @pl.when(pl.program_id(0) == 0)
        def _():
            dwout_ref[...] = jnp.zeros_like(dwout_ref)
            dg_ref[...] = jnp.zeros_like(dg_ref)
            db_ref[...] = jnp.zeros_like(db_ref)
            dbm_ref[...] = jnp.zeros_like(dbm_ref)

        rr = r_ref[...]
        mu = jnp.mean(rr, axis=-1, keepdims=True)
        xc = rr - mu
        rstd = lax.rsqrt(jnp.mean(xc * xc, axis=-1, keepdims=True) + EPS)
        xhat = xc * rstd
        do = do_ref[...]
        dg_ref[...] += jnp.sum(do * xhat, axis=0, keepdims=True)
        db_ref[...] += jnp.sum(do, axis=0, keepdims=True)
        dxh = do * g_ref[...]
        dr = rstd * (dxh - jnp.mean(dxh, axis=-1, keepdims=True) - xhat * jnp.mean(dxh * xhat, axis=-1, keepdims=True))
        dr_ref[...] = dr
        g_a = _sigmoid(ml_ref[:, :D] + bm_ref[:, :D])
        g_b = _sigmoid(ml_ref[:, D:] + bm_ref[:, D:])
        pa = _dot(ya_ref[...], wpa_ref[...], NN)
        pb = _dot(yb_ref[...], wpb_ref[...], NN)
        dwout_ref[...] += _dot(g_a * pa + g_b * pb, dr, TN)
        dm = _dot(dr, wout_ref[...], NT)
        dpa_ref[...] = (dm * g_a).astype(dpa_ref.dtype)
        dpb_ref[...] = (dm * g_b).astype(dpb_ref.dtype)
        dml_a = dm * pa * g_a * (1.0 - g_a)
        dml_b = dm * pb * g_b * (1.0 - g_b)
        dh_ref[:, :D] = dml_a.astype(dh_ref.dtype)
        dh_ref[:, D:] = dml_b.astype(dh_ref.dtype)
        dbm_ref[:, :D] += jnp.sum(dml_a, axis=0, keepdims=True)
        dbm_ref[:, D:] += jnp.sum(dml_b, axis=0, keepdims=True)

    tile = pl.BlockSpec((tt, D), lambda i: (i, 0))
    full = pl.BlockSpec((D, D), lambda i: (0, 0))
    vec = pl.BlockSpec((1, D), lambda i: (0, 0))
    vec2 = pl.BlockSpec((1, 2 * D), lambda i: (0, 0))
    mlb = pl.BlockSpec((tt, 2 * D), lambda i: (i, ML0 // (2 * D)))
    return _call(
        body, name="merge_bwd", grid=(s // tt,),
        in_specs=[mlb, tile, tile, tile, tile, full, full, full, vec2, vec],
        out_specs=[mlb, tile, tile, tile, full, vec, vec, vec2],
        out_shape=[
            jax.ShapeDtypeStruct((s, HP), MXU), jax.ShapeDtypeStruct((s, D), F32),
            jax.ShapeDtypeStruct((s, D), MXU), jax.ShapeDtypeStruct((s, D), MXU),
            jax.ShapeDtypeStruct((D, D), F32), jax.ShapeDtypeStruct((1, D), F32),
            jax.ShapeDtypeStruct((1, D), F32), jax.ShapeDtypeStruct((1, 2 * D), F32),
        ],
        args=(hh, r, ya, yb, dout, wpa, wpb, wout, b_merge, ln_g), vmem=VMEM_BIG)


def _proj_bwd(ya, yb, dpa, dpb, wpa, wpb, *, tt):
    s = ya.shape[0]

    def body(ya_ref, yb_ref, dpa_ref, dpb_ref, wpa_ref, wpb_ref, dya_ref, dyb_ref, dwa_ref, dwb_ref):
        @pl.when(pl.program_id(0) == 0)
        def _():
            dwa_ref[...] = jnp.zeros_like(dwa_ref)
            dwb_ref[...] = jnp.zeros_like(dwb_ref)

        for y_ref, dp_ref, w_ref, dy_ref, dw_ref in ((ya_ref, dpa_ref, wpa_ref, dya_ref, dwa_ref),
                                                     (yb_ref, dpb_ref, wpb_ref, dyb_ref, dwb_ref)):
            dp = dp_ref[...]
            dy_ref[...] = _dot(dp, w_ref[...], NT)
            dw_ref[...] += _dot(y_ref[...], dp, TN)

    tile = pl.BlockSpec((tt, D), lambda i: (i, 0))
    full = pl.BlockSpec((D, D), lambda i: (0, 0))
    return _call(
        body, name="proj_bwd", grid=(s // tt,), in_specs=[tile] * 4 + [full] * 2, out_specs=[tile, tile, full, full],
        out_shape=[jax.ShapeDtypeStruct((s, D), F32)] * 2 + [jax.ShapeDtypeStruct((D, D), F32)] * 2,
        args=(ya, yb, dpa, dpb, wpa, wpb), vmem=VMEM_BIG)


def _alpha_grads(dz, wup, hh, dh, *, tm):
    s = dz.shape[0]

    def body(dz_ref, w_ref, al_ref, _dh_in, dh_ref, dw_ref):
        @pl.when(pl.program_id(0) == 0)
        def _():
            dw_ref[...] = jnp.zeros_like(dw_ref)

        dz_t = dz_ref[...]
        dh_ref[...] = _dot(dz_t, w_ref[...], NT).astype(dh_ref.dtype)
        dw_ref[...] += _dot(al_ref[...], dz_t, TN)

    return _call(
        body, name="alpha_grads", grid=(s // tm,),
        in_specs=[pl.BlockSpec((tm, DK), lambda i: (i, 0)), pl.BlockSpec((AL_W, DK), lambda i: (0, 0)),
                  pl.BlockSpec((tm, 128), lambda i: (i, AL0 // 128)), ANY],
        out_specs=[pl.BlockSpec((tm, AL_W), lambda i: (i, AL0 // AL_W)), pl.BlockSpec((128, DK), lambda i: (0, 0))],
        out_shape=[jax.ShapeDtypeStruct(dh.shape, dh.dtype), jax.ShapeDtypeStruct((128, DK), F32)],
        args=(dz, wup, hh, dh), aliases={3: 0})


def _loss_head(y, target, *, tt):
    s = y.shape[0]

    def body(y_ref, t_ref, loss_ref, dy_ref):
        @pl.when(pl.program_id(0) == 0)
        def _():
            loss_ref[...] = jnp.zeros_like(loss_ref)

        err = y_ref[...] - t_ref[...]
        dy_ref[...] = err * (1.0 / D)
        per_tok = jnp.mean(err * err, axis=-1, keepdims=True)
        loss_ref[...] += 0.5 * jnp.sum(per_tok, axis=0, keepdims=True)

    tile = pl.BlockSpec((tt, D), lambda i: (i, 0))
    return _call(
        body, name="loss_head", grid=(s // tt,), in_specs=[tile, tile],
        out_specs=[pl.BlockSpec((1, 1), lambda i: (0, 0)), tile],
        out_shape=[jax.ShapeDtypeStruct((1, 1), F32), jax.ShapeDtypeStruct((s, D), F32)], args=(y, target))


def _adamw_math(share, w_ref, m_ref, v_ref, g_ref, d_ref, nm_ref, nv_ref):
    g = share(0).astype(F32)
    for q in range(1, N_DEV):
        g = g + share(q).astype(F32)
    g_ref[0] = g
    nm = ADAM_B1 * m_ref[0] + (1.0 - ADAM_B1) * g
    nv = ADAM_B2 * v_ref[0] + (1.0 - ADAM_B2) * (g * g)
    nm_ref[0] = nm
    nv_ref[0] = nv
    m_hat = nm / (1.0 - ADAM_B1 ** ADAM_STEP)
    v_hat = nv / (1.0 - ADAM_B2 ** ADAM_STEP)
    d_ref[0] = -ADAM_LR * (m_hat / (jnp.sqrt(v_hat) + ADAM_EPS) + ADAM_WD * w_ref[0])


def _adamw_layers(parts, w, m, v, *, tc, name):
    nl, rows, cols = w.shape
    nc = cols // tc

    def body(*refs):
        p_refs, rest = refs[:nl], refs[nl:]
        for j in range(nl):
            @pl.when(pl.program_id(0) == j)
            def _(p_ref=p_refs[j]):
                _adamw_math(lambda q: p_ref[q], *rest)

    def part_spec(j):
        return pl.BlockSpec((N_DEV, rows, tc), lambda l, c: (0, 0, jnp.where(l == j, c, jnp.where(l < j, 0, nc - 1))))

    tile = pl.BlockSpec((1, rows, tc), lambda l, c: (l, 0, c))
    return _call(
        body, name=name, grid=(nl, nc),
        in_specs=[part_spec(j) for j in range(nl)] + [tile, tile, tile],
        out_specs=[tile] * 4, out_shape=[jax.ShapeDtypeStruct((nl, rows, cols), F32)] * 4,
        args=(*parts, w, m, v))


def _adamw(parts, w, m, v, *, tr, tc, name):
    nl, rows, cols = w.shape

    def body(p_ref, *rest):
        _adamw_math(lambda q: p_ref[0, q], *rest)

    tile = pl.BlockSpec((1, tr, tc), lambda l, i, j: (l, i, j))
    return _call(
        body, name=name, grid=(nl, rows // tr, cols // tc),
        in_specs=[pl.BlockSpec((1, N_DEV, tr, tc), lambda l, i, j: (l, 0, i, j)), tile, tile, tile],
        out_specs=[tile] * 4, out_shape=[jax.ShapeDtypeStruct((nl, rows, cols), F32)] * 4,
        args=(parts, w, m, v), sem=("parallel", "parallel", "parallel"))


def _from_devices(g, axis):
    nd = g.ndim - 1
    perm = list(range(1, axis + 1)) + [0] + list(range(axis + 1, nd + 1))
    shape = list(g.shape[1:])
    shape[axis] *= N_DEV
    return jnp.transpose(g, perm).reshape(shape)


def _to_devices(a, axis):
    shape = list(a.shape)
    t = a.reshape(shape[:axis] + [N_DEV, shape[axis] // N_DEV] + shape[axis + 1:])
    return jnp.transpose(t, [axis] + list(range(0, axis)) + list(range(axis + 1, t.ndim)))


def _h_row_segments():
    segs = [(O_PI, PI0, IN_COLS - O_PI), (O_AL, AL0, RANK)]
    for h in range(HEADS):
        base = HD0 + h * HEAD_W
        segs += [(O_Q + h * HDK, base, HDK), (O_K + h * HDK, base + HDK, HDK),
                 (O_V + h * HDV, base + 2 * HDK, HDV), (O_GA + h * HDV, base + 2 * HDK + HDV, HDV)]
    return segs


def _h_weight_t(parts):
    return _move_rows(parts, _h_row_segments(), (HP, D), name="w_in_rows", zero=(AL0 + RANK, AL_W - RANK))


def _w_in_grad_parts_t(dwt):
    return _move_rows(dwt, [(d0, s0, n) for s0, d0, n in _h_row_segments()], (N_DEV, SHARD, D), name="w_in_grad_rows")


def kernel(x, w_in, w_alpha_up, b_alpha, gla_norm_g, w_pool_grp, pool_scale, b_merge, w_proj_a, w_proj_b, w_out, ln_g, ln_b, loss_target, m_w_in, m_w_alpha_up, m_b_alpha, m_gla_norm_g, m_w_pool_grp, m_pool_scale, m_b_merge, m_w_proj_a, m_w_proj_b, m_w_out, m_ln_g, m_ln_b, v_w_in, v_w_alpha_up, v_b_alpha, v_gla_norm_g, v_w_pool_grp, v_pool_scale, v_b_merge, v_w_proj_a, v_w_proj_b, v_w_out, v_ln_g, v_ln_b):
    s = x.shape[1]
    tt = min(256, s)
    tm = min(512, s)
    tb = min(1024, s)
    tn = HP // 3
    xs = x.reshape(s, D)

    tr3 = lambda a: jnp.transpose(a, (0, 2, 1))
    w_in_s = tr3(w_in).astype(WIRE)
    proj_s = jnp.stack([w_proj_a, w_proj_b, w_out], axis=1).astype(WIRE)
    pool_s = w_pool_grp.astype(WIRE)

    g_in, g_up, g_gn = _gather_first(w_in_s[0], [w_alpha_up.astype(WIRE), gla_norm_g], name="gather_first")
    wup = jnp.pad(_from_devices(g_up, 2), ((0, 0), (0, AL_W - RANK), (0, 0)))
    gn = _from_devices(g_gn, 2).reshape(DEPTH, 1, D)

    saved, wt_all, proj_all, pool_all = [], [], [], []
    cur, cur_b = xs, xs.astype(MXU)
    g_proj = g_pool = None
    for l in range(DEPTH):
        wt = _h_weight_t(g_in)
        nxt_l = l + 1 < DEPTH
        steps = (HP // tn) * (s // tb)
        res = _in_proj(cur_b, wt, tm=tb, tn=tn, ride=_ChipGather(w_in_s[l + 1], (2 * steps) // 3) if nxt_l else None)
        hh = res[0]
        if nxt_l:
            g_in = res[1]
        layers = ([0] if l == 0 else []) + ([l + 1] if nxt_l else [])
        res = _gla_fwd(hh, wup[l], b_alpha[l:l + 1], gn[l], tt=tb,
                       ride=_Exchange([(a[j], True) for j in layers for a in (proj_s, pool_s)]) if layers else None)
        o, ya, states = res[:3]
        got = {j: res[3 + 2 * t:5 + 2 * t] for t, j in enumerate(layers)}
        if l == 0:
            g_proj, g_pool = got[0]
        proj = _from_devices(g_proj, 1)
        pool = _from_devices(g_pool, 1)
        if nxt_l:
            g_proj, g_pool = got[l + 1]
        wt_all.append(wt), proj_all.append(proj), pool_all.append(pool)
        pooled, yb = _pool_fwd(hh, pool, pool_scale[l:l + 1], tt=tm)
        r, nxt, nxt_b = _merge_fwd(hh, cur, ya, yb, proj[0], proj[1], proj[2],
                                   b_merge[l:l + 1], ln_g[l:l + 1], ln_b[l:l + 1], tt=tm)
        saved.append(dict(xb=cur_b, hh=hh, o=o, ya=ya, states=states, pooled=pooled, yb=yb, r=r))
        cur, cur_b = nxt, nxt_b

    loss_part, dcur = _loss_head(cur, loss_target.reshape(s, D), tt=tm)
    loss = lax.psum(loss_part[0, 0], ("x", "y", "c"))

    small = {k: [None] * DEPTH for k in ("w_up", "b_alpha", "gnorm", "pool_scale", "b_merge", "ln_g", "ln_b")}
    parts = {k: [None] * DEPTH for k in ("w_in", "proj", "pool")}
    for l in range(DEPTH - 1, -1, -1):
        sv = saved[l]
        hh = sv["hh"]
        dh, dr, dpa, dpb, dw_out, dln_g, dln_b, db_merge = _merge_bwd(
            hh, sv["r"], sv["ya"], sv["yb"], dcur, proj_all[l][0], proj_all[l][1], proj_all[l][2], b_merge[l:l + 1], ln_g[l:l + 1], tt=tt)
        dya, dyb, dw_pa, dw_pb = _proj_bwd(sv["ya"], sv["yb"], dpa, dpb, proj_all[l][0], proj_all[l][1], tt=tm)
        dh, dw_pool, dscale = _pool_bwd(hh, pool_all[l], pool_scale[l:l + 1], sv["pooled"], dyb, dh, tt=tm)
        ride = _Exchange([(_to_devices(jnp.stack([dw_pa, dw_pb, dw_out]), 1).astype(WIRE), False),
                          (_to_devices(dw_pool, 1).astype(WIRE), False)])
        dh, dz, dgn, db_al, parts["proj"][l], parts["pool"][l] = _gla_bwd(
            hh, wup[l], b_alpha[l:l + 1], gn[l], sv["o"], sv["states"], dya, dh, tt=tb, ride=ride)
        dh, dw_up = _alpha_grads(dz, wup[l], hh, dh, tm=tb)
        dwt = _mm_tn(dh, sv["xb"], tm=tb, tk=tn, name="w_in_grad", out_dtype=WIRE)
        dcur, parts["w_in"][l] = _in_proj_bwd(dh, wt_all[l], dr, tm=tt, ride=_Exchange([(_w_in_grad_parts_t(dwt), False)]))

        small["w_up"][l] = dw_up[:RANK]
        small["b_alpha"][l] = db_al.reshape(DK)
        small["gnorm"][l] = dgn.reshape(HEADS, HDV)
        small["pool_scale"][l] = dscale[0]
        small["b_merge"][l] = db_merge[0]
        small["ln_g"][l], small["ln_b"][l] = dln_g[0], dln_b[0]
    grad_x = dcur[None]
    sm = {k: jnp.stack(v) for k, v in small.items()}

    rep = (("b_alpha", b_alpha, m_b_alpha, v_b_alpha), ("pool_scale", pool_scale, m_pool_scale, v_pool_scale),
           ("b_merge", b_merge, m_b_merge, v_b_merge), ("ln_g", ln_g, m_ln_g, v_ln_g), ("ln_b", ln_b, m_ln_b, v_ln_b))
    cat = lambda arrs: jnp.concatenate(arrs, axis=1)
    p_up, p_gn, p_rep = _exchange([(_to_devices(sm["w_up"], 2), False), (_to_devices(sm["gnorm"], 2), False),
                                   (cat([sm[nm] for nm, _, _, _ in rep]), True)], name="exchange_small_grads")

    def update(p, w, m, v, tr, name, layered=True, tc=None):
        shape = w.shape
        nl = shape[0] if layered else 1
        cols = shape[-1]
        flat = lambda a: a.reshape(nl, -1, cols)
        outs = _adamw(p.reshape(nl, N_DEV, -1, cols), flat(w), flat(m), flat(v), tr=tr, tc=tc or cols, name=name)
        return [o_.reshape(shape) for o_ in outs]

    res = {}
    res["w_in"] = [tr3(o_) for o_ in _adamw_layers(parts["w_in"], tr3(w_in), tr3(m_w_in), tr3(v_w_in), tc=128, name="adamw_w_in")]
    proj_p = jnp.stack(parts["proj"])
    for j, (nm, w, m, v) in enumerate((("w_proj_a", w_proj_a, m_w_proj_a, v_w_proj_a), ("w_proj_b", w_proj_b, m_w_proj_b, v_w_proj_b),
                                       ("w_out", w_out, m_w_out, v_w_out))):
        res[nm] = update(proj_p[:, :, j], w, m, v, D // N_DEV, "adamw_" + nm)
    res["w_pool_grp"] = update(jnp.stack(parts["pool"]), w_pool_grp, m_w_pool_grp, v_w_pool_grp, 128, "adamw_w_pool")
    res["w_alpha_up"] = update(p_up, w_alpha_up, m_w_alpha_up, v_w_alpha_up, DEPTH * RANK, "adamw_w_up", layered=False)
    res["gla_norm_g"] = update(p_gn, gla_norm_g, m_gla_norm_g, v_gla_norm_g, DEPTH * HEADS, "adamw_gnorm", layered=False)
    rep_out = update(p_rep, cat([w for _, w, _, _ in rep]), cat([m for _, _, m, _ in rep]), cat([v for _, _, _, v in rep]),
                     DEPTH, "adamw_small", layered=False)
    off = 0
    for nm, w, _, _ in rep:
        n = w.shape[1]
        res[nm] = [o_[:, off:off + n] for o_ in rep_out]
        off += n

    order = ("w_in", "w_alpha_up", "b_alpha", "gla_norm_g", "w_pool_grp", "pool_scale", "b_merge", "w_proj_a", "w_proj_b",
             "w_out", "ln_g", "ln_b")
    return (loss, grad_x, *[res[n][0] for n in order], *[res[n][1] for n in order],
            *[res[n][2] for n in order], *[res[n][3] for n in order])
```

```python
import jax
import jax.numpy as jnp
from jax import lax
from jax.experimental import pallas as pl
from jax.experimental.pallas import tpu as pltpu

F32 = jnp.float32
MXU = jnp.bfloat16
WIRE = jnp.bfloat16

N_DEV = 8
DEPTH = 4
D = 1024
HEADS = 4
DK = D // 2
HDK = DK // HEADS
HDV = D // HEADS
RANK = 16
CHUNK = 64
GATE_TAU = 16.0
POOL_WINDOWS = (2, 4, 8, 16)
PG = D // len(POOL_WINDOWS)
HALO = 16
IN_COLS = 7184
SHARD = IN_COLS // N_DEV
ALPHA = (2.0 * DEPTH) ** 0.25
EPS = 1e-5
Q_SCALE = HDK ** -0.5

ADAM_LR, ADAM_B1, ADAM_B2, ADAM_EPS, ADAM_WD, ADAM_STEP = 0.001, 0.9, 0.999, 1e-08, 0.01, 10

PI0, GB0, ML0, AL0, AL_W = 0, D, 2 * D, 4 * D, 512
HD0 = AL0 + AL_W
HEAD_W = 2 * HDK + 2 * HDV
HP = HD0 + HEADS * HEAD_W
HPB = 2
O_Q, O_K, O_V, O_GA, O_AL, O_PI, O_GB, O_ML = 0, DK, 2 * DK, 2 * DK + D, 2 * DK + 2 * D, 2 * DK + 2 * D + RANK, \
    2 * DK + 3 * D + RANK, 2 * DK + 4 * D + RANK

VMEM_BIG = 56 * 1024 * 1024
VMEM_MID = 40 * 1024 * 1024

NN = ((1,), (0,))
NT = ((1,), (1,))
TN = ((0,), (0,))

HBM = pl.BlockSpec(memory_space=pltpu.HBM)
ANY = pl.BlockSpec(memory_space=pl.ANY)


def _dot(a, b, dims):
    return lax.dot_general(a.astype(MXU), b.astype(MXU), (dims, ((), ())), preferred_element_type=F32)


def _params(sem, vmem):
    return pltpu.CompilerParams(dimension_semantics=sem, vmem_limit_bytes=vmem)


def _sigmoid(x):
    return 1.0 / (1.0 + jnp.exp(-x))


def _log_sigmoid(z):
    return jnp.minimum(z, 0.0) - jnp.log(1.0 + jnp.exp(-jnp.abs(z)))


class _Exchange:
    def __init__(self, items):
        self.items = [(s, bool(g)) for s, g in items]
        self.n = len(self.items)
        self.srcs = [s for s, _ in self.items]
        self.in_specs = [HBM] * self.n
        self.out_specs = [HBM] * self.n
        self.out_shape = [jax.ShapeDtypeStruct((N_DEV,) + tuple(s.shape if g else s.shape[1:]), s.dtype) for s, g in self.items]
        self.scratch = [pltpu.SemaphoreType.DMA((self.n * (N_DEV - 1),)), pltpu.SemaphoreType.DMA((self.n * (N_DEV - 1),)),
                        pltpu.SemaphoreType.DMA((self.n,))]

    def copies(self, src_refs, out_refs, send_sems, recv_sems, local_sems):
        x, y, c = lax.axis_index("x"), lax.axis_index("y"), lax.axis_index("c")
        me = 4 * x + 2 * y + c
        copies = []
        for t, (_, gather) in enumerate(self.items):
            src_ref, out_ref = src_refs[t], out_refs[t]
            copies.append(pltpu.make_async_copy(src_ref if gather else src_ref.at[me], out_ref.at[me], local_sems.at[t]))
            for k in range(1, N_DEV):
                px = 1 - x if k & 4 else x
                py = 1 - y if k & 2 else y
                pc = 1 - c if k & 1 else c
                peer = 4 * px + 2 * py + pc
                sem = t * (N_DEV - 1) + k - 1
                copies.append(pltpu.make_async_remote_copy(
                    src_ref=src_ref if gather else src_ref.at[peer],
                    dst_ref=out_ref.at[me],
                    send_sem=send_sems.at[sem],
                    recv_sem=recv_sems.at[sem],
                    device_id=(px, py, pc),
                    device_id_type=pl.DeviceIdType.MESH,
                ))
        return copies


    mid_step = None

    def start(self, *refs):
        for cp in self.copies(*refs):
            cp.start()

    def finish(self, *refs):
        for cp in self.copies(*refs):
            cp.wait()


class _ChipGather:
    def __init__(self, src, mid_step):
        self.n = 1
        self.srcs = [src]
        self.mid_step = mid_step
        self.in_specs = [HBM]
        self.out_specs = [HBM]
        self.out_shape = [jax.ShapeDtypeStruct((N_DEV,) + tuple(src.shape), src.dtype)]
        self.scratch = [pltpu.SemaphoreType.DMA((N_DEV - 1,)), pltpu.SemaphoreType.DMA((N_DEV - 1,)), pltpu.SemaphoreType.DMA((1,))]

    def _plan(self, src_refs, out_refs, send_sems, recv_sems, local_sems):
        src_ref, out_ref = src_refs[0], out_refs[0]
        x, y, c = lax.axis_index("x"), lax.axis_index("y"), lax.axis_index("c")
        me, sibling = (x, y, c), (x, y, 1 - c)
        chips = [(1 - x, y), (x, 1 - y), (1 - x, 1 - y)]

        def copy(k, block, to, src=None):
            slot = out_ref.at[4 * block[0] + 2 * block[1] + block[2]]
            return pltpu.make_async_remote_copy(
                src_ref=slot if src is None else src, dst_ref=slot, send_sem=send_sems.at[k], recv_sem=recv_sems.at[k],
                device_id=to, device_id_type=pl.DeviceIdType.MESH)

        mine = pltpu.make_async_copy(src_ref, out_ref.at[4 * x + 2 * y + c], local_sems.at[0])
        first = [copy(0, me, sibling, src=src_ref)] + [copy(1 + j, me, (*chip, c), src=src_ref) for j, chip in enumerate(chips)]
        landed = [copy(1 + j, (*chip, c), me) for j, chip in enumerate(chips)]
        passed = [copy(4 + j, (*chip, c), sibling) for j, chip in enumerate(chips)]
        from_sibling = [copy(0, sibling, me)] + [copy(4 + j, (*chip, 1 - c), me) for j, chip in enumerate(chips)]
        return mine, first, landed, passed, from_sibling

    def start(self, *refs):
        mine, first, _, _, _ = self._plan(*refs)
        mine.start()
        for cp in first:
            cp.start()

    def mid(self, *refs):
        _, _, landed, passed, _ = self._plan(*refs)
        for got, fwd in zip(landed, passed):
            got.wait_recv()
            fwd.start()

    def finish(self, *refs):
        mine, first, _, passed, from_sibling = self._plan(*refs)
        for cp in from_sibling:
            cp.wait_recv()
        for cp in first + passed:
            cp.wait_send()
        mine.wait()


def _grid_ends(grid):
    first = last = step = None
    for a, n in enumerate(grid):
        f = pl.program_id(a) == 0
        e = pl.program_id(a) == n - 1
        first = f if first is None else first & f
        last = e if last is None else last & e
        step = pl.program_id(a) if step is None else step * n + pl.program_id(a)
    return first, last, step


def _call(body, *, name, grid, in_specs, out_specs, out_shape, args, scratch=(), sem=None, vmem=VMEM_MID, ride=None, aliases=None):
    n_in, n_out, n_scr = len(in_specs), len(out_specs), len(scratch)
    sem = sem or ("arbitrary",) * len(grid)
    if ride is None:
        return pl.pallas_call(body, name=name, grid=grid, in_specs=in_specs, out_specs=out_specs, out_shape=out_shape,
                              scratch_shapes=list(scratch), compiler_params=_params(sem, vmem),
                              input_output_aliases=aliases or {})(*args)
    r = ride.n

    def riding(*refs):
        ins, rsrc = refs[:n_in], refs[n_in:n_in + r]
        outs, rout = refs[n_in + r:n_in + r + n_out], refs[n_in + r + n_out:n_in + 2 * r + n_out]
        scr = refs[n_in + 2 * r + n_out:n_in + 2 * r + n_out + n_scr]
        send_sems, recv_sems, local_sems = refs[n_in + 2 * r + n_out + n_scr:]
        first, last, step = _grid_ends(grid)
        comm = (rsrc, rout, send_sems, recv_sems, local_sems)

        @pl.when(first)
        def _():
            ride.start(*comm)

        body(*ins, *outs, *scr)

        if ride.mid_step is not None:
            @pl.when(step == ride.mid_step)
            def _():
                ride.mid(*comm)

        @pl.when(last)
        def _():
            ride.finish(*comm)

    return pl.pallas_call(riding, name=name, grid=grid, in_specs=list(in_specs) + ride.in_specs,
                          out_specs=list(out_specs) + ride.out_specs, out_shape=list(out_shape) + ride.out_shape,
                          scratch_shapes=list(scratch) + ride.scratch,
                          compiler_params=_params(("arbitrary",) * len(grid), vmem),
                          input_output_aliases=aliases or {})(*args, *ride.srcs)


def _exchange(items, *, name):
    ex = _Exchange(items)

    def body(*refs):
        copies = ex.copies(refs[:ex.n], refs[ex.n:2 * ex.n], *refs[2 * ex.n:])
        for cp in copies:
            cp.start()
        for cp in copies:
            cp.wait()

    return pl.pallas_call(body, name=name, in_specs=ex.in_specs, out_specs=ex.out_specs, out_shape=ex.out_shape,
                          scratch_shapes=ex.scratch)(*ex.srcs)


def _gather_first(big, smalls, *, name):
    ex = _Exchange([(a, True) for a in smalls])

    def body(*refs):
        big_ref, small_src = refs[0], refs[1:1 + ex.n]
        out_ref, small_out = refs[1 + ex.n], refs[2 + ex.n:2 + 2 * ex.n]
        send_sems, recv_sems, local_sem = refs[2 + 2 * ex.n:5 + 2 * ex.n]
        x, y, c = lax.axis_index("x"), lax.axis_index("y"), lax.axis_index("c")
        me, sibling = (x, y, c), (x, y, 1 - c)
        chips = [(1 - x, y), (x, 1 - y), (1 - x, 1 - y)]

        def slot(px, py, pc):
            return out_ref.at[4 * px + 2 * py + pc]

        def copy(k, block, to, src=None):
            return pltpu.make_async_remote_copy(
                src_ref=slot(*block) if src is None else src, dst_ref=slot(*block),
                send_sem=send_sems.at[k], recv_sem=recv_sems.at[k], device_id=to, device_id_type=pl.DeviceIdType.MESH)

        small = ex.copies(small_src, small_out, *refs[5 + 2 * ex.n:])
        mine = pltpu.make_async_copy(big_ref, slot(*me), local_sem)
        mine.start()
        first = [copy(0, me, sibling, src=big_ref)] + [copy(1 + j, me, (*chip, c), src=big_ref) for j, chip in enumerate(chips)]
        for cp in first + small:
            cp.start()
        passed = [copy(4 + j, (*chip, c), sibling) for j, chip in enumerate(chips)]
        for j, chip in enumerate(chips):
            copy(1 + j, (*chip, c), me).wait_recv()
            passed[j].start()
        copy(0, sibling, me).wait_recv()
        for j, chip in enumerate(chips):
            copy(4 + j, (*chip, 1 - c), me).wait_recv()
        for cp in first + passed:
            cp.wait_send()
        mine.wait()
        for cp in small:
            cp.wait()

    return pl.pallas_call(
        body, name=name, in_specs=[HBM] + ex.in_specs, out_specs=[HBM] + ex.out_specs,
        out_shape=[jax.ShapeDtypeStruct((N_DEV,) + tuple(big.shape), big.dtype)] + ex.out_shape,
        scratch_shapes=[pltpu.SemaphoreType.DMA((N_DEV - 1,)), pltpu.SemaphoreType.DMA((N_DEV - 1,)), pltpu.SemaphoreType.DMA]
        + ex.scratch)(big, *ex.srcs)


def _move_rows(src, segs, out_shape, *, name, zero=None):
    cols = src.shape[-1]
    step = 256

    def at(shape, row, m):
        return (slice(row, row + m),) if len(shape) == 2 else (row // SHARD, slice(row % SHARD, row % SHARD + m))

    def room(shape, row):
        return step if len(shape) == 2 else SHARD - row % SHARD

    def body(src_ref, out_ref):
        for s0, d0, n in segs:
            r = 0
            while r < n:
                m = min(step, n - r, room(src.shape, s0 + r), room(out_shape, d0 + r))
                out_ref[(*at(out_shape, d0 + r, m), slice(None))] = src_ref[(*at(src.shape, s0 + r, m), slice(None))]
                r += m
        if zero is not None:
            out_ref[zero[0]:zero[0] + zero[1], :] = jnp.zeros((zero[1], cols), src.dtype)

    vmem = pl.BlockSpec(memory_space=pltpu.VMEM)
    return pl.pallas_call(
        body, name=name, in_specs=[vmem], out_specs=vmem, out_shape=jax.ShapeDtypeStruct(tuple(out_shape), src.dtype),
        compiler_params=pltpu.CompilerParams(vmem_limit_bytes=VMEM_BIG))(src)


def _in_proj(xb, wt, *, tm, tn, ride=None):
    m, k = xb.shape
    n = wt.shape[0]

    def body(x_ref, w_ref, o_ref):
        o_ref[...] = _dot(x_ref[...], w_ref[...], NT)

    return _call(
        body, name="in_proj", grid=(n // tn, m // tm),
        in_specs=[pl.BlockSpec((tm, k), lambda j, i: (i, 0)), pl.BlockSpec((tn, k), lambda j, i: (j, 0))],
        out_specs=[pl.BlockSpec((tm, tn), lambda j, i: (i, j))],
        out_shape=[jax.ShapeDtypeStruct((m, n), F32)],
        args=(xb, wt), sem=("parallel", "parallel"), vmem=VMEM_BIG, ride=ride)


def _in_proj_bwd(dh, wt, dr, *, tm, ride=None):
    m, n = dh.shape
    k = wt.shape[1]

    def body(dh_ref, w_ref, dr_ref, o_ref):
        o_ref[...] = ALPHA * dr_ref[...] + _dot(dh_ref[...], w_ref[...], NN)

    return _call(
        body, name="in_proj_bwd", grid=(m // tm,),
        in_specs=[pl.BlockSpec((tm, n), lambda i: (i, 0)),
                  pl.BlockSpec((n, k), lambda i: (0, 0), pipeline_mode=pl.Buffered(1)),
                  pl.BlockSpec((tm, k), lambda i: (i, 0))],
        out_specs=[pl.BlockSpec((tm, k), lambda i: (i, 0))],
        out_shape=[jax.ShapeDtypeStruct((m, k), F32)],
        args=(dh, wt, dr), sem=("parallel",), vmem=VMEM_BIG, ride=ride)


def _mm_tn(a, dc, *, tm, tk, name, out_dtype=F32):
    m, k = a.shape
    n = dc.shape[1]
    ni = m // tm

    def body(a_ref, dc_ref, o_ref, acc):
        i = pl.program_id(1)

        @pl.when(i == 0)
        def _():
            acc[...] = jnp.zeros_like(acc)

        acc[...] += _dot(a_ref[...], dc_ref[...], TN)

        @pl.when(i == ni - 1)
        def _():
            o_ref[...] = acc[...].astype(o_ref.dtype)

    return _call(
        body, name=name, grid=(k // tk, ni),
        in_specs=[pl.BlockSpec((tm, tk), lambda j, i: (i, j)), pl.BlockSpec((tm, n), lambda j, i: (i, 0))],
        out_specs=[pl.BlockSpec((tk, n), lambda j, i: (j, 0))],
        out_shape=[jax.ShapeDtypeStruct((k, n), out_dtype)],
        scratch=[pltpu.VMEM((tk, n), F32)],
        args=(a, dc), sem=("parallel", "arbitrary"), vmem=VMEM_BIG)[0]


def _head_cols(p):
    b = p * HEAD_W
    return (slice(b, b + HDK), slice(b + HDK, b + 2 * HDK), slice(b + 2 * HDK, b + 2 * HDK + HDV),
            slice(b + 2 * HDK + HDV, b + HEAD_W))


def _seg_cumsum(v, reverse=False):
    t, w = v.shape
    hi = v.astype(MXU)
    lo = (v - hi.astype(F32)).astype(MXU)
    terms = jnp.concatenate([hi, lo], axis=1)
    row = lax.broadcasted_iota(jnp.int32, (CHUNK, CHUNK), 0)
    col = lax.broadcasted_iota(jnp.int32, (CHUNK, CHUNK), 1)
    ones = jnp.where((row <= col) if reverse else (row >= col), 1.0, 0.0).astype(MXU)
    out = []
    for c in range(t // CHUNK):
        y = _dot(ones, terms[c * CHUNK:(c + 1) * CHUNK], NN)
        out.append(y[:, :w] + y[:, w:])
    return jnp.concatenate(out, axis=0)


def _seg_rcumsum_rolls(v):
    t = v.shape[0]
    rowmod = lax.broadcasted_iota(jnp.int32, v.shape, 0) % CHUNK
    sh = 1
    while sh < CHUNK:
        v = v + jnp.where(rowmod < CHUNK - sh, pltpu.roll(v, t - sh, 0), 0.0)
        sh *= 2
    return v


def _gla_decay(alpha_ref, wup_ref, b_ref, g_scr):
    z = _dot(alpha_ref[...], wup_ref[...], NN) + b_ref[...]
    g_scr[...] = _seg_cumsum(_log_sigmoid(z) * (1.0 / GATE_TAU))
    return z


QE1, KE1, QE2, KE2, QA, KD = range(6)
EP, EM, EA, EDL = range(4)
GW = HPB * HDK


def _gla_operands(hd_ref, g_scr, opnd_scr, fac_scr=None):
    t = g_scr.shape[0]

    def chunk_row(r):
        return jnp.concatenate([jnp.broadcast_to(g_scr[c * CHUNK + r:c * CHUNK + r + 1, :], (CHUNK, GW))
                                for c in range(t // CHUNK)], axis=0)

    g = g_scr[...]
    g_last = chunk_row(CHUNK - 1)
    ref = 0.5 * (chunk_row(0) + g_last)
    fac = {EP: jnp.exp(g - ref), EM: jnp.exp(ref - g), EA: jnp.exp(g), EDL: jnp.exp(g_last - g)}
    if fac_scr is not None:
        for j, f in fac.items():
            fac_scr[j] = f
    for p in range(HPB):
        qc, kc, _, _ = _head_cols(p)
        gc = slice(p * HDK, (p + 1) * HDK)
        qs = hd_ref[:, qc] * Q_SCALE
        k = hd_ref[:, kc]
        for j, (x, f) in {QE1: (qs, EP), KE1: (k, EM), QE2: (qs, EM), KE2: (k, EP), QA: (qs, EA), KD: (k, EDL)}.items():
            opnd_scr[j, :, gc] = (x * fac[f][:, gc]).astype(opnd_scr.dtype)


def _lower_mask():
    return lax.broadcasted_iota(jnp.int32, (CHUNK, CHUNK), 0) >= lax.broadcasted_iota(jnp.int32, (CHUNK, CHUNK), 1)


def _scores(opnd_scr, rows, gc, lower):
    return jnp.where(lower, _dot(opnd_scr[QE1, rows, gc], opnd_scr[KE1, rows, gc], NT),
                     _dot(opnd_scr[QE2, rows, gc], opnd_scr[KE2, rows, gc], NT))


def _gla_specs(tt, row):
    return [
        pl.BlockSpec((tt, HPB * HEAD_W), lambda h, i: (row(i), HD0 // (HPB * HEAD_W) + h)),
        pl.BlockSpec((tt, 128), lambda h, i: (row(i), AL0 // 128)),
        pl.BlockSpec((128, HPB * HDK), lambda h, i: (0, h)),
        pl.BlockSpec((1, HPB * HDK), lambda h, i: (0, h)),
        pl.BlockSpec((1, HPB * HDV), lambda h, i: (0, h)),
    ]


def _gla_fwd(hh, wup, b_alpha, gnorm, *, tt, ride=None):
    s = hh.shape[0]
    nt = s // tt
    nct = tt // CHUNK

    def body(hd_ref, al_ref, wup_ref, b_ref, gn_ref, o_ref, ya_ref, st_ref, state, g_scr, opnd_scr):
        @pl.when(pl.program_id(1) == 0)
        def _():
            state[...] = jnp.zeros_like(state)

        _gla_decay(al_ref, wup_ref, b_ref, g_scr)
        _gla_operands(hd_ref, g_scr, opnd_scr)
        lower = _lower_mask()
        for c in range(nct):
            rows = slice(c * CHUNK, (c + 1) * CHUNK)
            for p in range(HPB):
                vc = _head_cols(p)[2]
                gc = slice(p * HDK, (p + 1) * HDK)
                v = hd_ref[rows, vc]
                st = state[p]
                st_ref[p, c] = st
                egl = jnp.exp(g_scr[(c + 1) * CHUNK - 1:(c + 1) * CHUNK, gc])
                o_ref[rows, p * HDV:(p + 1) * HDV] = (_dot(_scores(opnd_scr, rows, gc, lower), v, NN)
                                                      + _dot(opnd_scr[QA, rows, gc], st, NT))
                state[p] = st * egl + _dot(v, opnd_scr[KD, rows, gc], TN)
        for p in range(HPB):
            oc = slice(p * HDV, (p + 1) * HDV)
            o = o_ref[:, oc]
            ohat = o * lax.rsqrt(jnp.mean(o * o, axis=-1, keepdims=True) + EPS)
            ga = hd_ref[:, _head_cols(p)[3]]
            ya_ref[:, oc] = (ohat * gn_ref[:, oc] * (ga * _sigmoid(ga))).astype(ya_ref.dtype)

    return _call(
        body, name="gla_fwd", grid=(HEADS // HPB, nt),
        in_specs=_gla_specs(tt, lambda i: i),
        out_specs=[
            pl.BlockSpec((tt, HPB * HDV), lambda h, i: (i, h)),
            pl.BlockSpec((tt, HPB * HDV), lambda h, i: (i, h)),
            pl.BlockSpec((HPB, nct, HDV, HDK), lambda h, i: (h, i, 0, 0)),
        ],
        out_shape=[
            jax.ShapeDtypeStruct((s, D), F32),
            jax.ShapeDtypeStruct((s, D), MXU),
            jax.ShapeDtypeStruct((HEADS, s // CHUNK, HDV, HDK), F32),
        ],
        scratch=[pltpu.VMEM((HPB, HDV, HDK), F32), pltpu.VMEM((tt, GW), F32), pltpu.VMEM((6, tt, GW), MXU)],
        args=(hh, hh, wup, b_alpha, gnorm), ride=ride, vmem=VMEM_BIG)


def _gla_bwd(hh, wup, b_alpha, gnorm, o, states, dya, dh, *, tt, ride=None):
    s = hh.shape[0]
    nt = s // tt
    nct = tt // CHUNK

    def body(hd_ref, al_ref, wup_ref, b_ref, gn_ref, o_ref, st_ref, dya_ref, _dh_in,
             dh_ref, dz_ref, dgn_ref, db_ref, dstate, g_scr, dg_scr, do_scr, opnd_scr, fac_scr, res_scr, dgl_scr):
        @pl.when(pl.program_id(1) == 0)
        def _():
            dstate[...] = jnp.zeros_like(dstate)
            dgn_ref[...] = jnp.zeros_like(dgn_ref)
            db_ref[...] = jnp.zeros_like(db_ref)

        z = _gla_decay(al_ref, wup_ref, b_ref, g_scr)
        _gla_operands(hd_ref, g_scr, opnd_scr, fac_scr)

        for p in range(HPB):
            oc = slice(p * HDV, (p + 1) * HDV)
            gac = _head_cols(p)[3]
            o_t = o_ref[:, oc]
            rstd = lax.rsqrt(jnp.mean(o_t * o_t, axis=-1, keepdims=True) + EPS)
            ohat = o_t * rstd
            ga = hd_ref[:, gac]
            sg = _sigmoid(ga)
            dya_t = dya_ref[:, oc]
            gn = gn_ref[:, oc]
            dh_ref[:, gac] = (dya_t * ohat * gn * (sg * (1.0 + ga * (1.0 - sg)))).astype(dh_ref.dtype)
            don = dya_t * (ga * sg)
            dgn_ref[p] += jnp.sum(don * ohat, axis=0, keepdims=True)
            dohat = don * gn
            do_scr[:, oc] = rstd * (dohat - ohat * jnp.mean(dohat * ohat, axis=-1, keepdims=True))

        lower = _lower_mask()
        for c in range(nct - 1, -1, -1):
            rows = slice(c * CHUNK, (c + 1) * CHUNK)
            for p in range(HPB):
                vc = _head_cols(p)[2]
                gc = slice(p * HDK, (p + 1) * HDK)
                v = hd_ref[rows, vc]
                do = do_scr[rows, p * HDV:(p + 1) * HDV]
                st = st_ref[p, c]
                dst = dstate[p]
                egl = jnp.exp(g_scr[(c + 1) * CHUNK - 1:(c + 1) * CHUNK, gc])
                a = _scores(opnd_scr, rows, gc, lower)
                da = _dot(do, v, NT)
                da1 = jnp.where(lower, da, 0.0).astype(MXU)
                da2 = jnp.where(lower, 0.0, da).astype(MXU)
                res_scr[0, rows, gc] = _dot(da1, opnd_scr[KE1, rows, gc], NN)
                res_scr[1, rows, gc] = _dot(da1, opnd_scr[QE1, rows, gc], TN)
                res_scr[2, rows, gc] = _dot(da2, opnd_scr[KE2, rows, gc], NN)
                res_scr[3, rows, gc] = _dot(da2, opnd_scr[QE2, rows, gc], TN)
                res_scr[4, rows, gc] = _dot(do, st, NN)
                res_scr[5, rows, gc] = _dot(v, dst, NN)
                dh_ref[rows, vc] = (_dot(a, do, TN) + _dot(opnd_scr[KD, rows, gc], dst, NT)).astype(dh_ref.dtype)
                dgl_scr[c:c + 1, gc] = egl * jnp.sum(dst * st, axis=0, keepdims=True)
                dstate[p] = dst * egl + _dot(do, opnd_scr[QA, rows, gc], TN)

        p1, p2, p3 = res_scr[0] * fac_scr[EP], res_scr[2] * fac_scr[EM], res_scr[4] * fac_scr[EA]
        r1, r2, r3 = res_scr[1] * fac_scr[EM], res_scr[3] * fac_scr[EP], res_scr[5] * fac_scr[EDL]
        dq = (p1 + p2 + p3) * Q_SCALE
        dk = r1 + r2 + r3
        dgq = p1 - p2 + p3
        dgk = r2 - r1
        for p in range(HPB):
            qc, kc, _, _ = _head_cols(p)
            gc = slice(p * HDK, (p + 1) * HDK)
            dh_ref[:, qc] = dq[:, gc].astype(dh_ref.dtype)
            dh_ref[:, kc] = dk[:, gc].astype(dh_ref.dtype)
            k = hd_ref[:, kc]
            r3k = r3[:, gc] * k
            dg_scr[:, gc] = (hd_ref[:, qc] * Q_SCALE) * dgq[:, gc] + k * dgk[:, gc] - r3k
            for c in range(nct):
                last = slice((c + 1) * CHUNK - 1, (c + 1) * CHUNK)
                dg_scr[last, gc] += jnp.sum(r3k[c * CHUNK:(c + 1) * CHUNK], axis=0, keepdims=True) + dgl_scr[c:c + 1, gc]

        dz = _seg_rcumsum_rolls(dg_scr[...]) * _sigmoid(-z) * (1.0 / GATE_TAU)
        dz_ref[...] = dz.astype(dz_ref.dtype)
        for p in range(HPB):
            db_ref[p] += jnp.sum(dz[:, p * HDK:(p + 1) * HDK], axis=0, keepdims=True)

    rev = lambda i: nt - 1 - i
    in_specs = _gla_specs(tt, rev) + [
        pl.BlockSpec((tt, HPB * HDV), lambda h, i: (rev(i), h)),
        pl.BlockSpec((HPB, nct, HDV, HDK), lambda h, i: (h, rev(i), 0, 0)),
        pl.BlockSpec((tt, HPB * HDV), lambda h, i: (rev(i), h)),
        ANY,
    ]
    return _call(
        body, name="gla_bwd", grid=(HEADS // HPB, nt), in_specs=in_specs,
        out_specs=[
            pl.BlockSpec((tt, HPB * HEAD_W), lambda h, i: (rev(i), HD0 // (HPB * HEAD_W) + h)),
            pl.BlockSpec((tt, HPB * HDK), lambda h, i: (rev(i), h)),
            pl.BlockSpec((HPB, 1, HDV), lambda h, i: (h, 0, 0)),
            pl.BlockSpec((HPB, 1, HDK), lambda h, i: (h, 0, 0)),
        ],
        out_shape=[
            jax.ShapeDtypeStruct(dh.shape, dh.dtype),
            jax.ShapeDtypeStruct((s, DK), MXU),
            jax.ShapeDtypeStruct((HEADS, 1, HDV), F32),
            jax.ShapeDtypeStruct((HEADS, 1, HDK), F32),
        ],
        scratch=[pltpu.VMEM((HPB, HDV, HDK), F32), pltpu.VMEM((tt, GW), F32), pltpu.VMEM((tt, GW), F32),
                 pltpu.VMEM((tt, HPB * HDV), F32), pltpu.VMEM((6, tt, GW), MXU), pltpu.VMEM((4, tt, GW), F32),
                 pltpu.VMEM((6, tt, GW), F32), pltpu.VMEM((max(nct, 8), GW), F32)],
        args=(hh, hh, wup, b_alpha, gnorm, o, states, dya, dh), ride=ride, aliases={8: 0}, vmem=VMEM_BIG)


def _window_count(tile, tt, w):
    pos = tile * tt + lax.broadcasted_iota(jnp.int32, (tt, PG), 0) + 1
    return jnp.minimum(pos, w).astype(F32)


def _pool_fwd(hh, wpool, scale, *, tt):
    s = hh.shape[0]
    nt = s // tt

    def body(ug_ref, w_ref, sc_ref, pooled_ref, yb_ref, halo):
        i = pl.program_id(0)

        @pl.when(i == 0)
        def _():
            halo[...] = jnp.zeros_like(halo)

        for g, w in enumerate(POOL_WINDOWS):
            cols = slice(g * PG, (g + 1) * PG)
            u = ug_ref[:, cols]
            run = jnp.concatenate([halo[:, cols], u], axis=0)
            sh = 1
            while sh < w:
                run = run + pltpu.roll(run, sh, 0)
                sh *= 2
            pooled = run[HALO:, :] / _window_count(i, tt, w) - u
            pooled_ref[:, cols] = pooled.astype(pooled_ref.dtype)
            mixed = _dot(pooled, w_ref[g], NN)
            gb = ug_ref[:, slice(D + g * PG, D + (g + 1) * PG)]
            yb_ref[:, cols] = (mixed * sc_ref[:, cols] * (gb * _sigmoid(gb))).astype(yb_ref.dtype)
        halo[...] = ug_ref[tt - HALO:tt, :D]

    tile = pl.BlockSpec((tt, D), lambda i: (i, 0))
    return _call(
        body, name="pool_fwd", grid=(nt,),
        in_specs=[
            pl.BlockSpec((tt, 2 * D), lambda i: (i, PI0 // (2 * D))),
            pl.BlockSpec((len(POOL_WINDOWS), PG, PG), lambda i: (0, 0, 0)),
            pl.BlockSpec((1, D), lambda i: (0, 0)),
        ],
        out_specs=[tile] * 2,
        out_shape=[jax.ShapeDtypeStruct((s, D), MXU), jax.ShapeDtypeStruct((s, D), MXU)],
        scratch=[pltpu.VMEM((HALO, D), F32)], args=(hh, wpool, scale))


def _pool_bwd(hh, wpool, scale, pooled, dyb, dh, *, tt):
    s = hh.shape[0]
    nt = s // tt

    def body(gb_ref, w_ref, sc_ref, pooled_ref, dyb_ref, _dh_in, dh_ref, dw_ref, dsc_ref, halo):
        i = pl.program_id(0)
        tile = nt - 1 - i

        @pl.when(i == 0)
        def _():
            halo[...] = jnp.zeros_like(halo)
            dw_ref[...] = jnp.zeros_like(dw_ref)
            dsc_ref[...] = jnp.zeros_like(dsc_ref)

        for g, w in enumerate(POOL_WINDOWS):
            cols = slice(g * PG, (g + 1) * PG)
            gcols = slice(D + g * PG, D + (g + 1) * PG)
            gb = gb_ref[:, cols]
            sg = _sigmoid(gb)
            pooled = pooled_ref[:, cols]
            mixed = _dot(pooled, w_ref[g], NN)
            sc = sc_ref[:, cols]
            dyb = dyb_ref[:, cols]
            dh_ref[:, gcols] = (dyb * mixed * sc * (sg * (1.0 + gb * (1.0 - sg)))).astype(dh_ref.dtype)
            dms = dyb * (gb * sg)
            dsc_ref[:, cols] += jnp.sum(dms * mixed, axis=0, keepdims=True)
            dmixed = dms * sc
            dpooled = _dot(dmixed, w_ref[g], NT)
            dw_ref[g] += _dot(pooled, dmixed, TN)
            e = dpooled / _window_count(tile, tt, w)
            run = jnp.concatenate([e, halo[:, cols]], axis=0)
            sh = 1
            while sh < w:
                run = run + pltpu.roll(run, tt + HALO - sh, 0)
                sh *= 2
            dh_ref[:, cols] = (run[:tt, :] - dpooled).astype(dh_ref.dtype)
            halo[:, cols] = e[:HALO, :]

    rev = lambda i: nt - 1 - i
    tile = pl.BlockSpec((tt, D), lambda i: (rev(i), 0))
    wspec = pl.BlockSpec((len(POOL_WINDOWS), PG, PG), lambda i: (0, 0, 0))
    vec = pl.BlockSpec((1, D), lambda i: (0, 0))
    return _call(
        body, name="pool_bwd", grid=(nt,),
        in_specs=[pl.BlockSpec((tt, D), lambda i: (rev(i), GB0 // D)), wspec, vec, tile, tile, ANY],
        out_specs=[pl.BlockSpec((tt, 2 * D), lambda i: (rev(i), PI0 // (2 * D))), wspec, vec],
        out_shape=[jax.ShapeDtypeStruct(dh.shape, dh.dtype), jax.ShapeDtypeStruct((len(POOL_WINDOWS), PG, PG), F32),
                   jax.ShapeDtypeStruct((1, D), F32)],
        scratch=[pltpu.VMEM((HALO, D), F32)], args=(hh, wpool, scale, pooled, dyb, dh), aliases={5: 0})


def _merge_fwd(hh, x, ya, yb, wpa, wpb, wout, b_merge, ln_g, ln_b, *, tt):
    s = x.shape[0]

    def body(ml_ref, x_ref, ya_ref, yb_ref, wpa_ref, wpb_ref, wout_ref, bm_ref, g_ref, b_ref, r_ref, xn_ref, xnb_ref):
        pa = _dot(ya_ref[...], wpa_ref[...], NN)
        pb = _dot(yb_ref[...], wpb_ref[...], NN)
        merged = _sigmoid(ml_ref[:, :D] + bm_ref[:, :D]) * pa + _sigmoid(ml_ref[:, D:] + bm_ref[:, D:]) * pb
        r = ALPHA * x_ref[...] + _dot(merged, wout_ref[...], NN)
        r_ref[...] = r
        mu = jnp.mean(r, axis=-1, keepdims=True)
        xc = r - mu
        var = jnp.mean(xc * xc, axis=-1, keepdims=True)
        xn = xc * lax.rsqrt(var + EPS) * g_ref[...] + b_ref[...]
        xn_ref[...] = xn
        xnb_ref[...] = xn.astype(xnb_ref.dtype)

    tile = pl.BlockSpec((tt, D), lambda i: (i, 0))
    full = pl.BlockSpec((D, D), lambda i: (0, 0), pipeline_mode=pl.Buffered(1))
    vec = pl.BlockSpec((1, D), lambda i: (0, 0))
    return _call(
        body, name="merge_fwd", grid=(s // tt,),
        in_specs=[pl.BlockSpec((tt, 2 * D), lambda i: (i, ML0 // (2 * D))), tile, tile, tile, full, full, full,
                  pl.BlockSpec((1, 2 * D), lambda i: (0, 0)), vec, vec],
        out_specs=[tile] * 3, out_shape=[jax.ShapeDtypeStruct((s, D), F32)] * 2 + [jax.ShapeDtypeStruct((s, D), MXU)],
        args=(hh, x, ya, yb, wpa, wpb, wout, b_merge, ln_g, ln_b), sem=("parallel",), vmem=VMEM_BIG)


def _merge_bwd(hh, r, ya, yb, dout, wpa, wpb, wout, b_merge, ln_g, *, tt):
    s = r.shape[0]

    def body(ml_ref, r_ref, ya_ref, yb_ref, do_ref, wpa_ref, wpb_ref, wout_ref, bm_ref, g_ref,
             dh_ref, dr_ref, dpa_ref, dpb_ref, dwout_ref, dg_ref, db_ref, dbm_ref):
        @pl.when(pl.program_id(0) == 0)
        def _():
            dwout_ref[...] = jnp.zeros_like(dwout_ref)
            dg_ref[...] = jnp.zeros_like(dg_ref)
            db_ref[...] = jnp.zeros_like(db_ref)
            dbm_ref[...] = jnp.zeros_like(dbm_ref)

        rr = r_ref[...]
        mu = jnp.mean(rr, axis=-1, keepdims=True)
        xc = rr - mu
        rstd = lax.rsqrt(jnp.mean(xc * xc, axis=-1, keepdims=True) + EPS)
        xhat = xc * rstd
        do = do_ref[...]
        dg_ref[...] += jnp.sum(do * xhat, axis=0, keepdims=True)
        db_ref[...] += jnp.sum(do, axis=0, keepdims=True)
        dxh = do * g_ref[...]
        dr = rstd * (dxh - jnp.mean(dxh, axis=-1, keepdims=True) - xhat * jnp.mean(dxh * xhat, axis=-1, keepdims=True))
        dr_ref[...] = dr
        g_a = _sigmoid(ml_ref[:, :D] + bm_ref[:, :D])
        g_b = _sigmoid(ml_ref[:, D:] + bm_ref[:, D:])
        pa = _dot(ya_ref[...], wpa_ref[...], NN)
        pb = _dot(yb_ref[...], wpb_ref[...], NN)
        dwout_ref[...] += _dot(g_a * pa + g_b * pb, dr, TN)
        dm = _dot(dr, wout_ref[...], NT)
        dpa_ref[...] = (dm * g_a).astype(dpa_ref.dtype)
        dpb_ref[...] = (dm * g_b).astype(dpb_ref.dtype)
        dml_a = dm * pa * g_a * (1.0 - g_a)
        dml_b = dm * pb * g_b * (1.0 - g_b)
        dh_ref[:, :D] = dml_a.astype(dh_ref.dtype)
        dh_ref[:, D:] = dml_b.astype(dh_ref.dtype)
        dbm_ref[:, :D] += jnp.sum(dml_a, axis=0, keepdims=True)
        dbm_ref[:, D:] += jnp.sum(dml_b, axis=0, keepdims=True)

    tile = pl.BlockSpec((tt, D), lambda i: (i, 0))
    full = pl.BlockSpec((D, D), lambda i: (0, 0))
    vec = pl.BlockSpec((1, D), lambda i: (0, 0))
    vec2 = pl.BlockSpec((1, 2 * D), lambda i: (0, 0))
    mlb = pl.BlockSpec((tt, 2 * D), lambda i: (i, ML0 // (2 * D)))
    return _call(
        body, name="merge_bwd", grid=(s // tt,),
        in_specs=[mlb, tile, tile, tile, tile, full, full, full, vec2, vec],
        out_specs=[mlb, tile, tile, tile, full, vec, vec, vec2],
        out_shape=[
            jax.ShapeDtypeStruct((s, HP), MXU), jax.ShapeDtypeStruct((s, D), F32),
            jax.ShapeDtypeStruct((s, D), MXU), jax.ShapeDtypeStruct((s, D), MXU),
            jax.ShapeDtypeStruct((D, D), F32), jax.ShapeDtypeStruct((1, D), F32),
            jax.ShapeDtypeStruct((1, D), F32), jax.ShapeDtypeStruct((1, 2 * D), F32),
        ],
        args=(hh, r, ya, yb, dout, wpa, wpb, wout, b_merge, ln_g), vmem=VMEM_BIG)


def _proj_bwd(ya, yb, dpa, dpb, wpa, wpb, *, tt):
    s = ya.shape[0]

    def body(ya_ref, yb_ref, dpa_ref, dpb_ref, wpa_ref, wpb_ref, dya_ref, dyb_ref, dwa_ref, dwb_ref):
        @pl.when(pl.program_id(0) == 0)
        def _():
            dwa_ref[...] = jnp.zeros_like(dwa_ref)
            dwb_ref[...] = jnp.zeros_like(dwb_ref)

        for y_ref, dp_ref, w_ref, dy_ref, dw_ref in ((ya_ref, dpa_ref, wpa_ref, dya_ref, dwa_ref),
                                                     (yb_ref, dpb_ref, wpb_ref, dyb_ref, dwb_ref)):
            dp = dp_ref[...]
            dy_ref[...] = _dot(dp, w_ref[...], NT)
            dw_ref[...] += _dot(y_ref[...], dp, TN)

    tile = pl.BlockSpec((tt, D), lambda i: (i, 0))
    full = pl.BlockSpec((D, D), lambda i: (0, 0))
    return _call(
        body, name="proj_bwd", grid=(s // tt,), in_specs=[tile] * 4 + [full] * 2, out_specs=[tile, tile, full, full],
        out_shape=[jax.ShapeDtypeStruct((s, D), F32)] * 2 + [jax.ShapeDtypeStruct((D, D), F32)] * 2,
        args=(ya, yb, dpa, dpb, wpa, wpb), vmem=VMEM_BIG)


def _alpha_grads(dz, wup, hh, dh, *, tm):
    s = dz.shape[0]

    def body(dz_ref, w_ref, al_ref, _dh_in, dh_ref, dw_ref):
        @pl.when(pl.program_id(0) == 0)
        def _():
            dw_ref[...] = jnp.zeros_like(dw_ref)

        dz_t = dz_ref[...]
        dh_ref[...] = _dot(dz_t, w_ref[...], NT).astype(dh_ref.dtype)
        dw_ref[...] += _dot(al_ref[...], dz_t, TN)

    return _call(
        body, name="alpha_grads", grid=(s // tm,),
        in_specs=[pl.BlockSpec((tm, DK), lambda i: (i, 0)), pl.BlockSpec((AL_W, DK), lambda i: (0, 0)),
                  pl.BlockSpec((tm, 128), lambda i: (i, AL0 // 128)), ANY],
        out_specs=[pl.BlockSpec((tm, AL_W), lambda i: (i, AL0 // AL_W)), pl.BlockSpec((128, DK), lambda i: (0, 0))],
        out_shape=[jax.ShapeDtypeStruct(dh.shape, dh.dtype), jax.ShapeDtypeStruct((128, DK), F32)],
        args=(dz, wup, hh, dh), aliases={3: 0})


def _loss_head(y, target, *, tt):
    s = y.shape[0]

    def body(y_ref, t_ref, loss_ref, dy_ref):
        @pl.when(pl.program_id(0) == 0)
        def _():
            loss_ref[...] = jnp.zeros_like(loss_ref)

        err = y_ref[...] - t_ref[...]
        dy_ref[...] = err * (1.0 / D)
        per_tok = jnp.mean(err * err, axis=-1, keepdims=True)
        loss_ref[...] += 0.5 * jnp.sum(per_tok, axis=0, keepdims=True)

    tile = pl.BlockSpec((tt, D), lambda i: (i, 0))
    return _call(
        body, name="loss_head", grid=(s // tt,), in_specs=[tile, tile],
        out_specs=[pl.BlockSpec((1, 1), lambda i: (0, 0)), tile],
        out_shape=[jax.ShapeDtypeStruct((1, 1), F32), jax.ShapeDtypeStruct((s, D), F32)], args=(y, target))


def _adamw_math(share, w_ref, m_ref, v_ref, g_ref, d_ref, nm_ref, nv_ref):
    g = share(0).astype(F32)
    for q in range(1, N_DEV):
        g = g + share(q).astype(F32)
    g_ref[0] = g
    nm = ADAM_B1 * m_ref[0] + (1.0 - ADAM_B1) * g
    nv = ADAM_B2 * v_ref[0] + (1.0 - ADAM_B2) * (g * g)
    nm_ref[0] = nm
    nv_ref[0] = nv
    m_hat = nm / (1.0 - ADAM_B1 ** ADAM_STEP)
    v_hat = nv / (1.0 - ADAM_B2 ** ADAM_STEP)
    d_ref[0] = -ADAM_LR * (m_hat / (jnp.sqrt(v_hat) + ADAM_EPS) + ADAM_WD * w_ref[0])


def _adamw_layers(parts, w, m, v, *, tc, name):
    nl, rows, cols = w.shape
    nc = cols // tc

    def body(*refs):
        p_refs, rest = refs[:nl], refs[nl:]
        for j in range(nl):
            @pl.when(pl.program_id(0) == j)
            def _(p_ref=p_refs[j]):
                _adamw_math(lambda q: p_ref[q], *rest)

    def part_spec(j):
        return pl.BlockSpec((N_DEV, rows, tc), lambda l, c: (0, 0, jnp.where(l == j, c, jnp.where(l < j, 0, nc - 1))))

    tile = pl.BlockSpec((1, rows, tc), lambda l, c: (l, 0, c))
    return _call(
        body, name=name, grid=(nl, nc),
        in_specs=[part_spec(j) for j in range(nl)] + [tile, tile, tile],
        out_specs=[tile] * 4, out_shape=[jax.ShapeDtypeStruct((nl, rows, cols), F32)] * 4,
        args=(*parts, w, m, v))


def _adamw(parts, w, m, v, *, tr, tc, name):
    nl, rows, cols = w.shape

    def body(p_ref, *rest):
        _adamw_math(lambda q: p_ref[0, q], *rest)

    tile = pl.BlockSpec((1, tr, tc), lambda l, i, j: (l, i, j))
    return _call(
        body, name=name, grid=(nl, rows // tr, cols // tc),
        in_specs=[pl.BlockSpec((1, N_DEV, tr, tc), lambda l, i, j: (l, 0, i, j)), tile, tile, tile],
        out_specs=[tile] * 4, out_shape=[jax.ShapeDtypeStruct((nl, rows, cols), F32)] * 4,
        args=(parts, w, m, v), sem=("parallel", "parallel", "parallel"))


def _from_devices(g, axis):
    nd = g.ndim - 1
    perm = list(range(1, axis + 1)) + [0] + list(range(axis + 1, nd + 1))
    shape = list(g.shape[1:])
    shape[axis] *= N_DEV
    return jnp.transpose(g, perm).reshape(shape)


def _to_devices(a, axis):
    shape = list(a.shape)
    t = a.reshape(shape[:axis] + [N_DEV, shape[axis] // N_DEV] + shape[axis + 1:])
    return jnp.transpose(t, [axis] + list(range(0, axis)) + list(range(axis + 1, t.ndim)))


def _h_row_segments():
    segs = [(O_PI, PI0, IN_COLS - O_PI), (O_AL, AL0, RANK)]
    for h in range(HEADS):
        base = HD0 + h * HEAD_W
        segs += [(O_Q + h * HDK, base, HDK), (O_K + h * HDK, base + HDK, HDK),
                 (O_V + h * HDV, base + 2 * HDK, HDV), (O_GA + h * HDV, base + 2 * HDK + HDV, HDV)]
    return segs


def _h_weight_t(parts):
    return _move_rows(parts, _h_row_segments(), (HP, D), name="w_in_rows", zero=(AL0 + RANK, AL_W - RANK))


def _w_in_grad_parts_t(dwt):
    return _move_rows(dwt, [(d0, s0, n) for s0, d0, n in _h_row_segments()], (N_DEV, SHARD, D), name="w_in_grad_rows")


def kernel(x, w_in, w_alpha_up, b_alpha, gla_norm_g, w_pool_grp, pool_scale, b_merge, w_proj_a, w_proj_b, w_out, ln_g, ln_b, loss_target, m_w_in, m_w_alpha_up, m_b_alpha, m_gla_norm_g, m_w_pool_grp, m_pool_scale, m_b_merge, m_w_proj_a, m_w_proj_b, m_w_out, m_ln_g, m_ln_b, v_w_in, v_w_alpha_up, v_b_alpha, v_gla_norm_g, v_w_pool_grp, v_pool_scale, v_b_merge, v_w_proj_a, v_w_proj_b, v_w_out, v_ln_g, v_ln_b):
    s = x.shape[1]
    tt = min(256, s)
    tm = min(512, s)
    tb = min(1024, s)
    tn = HP // 3
    xs = x.reshape(s, D)

    tr3 = lambda a: jnp.transpose(a, (0, 2, 1))
    w_in_s = tr3(w_in).astype(WIRE)
    proj_s = jnp.stack([w_proj_a, w_proj_b, w_out], axis=1).astype(WIRE)
    pool_s = w_pool_grp.astype(WIRE)

    g_in, g_up, g_gn = _gather_first(w_in_s[0], [w_alpha_up.astype(WIRE), gla_norm_g], name="gather_first")
    wup = jnp.pad(_from_devices(g_up, 2), ((0, 0), (0, AL_W - RANK), (0, 0)))
    gn = _from_devices(g_gn, 2).reshape(DEPTH, 1, D)

    saved, wt_all, proj_all, pool_all = [], [], [], []
    cur, cur_b = xs, xs.astype(MXU)
    g_proj = g_pool = None
    for l in range(DEPTH):
        wt = _h_weight_t(g_in)
        nxt_l = l + 1 < DEPTH
        steps = (HP // tn) * (s // tb)
        res = _in_proj(cur_b, wt, tm=tb, tn=tn, ride=_ChipGather(w_in_s[l + 1], (2 * steps) // 3) if nxt_l else None)
        hh = res[0]
        if nxt_l:
            g_in = res[1]
        layers = ([0] if l == 0 else []) + ([l + 1] if nxt_l else [])
        res = _gla_fwd(hh, wup[l], b_alpha[l:l + 1], gn[l], tt=tb,
                       ride=_Exchange([(a[j], True) for j in layers for a in (proj_s, pool_s)]) if layers else None)
        o, ya, states = res[:3]
        got = {j: res[3 + 2 * t:5 + 2 * t] for t, j in enumerate(layers)}
        if l == 0:
            g_proj, g_pool = got[0]
        proj = _from_devices(g_proj, 1)
        pool = _from_devices(g_pool, 1)
        if nxt_l:
            g_proj, g_pool = got[l + 1]
        wt_all.append(wt), proj_all.append(proj), pool_all.append(pool)
        pooled, yb = _pool_fwd(hh, pool, pool_scale[l:l + 1], tt=tm)
        r, nxt, nxt_b = _merge_fwd(hh, cur, ya, yb, proj[0], proj[1], proj[2],
                                   b_merge[l:l + 1], ln_g[l:l + 1], ln_b[l:l + 1], tt=tm)
        saved.append(dict(xb=cur_b, hh=hh, o=o, ya=ya, states=states, pooled=pooled, yb=yb, r=r))
        cur, cur_b = nxt, nxt_b

    loss_part, dcur = _loss_head(cur, loss_target.reshape(s, D), tt=tm)
    loss = lax.psum(loss_part[0, 0], ("x", "y", "c"))

    small = {k: [None] * DEPTH for k in ("w_up", "b_alpha", "gnorm", "pool_scale", "b_merge", "ln_g", "ln_b")}
    parts = {k: [None] * DEPTH for k in ("w_in", "proj", "pool")}
    for l in range(DEPTH - 1, -1, -1):
        sv = saved[l]
        hh = sv["hh"]
        dh, dr, dpa, dpb, dw_out, dln_g, dln_b, db_merge = _merge_bwd(
            hh, sv["r"], sv["ya"], sv["yb"], dcur, proj_all[l][0], proj_all[l][1], proj_all[l][2], b_merge[l:l + 1], ln_g[l:l + 1], tt=tt)
        dya, dyb, dw_pa, dw_pb = _proj_bwd(sv["ya"], sv["yb"], dpa, dpb, proj_all[l][0], proj_all[l][1], tt=tm)
        dh, dw_pool, dscale = _pool_bwd(hh, pool_all[l], pool_scale[l:l + 1], sv["pooled"], dyb, dh, tt=tm)
        ride = _Exchange([(_to_devices(jnp.stack([dw_pa, dw_pb, dw_out]), 1).astype(WIRE), False),
                          (_to_devices(dw_pool, 1).astype(WIRE), False)])
        dh, dz, dgn, db_al, parts["proj"][l], parts["pool"][l] = _gla_bwd(
            hh, wup[l], b_alpha[l:l + 1], gn[l], sv["o"], sv["states"], dya, dh, tt=tb, ride=ride)
        dh, dw_up = _alpha_grads(dz, wup[l], hh, dh, tm=tb)
        dwt = _mm_tn(dh, sv["xb"], tm=tb, tk=tn, name="w_in_grad", out_dtype=WIRE)
        dcur, parts["w_in"][l] = _in_proj_bwd(dh, wt_all[l], dr, tm=tt, ride=_Exchange([(_w_in_grad_parts_t(dwt), False)]))

        small["w_up"][l] = dw_up[:RANK]
        small["b_alpha"][l] = db_al.reshape(DK)
        small["gnorm"][l] = dgn.reshape(HEADS, HDV)
        small["pool_scale"][l] = dscale[0]
        small["b_merge"][l] = db_merge[0]
        small["ln_g"][l], small["ln_b"][l] = dln_g[0], dln_b[0]
    grad_x = dcur[None]
    sm = {k: jnp.stack(v) for k, v in small.items()}

    rep = (("b_alpha", b_alpha, m_b_alpha, v_b_alpha), ("pool_scale", pool_scale, m_pool_scale, v_pool_scale),
           ("b_merge", b_merge, m_b_merge, v_b_merge), ("ln_g", ln_g, m_ln_g, v_ln_g), ("ln_b", ln_b, m_ln_b, v_ln_b))
    cat = lambda arrs: jnp.concatenate(arrs, axis=1)
    p_up, p_gn, p_rep = _exchange([(_to_devices(sm["w_up"], 2), False), (_to_devices(sm["gnorm"], 2), False),
                                   (cat([sm[nm] for nm, _, _, _ in rep]), True)], name="exchange_small_grads")

    def update(p, w, m, v, tr, name, layered=True, tc=None):
        shape = w.shape
        nl = shape[0] if layered else 1
        cols = shape[-1]
        flat = lambda a: a.reshape(nl, -1, cols)
        outs = _adamw(p.reshape(nl, N_DEV, -1, cols), flat(w), flat(m), flat(v), tr=tr, tc=tc or cols, name=name)
        return [o_.reshape(shape) for o_ in outs]

    res = {}
    res["w_in"] = [tr3(o_) for o_ in _adamw_layers(parts["w_in"], tr3(w_in), tr3(m_w_in), tr3(v_w_in), tc=128, name="adamw_w_in")]
    for j, (nm, w, m, v) in enumerate((("w_proj_a", w_proj_a, m_w_proj_a, v_w_proj_a), ("w_proj_b", w_proj_b, m_w_proj_b, v_w_proj_b),
                                       ("w_out", w_out, m_w_out, v_w_out))):
        res[nm] = _adamw_layers([p[:, j] for p in parts["proj"]], w, m, v, tc=D, name="adamw_" + nm)
    res["w_pool_grp"] = update(jnp.stack(parts["pool"]), w_pool_grp, m_w_pool_grp, v_w_pool_grp, 128, "adamw_w_pool")
    res["w_alpha_up"] = update(p_up, w_alpha_up, m_w_alpha_up, v_w_alpha_up, DEPTH * RANK, "adamw_w_up", layered=False)
    res["gla_norm_g"] = update(p_gn, gla_norm_g, m_gla_norm_g, v_gla_norm_g, DEPTH * HEADS, "adamw_gnorm", layered=False)
    rep_out = update(p_rep, cat([w for _, w, _, _ in rep]), cat([m for _, _, m, _ in rep]), cat([v for _, _, _, v in rep]),
                     DEPTH, "adamw_small", layered=False)
    off = 0
    for nm, w, _, _ in rep:
        n = w.shape[1]
        res[nm] = [o_[:, off:off + n] for o_ in rep_out]
        off += n

    order = ("w_in", "w_alpha_up", "b_alpha", "gla_norm_g", "w_pool_grp", "pool_scale", "b_merge", "w_proj_a", "w_proj_b",
             "w_out", "ln_g", "ln_b")
    return (loss, grad_x, *[res[n][0] for n in order], *[res[n][1] for n in order],
            *[res[n][2] for n in order], *[res[n][3] for n in order])
```

```python
import jax
import jax.numpy as jnp
from jax import lax
from jax.experimental import pallas as pl
from jax.experimental.pallas import tpu as pltpu

F32 = jnp.float32
MXU = jnp.bfloat16
WIRE = jnp.bfloat16

N_DEV = 8
DEPTH = 4
D = 1024
HEADS = 4
DK = D // 2
HDK = DK // HEADS
HDV = D // HEADS
RANK = 16
CHUNK = 64
GATE_TAU = 16.0
POOL_WINDOWS = (2, 4, 8, 16)
PG = D // len(POOL_WINDOWS)
HALO = 16
IN_COLS = 7184
SHARD = IN_COLS // N_DEV
ALPHA = (2.0 * DEPTH) ** 0.25
EPS = 1e-5
Q_SCALE = HDK ** -0.5

ADAM_LR, ADAM_B1, ADAM_B2, ADAM_EPS, ADAM_WD, ADAM_STEP = 0.001, 0.9, 0.999, 1e-08, 0.01, 10

PI0, GB0, ML0, AL0, AL_W = 0, D, 2 * D, 4 * D, 512
HD0 = AL0 + AL_W
HEAD_W = 2 * HDK + 2 * HDV
HP = HD0 + HEADS * HEAD_W
HPB = 2
O_Q, O_K, O_V, O_GA, O_AL, O_PI, O_GB, O_ML = 0, DK, 2 * DK, 2 * DK + D, 2 * DK + 2 * D, 2 * DK + 2 * D + RANK, \
    2 * DK + 3 * D + RANK, 2 * DK + 4 * D + RANK

VMEM_BIG = 56 * 1024 * 1024
VMEM_MID = 40 * 1024 * 1024

NN = ((1,), (0,))
NT = ((1,), (1,))
TN = ((0,), (0,))

HBM = pl.BlockSpec(memory_space=pltpu.HBM)
ANY = pl.BlockSpec(memory_space=pl.ANY)


def _dot(a, b, dims):
    return lax.dot_general(a.astype(MXU), b.astype(MXU), (dims, ((), ())), preferred_element_type=F32)


def _params(sem, vmem):
    return pltpu.CompilerParams(dimension_semantics=sem, vmem_limit_bytes=vmem)


def _sigmoid(x):
    return 1.0 / (1.0 + jnp.exp(-x))


def _log_sigmoid(z):
    return jnp.minimum(z, 0.0) - jnp.log(1.0 + jnp.exp(-jnp.abs(z)))


class _Exchange:
    def __init__(self, items):
        self.items = [(s, bool(g)) for s, g in items]
        self.n = len(self.items)
        self.srcs = [s for s, _ in self.items]
        self.in_specs = [HBM] * self.n
        self.out_specs = [HBM] * self.n
        self.out_shape = [jax.ShapeDtypeStruct((N_DEV,) + tuple(s.shape if g else s.shape[1:]), s.dtype) for s, g in self.items]
        self.scratch = [pltpu.SemaphoreType.DMA((self.n * (N_DEV - 1),)), pltpu.SemaphoreType.DMA((self.n * (N_DEV - 1),)),
                        pltpu.SemaphoreType.DMA((self.n,))]

    def copies(self, src_refs, out_refs, send_sems, recv_sems, local_sems):
        x, y, c = lax.axis_index("x"), lax.axis_index("y"), lax.axis_index("c")
        me = 4 * x + 2 * y + c
        copies = []
        for t, (_, gather) in enumerate(self.items):
            src_ref, out_ref = src_refs[t], out_refs[t]
            copies.append(pltpu.make_async_copy(src_ref if gather else src_ref.at[me], out_ref.at[me], local_sems.at[t]))
            for k in range(1, N_DEV):
                px = 1 - x if k & 4 else x
                py = 1 - y if k & 2 else y
                pc = 1 - c if k & 1 else c
                peer = 4 * px + 2 * py + pc
                sem = t * (N_DEV - 1) + k - 1
                copies.append(pltpu.make_async_remote_copy(
                    src_ref=src_ref if gather else src_ref.at[peer],
                    dst_ref=out_ref.at[me],
                    send_sem=send_sems.at[sem],
                    recv_sem=recv_sems.at[sem],
                    device_id=(px, py, pc),
                    device_id_type=pl.DeviceIdType.MESH,
                ))
        return copies


    mid_step = None

    def start(self, *refs):
        for cp in self.copies(*refs):
            cp.start()

    def finish(self, *refs):
        for cp in self.copies(*refs):
            cp.wait()


class _ChipGather:
    def __init__(self, src, mid_step):
        self.n = 1
        self.srcs = [src]
        self.mid_step = mid_step
        self.in_specs = [HBM]
        self.out_specs = [HBM]
        self.out_shape = [jax.ShapeDtypeStruct((N_DEV,) + tuple(src.shape), src.dtype)]
        self.scratch = [pltpu.SemaphoreType.DMA((N_DEV - 1,)), pltpu.SemaphoreType.DMA((N_DEV - 1,)), pltpu.SemaphoreType.DMA((1,))]

    def _plan(self, src_refs, out_refs, send_sems, recv_sems, local_sems):
        src_ref, out_ref = src_refs[0], out_refs[0]
        x, y, c = lax.axis_index("x"), lax.axis_index("y"), lax.axis_index("c")
        me, sibling = (x, y, c), (x, y, 1 - c)
        chips = [(1 - x, y), (x, 1 - y), (1 - x, 1 - y)]

        def copy(k, block, to, src=None):
            slot = out_ref.at[4 * block[0] + 2 * block[1] + block[2]]
            return pltpu.make_async_remote_copy(
                src_ref=slot if src is None else src, dst_ref=slot, send_sem=send_sems.at[k], recv_sem=recv_sems.at[k],
                device_id=to, device_id_type=pl.DeviceIdType.MESH)

        mine = pltpu.make_async_copy(src_ref, out_ref.at[4 * x + 2 * y + c], local_sems.at[0])
        first = [copy(0, me, sibling, src=src_ref)] + [copy(1 + j, me, (*chip, c), src=src_ref) for j, chip in enumerate(chips)]
        landed = [copy(1 + j, (*chip, c), me) for j, chip in enumerate(chips)]
        passed = [copy(4 + j, (*chip, c), sibling) for j, chip in enumerate(chips)]
        from_sibling = [copy(0, sibling, me)] + [copy(4 + j, (*chip, 1 - c), me) for j, chip in enumerate(chips)]
        return mine, first, landed, passed, from_sibling

    def start(self, *refs):
        mine, first, _, _, _ = self._plan(*refs)
        mine.start()
        for cp in first:
            cp.start()

    def mid(self, *refs):
        _, _, landed, passed, _ = self._plan(*refs)
        for got, fwd in zip(landed, passed):
            got.wait_recv()
            fwd.start()

    def finish(self, *refs):
        mine, first, _, passed, from_sibling = self._plan(*refs)
        for cp in from_sibling:
            cp.wait_recv()
        for cp in first + passed:
            cp.wait_send()
        mine.wait()


def _grid_ends(grid):
    first = last = step = None
    for a, n in enumerate(grid):
        f = pl.program_id(a) == 0
        e = pl.program_id(a) == n - 1
        first = f if first is None else first & f
        last = e if last is None else last & e
        step = pl.program_id(a) if step is None else step * n + pl.program_id(a)
    return first, last, step


def _call(body, *, name, grid, in_specs, out_specs, out_shape, args, scratch=(), sem=None, vmem=VMEM_MID, ride=None, aliases=None):
    n_in, n_out, n_scr = len(in_specs), len(out_specs), len(scratch)
    sem = sem or ("arbitrary",) * len(grid)
    if ride is None:
        return pl.pallas_call(body, name=name, grid=grid, in_specs=in_specs, out_specs=out_specs, out_shape=out_shape,
                              scratch_shapes=list(scratch), compiler_params=_params(sem, vmem),
                              input_output_aliases=aliases or {})(*args)
    r = ride.n

    def riding(*refs):
        ins, rsrc = refs[:n_in], refs[n_in:n_in + r]
        outs, rout = refs[n_in + r:n_in + r + n_out], refs[n_in + r + n_out:n_in + 2 * r + n_out]
        scr = refs[n_in + 2 * r + n_out:n_in + 2 * r + n_out + n_scr]
        send_sems, recv_sems, local_sems = refs[n_in + 2 * r + n_out + n_scr:]
        first, last, step = _grid_ends(grid)
        comm = (rsrc, rout, send_sems, recv_sems, local_sems)

        @pl.when(first)
        def _():
            ride.start(*comm)

        body(*ins, *outs, *scr)

        if ride.mid_step is not None:
            @pl.when(step == ride.mid_step)
            def _():
                ride.mid(*comm)

        @pl.when(last)
        def _():
            ride.finish(*comm)

    return pl.pallas_call(riding, name=name, grid=grid, in_specs=list(in_specs) + ride.in_specs,
                          out_specs=list(out_specs) + ride.out_specs, out_shape=list(out_shape) + ride.out_shape,
                          scratch_shapes=list(scratch) + ride.scratch,
                          compiler_params=_params(("arbitrary",) * len(grid), vmem),
                          input_output_aliases=aliases or {})(*args, *ride.srcs)


def _exchange(items, *, name):
    ex = _Exchange(items)

    def body(*refs):
        copies = ex.copies(refs[:ex.n], refs[ex.n:2 * ex.n], *refs[2 * ex.n:])
        for cp in copies:
            cp.start()
        for cp in copies:
            cp.wait()

    return pl.pallas_call(body, name=name, in_specs=ex.in_specs, out_specs=ex.out_specs, out_shape=ex.out_shape,
                          scratch_shapes=ex.scratch)(*ex.srcs)


def _gather_first(big, smalls, *, name):
    ex = _Exchange([(a, True) for a in smalls])

    def body(*refs):
        big_ref, small_src = refs[0], refs[1:1 + ex.n]
        out_ref, small_out = refs[1 + ex.n], refs[2 + ex.n:2 + 2 * ex.n]
        send_sems, recv_sems, local_sem = refs[2 + 2 * ex.n:5 + 2 * ex.n]
        x, y, c = lax.axis_index("x"), lax.axis_index("y"), lax.axis_index("c")
        me, sibling = (x, y, c), (x, y, 1 - c)
        chips = [(1 - x, y), (x, 1 - y), (1 - x, 1 - y)]

        def slot(px, py, pc):
            return out_ref.at[4 * px + 2 * py + pc]

        def copy(k, block, to, src=None):
            return pltpu.make_async_remote_copy(
                src_ref=slot(*block) if src is None else src, dst_ref=slot(*block),
                send_sem=send_sems.at[k], recv_sem=recv_sems.at[k], device_id=to, device_id_type=pl.DeviceIdType.MESH)

        small = ex.copies(small_src, small_out, *refs[5 + 2 * ex.n:])
        mine = pltpu.make_async_copy(big_ref, slot(*me), local_sem)
        mine.start()
        first = [copy(0, me, sibling, src=big_ref)] + [copy(1 + j, me, (*chip, c), src=big_ref) for j, chip in enumerate(chips)]
        for cp in first + small:
            cp.start()
        passed = [copy(4 + j, (*chip, c), sibling) for j, chip in enumerate(chips)]
        for j, chip in enumerate(chips):
            copy(1 + j, (*chip, c), me).wait_recv()
            passed[j].start()
        copy(0, sibling, me).wait_recv()
        for j, chip in enumerate(chips):
            copy(4 + j, (*chip, 1 - c), me).wait_recv()
        for cp in first + passed:
            cp.wait_send()
        mine.wait()
        for cp in small:
            cp.wait()

    return pl.pallas_call(
        body, name=name, in_specs=[HBM] + ex.in_specs, out_specs=[HBM] + ex.out_specs,
        out_shape=[jax.ShapeDtypeStruct((N_DEV,) + tuple(big.shape), big.dtype)] + ex.out_shape,
        scratch_shapes=[pltpu.SemaphoreType.DMA((N_DEV - 1,)), pltpu.SemaphoreType.DMA((N_DEV - 1,)), pltpu.SemaphoreType.DMA]
        + ex.scratch)(big, *ex.srcs)


def _move_rows(src, segs, out_shape, *, name, zero=None):
    cols = src.shape[-1]
    step = 256

    def at(shape, row, m):
        return (slice(row, row + m),) if len(shape) == 2 else (row // SHARD, slice(row % SHARD, row % SHARD + m))

    def room(shape, row):
        return step if len(shape) == 2 else SHARD - row % SHARD

    def body(src_ref, out_ref):
        for s0, d0, n in segs:
            r = 0
            while r < n:
                m = min(step, n - r, room(src.shape, s0 + r), room(out_shape, d0 + r))
                out_ref[(*at(out_shape, d0 + r, m), slice(None))] = src_ref[(*at(src.shape, s0 + r, m), slice(None))]
                r += m
        if zero is not None:
            out_ref[zero[0]:zero[0] + zero[1], :] = jnp.zeros((zero[1], cols), src.dtype)

    vmem = pl.BlockSpec(memory_space=pltpu.VMEM)
    return pl.pallas_call(
        body, name=name, in_specs=[vmem], out_specs=vmem, out_shape=jax.ShapeDtypeStruct(tuple(out_shape), src.dtype),
        compiler_params=pltpu.CompilerParams(vmem_limit_bytes=VMEM_BIG))(src)


def _in_proj(xb, wt, *, tm, tn, ride=None):
    m, k = xb.shape
    n = wt.shape[0]

    def body(x_ref, w_ref, o_ref):
        o_ref[...] = _dot(x_ref[...], w_ref[...], NT)

    return _call(
        body, name="in_proj", grid=(n // tn, m // tm),
        in_specs=[pl.BlockSpec((tm, k), lambda j, i: (i, 0)), pl.BlockSpec((tn, k), lambda j, i: (j, 0))],
        out_specs=[pl.BlockSpec((tm, tn), lambda j, i: (i, j))],
        out_shape=[jax.ShapeDtypeStruct((m, n), F32)],
        args=(xb, wt), sem=("parallel", "parallel"), vmem=VMEM_BIG, ride=ride)


def _in_proj_bwd(dh, wt, dr, *, tm, ride=None):
    m, n = dh.shape
    k = wt.shape[1]

    def body(dh_ref, w_ref, dr_ref, o_ref):
        o_ref[...] = ALPHA * dr_ref[...] + _dot(dh_ref[...], w_ref[...], NN)

    return _call(
        body, name="in_proj_bwd", grid=(m // tm,),
        in_specs=[pl.BlockSpec((tm, n), lambda i: (i, 0)),
                  pl.BlockSpec((n, k), lambda i: (0, 0), pipeline_mode=pl.Buffered(1)),
                  pl.BlockSpec((tm, k), lambda i: (i, 0))],
        out_specs=[pl.BlockSpec((tm, k), lambda i: (i, 0))],
        out_shape=[jax.ShapeDtypeStruct((m, k), F32)],
        args=(dh, wt, dr), sem=("parallel",), vmem=VMEM_BIG, ride=ride)


def _mm_tn(a, dc, *, tm, tk, name, out_dtype=F32):
    m, k = a.shape
    n = dc.shape[1]
    ni = m // tm

    def body(a_ref, dc_ref, o_ref, acc):
        i = pl.program_id(1)

        @pl.when(i == 0)
        def _():
            acc[...] = jnp.zeros_like(acc)

        acc[...] += _dot(a_ref[...], dc_ref[...], TN)

        @pl.when(i == ni - 1)
        def _():
            o_ref[...] = acc[...].astype(o_ref.dtype)

    return _call(
        body, name=name, grid=(k // tk, ni),
        in_specs=[pl.BlockSpec((tm, tk), lambda j, i: (i, j)), pl.BlockSpec((tm, n), lambda j, i: (i, 0))],
        out_specs=[pl.BlockSpec((tk, n), lambda j, i: (j, 0))],
        out_shape=[jax.ShapeDtypeStruct((k, n), out_dtype)],
        scratch=[pltpu.VMEM((tk, n), F32)],
        args=(a, dc), sem=("parallel", "arbitrary"), vmem=VMEM_BIG)[0]


def _head_cols(p):
    b = p * HEAD_W
    return (slice(b, b + HDK), slice(b + HDK, b + 2 * HDK), slice(b + 2 * HDK, b + 2 * HDK + HDV),
            slice(b + 2 * HDK + HDV, b + HEAD_W))


def _seg_cumsum(v, reverse=False):
    t, w = v.shape
    hi = v.astype(MXU)
    lo = (v - hi.astype(F32)).astype(MXU)
    terms = jnp.concatenate([hi, lo], axis=1)
    row = lax.broadcasted_iota(jnp.int32, (CHUNK, CHUNK), 0)
    col = lax.broadcasted_iota(jnp.int32, (CHUNK, CHUNK), 1)
    ones = jnp.where((row <= col) if reverse else (row >= col), 1.0, 0.0).astype(MXU)
    out = []
    for c in range(t // CHUNK):
        y = _dot(ones, terms[c * CHUNK:(c + 1) * CHUNK], NN)
        out.append(y[:, :w] + y[:, w:])
    return jnp.concatenate(out, axis=0)


def _seg_rcumsum_rolls(v):
    t = v.shape[0]
    rowmod = lax.broadcasted_iota(jnp.int32, v.shape, 0) % CHUNK
    sh = 1
    while sh < CHUNK:
        v = v + jnp.where(rowmod < CHUNK - sh, pltpu.roll(v, t - sh, 0), 0.0)
        sh *= 2
    return v


def _gla_decay(alpha_ref, wup_ref, b_ref, g_scr):
    z = _dot(alpha_ref[...], wup_ref[...], NN) + b_ref[...]
    g_scr[...] = _seg_cumsum(_log_sigmoid(z) * (1.0 / GATE_TAU))
    return z


QE1, KE1, QE2, KE2, QA, KD = range(6)
EP, EM, EA, EDL = range(4)
GW = HPB * HDK


def _gla_operands(hd_ref, g_scr, opnd_scr, fac_scr=None):
    t = g_scr.shape[0]

    def chunk_row(r):
        return jnp.concatenate([jnp.broadcast_to(g_scr[c * CHUNK + r:c * CHUNK + r + 1, :], (CHUNK, GW))
                                for c in range(t // CHUNK)], axis=0)

    g = g_scr[...]
    g_last = chunk_row(CHUNK - 1)
    ref = 0.5 * (chunk_row(0) + g_last)
    fac = {EP: jnp.exp(g - ref), EM: jnp.exp(ref - g), EA: jnp.exp(g), EDL: jnp.exp(g_last - g)}
    if fac_scr is not None:
        for j, f in fac.items():
            fac_scr[j] = f
    for p in range(HPB):
        qc, kc, _, _ = _head_cols(p)
        gc = slice(p * HDK, (p + 1) * HDK)
        qs = hd_ref[:, qc] * Q_SCALE
        k = hd_ref[:, kc]
        for j, (x, f) in {QE1: (qs, EP), KE1: (k, EM), QE2: (qs, EM), KE2: (k, EP), QA: (qs, EA), KD: (k, EDL)}.items():
            opnd_scr[j, :, gc] = (x * fac[f][:, gc]).astype(opnd_scr.dtype)


def _lower_mask():
    return lax.broadcasted_iota(jnp.int32, (CHUNK, CHUNK), 0) >= lax.broadcasted_iota(jnp.int32, (CHUNK, CHUNK), 1)


def _scores(opnd_scr, rows, gc, lower):
    return jnp.where(lower, _dot(opnd_scr[QE1, rows, gc], opnd_scr[KE1, rows, gc], NT),
                     _dot(opnd_scr[QE2, rows, gc], opnd_scr[KE2, rows, gc], NT))


def _gla_specs(tt, row):
    return [
        pl.BlockSpec((tt, HPB * HEAD_W), lambda h, i: (row(i), HD0 // (HPB * HEAD_W) + h)),
        pl.BlockSpec((tt, 128), lambda h, i: (row(i), AL0 // 128)),
        pl.BlockSpec((128, HPB * HDK), lambda h, i: (0, h)),
        pl.BlockSpec((1, HPB * HDK), lambda h, i: (0, h)),
        pl.BlockSpec((1, HPB * HDV), lambda h, i: (0, h)),
    ]


def _gla_fwd(hh, wup, b_alpha, gnorm, *, tt, ride=None):
    s = hh.shape[0]
    nt = s // tt
    nct = tt // CHUNK

    def body(hd_ref, al_ref, wup_ref, b_ref, gn_ref, o_ref, ya_ref, st_ref, state, g_scr, opnd_scr):
        @pl.when(pl.program_id(1) == 0)
        def _():
            state[...] = jnp.zeros_like(state)

        _gla_decay(al_ref, wup_ref, b_ref, g_scr)
        _gla_operands(hd_ref, g_scr, opnd_scr)
        lower = _lower_mask()
        for c in range(nct):
            rows = slice(c * CHUNK, (c + 1) * CHUNK)
            for p in range(HPB):
                vc = _head_cols(p)[2]
                gc = slice(p * HDK, (p + 1) * HDK)
                v = hd_ref[rows, vc]
                st = state[p]
                st_ref[p, c] = st
                egl = jnp.exp(g_scr[(c + 1) * CHUNK - 1:(c + 1) * CHUNK, gc])
                o_ref[rows, p * HDV:(p + 1) * HDV] = (_dot(_scores(opnd_scr, rows, gc, lower), v, NN)
                                                      + _dot(opnd_scr[QA, rows, gc], st, NT))
                state[p] = st * egl + _dot(v, opnd_scr[KD, rows, gc], TN)
        for p in range(HPB):
            oc = slice(p * HDV, (p + 1) * HDV)
            o = o_ref[:, oc]
            ohat = o * lax.rsqrt(jnp.mean(o * o, axis=-1, keepdims=True) + EPS)
            ga = hd_ref[:, _head_cols(p)[3]]
            ya_ref[:, oc] = (ohat * gn_ref[:, oc] * (ga * _sigmoid(ga))).astype(ya_ref.dtype)

    return _call(
        body, name="gla_fwd", grid=(HEADS // HPB, nt),
        in_specs=_gla_specs(tt, lambda i: i),
        out_specs=[
            pl.BlockSpec((tt, HPB * HDV), lambda h, i: (i, h)),
            pl.BlockSpec((tt, HPB * HDV), lambda h, i: (i, h)),
            pl.BlockSpec((HPB, nct, HDV, HDK), lambda h, i: (h, i, 0, 0)),
        ],
        out_shape=[
            jax.ShapeDtypeStruct((s, D), F32),
            jax.ShapeDtypeStruct((s, D), MXU),
            jax.ShapeDtypeStruct((HEADS, s // CHUNK, HDV, HDK), F32),
        ],
        scratch=[pltpu.VMEM((HPB, HDV, HDK), F32), pltpu.VMEM((tt, GW), F32), pltpu.VMEM((6, tt, GW), MXU)],
        args=(hh, hh, wup, b_alpha, gnorm), ride=ride, vmem=VMEM_BIG)


def _gla_bwd(hh, wup, b_alpha, gnorm, o, states, dya, dh, *, tt, ride=None):
    s = hh.shape[0]
    nt = s // tt
    nct = tt // CHUNK

    def body(hd_ref, al_ref, wup_ref, b_ref, gn_ref, o_ref, st_ref, dya_ref, _dh_in,
             dh_ref, dz_ref, dgn_ref, db_ref, dstate, g_scr, dg_scr, do_scr, opnd_scr, fac_scr, res_scr, dgl_scr):
        @pl.when(pl.program_id(1) == 0)
        def _():
            dstate[...] = jnp.zeros_like(dstate)
            dgn_ref[...] = jnp.zeros_like(dgn_ref)
            db_ref[...] = jnp.zeros_like(db_ref)

        z = _gla_decay(al_ref, wup_ref, b_ref, g_scr)
        _gla_operands(hd_ref, g_scr, opnd_scr, fac_scr)

        for p in range(HPB):
            oc = slice(p * HDV, (p + 1) * HDV)
            gac = _head_cols(p)[3]
            o_t = o_ref[:, oc]
            rstd = lax.rsqrt(jnp.mean(o_t * o_t, axis=-1, keepdims=True) + EPS)
            ohat = o_t * rstd
            ga = hd_ref[:, gac]
            sg = _sigmoid(ga)
            dya_t = dya_ref[:, oc]
            gn = gn_ref[:, oc]
            dh_ref[:, gac] = (dya_t * ohat * gn * (sg * (1.0 + ga * (1.0 - sg)))).astype(dh_ref.dtype)
            don = dya_t * (ga * sg)
            dgn_ref[p] += jnp.sum(don * ohat, axis=0, keepdims=True)
            dohat = don * gn
            do_scr[:, oc] = rstd * (dohat - ohat * jnp.mean(dohat * ohat, axis=-1, keepdims=True))

        lower = _lower_mask()
        for c in range(nct - 1, -1, -1):
            rows = slice(c * CHUNK, (c + 1) * CHUNK)
            for p in range(HPB):
                vc = _head_cols(p)[2]
                gc = slice(p * HDK, (p + 1) * HDK)
                v = hd_ref[rows, vc]
                do = do_scr[rows, p * HDV:(p + 1) * HDV]
                st = st_ref[p, c]
                dst = dstate[p]
                egl = jnp.exp(g_scr[(c + 1) * CHUNK - 1:(c + 1) * CHUNK, gc])
                a = _scores(opnd_scr, rows, gc, lower)
                da = _dot(do, v, NT)
                da1 = jnp.where(lower, da, 0.0).astype(MXU)
                da2 = jnp.where(lower, 0.0, da).astype(MXU)
                res_scr[0, rows, gc] = _dot(da1, opnd_scr[KE1, rows, gc], NN)
                res_scr[1, rows, gc] = _dot(da1, opnd_scr[QE1, rows, gc], TN)
                res_scr[2, rows, gc] = _dot(da2, opnd_scr[KE2, rows, gc], NN)
                res_scr[3, rows, gc] = _dot(da2, opnd_scr[QE2, rows, gc], TN)
                res_scr[4, rows, gc] = _dot(do, st, NN)
                res_scr[5, rows, gc] = _dot(v, dst, NN)
                dh_ref[rows, vc] = (_dot(a, do, TN) + _dot(opnd_scr[KD, rows, gc], dst, NT)).astype(dh_ref.dtype)
                dgl_scr[c:c + 1, gc] = egl * jnp.sum(dst * st, axis=0, keepdims=True)
                dstate[p] = dst * egl + _dot(do, opnd_scr[QA, rows, gc], TN)

        p1, p2, p3 = res_scr[0] * fac_scr[EP], res_scr[2] * fac_scr[EM], res_scr[4] * fac_scr[EA]
        r1, r2, r3 = res_scr[1] * fac_scr[EM], res_scr[3] * fac_scr[EP], res_scr[5] * fac_scr[EDL]
        dq = (p1 + p2 + p3) * Q_SCALE
        dk = r1 + r2 + r3
        dgq = p1 - p2 + p3
        dgk = r2 - r1
        for p in range(HPB):
            qc, kc, _, _ = _head_cols(p)
            gc = slice(p * HDK, (p + 1) * HDK)
            dh_ref[:, qc] = dq[:, gc].astype(dh_ref.dtype)
            dh_ref[:, kc] = dk[:, gc].astype(dh_ref.dtype)
            k = hd_ref[:, kc]
            r3k = r3[:, gc] * k
            dg_scr[:, gc] = (hd_ref[:, qc] * Q_SCALE) * dgq[:, gc] + k * dgk[:, gc] - r3k
            for c in range(nct):
                last = slice((c + 1) * CHUNK - 1, (c + 1) * CHUNK)
                dg_scr[last, gc] += jnp.sum(r3k[c * CHUNK:(c + 1) * CHUNK], axis=0, keepdims=True) + dgl_scr[c:c + 1, gc]

        dz = _seg_rcumsum_rolls(dg_scr[...]) * _sigmoid(-z) * (1.0 / GATE_TAU)
        dz_ref[...] = dz.astype(dz_ref.dtype)
        for p in range(HPB):
            db_ref[p] += jnp.sum(dz[:, p * HDK:(p + 1) * HDK], axis=0, keepdims=True)

    rev = lambda i: nt - 1 - i
    in_specs = _gla_specs(tt, rev) + [
        pl.BlockSpec((tt, HPB * HDV), lambda h, i: (rev(i), h)),
        pl.BlockSpec((HPB, nct, HDV, HDK), lambda h, i: (h, rev(i), 0, 0)),
        pl.BlockSpec((tt, HPB * HDV), lambda h, i: (rev(i), h)),
        ANY,
    ]
    return _call(
        body, name="gla_bwd", grid=(HEADS // HPB, nt), in_specs=in_specs,
        out_specs=[
            pl.BlockSpec((tt, HPB * HEAD_W), lambda h, i: (rev(i), HD0 // (HPB * HEAD_W) + h)),
            pl.BlockSpec((tt, HPB * HDK), lambda h, i: (rev(i), h)),
            pl.BlockSpec((HPB, 1, HDV), lambda h, i: (h, 0, 0)),
            pl.BlockSpec((HPB, 1, HDK), lambda h, i: (h, 0, 0)),
        ],
        out_shape=[
            jax.ShapeDtypeStruct(dh.shape, dh.dtype),
            jax.ShapeDtypeStruct((s, DK), MXU),
            jax.ShapeDtypeStruct((HEADS, 1, HDV), F32),
            jax.ShapeDtypeStruct((HEADS, 1, HDK), F32),
        ],
        scratch=[pltpu.VMEM((HPB, HDV, HDK), F32), pltpu.VMEM((tt, GW), F32), pltpu.VMEM((tt, GW), F32),
                 pltpu.VMEM((tt, HPB * HDV), F32), pltpu.VMEM((6, tt, GW), MXU), pltpu.VMEM((4, tt, GW), F32),
                 pltpu.VMEM((6, tt, GW), F32), pltpu.VMEM((max(nct, 8), GW), F32)],
        args=(hh, hh, wup, b_alpha, gnorm, o, states, dya, dh), ride=ride, aliases={8: 0}, vmem=VMEM_BIG)


def _window_count(tile, tt, w):
    pos = tile * tt + lax.broadcasted_iota(jnp.int32, (tt, PG), 0) + 1
    return jnp.minimum(pos, w).astype(F32)


def _pool_fwd(hh, wpool, scale, *, tt):
    s = hh.shape[0]
    nt = s // tt

    def body(ug_ref, w_ref, sc_ref, pooled_ref, yb_ref, halo):
        i = pl.program_id(0)

        @pl.when(i == 0)
        def _():
            halo[...] = jnp.zeros_like(halo)

        for g, w in enumerate(POOL_WINDOWS):
            cols = slice(g * PG, (g + 1) * PG)
            u = ug_ref[:, cols]
            run = jnp.concatenate([halo[:, cols], u], axis=0)
            sh = 1
            while sh < w:
                run = run + pltpu.roll(run, sh, 0)
                sh *= 2
            pooled = run[HALO:, :] / _window_count(i, tt, w) - u
            pooled_ref[:, cols] = pooled.astype(pooled_ref.dtype)
            mixed = _dot(pooled, w_ref[g], NN)
            gb = ug_ref[:, slice(D + g * PG, D + (g + 1) * PG)]
            yb_ref[:, cols] = (mixed * sc_ref[:, cols] * (gb * _sigmoid(gb))).astype(yb_ref.dtype)
        halo[...] = ug_ref[tt - HALO:tt, :D]

    tile = pl.BlockSpec((tt, D), lambda i: (i, 0))
    return _call(
        body, name="pool_fwd", grid=(nt,),
        in_specs=[
            pl.BlockSpec((tt, 2 * D), lambda i: (i, PI0 // (2 * D))),
            pl.BlockSpec((len(POOL_WINDOWS), PG, PG), lambda i: (0, 0, 0)),
            pl.BlockSpec((1, D), lambda i: (0, 0)),
        ],
        out_specs=[tile] * 2,
        out_shape=[jax.ShapeDtypeStruct((s, D), MXU), jax.ShapeDtypeStruct((s, D), MXU)],
        scratch=[pltpu.VMEM((HALO, D), F32)], args=(hh, wpool, scale))


def _pool_bwd(hh, wpool, scale, pooled, dyb, dh, *, tt):
    s = hh.shape[0]
    nt = s // tt

    def body(gb_ref, w_ref, sc_ref, pooled_ref, dyb_ref, _dh_in, dh_ref, dw_ref, dsc_ref, halo):
        i = pl.program_id(0)
        tile = nt - 1 - i

        @pl.when(i == 0)
        def _():
            halo[...] = jnp.zeros_like(halo)
            dw_ref[...] = jnp.zeros_like(dw_ref)
            dsc_ref[...] = jnp.zeros_like(dsc_ref)

        for g, w in enumerate(POOL_WINDOWS):
            cols = slice(g * PG, (g + 1) * PG)
            gcols = slice(D + g * PG, D + (g + 1) * PG)
            gb = gb_ref[:, cols]
            sg = _sigmoid(gb)
            pooled = pooled_ref[:, cols]
            mixed = _dot(pooled, w_ref[g], NN)
            sc = sc_ref[:, cols]
            dyb = dyb_ref[:, cols]
            dh_ref[:, gcols] = (dyb * mixed * sc * (sg * (1.0 + gb * (1.0 - sg)))).astype(dh_ref.dtype)
            dms = dyb * (gb * sg)
            dsc_ref[:, cols] += jnp.sum(dms * mixed, axis=0, keepdims=True)
            dmixed = dms * sc
            dpooled = _dot(dmixed, w_ref[g], NT)
            dw_ref[g] += _dot(pooled, dmixed, TN)
            e = dpooled / _window_count(tile, tt, w)
            run = jnp.concatenate([e, halo[:, cols]], axis=0)
            sh = 1
            while sh < w:
                run = run + pltpu.roll(run, tt + HALO - sh, 0)
                sh *= 2
            dh_ref[:, cols] = (run[:tt, :] - dpooled).astype(dh_ref.dtype)
            halo[:, cols] = e[:HALO, :]

    rev = lambda i: nt - 1 - i
    tile = pl.BlockSpec((tt, D), lambda i: (rev(i), 0))
    wspec = pl.BlockSpec((len(POOL_WINDOWS), PG, PG), lambda i: (0, 0, 0))
    vec = pl.BlockSpec((1, D), lambda i: (0, 0))
    return _call(
        body, name="pool_bwd", grid=(nt,),
        in_specs=[pl.BlockSpec((tt, D), lambda i: (rev(i), GB0 // D)), wspec, vec, tile, tile, ANY],
        out_specs=[pl.BlockSpec((tt, 2 * D), lambda i: (rev(i), PI0 // (2 * D))), wspec, vec],
        out_shape=[jax.ShapeDtypeStruct(dh.shape, dh.dtype), jax.ShapeDtypeStruct((len(POOL_WINDOWS), PG, PG), F32),
                   jax.ShapeDtypeStruct((1, D), F32)],
        scratch=[pltpu.VMEM((HALO, D), F32)], args=(hh, wpool, scale, pooled, dyb, dh), aliases={5: 0})


def _merge_fwd(hh, x, ya, yb, wpa, wpb, wout, b_merge, ln_g, ln_b, *, tt):
    s = x.shape[0]

    def body(ml_ref, x_ref, ya_ref, yb_ref, wpa_ref, wpb_ref, wout_ref, bm_ref, g_ref, b_ref, r_ref, xn_ref, xnb_ref):
        pa = _dot(ya_ref[...], wpa_ref[...], NN)
        pb = _dot(yb_ref[...], wpb_ref[...], NN)
        merged = _sigmoid(ml_ref[:, :D] + bm_ref[:, :D]) * pa + _sigmoid(ml_ref[:, D:] + bm_ref[:, D:]) * pb
        r = ALPHA * x_ref[...] + _dot(merged, wout_ref[...], NN)
        r_ref[...] = r
        mu = jnp.mean(r, axis=-1, keepdims=True)
        xc = r - mu
        var = jnp.mean(xc * xc, axis=-1, keepdims=True)
        xn = xc * lax.rsqrt(var + EPS) * g_ref[...] + b_ref[...]
        xn_ref[...] = xn
        xnb_ref[...] = xn.astype(xnb_ref.dtype)

    tile = pl.BlockSpec((tt, D), lambda i: (i, 0))
    full = pl.BlockSpec((D, D), lambda i: (0, 0), pipeline_mode=pl.Buffered(1))
    vec = pl.BlockSpec((1, D), lambda i: (0, 0))
    return _call(
        body, name="merge_fwd", grid=(s // tt,),
        in_specs=[pl.BlockSpec((tt, 2 * D), lambda i: (i, ML0 // (2 * D))), tile, tile, tile, full, full, full,
                  pl.BlockSpec((1, 2 * D), lambda i: (0, 0)), vec, vec],
        out_specs=[tile] * 3, out_shape=[jax.ShapeDtypeStruct((s, D), F32)] * 2 + [jax.ShapeDtypeStruct((s, D), MXU)],
        args=(hh, x, ya, yb, wpa, wpb, wout, b_merge, ln_g, ln_b), sem=("parallel",), vmem=VMEM_BIG)


def _merge_bwd(hh, r, ya, yb, dout, wpa, wpb, wout, b_merge, ln_g, *, tt):
    s = r.shape[0]

    def body(ml_ref, r_ref, ya_ref, yb_ref, do_ref, wpa_ref, wpb_ref, wout_ref, bm_ref, g_ref,
             dh_ref, dr_ref, dpa_ref, dpb_ref, dwout_ref, dg_ref, db_ref, dbm_ref):
        @pl.when(pl.program_id(0) == 0)
        def _():
            dwout_ref[...] = jnp.zeros_like(dwout_ref)
            dg_ref[...] = jnp.zeros_like(dg_ref)
            db_ref[...] = jnp.zeros_like(db_ref)
            dbm_ref[...] = jnp.zeros_like(dbm_ref)

        rr = r_ref[...]
        mu = jnp.mean(rr, axis=-1, keepdims=True)
        xc = rr - mu
        rstd = lax.rsqrt(jnp.mean(xc * xc, axis=-1, keepdims=True) + EPS)
        xhat = xc * rstd
        do = do_ref[...]
        dg_ref[...] += jnp.sum(do * xhat, axis=0, keepdims=True)
        db_ref[...] += jnp.sum(do, axis=0, keepdims=True)
        dxh = do * g_ref[...]
        dr = rstd * (dxh - jnp.mean(dxh, axis=-1, keepdims=True) - xhat * jnp.mean(dxh * xhat, axis=-1, keepdims=True))
        dr_ref[...] = dr
        g_a = _sigmoid(ml_ref[:, :D] + bm_ref[:, :D])
        g_b = _sigmoid(ml_ref[:, D:] + bm_ref[:, D:])
        pa = _dot(ya_ref[...], wpa_ref[...], NN)
        pb = _dot(yb_ref[...], wpb_ref[...], NN)
        dwout_ref[...] += _dot(g_a * pa + g_b * pb, dr, TN)
        dm = _dot(dr, wout_ref[...], NT)
        dpa_ref[...] = (dm * g_a).astype(dpa_ref.dtype)
        dpb_ref[...] = (dm * g_b).astype(dpb_ref.dtype)
        dml_a = dm * pa * g_a * (1.0 - g_a)
        dml_b = dm * pb * g_b * (1.0 - g_b)
        dh_ref[:, :D] = dml_a.astype(dh_ref.dtype)
        dh_ref[:, D:] = dml_b.astype(dh_ref.dtype)
        dbm_ref[:, :D] += jnp.sum(dml_a, axis=0, keepdims=True)
        dbm_ref[:, D:] += jnp.sum(dml_b, axis=0, keepdims=True)

    tile = pl.BlockSpec((tt, D), lambda i: (i, 0))
    full = pl.BlockSpec((D, D), lambda i: (0, 0))
    vec = pl.BlockSpec((1, D), lambda i: (0, 0))
    vec2 = pl.BlockSpec((1, 2 * D), lambda i: (0, 0))
    mlb = pl.BlockSpec((tt, 2 * D), lambda i: (i, ML0 // (2 * D)))
    return _call(
        body, name="merge_bwd", grid=(s // tt,),
        in_specs=[mlb, tile, tile, tile, tile, full, full, full, vec2, vec],
        out_specs=[mlb, tile, tile, tile, full, vec, vec, vec2],
        out_shape=[
            jax.ShapeDtypeStruct((s, HP), MXU), jax.ShapeDtypeStruct((s, D), F32),
            jax.ShapeDtypeStruct((s, D), MXU), jax.ShapeDtypeStruct((s, D), MXU),
            jax.ShapeDtypeStruct((D, D), F32), jax.ShapeDtypeStruct((1, D), F32),
            jax.ShapeDtypeStruct((1, D), F32), jax.ShapeDtypeStruct((1, 2 * D), F32),
        ],
        args=(hh, r, ya, yb, dout, wpa, wpb, wout, b_merge, ln_g), vmem=VMEM_BIG)


def _proj_bwd(ya, yb, dpa, dpb, wpa, wpb, *, tt):
    s = ya.shape[0]

    def body(ya_ref, yb_ref, dpa_ref, dpb_ref, wpa_ref, wpb_ref, dya_ref, dyb_ref, dwa_ref, dwb_ref):
        @pl.when(pl.program_id(0) == 0)
        def _():
            dwa_ref[...] = jnp.zeros_like(dwa_ref)
            dwb_ref[...] = jnp.zeros_like(dwb_ref)

        for y_ref, dp_ref, w_ref, dy_ref, dw_ref in ((ya_ref, dpa_ref, wpa_ref, dya_ref, dwa_ref),
                                                     (yb_ref, dpb_ref, wpb_ref, dyb_ref, dwb_ref)):
            dp = dp_ref[...]
            dy_ref[...] = _dot(dp, w_ref[...], NT)
            dw_ref[...] += _dot(y_ref[...], dp, TN)

    tile = pl.BlockSpec((tt, D), lambda i: (i, 0))
    full = pl.BlockSpec((D, D), lambda i: (0, 0))
    return _call(
        body, name="proj_bwd", grid=(s // tt,), in_specs=[tile] * 4 + [full] * 2, out_specs=[tile, tile, full, full],
        out_shape=[jax.ShapeDtypeStruct((s, D), F32)] * 2 + [jax.ShapeDtypeStruct((D, D), F32)] * 2,
        args=(ya, yb, dpa, dpb, wpa, wpb), vmem=VMEM_BIG)


def _alpha_grads(dz, wup, hh, dh, *, tm):
    s = dz.shape[0]

    def body(dz_ref, w_ref, al_ref, _dh_in, dh_ref, dw_ref):
        @pl.when(pl.program_id(0) == 0)
        def _():
            dw_ref[...] = jnp.zeros_like(dw_ref)

        dz_t = dz_ref[...]
        dh_ref[...] = _dot(dz_t, w_ref[...], NT).astype(dh_ref.dtype)
        dw_ref[...] += _dot(al_ref[...], dz_t, TN)

    return _call(
        body, name="alpha_grads", grid=(s // tm,),
        in_specs=[pl.BlockSpec((tm, DK), lambda i: (i, 0)), pl.BlockSpec((AL_W, DK), lambda i: (0, 0)),
                  pl.BlockSpec((tm, 128), lambda i: (i, AL0 // 128)), ANY],
        out_specs=[pl.BlockSpec((tm, AL_W), lambda i: (i, AL0 // AL_W)), pl.BlockSpec((128, DK), lambda i: (0, 0))],
        out_shape=[jax.ShapeDtypeStruct(dh.shape, dh.dtype), jax.ShapeDtypeStruct((128, DK), F32)],
        args=(dz, wup, hh, dh), aliases={3: 0})


def _loss_head(y, target, *, tt):
    s = y.shape[0]

    def body(y_ref, t_ref, loss_ref, dy_ref):
        @pl.when(pl.program_id(0) == 0)
        def _():
            loss_ref[...] = jnp.zeros_like(loss_ref)

        err = y_ref[...] - t_ref[...]
        dy_ref[...] = err * (1.0 / D)
        per_tok = jnp.mean(err * err, axis=-1, keepdims=True)
        loss_ref[...] += 0.5 * jnp.sum(per_tok, axis=0, keepdims=True)

    tile = pl.BlockSpec((tt, D), lambda i: (i, 0))
    return _call(
        body, name="loss_head", grid=(s // tt,), in_specs=[tile, tile],
        out_specs=[pl.BlockSpec((1, 1), lambda i: (0, 0)), tile],
        out_shape=[jax.ShapeDtypeStruct((1, 1), F32), jax.ShapeDtypeStruct((s, D), F32)], args=(y, target))


def _adamw_math(share, w_ref, m_ref, v_ref, g_ref, d_ref, nm_ref, nv_ref):
    g = share(0).astype(F32)
    for q in range(1, N_DEV):
        g = g + share(q).astype(F32)
    g_ref[0] = g
    nm = ADAM_B1 * m_ref[0] + (1.0 - ADAM_B1) * g
    nv = ADAM_B2 * v_ref[0] + (1.0 - ADAM_B2) * (g * g)
    nm_ref[0] = nm
    nv_ref[0] = nv
    m_hat = nm / (1.0 - ADAM_B1 ** ADAM_STEP)
    v_hat = nv / (1.0 - ADAM_B2 ** ADAM_STEP)
    d_ref[0] = -ADAM_LR * (m_hat / (jnp.sqrt(v_hat) + ADAM_EPS) + ADAM_WD * w_ref[0])


def _adamw_layers(parts, w, m, v, *, tc, name):
    nl, rows, cols = w.shape
    nc = cols // tc

    def body(*refs):
        p_refs, rest = refs[:nl], refs[nl:]
        for j in range(nl):
            @pl.when(pl.program_id(0) == j)
            def _(p_ref=p_refs[j]):
                _adamw_math(lambda q: p_ref[q], *rest)

    def part_spec(j):
        return pl.BlockSpec((N_DEV, rows, tc), lambda l, c: (0, 0, jnp.where(l == j, c, jnp.where(l < j, 0, nc - 1))))

    tile = pl.BlockSpec((1, rows, tc), lambda l, c: (l, 0, c))
    return _call(
        body, name=name, grid=(nl, nc),
        in_specs=[part_spec(j) for j in range(nl)] + [tile, tile, tile],
        out_specs=[tile] * 4, out_shape=[jax.ShapeDtypeStruct((nl, rows, cols), F32)] * 4,
        args=(*parts, w, m, v))


def _adamw(parts, w, m, v, *, tr, tc, name):
    nl, rows, cols = w.shape

    def body(p_ref, *rest):
        _adamw_math(lambda q: p_ref[0, q], *rest)

    tile = pl.BlockSpec((1, tr, tc), lambda l, i, j: (l, i, j))
    return _call(
        body, name=name, grid=(nl, rows // tr, cols // tc),
        in_specs=[pl.BlockSpec((1, N_DEV, tr, tc), lambda l, i, j: (l, 0, i, j)), tile, tile, tile],
        out_specs=[tile] * 4, out_shape=[jax.ShapeDtypeStruct((nl, rows, cols), F32)] * 4,
        args=(parts, w, m, v), sem=("parallel", "parallel", "parallel"))


def _from_devices(g, axis):
    nd = g.ndim - 1
    perm = list(range(1, axis + 1)) + [0] + list(range(axis + 1, nd + 1))
    shape = list(g.shape[1:])
    shape[axis] *= N_DEV
    return jnp.transpose(g, perm).reshape(shape)


def _to_devices(a, axis):
    shape = list(a.shape)
    t = a.reshape(shape[:axis] + [N_DEV, shape[axis] // N_DEV] + shape[axis + 1:])
    return jnp.transpose(t, [axis] + list(range(0, axis)) + list(range(axis + 1, t.ndim)))


def _h_row_segments():
    segs = [(O_PI, PI0, IN_COLS - O_PI), (O_AL, AL0, RANK)]
    for h in range(HEADS):
        base = HD0 + h * HEAD_W
        segs += [(O_Q + h * HDK, base, HDK), (O_K + h * HDK, base + HDK, HDK),
                 (O_V + h * HDV, base + 2 * HDK, HDV), (O_GA + h * HDV, base + 2 * HDK + HDV, HDV)]
    return segs


def _h_weight_t(parts):
    return _move_rows(parts, _h_row_segments(), (HP, D), name="w_in_rows", zero=(AL0 + RANK, AL_W - RANK))


def _w_in_grad_parts_t(dwt):
    return _move_rows(dwt, [(d0, s0, n) for s0, d0, n in _h_row_segments()], (N_DEV, SHARD, D), name="w_in_grad_rows")


def kernel(x, w_in, w_alpha_up, b_alpha, gla_norm_g, w_pool_grp, pool_scale, b_merge, w_proj_a, w_proj_b, w_out, ln_g, ln_b, loss_target, m_w_in, m_w_alpha_up, m_b_alpha, m_gla_norm_g, m_w_pool_grp, m_pool_scale, m_b_merge, m_w_proj_a, m_w_proj_b, m_w_out, m_ln_g, m_ln_b, v_w_in, v_w_alpha_up, v_b_alpha, v_gla_norm_g, v_w_pool_grp, v_pool_scale, v_b_merge, v_w_proj_a, v_w_proj_b, v_w_out, v_ln_g, v_ln_b):
    s = x.shape[1]
    tt = min(256, s)
    tm = min(512, s)
    tb = min(1024, s)
    tn = HP // 3
    xs = x.reshape(s, D)

    tr3 = lambda a: jnp.transpose(a, (0, 2, 1))
    w_in_s = tr3(w_in).astype(WIRE)
    proj_s = jnp.stack([w_proj_a, w_proj_b, w_out], axis=1).astype(WIRE)
    pool_s = w_pool_grp.astype(WIRE)

    g_in, g_up, g_gn = _gather_first(w_in_s[0], [w_alpha_up.astype(WIRE), gla_norm_g], name="gather_first")
    wup = jnp.pad(_from_devices(g_up, 2), ((0, 0), (0, AL_W - RANK), (0, 0)))
    gn = _from_devices(g_gn, 2).reshape(DEPTH, 1, D)

    saved, wt_all, proj_all, pool_all = [], [], [], []
    cur, cur_b = xs, xs.astype(MXU)
    g_proj = g_pool = None
    for l in range(DEPTH):
        wt = _h_weight_t(g_in)
        nxt_l = l + 1 < DEPTH
        steps = (HP // tn) * (s // tb)
        res = _in_proj(cur_b, wt, tm=tb, tn=tn, ride=_ChipGather(w_in_s[l + 1], (2 * steps) // 3) if nxt_l else None)
        hh = res[0]
        if nxt_l:
            g_in = res[1]
        layers = ([0] if l == 0 else []) + ([l + 1] if nxt_l else [])
        res = _gla_fwd(hh, wup[l], b_alpha[l:l + 1], gn[l], tt=tb,
                       ride=_Exchange([(a[j], True) for j in layers for a in (proj_s, pool_s)]) if layers else None)
        o, ya, states = res[:3]
        got = {j: res[3 + 2 * t:5 + 2 * t] for t, j in enumerate(layers)}
        if l == 0:
            g_proj, g_pool = got[0]
        proj = _from_devices(g_proj, 1)
        pool = _from_devices(g_pool, 1)
        if nxt_l:
            g_proj, g_pool = got[l + 1]
        wt_all.append(wt), proj_all.append(proj), pool_all.append(pool)
        pooled, yb = _pool_fwd(hh, pool, pool_scale[l:l + 1], tt=tb)
        r, nxt, nxt_b = _merge_fwd(hh, cur, ya, yb, proj[0], proj[1], proj[2],
                                   b_merge[l:l + 1], ln_g[l:l + 1], ln_b[l:l + 1], tt=tm)
        saved.append(dict(xb=cur_b, hh=hh, o=o, ya=ya, states=states, pooled=pooled, yb=yb, r=r))
        cur, cur_b = nxt, nxt_b

    loss_part, dcur = _loss_head(cur, loss_target.reshape(s, D), tt=tm)
    loss = lax.psum(loss_part[0, 0], ("x", "y", "c"))

    small = {k: [None] * DEPTH for k in ("w_up", "b_alpha", "gnorm", "pool_scale", "b_merge", "ln_g", "ln_b")}
    parts = {k: [None] * DEPTH for k in ("w_in", "proj", "pool")}
    for l in range(DEPTH - 1, -1, -1):
        sv = saved[l]
        hh = sv["hh"]
        dh, dr, dpa, dpb, dw_out, dln_g, dln_b, db_merge = _merge_bwd(
            hh, sv["r"], sv["ya"], sv["yb"], dcur, proj_all[l][0], proj_all[l][1], proj_all[l][2], b_merge[l:l + 1], ln_g[l:l + 1], tt=tt)
        dya, dyb, dw_pa, dw_pb = _proj_bwd(sv["ya"], sv["yb"], dpa, dpb, proj_all[l][0], proj_all[l][1], tt=tm)
        dh, dw_pool, dscale = _pool_bwd(hh, pool_all[l], pool_scale[l:l + 1], sv["pooled"], dyb, dh, tt=tb)
        ride = _Exchange([(_to_devices(jnp.stack([dw_pa, dw_pb, dw_out]), 1).astype(WIRE), False),
                          (_to_devices(dw_pool, 1).astype(WIRE), False)])
        dh, dz, dgn, db_al, parts["proj"][l], parts["pool"][l] = _gla_bwd(
            hh, wup[l], b_alpha[l:l + 1], gn[l], sv["o"], sv["states"], dya, dh, tt=tb, ride=ride)
        dh, dw_up = _alpha_grads(dz, wup[l], hh, dh, tm=tb)
        dwt = _mm_tn(dh, sv["xb"], tm=tb, tk=tn, name="w_in_grad", out_dtype=WIRE)
        dcur, parts["w_in"][l] = _in_proj_bwd(dh, wt_all[l], dr, tm=tt, ride=_Exchange([(_w_in_grad_parts_t(dwt), False)]))

        small["w_up"][l] = dw_up[:RANK]
        small["b_alpha"][l] = db_al.reshape(DK)
        small["gnorm"][l] = dgn.reshape(HEADS, HDV)
        small["pool_scale"][l] = dscale[0]
        small["b_merge"][l] = db_merge[0]
        small["ln_g"][l], small["ln_b"][l] = dln_g[0], dln_b[0]
    grad_x = dcur[None]
    sm = {k: jnp.stack(v) for k, v in small.items()}

    rep = (("b_alpha", b_alpha, m_b_alpha, v_b_alpha), ("pool_scale", pool_scale, m_pool_scale, v_pool_scale),
           ("b_merge", b_merge, m_b_merge, v_b_merge), ("ln_g", ln_g, m_ln_g, v_ln_g), ("ln_b", ln_b, m_ln_b, v_ln_b))
    cat = lambda arrs: jnp.concatenate(arrs, axis=1)
    p_up, p_gn, p_rep = _exchange([(_to_devices(sm["w_up"], 2), False), (_to_devices(sm["gnorm"], 2), False),
                                   (cat([sm[nm] for nm, _, _, _ in rep]), True)], name="exchange_small_grads")

    def update(p, w, m, v, tr, name, layered=True, tc=None):
        shape = w.shape
        nl = shape[0] if layered else 1
        cols = shape[-1]
        flat = lambda a: a.reshape(nl, -1, cols)
        outs = _adamw(p.reshape(nl, N_DEV, -1, cols), flat(w), flat(m), flat(v), tr=tr, tc=tc or cols, name=name)
        return [o_.reshape(shape) for o_ in outs]

    res = {}
    res["w_in"] = [tr3(o_) for o_ in _adamw_layers(parts["w_in"], tr3(w_in), tr3(m_w_in), tr3(v_w_in), tc=128, name="adamw_w_in")]
    for j, (nm, w, m, v) in enumerate((("w_proj_a", w_proj_a, m_w_proj_a, v_w_proj_a), ("w_proj_b", w_proj_b, m_w_proj_b, v_w_proj_b),
                                       ("w_out", w_out, m_w_out, v_w_out))):
        res[nm] = _adamw_layers([p[:, j] for p in parts["proj"]], w, m, v, tc=D, name="adamw_" + nm)
    res["w_pool_grp"] = update(jnp.stack(parts["pool"]), w_pool_grp, m_w_pool_grp, v_w_pool_grp, 128, "adamw_w_pool")
    res["w_alpha_up"] = update(p_up, w_alpha_up, m_w_alpha_up, v_w_alpha_up, DEPTH * RANK, "adamw_w_up", layered=False)
    res["gla_norm_g"] = update(p_gn, gla_norm_g, m_gla_norm_g, v_gla_norm_g, DEPTH * HEADS, "adamw_gnorm", layered=False)
    rep_out = update(p_rep, cat([w for _, w, _, _ in rep]), cat([m for _, _, m, _ in rep]), cat([v for _, _, _, v in rep]),
                     DEPTH, "adamw_small", layered=False)
    off = 0
    for nm, w, _, _ in rep:
        n = w.shape[1]
        res[nm] = [o_[:, off:off + n] for o_ in rep_out]
        off += n

    order = ("w_in", "w_alpha_up", "b_alpha", "gla_norm_g", "w_pool_grp", "pool_scale", "b_merge", "w_proj_a", "w_proj_b",
             "w_out", "ln_g", "ln_b")
    return (loss, grad_x, *[res[n][0] for n in order], *[res[n][1] for n in order],
            *[res[n][2] for n in order], *[res[n][3] for n in order])
```
